```python
import math
import jax, jax.numpy as jnp
from jax import lax
import numpy as np

D_MODEL = 1024
BATCH = 16
SEQ = 4096
DEPTH = 1
DEC_BATCH = 8
DEC_SEQ = 8192
PAST_LEN = 128

HEAD_DIM = 64
A_Q_HEADS = 8
A_KV_HEADS = 2
B_HEADS = 6
GRID_W = 64
ROPE_THETA = 10000.0
Q_BLOCK = 128
WINDOW_DILATIONS = ((128, 1), (512, 4), (2048, 16))
N_EXPERTS = 64
TOP_K = 8
N_GROUPS = 8
TOPK_GROUPS = 4
D_EXPERT = 256
D_SHARED = 256
ROUTE_SCALE = 2.5
MOE_BLOCK = 128
MOE_CHUNK = 16
PLE_DIM = 256
NEG_BIG = -1e30
EPS = 1e-6
A_WIDTH = A_Q_HEADS * HEAD_DIM
A_KV_WIDTH = A_KV_HEADS * HEAD_DIM
B_WIDTH = B_HEADS * HEAD_DIM
IN_WIDTH = A_WIDTH + 2 * A_KV_WIDTH + 3 * B_WIDTH + 2 * D_MODEL

kernel_name = "hybrid_axial_gqa_dilated_moe_encoder"


def rmsnorm(x, g):
    xf = x.astype(jnp.float32)
    y = xf * lax.rsqrt(jnp.mean(xf * xf, axis=-1, keepdims=True) + EPS)
    return (y * g.astype(jnp.float32)).astype(x.dtype)


def apply_rope(x, ang):
    xf = x.astype(jnp.float32)
    xe, xo = xf[..., 0::2], xf[..., 1::2]
    c = jnp.cos(ang)[None, :, None, :]
    s = jnp.sin(ang)[None, :, None, :]
    out = jnp.stack([xe * c - xo * s, xe * s + xo * c], axis=-1).reshape(x.shape)
    return out.astype(x.dtype)


def axial_angles(S):
    rows = S // GRID_W
    row = jnp.repeat(jnp.arange(rows, dtype=jnp.float32), GRID_W)
    col = jnp.tile(jnp.arange(GRID_W, dtype=jnp.float32), rows)
    half = HEAD_DIM // 2
    inv = ROPE_THETA ** (-jnp.arange(0, half, 2, dtype=jnp.float32) / half)
    return jnp.concatenate([row[:, None] * inv, col[:, None] * inv], axis=-1)


def linear_angles(S):
    t = jnp.arange(S, dtype=jnp.float32)
    inv = ROPE_THETA ** (-jnp.arange(0, HEAD_DIM, 2, dtype=jnp.float32) / HEAD_DIM)
    return t[:, None] * inv


def dense_gqa(q, k, v):
    B, S, Hkv, G, D = q.shape
    nq = S // Q_BLOCK
    scale = D ** -0.5
    qb = jnp.moveaxis(q.reshape(B, nq, Q_BLOCK, Hkv, G, D), 1, 0)

    def block(qi):
        s = jnp.einsum("bqhgd,bshd->bhgqs", qi, k).astype(jnp.float32) * scale
        pr = jax.nn.softmax(s, axis=-1).astype(v.dtype)
        return jnp.einsum("bhgqs,bshd->bqhgd", pr, v)

    o = lax.map(block, qb)
    return jnp.moveaxis(o, 0, 1).reshape(B, S, Hkv * G * D)


def dilated_window_attn(q, k, v, dil, half):
    B, S, H, D = q.shape
    L = S // dil
    nb = -(-L // half)
    Lp = nb * half
    scale = D ** -0.5

    def strided(a):
        return a.reshape(B, L, dil, H, D).transpose(0, 2, 1, 3, 4)

    qs = jnp.pad(strided(q), ((0, 0), (0, 0), (0, Lp - L), (0, 0), (0, 0))).reshape(B, dil, nb, half, H, D)

    def windows(a):
        ap = jnp.pad(strided(a), ((0, 0), (0, 0), (half, Lp - L + half), (0, 0), (0, 0)))
        ap = ap.reshape(B, dil, nb + 2, half, H, D)
        return jnp.concatenate([ap[:, :, 0:nb], ap[:, :, 1:nb + 1], ap[:, :, 2:nb + 2]], axis=3)

    ks, vs = windows(k), windows(v)
    n = jnp.arange(nb)[:, None, None]
    i = jnp.arange(half)[None, :, None]
    j = jnp.arange(3 * half)[None, None, :]
    qpos = n * half + i
    kpos = n * half - half + j
    valid = (jnp.abs(kpos - qpos) <= half) & (kpos >= 0) & (kpos < L)

    s = jnp.einsum("brnihd,brnjhd->brnhij", qs, ks).astype(jnp.float32) * scale
    s = jnp.where(valid[None, None, :, None], s, NEG_BIG)
    m = jnp.max(s, axis=-1, keepdims=True)
    e = jnp.exp(s - m)
    l = jnp.sum(e, axis=-1, keepdims=True)
    o = jnp.einsum("brnhij,brnjhd->brnihd", (e / l).astype(v.dtype), vs)
    lse = (m + jnp.log(l))[..., 0].transpose(0, 1, 2, 4, 3)
    o = o.reshape(B, dil, Lp, H, D)[:, :, :L].transpose(0, 2, 1, 3, 4).reshape(B, S, H, D)
    lse = lse.reshape(B, dil, Lp, H)[:, :, :L].transpose(0, 2, 1, 3).reshape(B, S, H)
    return o, lse


def route(h, router_w, router_bias):
    T = h.shape[0]
    scores = jax.nn.sigmoid(h.astype(jnp.float32) @ router_w.astype(jnp.float32))
    sel = scores + router_bias.astype(jnp.float32)
    grp = sel.reshape(T, N_GROUPS, N_EXPERTS // N_GROUPS)
    grp_score = jnp.sum(lax.top_k(grp, 2)[0], axis=-1)
    _, gidx = lax.top_k(grp_score, TOPK_GROUPS)
    gmask = jnp.sum(jax.nn.one_hot(gidx, N_GROUPS, dtype=jnp.float32), axis=1) > 0
    emask = jnp.repeat(gmask, N_EXPERTS // N_GROUPS, axis=1)
    _, idx = lax.top_k(jnp.where(emask, sel, NEG_BIG), TOP_K)
    w = jnp.take_along_axis(scores, idx, axis=1)
    w = w / jnp.sum(w, axis=-1, keepdims=True) * ROUTE_SCALE
    return idx, w


def routed_experts(h, idx, wts, w_gate, w_up, w_down):
    T, D = h.shape
    A = T * TOP_K
    e_flat = idx.reshape(-1).astype(jnp.int32)
    tok_flat = jnp.repeat(jnp.arange(T, dtype=jnp.int32), TOP_K)
    w_flat = wts.reshape(-1)
    order = jnp.argsort(e_flat)
    e_sorted = e_flat[order]
    counts = jnp.zeros((N_EXPERTS,), jnp.int32).at[e_flat].add(1)
    padded = (counts + MOE_BLOCK - 1) // MOE_BLOCK * MOE_BLOCK
    pad_end = jnp.cumsum(padded)
    pad_start = pad_end - padded
    grp_start = jnp.cumsum(counts) - counts
    rank = jnp.arange(A, dtype=jnp.int32) - grp_start[e_sorted]
    dest = pad_start[e_sorted] + rank
    unit = MOE_BLOCK * MOE_CHUNK
    n_rows = -(-(A + N_EXPERTS * MOE_BLOCK) // unit) * unit
    buf_tok = jnp.full((n_rows,), T, jnp.int32).at[dest].set(tok_flat[order])
    buf_w = jnp.zeros((n_rows,), h.dtype).at[dest].set(w_flat[order].astype(h.dtype))
    n_blocks = n_rows // MOE_BLOCK
    blk_start = jnp.arange(n_blocks, dtype=jnp.int32) * MOE_BLOCK
    blk_e = jnp.minimum(jnp.searchsorted(pad_end, blk_start, side="right"), N_EXPERTS - 1)
    h_pad = jnp.concatenate([h, jnp.zeros((1, D), h.dtype)], axis=0)
    n_chunks = n_blocks // MOE_CHUNK
    tok_c = buf_tok.reshape(n_chunks, MOE_CHUNK, MOE_BLOCK)
    w_c = buf_w.reshape(n_chunks, MOE_CHUNK, MOE_BLOCK)
    e_c = blk_e.reshape(n_chunks, MOE_CHUNK)

    def chunk(args):
        tok, eid, wr = args
        xs = h_pad[tok]
        g = jnp.einsum("cbd,cdf->cbf", xs, w_gate[eid])
        u = jnp.einsum("cbd,cdf->cbf", xs, w_up[eid])
        y = jnp.einsum("cbf,cfd->cbd", jax.nn.silu(g) * u, w_down[eid])
        return y * wr[..., None]

    ys = lax.map(chunk, (tok_c, e_c, w_c))
    out = jax.ops.segment_sum(ys.reshape(-1, D), buf_tok, num_segments=T + 1)
    return out[:T]


def encoder_layer(h, pl, norm_mix, w_in, q_norm_a, k_norm_a, q_norm_b, k_norm_b, w_branch_a, w_branch_b,
                  w_out, norm_ffn, router_w, router_bias, expert_gate, expert_up, expert_down,
                  shared_gate, shared_up, shared_down, ple_proj, ple_gate):
    B, S, D = h.shape
    u = rmsnorm(h, norm_mix)
    z = u @ w_in
    cuts = [A_WIDTH, A_WIDTH + A_KV_WIDTH, A_WIDTH + 2 * A_KV_WIDTH,
            A_WIDTH + 2 * A_KV_WIDTH + B_WIDTH, A_WIDTH + 2 * A_KV_WIDTH + 2 * B_WIDTH,
            A_WIDTH + 2 * A_KV_WIDTH + 3 * B_WIDTH, A_WIDTH + 2 * A_KV_WIDTH + 3 * B_WIDTH + D_MODEL]
    qa, ka, va, qb, kb, vb, ga, gb = jnp.split(z, cuts, axis=-1)

    ang_a = axial_angles(S)
    qa = apply_rope(rmsnorm(qa.reshape(B, S, A_Q_HEADS, HEAD_DIM), q_norm_a), ang_a)
    ka = apply_rope(rmsnorm(ka.reshape(B, S, A_KV_HEADS, HEAD_DIM), k_norm_a), ang_a)
    va = va.reshape(B, S, A_KV_HEADS, HEAD_DIM)
    qa = qa.reshape(B, S, A_KV_HEADS, A_Q_HEADS // A_KV_HEADS, HEAD_DIM)
    ya = dense_gqa(qa, ka, va)

    ang_b = linear_angles(S)
    qb = apply_rope(rmsnorm(qb.reshape(B, S, B_HEADS, HEAD_DIM), q_norm_b), ang_b)
    kb = apply_rope(rmsnorm(kb.reshape(B, S, B_HEADS, HEAD_DIM), k_norm_b), ang_b)
    vb = vb.reshape(B, S, B_HEADS, HEAD_DIM)
    outs, lses = [], []
    for window, dil in WINDOW_DILATIONS:
        o_g, lse_g = dilated_window_attn(qb, kb, vb, dil, window // (2 * dil))
        outs.append(o_g)
        lses.append(lse_g)
    mix_w = jax.nn.softmax(jnp.stack(lses, axis=0), axis=0)
    yb = jnp.sum(mix_w[..., None].astype(vb.dtype) * jnp.stack(outs, axis=0), axis=0).reshape(B, S, B_WIDTH)

    merged = jax.nn.sigmoid(ga) * (ya @ w_branch_a) + jax.nn.sigmoid(gb) * (yb @ w_branch_b)
    h = h + merged @ w_out

    u2 = rmsnorm(h, norm_ffn).reshape(B * S, D)
    idx, wts = route(u2, router_w, router_bias)
    routed = routed_experts(u2, idx, wts, expert_gate, expert_up, expert_down)
    shared = (jax.nn.silu(u2 @ shared_gate) * (u2 @ shared_up)) @ shared_down
    h = h + (routed + shared).reshape(B, S, D)

    h = h + jax.nn.sigmoid(h @ ple_gate) * (pl @ ple_proj)
    return h


def setup_inputs(seed: int = 0) -> dict:
    key = jax.random.key(seed)
    ks = jax.random.split(key, 32)
    f32 = jnp.float32

    def w(k, shape, fan_in):
        return jax.random.normal(k, shape, f32) * fan_in ** -0.5

    def gain(k, shape):
        return 1.0 + 0.05 * jax.random.normal(k, shape, f32)

    return {
        "x_prompt": jax.random.normal(ks[0], (BATCH, SEQ, D_MODEL), f32),
        "x_sample": jax.random.normal(ks[1], (DEC_BATCH, DEC_SEQ, D_MODEL), f32),
        "p_prompt": jax.random.normal(ks[2], (DEPTH, BATCH, SEQ, PLE_DIM), f32),
        "p_sample": jax.random.normal(ks[3], (DEPTH, DEC_BATCH, DEC_SEQ, PLE_DIM), f32),
        "norm_mix": gain(ks[4], (DEPTH, D_MODEL)),
        "w_in": w(ks[5], (DEPTH, D_MODEL, IN_WIDTH), D_MODEL),
        "q_norm_a": gain(ks[6], (DEPTH, HEAD_DIM)),
        "k_norm_a": gain(ks[7], (DEPTH, HEAD_DIM)),
        "q_norm_b": gain(ks[8], (DEPTH, HEAD_DIM)),
        "k_norm_b": gain(ks[9], (DEPTH, HEAD_DIM)),
        "w_branch_a": w(ks[10], (DEPTH, A_WIDTH, D_MODEL), A_WIDTH),
        "w_branch_b": w(ks[11], (DEPTH, B_WIDTH, D_MODEL), B_WIDTH),
        "w_out": w(ks[12], (DEPTH, D_MODEL, D_MODEL), D_MODEL),
        "norm_ffn": gain(ks[13], (DEPTH, D_MODEL)),
        "router_w": w(ks[14], (DEPTH, D_MODEL, N_EXPERTS), D_MODEL),
        "router_bias": 0.01 * jax.random.normal(ks[15], (DEPTH, N_EXPERTS), f32),
        "expert_gate": w(ks[16], (DEPTH, N_EXPERTS, D_MODEL, D_EXPERT), D_MODEL),
        "expert_up": w(ks[17], (DEPTH, N_EXPERTS, D_MODEL, D_EXPERT), D_MODEL),
        "expert_down": w(ks[18], (DEPTH, N_EXPERTS, D_EXPERT, D_MODEL), D_EXPERT),
        "shared_gate": w(ks[19], (DEPTH, D_MODEL, D_SHARED), D_MODEL),
        "shared_up": w(ks[20], (DEPTH, D_MODEL, D_SHARED), D_MODEL),
        "shared_down": w(ks[21], (DEPTH, D_SHARED, D_MODEL), D_SHARED),
        "ple_proj": w(ks[22], (DEPTH, PLE_DIM, D_MODEL), PLE_DIM),
        "ple_gate": w(ks[23], (DEPTH, D_MODEL, D_MODEL), D_MODEL),
    }


def run_trunk(x, p, layer_weights):
    h = x
    for i in range(DEPTH):
        h = encoder_layer(h, p[i], *[wt[i] for wt in layer_weights])
    return h


def reference(x_prompt, x_sample, p_prompt, p_sample, norm_mix, w_in, q_norm_a, k_norm_a, q_norm_b, k_norm_b,
              w_branch_a, w_branch_b, w_out, norm_ffn, router_w, router_bias, expert_gate, expert_up,
              expert_down, shared_gate, shared_up, shared_down, ple_proj, ple_gate):
    layer_weights = (norm_mix, w_in, q_norm_a, k_norm_a, q_norm_b, k_norm_b, w_branch_a, w_branch_b,
                     w_out, norm_ffn, router_w, router_bias, expert_gate, expert_up, expert_down,
                     shared_gate, shared_up, shared_down, ple_proj, ple_gate)
    y_prompt = run_trunk(x_prompt, p_prompt, layer_weights)
    y_sample = run_trunk(x_sample, p_sample, layer_weights)
    return (y_prompt, y_sample)
```

```python
import functools

import jax
import jax.numpy as jnp
from jax import lax
from jax.experimental import pallas as pl
from jax.experimental.pallas import tpu as pltpu

D_MODEL = 1024
HEAD_DIM = 64
A_Q_HEADS = 8
A_KV_HEADS = 2
B_HEADS = 6
GRID_W = 64
ROPE_THETA = 10000.0
WINDOW_DILATIONS = ((128, 1), (512, 4), (2048, 16))
N_EXPERTS = 64
TOP_K = 8
N_GROUPS = 8
TOPK_GROUPS = 4
D_EXPERT = 256
D_SHARED = 256
ROUTE_SCALE = 2.5
PLE_DIM = 256
NEG_BIG = -1e30
EPS = 1e-6

A_WIDTH = A_Q_HEADS * HEAD_DIM
A_KV_WIDTH = A_KV_HEADS * HEAD_DIM
B_WIDTH = B_HEADS * HEAD_DIM
GROUP_Q = A_Q_HEADS // A_KV_HEADS
EXPERTS_PER_GROUP = N_EXPERTS // N_GROUPS

LANES = 128
HALF_WINDOW = 64
KEY_SPAN = 256

KA_COLS = 2 * LANES
VA_COLS = 2 * LANES
OFF_QA = 0
OFF_KA = OFF_QA + A_WIDTH
OFF_VA = OFF_KA + KA_COLS
OFF_QB = OFF_VA + VA_COLS
OFF_KB = OFF_QB + B_WIDTH
OFF_VB = OFF_KB + B_WIDTH
OFF_GA = OFF_VB + B_WIDTH
OFF_GB = OFF_GA + D_MODEL
PROJ_COLS = OFF_GB + D_MODEL

TOKEN_TILE = 512
FLASH_TQ = 256
FLASH_TK = 512
DIL_QBLOCK = 128
MOE_ROWS = 256

VMEM_LIMIT = 52 * 1024 * 1024

bf16 = jnp.bfloat16
f32 = jnp.float32


def _cparams(sem):
    return pltpu.CompilerParams(dimension_semantics=sem, vmem_limit_bytes=VMEM_LIMIT)


def _in_proj_kernel(x_ref, nm_ref, w_ref, gqa_ref, gka_ref, gqb_ref, gkb_ref,
                    ca_ref, sa_ref, cb_ref, sb_ref, bd_ref,
                    qa_ref, ka_ref, va_ref, qb_ref, kb_ref, vb_ref, ga_ref, gb_ref):
    x = x_ref[...]
    r = lax.rsqrt(jnp.mean(x * x, axis=-1, keepdims=True) + EPS)
    u = (x * r * nm_ref[...]).astype(bf16)
    rows = x.shape[0]
    lane = lax.broadcasted_iota(jnp.int32, (rows, LANES), 1)
    first_half = (lane % HEAD_DIM) < (HEAD_DIM // 2)

    def proj(off, width):
        return jnp.dot(u, w_ref[:, off:off + width], preferred_element_type=f32)

    def norm_rope(off, width, g_ref, c_ref, s_ref, out_ref):
        z = proj(off, width)
        c = c_ref[...]
        s = s_ref[...]
        for c0 in range(0, width, LANES):
            zc = z[:, c0:c0 + LANES]
            ms = jnp.dot((zc * zc).astype(bf16), bd_ref[...], preferred_element_type=f32) * (1.0 / HEAD_DIM)
            zn = zc * lax.rsqrt(ms + EPS) * g_ref[:, c0:c0 + LANES]
            sw = jnp.where(first_half, pltpu.roll(zn, LANES - HEAD_DIM // 2, 1), pltpu.roll(zn, HEAD_DIM // 2, 1))
            out_ref[:, c0:c0 + LANES] = (zn * c + sw * s).astype(out_ref.dtype)

    norm_rope(OFF_QA, A_WIDTH, gqa_ref, ca_ref, sa_ref, qa_ref)
    norm_rope(OFF_KA, KA_COLS, gka_ref, ca_ref, sa_ref, ka_ref)
    va_ref[...] = proj(OFF_VA, VA_COLS).astype(va_ref.dtype)
    norm_rope(OFF_QB, B_WIDTH, gqb_ref, cb_ref, sb_ref, qb_ref)
    norm_rope(OFF_KB, B_WIDTH, gkb_ref, cb_ref, sb_ref, kb_ref)
    vb_ref[...] = proj(OFF_VB, B_WIDTH).astype(vb_ref.dtype)
    ga_ref[...] = jax.nn.sigmoid(proj(OFF_GA, D_MODEL)).astype(ga_ref.dtype)
    gb_ref[...] = jax.nn.sigmoid(proj(OFF_GB, D_MODEL)).astype(gb_ref.dtype)


def _in_proj(x2, S, prm):
    T = x2.shape[0]
    tm = TOKEN_TILE
    n_pos = S // tm
    row = lambda i: (i, 0)
    pos = lambda i: (i % n_pos, 0)
    const = lambda i: (0, 0)
    widths = (A_WIDTH, KA_COLS, VA_COLS, B_WIDTH, B_WIDTH, B_WIDTH, D_MODEL, D_MODEL)
    return pl.pallas_call(
        _in_proj_kernel,
        grid=(T // tm,),
        in_specs=[
            pl.BlockSpec((tm, D_MODEL), row),
            pl.BlockSpec((1, D_MODEL), const),
            pl.BlockSpec((D_MODEL, PROJ_COLS), const),
            pl.BlockSpec((1, A_WIDTH), const),
            pl.BlockSpec((1, KA_COLS), const),
            pl.BlockSpec((1, B_WIDTH), const),
            pl.BlockSpec((1, B_WIDTH), const),
            pl.BlockSpec((tm, LANES), pos),
            pl.BlockSpec((tm, LANES), pos),
            pl.BlockSpec((tm, LANES), pos),
            pl.BlockSpec((tm, LANES), pos),
            pl.BlockSpec((LANES, LANES), const),
        ],
        out_specs=[pl.BlockSpec((tm, w), row) for w in widths],
        out_shape=[jax.ShapeDtypeStruct((T, w), bf16) for w in widths],
        compiler_params=_cparams(("parallel",)),
    )(x2, prm["norm_mix"], prm["w_in"], prm["gqa"], prm["gka"], prm["gqb"], prm["gkb"],
      prm["cos_a"][S], prm["sin_a"][S], prm["cos_b"][S], prm["sin_b"][S], prm["bd"])


def _flash_kernel(q_ref, k_ref, v_ref, o_ref, *, tq, tk, seq):
    lane = lax.broadcasted_iota(jnp.int32, (tq, LANES), 1)
    lo = lane < HEAD_DIM
    q = q_ref[0]
    q01, q23 = q[:, :LANES], q[:, LANES:]
    zero = jnp.zeros_like(q01)
    qs = jnp.concatenate([jnp.where(lo, q01, zero), jnp.where(lo, zero, q01),
                          jnp.where(lo, q23, zero), jnp.where(lo, zero, q23)], axis=0)
    vlane = lax.broadcasted_iota(jnp.int32, (tk, LANES), 1) < HEAD_DIM

    def body(j, carry):
        m, acc = carry
        start = pl.multiple_of(j * tk, tk)
        kk = k_ref[0, pl.ds(start, tk), :]
        vv = v_ref[0, pl.ds(start, tk), :]
        vv = jnp.where(vlane, vv, jnp.ones_like(vv))
        s = lax.dot_general(qs, kk, (((1,), (1,)), ((), ())), preferred_element_type=f32)
        m_new = jnp.maximum(m, jnp.max(s, axis=-1, keepdims=True))
        p = jnp.exp(s - m_new)
        alpha = jnp.exp(m - m_new)
        acc = acc * alpha + jnp.dot(p.astype(bf16), vv, preferred_element_type=f32)
        return m_new, acc

    m0 = jnp.full((GROUP_Q * tq, 1), NEG_BIG, f32)
    acc0 = jnp.zeros((GROUP_Q * tq, LANES), f32)
    _, acc = lax.fori_loop(0, seq // tk, body, (m0, acc0))
    lo4 = lax.broadcasted_iota(jnp.int32, (GROUP_Q * tq, LANES), 1) < HEAD_DIM
    denom = jnp.where(lo4, pltpu.roll(acc, HEAD_DIM, 1), 1.0)
    o = acc / denom
    heads = [o[h * tq:(h + 1) * tq] for h in range(GROUP_Q)]
    out01 = jnp.where(lo, heads[0], pltpu.roll(heads[1], HEAD_DIM, 1))
    out23 = jnp.where(lo, heads[2], pltpu.roll(heads[3], HEAD_DIM, 1))
    o_ref[0] = jnp.concatenate([out01, out23], axis=1).astype(o_ref.dtype)


def _flash(qa, ka, va):
    B, S, _ = qa.shape
    tq, tk = FLASH_TQ, FLASH_TK
    return pl.pallas_call(
        functools.partial(_flash_kernel, tq=tq, tk=tk, seq=S),
        grid=(B, A_KV_HEADS, S // tq),
        in_specs=[
            pl.BlockSpec((1, tq, GROUP_Q * HEAD_DIM), lambda b, g, i: (b, i, g)),
            pl.BlockSpec((1, S, LANES), lambda b, g, i: (b, 0, g)),
            pl.BlockSpec((1, S, LANES), lambda b, g, i: (b, 0, g)),
        ],
        out_specs=pl.BlockSpec((1, tq, GROUP_Q * HEAD_DIM), lambda b, g, i: (b, i, g)),
        out_shape=jax.ShapeDtypeStruct((B, S, A_WIDTH), bf16),
        compiler_params=_cparams(("parallel", "parallel", "arbitrary")),
    )(qa, ka, va)


def _dilated_kernel(q_ref, k_ref, v_ref, o_ref, lse_ref, *, length, tms):
    i = pl.program_id(2)
    lane = lax.broadcasted_iota(jnp.int32, (DIL_QBLOCK, LANES), 1)
    lo = lane < HEAD_DIM
    for sb in range(tms // DIL_QBLOCK):
        m0 = i * tms + sb * DIL_QBLOCK
        start = pl.multiple_of(jnp.clip(m0 - HALF_WINDOW, 0, length - KEY_SPAN), HALF_WINDOW)
        qpos = m0 + lax.broadcasted_iota(jnp.int32, (DIL_QBLOCK, KEY_SPAN), 0)
        kpos = start + lax.broadcasted_iota(jnp.int32, (DIL_QBLOCK, KEY_SPAN), 1)
        valid = jnp.abs(kpos - qpos) <= HALF_WINDOW
        r0 = sb * DIL_QBLOCK
        for c in range(B_WIDTH // LANES):
            cs = slice(c * LANES, (c + 1) * LANES)
            qp = q_ref[0, r0:r0 + DIL_QBLOCK, cs]
            kp = k_ref[0, pl.ds(start, KEY_SPAN), cs]
            vp = v_ref[0, pl.ds(start, KEY_SPAN), cs]
            zero = jnp.zeros_like(qp)
            outs, lses = [], []
            for half in range(2):
                qm = jnp.where(lo, qp, zero) if half == 0 else jnp.where(lo, zero, qp)
                s = lax.dot_general(qm, kp, (((1,), (1,)), ((), ())), preferred_element_type=f32)
                s = jnp.where(valid, s, NEG_BIG)
                m = jnp.max(s, axis=-1, keepdims=True)
                e = jnp.exp(s - m)
                l = jnp.sum(e, axis=-1, keepdims=True)
                outs.append(jnp.dot((e / l).astype(bf16), vp, preferred_element_type=f32))
                lses.append(m + jnp.log(l))
            o_ref[0, r0:r0 + DIL_QBLOCK, cs] = jnp.where(lo, outs[0], outs[1]).astype(o_ref.dtype)
            lse_ref[0, r0:r0 + DIL_QBLOCK, cs] = jnp.where(lo, lses[0], lses[1])


def _dilated(qb, kb, vb, dil):
    B, S, _ = qb.shape
    length = S // dil
    tms = min(length, 512)
    view = lambda a: a.reshape(B, length, dil * B_WIDTH)
    o, lse = pl.pallas_call(
        functools.partial(_dilated_kernel, length=length, tms=tms),
        grid=(B, dil, length // tms),
        in_specs=[
            pl.BlockSpec((1, tms, B_WIDTH), lambda b, r, i: (b, i, r)),
            pl.BlockSpec((1, length, B_WIDTH), lambda b, r, i: (b, 0, r)),
            pl.BlockSpec((1, length, B_WIDTH), lambda b, r, i: (b, 0, r)),
        ],
        out_specs=[
            pl.BlockSpec((1, tms, B_WIDTH), lambda b, r, i: (b, i, r)),
            pl.BlockSpec((1, tms, B_WIDTH), lambda b, r, i: (b, i, r)),
        ],
        out_shape=[
            jax.ShapeDtypeStruct((B, length, dil * B_WIDTH), bf16),
            jax.ShapeDtypeStruct((B, length, dil * B_WIDTH), f32),
        ],
        compiler_params=_cparams(("parallel", "parallel", "arbitrary")),
    )(view(qb), view(kb), view(vb))
    return o.reshape(B * S, B_WIDTH), lse.reshape(B * S, B_WIDTH)


def _first_index_of_max(cur, idx, n):
    mx = jnp.max(cur, axis=0, keepdims=True)
    return jnp.min(jnp.where(cur == mx, idx, n), axis=0, keepdims=True)


def _merge_route_kernel(x_ref, ya_ref, o1_ref, o2_ref, o3_ref, l1_ref, l2_ref, l3_ref, ga_ref, gb_ref,
                        wa_ref, wb_ref, wo_ref, nf_ref, rhi_ref, rlo_ref, bias_ref, tri_ref,
                        h_ref, u_ref, idx_ref, wt_ref, rank_ref, cnt_ref, run_ref):
    step = pl.program_id(0)

    @pl.when(step == 0)
    def _():
        run_ref[...] = jnp.zeros_like(run_ref)

    l1, l2, l3 = l1_ref[...], l2_ref[...], l3_ref[...]
    lm = jnp.maximum(jnp.maximum(l1, l2), l3)
    e1, e2, e3 = jnp.exp(l1 - lm), jnp.exp(l2 - lm), jnp.exp(l3 - lm)
    yb = (e1 * o1_ref[...].astype(f32) + e2 * o2_ref[...].astype(f32) + e3 * o3_ref[...].astype(f32)) / (e1 + e2 + e3)

    pa = jnp.dot(ya_ref[...], wa_ref[...], preferred_element_type=f32)
    pb = jnp.dot(yb.astype(bf16), wb_ref[...], preferred_element_type=f32)
    merged = ga_ref[...].astype(f32) * pa + gb_ref[...].astype(f32) * pb
    h = x_ref[...] + jnp.dot(merged.astype(bf16), wo_ref[...], preferred_element_type=f32)
    h_ref[...] = h

    r = lax.rsqrt(jnp.mean(h * h, axis=-1, keepdims=True) + EPS)
    uf = h * r * nf_ref[...]
    u_hi = uf.astype(bf16)
    u_ref[...] = u_hi
    u_lo = (uf - u_hi.astype(f32)).astype(bf16)
    logits = (jnp.dot(u_hi, rhi_ref[...], preferred_element_type=f32)
              + jnp.dot(u_lo, rhi_ref[...], preferred_element_type=f32)
              + jnp.dot(u_hi, rlo_ref[...], preferred_element_type=f32))
    scores = jax.nn.sigmoid(logits.T[:N_EXPERTS, :])
    sel = scores + bias_ref[...]
    tm = sel.shape[1]

    i8 = lax.broadcasted_iota(jnp.int32, (EXPERTS_PER_GROUP, tm), 0)
    neg_inf = jnp.float32(-jnp.inf)
    gscore = jnp.zeros((N_GROUPS, tm), f32)
    for g in range(N_GROUPS):
        blk = sel[g * EXPERTS_PER_GROUP:(g + 1) * EXPERTS_PER_GROUP, :]
        top1 = jnp.max(blk, axis=0, keepdims=True)
        first = _first_index_of_max(blk, i8, EXPERTS_PER_GROUP)
        top2 = jnp.max(jnp.where(i8 == first, neg_inf, blk), axis=0, keepdims=True)
        gscore = jnp.where(i8 == g, top1 + top2, gscore)
    gkeep = jnp.zeros((N_GROUPS, tm), jnp.bool_)
    cur = gscore
    for _ in range(TOPK_GROUPS):
        pick = i8 == _first_index_of_max(cur, i8, N_GROUPS)
        gkeep = jnp.logical_or(gkeep, pick)
        cur = jnp.where(pick, neg_inf, cur)
    cur = jnp.concatenate(
        [jnp.where(gkeep[g:g + 1, :], sel[g * EXPERTS_PER_GROUP:(g + 1) * EXPERTS_PER_GROUP, :], NEG_BIG)
         for g in range(N_GROUPS)], axis=0)

    ie = lax.broadcasted_iota(jnp.int32, (N_EXPERTS, tm), 0)
    ik = lax.broadcasted_iota(jnp.int32, (TOP_K, tm), 0)
    idx_out = jnp.zeros((TOP_K, tm), jnp.int32)
    w_out = jnp.zeros((TOP_K, tm), f32)
    chosen = jnp.zeros((N_EXPERTS, tm), jnp.bool_)
    for k in range(TOP_K):
        fi = _first_index_of_max(cur, ie, N_EXPERTS)
        pick = ie == fi
        wk = jnp.sum(jnp.where(pick, scores, 0.0), axis=0, keepdims=True)
        idx_out = jnp.where(ik == k, fi, idx_out)
        w_out = jnp.where(ik == k, wk, w_out)
        chosen = jnp.logical_or(chosen, pick)
        cur = jnp.where(pick, neg_inf, cur)
    w_out = w_out / jnp.sum(w_out, axis=0, keepdims=True) * ROUTE_SCALE
    idx_ref[...] = idx_out
    wt_ref[...] = w_out

    sel01 = jnp.where(chosen, 1.0, 0.0)
    before = jnp.dot(sel01.astype(bf16), tri_ref[...], preferred_element_type=f32) + run_ref[:, 0:1]
    rank_out = jnp.zeros((TOP_K, tm), f32)
    for k in range(TOP_K):
        rk = jnp.sum(jnp.where(ie == idx_out[k:k + 1, :], before, 0.0), axis=0, keepdims=True)
        rank_out = jnp.where(ik == k, rk, rank_out)
    rank_ref[...] = rank_out.astype(jnp.int32)
    run_ref[...] = run_ref[...] + jnp.sum(sel01, axis=1, keepdims=True)
    cnt_ref[...] = run_ref[...].astype(jnp.int32)


def _merge_route(x2, ya, outs, lses, sga, sgb, prm):
    T = x2.shape[0]
    tm = TOKEN_TILE
    row = lambda i: (i, 0)
    col = lambda i: (0, i)
    const = lambda i: (0, 0)
    tok = lambda w: pl.BlockSpec((tm, w), row)
    full = lambda a: pl.BlockSpec(a.shape, const)
    return pl.pallas_call(
        _merge_route_kernel,
        grid=(T // tm,),
        in_specs=[tok(D_MODEL), tok(A_WIDTH), tok(B_WIDTH), tok(B_WIDTH), tok(B_WIDTH),
                  tok(B_WIDTH), tok(B_WIDTH), tok(B_WIDTH), tok(D_MODEL), tok(D_MODEL),
                  full(prm["w_branch_a"]), full(prm["w_branch_b"]), full(prm["w_out"]), full(prm["norm_ffn"]),
                  full(prm["router_hi"]), full(prm["router_lo"]), full(prm["router_bias"]), full(prm["tri"])],
        out_specs=[tok(D_MODEL), tok(D_MODEL),
                   pl.BlockSpec((TOP_K, tm), col), pl.BlockSpec((TOP_K, tm), col), pl.BlockSpec((TOP_K, tm), col),
                   pl.BlockSpec((N_EXPERTS, LANES), const)],
        out_shape=[jax.ShapeDtypeStruct((T, D_MODEL), f32), jax.ShapeDtypeStruct((T, D_MODEL), bf16),
                   jax.ShapeDtypeStruct((TOP_K, T), jnp.int32), jax.ShapeDtypeStruct((TOP_K, T), f32),
                   jax.ShapeDtypeStruct((TOP_K, T), jnp.int32),
                   jax.ShapeDtypeStruct((N_EXPERTS, LANES), jnp.int32)],
        scratch_shapes=[pltpu.VMEM((N_EXPERTS, LANES), f32)],
        compiler_params=_cparams(("arbitrary",)),
    )(x2, ya, *outs, *lses, sga, sgb,
      prm["w_branch_a"], prm["w_branch_b"], prm["w_out"], prm["norm_ffn"],
      prm["router_hi"], prm["router_lo"], prm["router_bias"], prm["tri"])


def _experts_kernel(be_ref, nu_ref, x_ref, rw_ref, wgu_ref, wd_ref, y_ref):
    del be_ref
    i = pl.program_id(0)

    @pl.when(i < nu_ref[0])
    def _():
        gu = jnp.dot(x_ref[...], wgu_ref[0], preferred_element_type=f32)
        g, u = gu[:, :D_EXPERT], gu[:, D_EXPERT:]
        a = g * jax.nn.sigmoid(g) * u * rw_ref[...]
        y_ref[...] = jnp.dot(a.astype(bf16), wd_ref[0], preferred_element_type=f32).astype(y_ref.dtype)

    @pl.when(i >= nu_ref[0])
    def _():
        y_ref[...] = jnp.zeros_like(y_ref)


def _experts(xs, roww, blk_e, n_used, prm):
    n_rows = xs.shape[0]
    bm = MOE_ROWS
    return pl.pallas_call(
        _experts_kernel,
        grid_spec=pltpu.PrefetchScalarGridSpec(
            num_scalar_prefetch=2,
            grid=(n_rows // bm,),
            in_specs=[
                pl.BlockSpec((bm, D_MODEL), lambda i, be, nu: (jnp.minimum(i, nu[0] - 1), 0)),
                pl.BlockSpec((bm, 1), lambda i, be, nu: (jnp.minimum(i, nu[0] - 1), 0)),
                pl.BlockSpec((1, D_MODEL, 2 * D_EXPERT), lambda i, be, nu: (be[i], 0, 0)),
                pl.BlockSpec((1, D_EXPERT, D_MODEL), lambda i, be, nu: (be[i], 0, 0)),
            ],
            out_specs=pl.BlockSpec((bm, D_MODEL), lambda i, be, nu: (i, 0)),
        ),
        out_shape=jax.ShapeDtypeStruct((n_rows, D_MODEL), bf16),
        compiler_params=_cparams(("arbitrary",)),
    )(blk_e, n_used, xs, roww, prm["expert_gu"], prm["expert_down"])


def _final_kernel(h_ref, u_ref, yg_ref, p_ref, sgu_ref, sd_ref, pg_ref, pp_ref, o_ref):
    routed = yg_ref[0].astype(f32)
    for k in range(1, TOP_K):
        routed = routed + yg_ref[k].astype(f32)
    gu = jnp.dot(u_ref[...], sgu_ref[...], preferred_element_type=f32)
    g, u = gu[:, :D_SHARED], gu[:, D_SHARED:]
    shared = jnp.dot((g * jax.nn.sigmoid(g) * u).astype(bf16), sd_ref[...], preferred_element_type=f32)
    h = h_ref[...] + (routed + shared)
    gate = jax.nn.sigmoid(jnp.dot(h.astype(bf16), pg_ref[...], preferred_element_type=f32))
    emb = jnp.dot(p_ref[...].astype(bf16), pp_ref[...], preferred_element_type=f32)
    o_ref[...] = h + gate * emb


def _final(h1, u2, yg, p2, prm):
    T = h1.shape[0]
    tm = TOKEN_TILE
    row = lambda i: (i, 0)
    const = lambda i: (0, 0)
    full = lambda a: pl.BlockSpec(a.shape, const)
    return pl.pallas_call(
        _final_kernel,
        grid=(T // tm,),
        in_specs=[pl.BlockSpec((tm, D_MODEL), row), pl.BlockSpec((tm, D_MODEL), row),
                  pl.BlockSpec((TOP_K, tm, D_MODEL), lambda i: (0, i, 0)),
                  pl.BlockSpec((tm, PLE_DIM), row),
                  full(prm["shared_gu"]), full(prm["shared_down"]), full(prm["ple_gate"]), full(prm["ple_proj"])],
        out_specs=pl.BlockSpec((tm, D_MODEL), row),
        out_shape=jax.ShapeDtypeStruct((T, D_MODEL), f32),
        compiler_params=_cparams(("parallel",)),
    )(h1, u2, yg, p2, prm["shared_gu"], prm["shared_down"], prm["ple_gate"], prm["ple_proj"])


def _deinterleave(n_heads):
    base = jnp.concatenate([jnp.arange(0, HEAD_DIM, 2), jnp.arange(1, HEAD_DIM, 2)])
    return (jnp.arange(n_heads)[:, None] * HEAD_DIM + base[None, :]).reshape(-1)


def _rope_tables(ang):
    c, s = jnp.cos(ang), jnp.sin(ang)
    return jnp.tile(jnp.concatenate([c, c], axis=-1), (1, 2)), jnp.tile(jnp.concatenate([-s, s], axis=-1), (1, 2))


def _axial_angles(S):
    rows = S // GRID_W
    row = jnp.repeat(jnp.arange(rows, dtype=f32), GRID_W)
    col = jnp.tile(jnp.arange(GRID_W, dtype=f32), rows)
    half = HEAD_DIM // 2
    inv = ROPE_THETA ** (-jnp.arange(0, half, 2, dtype=f32) / half)
    return jnp.concatenate([row[:, None] * inv, col[:, None] * inv], axis=-1)


def _linear_angles(S):
    t = jnp.arange(S, dtype=f32)
    inv = ROPE_THETA ** (-jnp.arange(0, HEAD_DIM, 2, dtype=f32) / HEAD_DIM)
    return t[:, None] * inv


def _prepare(seq_lens, norm_mix, w_in, q_norm_a, k_norm_a, q_norm_b, k_norm_b, w_branch_a, w_branch_b, w_out,
             norm_ffn, router_w, router_bias, expert_gate, expert_up, expert_down, shared_gate, shared_up,
             shared_down, ple_proj, ple_gate):
    cuts = [A_WIDTH, A_WIDTH + A_KV_WIDTH, A_WIDTH + 2 * A_KV_WIDTH,
            A_WIDTH + 2 * A_KV_WIDTH + B_WIDTH, A_WIDTH + 2 * A_KV_WIDTH + 2 * B_WIDTH,
            A_WIDTH + 2 * A_KV_WIDTH + 3 * B_WIDTH, A_WIDTH + 2 * A_KV_WIDTH + 3 * B_WIDTH + D_MODEL]
    wqa, wka, wva, wqb, wkb, wvb, wga, wgb = jnp.split(w_in, cuts, axis=-1)
    wqa = wqa[:, _deinterleave(A_Q_HEADS)]
    wka = wka[:, _deinterleave(A_KV_HEADS)]
    wqb = wqb[:, _deinterleave(B_HEADS)]
    wkb = wkb[:, _deinterleave(B_HEADS)]
    k0, k1 = wka[:, :HEAD_DIM], wka[:, HEAD_DIM:]
    v0, v1 = wva[:, :HEAD_DIM], wva[:, HEAD_DIM:]
    vz = jnp.zeros_like(v0)
    w_all = jnp.concatenate([wqa, k0, k0, k1, k1, v0, vz, v1, vz, wqb, wkb, wvb, wga, wgb], axis=-1).astype(bf16)
    perm = _deinterleave(1)
    scale = HEAD_DIM ** -0.5
    tile = lambda g, n: jnp.tile(g[perm], n)[None, :]
    router_pad = jnp.pad(router_w, ((0, 0), (0, LANES - N_EXPERTS)))
    router_hi = router_pad.astype(bf16)
    tm = TOKEN_TILE
    prm = {
        "norm_mix": norm_mix[None, :], "w_in": w_all,
        "gqa": tile(q_norm_a, A_Q_HEADS) * scale, "gka": tile(k_norm_a, KA_COLS // HEAD_DIM),
        "gqb": tile(q_norm_b, B_HEADS) * scale, "gkb": tile(k_norm_b, B_HEADS),
        "bd": jnp.kron(jnp.eye(LANES // HEAD_DIM, dtype=f32), jnp.ones((HEAD_DIM, HEAD_DIM), f32)).astype(bf16),
        "w_branch_a": w_branch_a.astype(bf16), "w_branch_b": w_branch_b.astype(bf16), "w_out": w_out.astype(bf16),
        "norm_ffn": norm_ffn[None, :],
        "router_hi": router_hi, "router_lo": (router_pad - router_hi.astype(f32)).astype(bf16),
        "router_bias": router_bias[:, None],
        "tri": (jnp.arange(tm)[:, None] < jnp.arange(tm)[None, :]).astype(bf16),
        "expert_gu": jnp.concatenate([expert_gate, expert_up], axis=-1).astype(bf16),
        "expert_down": expert_down.astype(bf16),
        "shared_gu": jnp.concatenate([shared_gate, shared_up], axis=-1).astype(bf16),
        "shared_down": shared_down.astype(bf16),
        "ple_gate": ple_gate.astype(bf16), "ple_proj": ple_proj.astype(bf16),
        "cos_a": {}, "sin_a": {}, "cos_b": {}, "sin_b": {},
    }
    for S in seq_lens:
        prm["cos_a"][S], prm["sin_a"][S] = _rope_tables(_axial_angles(S))
        prm["cos_b"][S], prm["sin_b"][S] = _rope_tables(_linear_angles(S))
    return prm


def _dispatch_plan(idx, rank, wts, counts, T):
    bm = MOE_ROWS
    n_rows = (T * TOP_K + N_EXPERTS * bm) // bm * bm
    n_blocks = n_rows // bm
    padded = (counts + bm - 1) // bm * bm
    pad_end = jnp.cumsum(padded)
    pad_start = pad_end - padded
    dest = pad_start[idx] + rank
    blk_e = jnp.minimum(jnp.searchsorted(pad_end, jnp.arange(n_blocks, dtype=jnp.int32) * bm, side="right"),
                        N_EXPERTS - 1).astype(jnp.int32)
    n_used = (pad_end[-1:] // bm).astype(jnp.int32)
    tok = jnp.broadcast_to(jnp.arange(T, dtype=jnp.int32)[None, :], (TOP_K, T))
    src_tok = jnp.zeros((n_rows,), jnp.int32).at[dest.reshape(-1)].set(tok.reshape(-1))
    roww = jnp.zeros((n_rows,), f32).at[dest.reshape(-1)].set(wts.reshape(-1))
    return dest, src_tok, roww[:, None], blk_e, n_used


def _layer(x, p, prm):
    B, S, _ = x.shape
    T = B * S
    x2 = x.reshape(T, D_MODEL)
    qa, ka, va, qb, kb, vb, sga, sgb = _in_proj(x2, S, prm)
    shape3 = lambda a: a.reshape(B, S, a.shape[-1])
    ya = _flash(shape3(qa), shape3(ka), shape3(va)).reshape(T, A_WIDTH)
    outs, lses = [], []
    for _, dil in WINDOW_DILATIONS:
        o_d, lse_d = _dilated(shape3(qb), shape3(kb), shape3(vb), dil)
        outs.append(o_d)
        lses.append(lse_d)
    h1, u2, idx, wts, rank, cnt = _merge_route(x2, ya, outs, lses, sga, sgb, prm)
    dest, src_tok, roww, blk_e, n_used = _dispatch_plan(idx, rank, wts, cnt[:, 0], T)
    xs = jnp.take(u2, src_tok, axis=0)
    ys = _experts(xs, roww, blk_e, n_used, prm)
    yg = jnp.take(ys, dest, axis=0)
    out = _final(h1, u2, yg, p.reshape(T, PLE_DIM), prm)
    return out.reshape(B, S, D_MODEL)


def kernel(x_prompt, x_sample, p_prompt, p_sample, norm_mix, w_in, q_norm_a, k_norm_a, q_norm_b, k_norm_b,
           w_branch_a, w_branch_b, w_out, norm_ffn, router_w, router_bias, expert_gate, expert_up, expert_down,
           shared_gate, shared_up, shared_down, ple_proj, ple_gate):
    weights = (norm_mix, w_in, q_norm_a, k_norm_a, q_norm_b, k_norm_b, w_branch_a, w_branch_b, w_out, norm_ffn,
               router_w, router_bias, expert_gate, expert_up, expert_down, shared_gate, shared_up, shared_down,
               ple_proj, ple_gate)
    depth = norm_mix.shape[0]
    seq_lens = sorted({x_prompt.shape[1], x_sample.shape[1]})
    hp, hs = x_prompt, x_sample
    for i in range(depth):
        prm = _prepare(seq_lens, *[w[i] for w in weights])
        hp = _layer(hp, p_prompt[i], prm)
        hs = _layer(hs, p_sample[i], prm)
    return hp, hs
```

```python
import functools

import jax
import jax.numpy as jnp
from jax import lax
from jax.experimental import pallas as pl
from jax.experimental.pallas import tpu as pltpu
from jax.experimental.pallas import tpu_sc as plsc

D_MODEL = 1024
HEAD_DIM = 64
A_Q_HEADS = 8
A_KV_HEADS = 2
B_HEADS = 6
GRID_W = 64
ROPE_THETA = 10000.0
WINDOW_DILATIONS = ((128, 1), (512, 4), (2048, 16))
N_EXPERTS = 64
TOP_K = 8
N_GROUPS = 8
TOPK_GROUPS = 4
D_EXPERT = 256
D_SHARED = 256
ROUTE_SCALE = 2.5
PLE_DIM = 256
NEG_BIG = -1e30
EPS = 1e-6

A_WIDTH = A_Q_HEADS * HEAD_DIM
A_KV_WIDTH = A_KV_HEADS * HEAD_DIM
B_WIDTH = B_HEADS * HEAD_DIM
GROUP_Q = A_Q_HEADS // A_KV_HEADS
EXPERTS_PER_GROUP = N_EXPERTS // N_GROUPS

LANES = 128
HALF_WINDOW = 64
KEY_SPAN = 256

KA_COLS = 2 * LANES
VA_COLS = 2 * LANES
OFF_QA = 0
OFF_KA = OFF_QA + A_WIDTH
OFF_VA = OFF_KA + KA_COLS
OFF_QB = OFF_VA + VA_COLS
OFF_KB = OFF_QB + B_WIDTH
OFF_VB = OFF_KB + B_WIDTH
OFF_GA = OFF_VB + B_WIDTH
OFF_GB = OFF_GA + D_MODEL
PROJ_COLS = OFF_GB + D_MODEL

TOKEN_TILE = 512
FLASH_TQ = 256
FLASH_TK = 512
DIL_QBLOCK = 128
MOE_ROWS = 256

VMEM_LIMIT = 52 * 1024 * 1024

HALF_MODEL = D_MODEL // 2
ROW_CHUNKS = HALF_MODEL // LANES
SC_CORES = 2
SC_SUBCORES = 16
SC_WORKERS = SC_CORES * SC_SUBCORES
SC_WINDOW = 128
SC_SLABS = 4

bf16 = jnp.bfloat16
f32 = jnp.float32


def _cparams(sem):
    return pltpu.CompilerParams(dimension_semantics=sem, vmem_limit_bytes=VMEM_LIMIT)


def _pack_rows(v):
    lo = lax.bitcast_convert_type(v[:, :HALF_MODEL].astype(bf16).astype(f32), jnp.uint32) >> 16
    hi = lax.bitcast_convert_type(v[:, HALF_MODEL:].astype(bf16).astype(f32), jnp.uint32) & jnp.uint32(0xFFFF0000)
    w = lo | hi
    return [w[:, c * LANES:(c + 1) * LANES] for c in range(ROW_CHUNKS)]


def _unpack_rows(chunks):
    w = jnp.concatenate(chunks, axis=1)
    lo = lax.bitcast_convert_type(w << 16, f32)
    hi = lax.bitcast_convert_type(w & jnp.uint32(0xFFFF0000), f32)
    return lo, hi


def _in_proj_kernel(x_ref, nm_ref, w_ref, gqa_ref, gka_ref, gqb_ref, gkb_ref,
                    ca_ref, sa_ref, cb_ref, sb_ref, bd_ref,
                    qa_ref, ka_ref, va_ref, qb_ref, kb_ref, vb_ref, ga_ref, gb_ref):
    x = x_ref[...]
    r = lax.rsqrt(jnp.mean(x * x, axis=-1, keepdims=True) + EPS)
    u = (x * r * nm_ref[...]).astype(bf16)
    rows = x.shape[0]
    lane = lax.broadcasted_iota(jnp.int32, (rows, LANES), 1)
    first_half = (lane % HEAD_DIM) < (HEAD_DIM // 2)

    def proj(off, width):
        return jnp.dot(u, w_ref[:, off:off + width], preferred_element_type=f32)

    def norm_rope(off, width, g_ref, c_ref, s_ref, out_ref):
        z = proj(off, width)
        c = c_ref[...]
        s = s_ref[...]
        for c0 in range(0, width, LANES):
            zc = z[:, c0:c0 + LANES]
            ms = jnp.dot((zc * zc).astype(bf16), bd_ref[...], preferred_element_type=f32) * (1.0 / HEAD_DIM)
            zn = zc * lax.rsqrt(ms + EPS) * g_ref[:, c0:c0 + LANES]
            sw = jnp.where(first_half, pltpu.roll(zn, LANES - HEAD_DIM // 2, 1), pltpu.roll(zn, HEAD_DIM // 2, 1))
            out_ref[:, c0:c0 + LANES] = (zn * c + sw * s).astype(out_ref.dtype)

    norm_rope(OFF_QA, A_WIDTH, gqa_ref, ca_ref, sa_ref, qa_ref)
    norm_rope(OFF_KA, KA_COLS, gka_ref, ca_ref, sa_ref, ka_ref)
    va_ref[...] = proj(OFF_VA, VA_COLS).astype(va_ref.dtype)
    norm_rope(OFF_QB, B_WIDTH, gqb_ref, cb_ref, sb_ref, qb_ref)
    norm_rope(OFF_KB, B_WIDTH, gkb_ref, cb_ref, sb_ref, kb_ref)
    vb_ref[...] = proj(OFF_VB, B_WIDTH).astype(vb_ref.dtype)
    ga_ref[...] = jax.nn.sigmoid(proj(OFF_GA, D_MODEL)).astype(ga_ref.dtype)
    gb_ref[...] = jax.nn.sigmoid(proj(OFF_GB, D_MODEL)).astype(gb_ref.dtype)


def _in_proj(x2, S, prm):
    T = x2.shape[0]
    tm = TOKEN_TILE
    n_pos = S // tm
    row = lambda i: (i, 0)
    pos = lambda i: (i % n_pos, 0)
    const = lambda i: (0, 0)
    widths = (A_WIDTH, KA_COLS, VA_COLS, B_WIDTH, B_WIDTH, B_WIDTH, D_MODEL, D_MODEL)
    return pl.pallas_call(
        _in_proj_kernel,
        grid=(T // tm,),
        in_specs=[
            pl.BlockSpec((tm, D_MODEL), row),
            pl.BlockSpec((1, D_MODEL), const),
            pl.BlockSpec((D_MODEL, PROJ_COLS), const),
            pl.BlockSpec((1, A_WIDTH), const),
            pl.BlockSpec((1, KA_COLS), const),
            pl.BlockSpec((1, B_WIDTH), const),
            pl.BlockSpec((1, B_WIDTH), const),
            pl.BlockSpec((tm, LANES), pos),
            pl.BlockSpec((tm, LANES), pos),
            pl.BlockSpec((tm, LANES), pos),
            pl.BlockSpec((tm, LANES), pos),
            pl.BlockSpec((LANES, LANES), const),
        ],
        out_specs=[pl.BlockSpec((tm, w), row) for w in widths],
        out_shape=[jax.ShapeDtypeStruct((T, w), bf16) for w in widths],
        compiler_params=_cparams(("parallel",)),
    )(x2, prm["norm_mix"], prm["w_in"], prm["gqa"], prm["gka"], prm["gqb"], prm["gkb"],
      prm["cos_a"][S], prm["sin_a"][S], prm["cos_b"][S], prm["sin_b"][S], prm["bd"])


def _flash_kernel(q_ref, k_ref, v_ref, o_ref, *, tq, tk, seq):
    lane = lax.broadcasted_iota(jnp.int32, (tq, LANES), 1)
    lo = lane < HEAD_DIM
    q = q_ref[0]
    q01, q23 = q[:, :LANES], q[:, LANES:]
    zero = jnp.zeros_like(q01)
    qs = jnp.concatenate([jnp.where(lo, q01, zero), jnp.where(lo, zero, q01),
                          jnp.where(lo, q23, zero), jnp.where(lo, zero, q23)], axis=0)
    vlane = lax.broadcasted_iota(jnp.int32, (tk, LANES), 1) < HEAD_DIM

    def body(j, carry):
        m, acc = carry
        start = pl.multiple_of(j * tk, tk)
        kk = k_ref[0, pl.ds(start, tk), :]
        vv = v_ref[0, pl.ds(start, tk), :]
        vv = jnp.where(vlane, vv, jnp.ones_like(vv))
        s = lax.dot_general(qs, kk, (((1,), (1,)), ((), ())), preferred_element_type=f32)
        m_new = jnp.maximum(m, jnp.max(s, axis=-1, keepdims=True))
        p = jnp.exp(s - m_new)
        alpha = jnp.exp(m - m_new)
        acc = acc * alpha + jnp.dot(p.astype(bf16), vv, preferred_element_type=f32)
        return m_new, acc

    m0 = jnp.full((GROUP_Q * tq, 1), NEG_BIG, f32)
    acc0 = jnp.zeros((GROUP_Q * tq, LANES), f32)
    _, acc = lax.fori_loop(0, seq // tk, body, (m0, acc0))
    lo4 = lax.broadcasted_iota(jnp.int32, (GROUP_Q * tq, LANES), 1) < HEAD_DIM
    denom = jnp.where(lo4, pltpu.roll(acc, HEAD_DIM, 1), 1.0)
    o = acc / denom
    heads = [o[h * tq:(h + 1) * tq] for h in range(GROUP_Q)]
    out01 = jnp.where(lo, heads[0], pltpu.roll(heads[1], HEAD_DIM, 1))
    out23 = jnp.where(lo, heads[2], pltpu.roll(heads[3], HEAD_DIM, 1))
    o_ref[0] = jnp.concatenate([out01, out23], axis=1).astype(o_ref.dtype)


def _flash(qa, ka, va):
    B, S, _ = qa.shape
    tq, tk = FLASH_TQ, FLASH_TK
    return pl.pallas_call(
        functools.partial(_flash_kernel, tq=tq, tk=tk, seq=S),
        grid=(B, A_KV_HEADS, S // tq),
        in_specs=[
            pl.BlockSpec((1, tq, GROUP_Q * HEAD_DIM), lambda b, g, i: (b, i, g)),
            pl.BlockSpec((1, S, LANES), lambda b, g, i: (b, 0, g)),
            pl.BlockSpec((1, S, LANES), lambda b, g, i: (b, 0, g)),
        ],
        out_specs=pl.BlockSpec((1, tq, GROUP_Q * HEAD_DIM), lambda b, g, i: (b, i, g)),
        out_shape=jax.ShapeDtypeStruct((B, S, A_WIDTH), bf16),
        compiler_params=_cparams(("parallel", "parallel", "arbitrary")),
    )(qa, ka, va)


def _dilated_kernel(q_ref, k_ref, v_ref, o_ref, lse_ref, *, length, tms):
    i = pl.program_id(2)
    lane = lax.broadcasted_iota(jnp.int32, (DIL_QBLOCK, LANES), 1)
    lo = lane < HEAD_DIM
    for sb in range(tms // DIL_QBLOCK):
        m0 = i * tms + sb * DIL_QBLOCK
        start = pl.multiple_of(jnp.clip(m0 - HALF_WINDOW, 0, length - KEY_SPAN), HALF_WINDOW)
        qpos = m0 + lax.broadcasted_iota(jnp.int32, (DIL_QBLOCK, KEY_SPAN), 0)
        kpos = start + lax.broadcasted_iota(jnp.int32, (DIL_QBLOCK, KEY_SPAN), 1)
        valid = jnp.abs(kpos - qpos) <= HALF_WINDOW
        r0 = sb * DIL_QBLOCK
        for c in range(B_WIDTH // LANES):
            cs = slice(c * LANES, (c + 1) * LANES)
            qp = q_ref[0, r0:r0 + DIL_QBLOCK, cs]
            kp = k_ref[0, pl.ds(start, KEY_SPAN), cs]
            vp = v_ref[0, pl.ds(start, KEY_SPAN), cs]
            zero = jnp.zeros_like(qp)
            outs, lses = [], []
            for half in range(2):
                qm = jnp.where(lo, qp, zero) if half == 0 else jnp.where(lo, zero, qp)
                s = lax.dot_general(qm, kp, (((1,), (1,)), ((), ())), preferred_element_type=f32)
                s = jnp.where(valid, s, NEG_BIG)
                m = jnp.max(s, axis=-1, keepdims=True)
                e = jnp.exp(s - m)
                l = jnp.sum(e, axis=-1, keepdims=True)
                outs.append(jnp.dot((e / l).astype(bf16), vp, preferred_element_type=f32))
                lses.append(m + jnp.log(l))
            o_ref[0, r0:r0 + DIL_QBLOCK, cs] = jnp.where(lo, outs[0], outs[1]).astype(o_ref.dtype)
            lse_ref[0, r0:r0 + DIL_QBLOCK, cs] = jnp.where(lo, lses[0], lses[1])


def _dilated(qb, kb, vb, dil):
    B, S, _ = qb.shape
    length = S // dil
    tms = min(length, 512)
    view = lambda a: a.reshape(B, length, dil * B_WIDTH)
    o, lse = pl.pallas_call(
        functools.partial(_dilated_kernel, length=length, tms=tms),
        grid=(B, dil, length // tms),
        in_specs=[
            pl.BlockSpec((1, tms, B_WIDTH), lambda b, r, i: (b, i, r)),
            pl.BlockSpec((1, length, B_WIDTH), lambda b, r, i: (b, 0, r)),
            pl.BlockSpec((1, length, B_WIDTH), lambda b, r, i: (b, 0, r)),
        ],
        out_specs=[
            pl.BlockSpec((1, tms, B_WIDTH), lambda b, r, i: (b, i, r)),
            pl.BlockSpec((1, tms, B_WIDTH), lambda b, r, i: (b, i, r)),
        ],
        out_shape=[
            jax.ShapeDtypeStruct((B, length, dil * B_WIDTH), bf16),
            jax.ShapeDtypeStruct((B, length, dil * B_WIDTH), f32),
        ],
        compiler_params=_cparams(("parallel", "parallel", "arbitrary")),
    )(view(qb), view(kb), view(vb))
    return o.reshape(B * S, B_WIDTH), lse.reshape(B * S, B_WIDTH)


def _first_index_of_max(cur, idx, n):
    mx = jnp.max(cur, axis=0, keepdims=True)
    return jnp.min(jnp.where(cur == mx, idx, n), axis=0, keepdims=True)


def _merge_route_kernel(x_ref, ya_ref, o1_ref, o2_ref, o3_ref, l1_ref, l2_ref, l3_ref, ga_ref, gb_ref,
                        wa_ref, wb_ref, wo_ref, nf_ref, rhi_ref, rlo_ref, bias_ref, tri_ref,
                        h_ref, u_ref, idx_ref, wt_ref, rank_ref, cnt_ref, run_ref):
    step = pl.program_id(0)

    @pl.when(step == 0)
    def _():
        run_ref[...] = jnp.zeros_like(run_ref)

    l1, l2, l3 = l1_ref[...], l2_ref[...], l3_ref[...]
    lm = jnp.maximum(jnp.maximum(l1, l2), l3)
    e1, e2, e3 = jnp.exp(l1 - lm), jnp.exp(l2 - lm), jnp.exp(l3 - lm)
    yb = (e1 * o1_ref[...].astype(f32) + e2 * o2_ref[...].astype(f32) + e3 * o3_ref[...].astype(f32)) / (e1 + e2 + e3)

    pa = jnp.dot(ya_ref[...], wa_ref[...], preferred_element_type=f32)
    pb = jnp.dot(yb.astype(bf16), wb_ref[...], preferred_element_type=f32)
    merged = ga_ref[...].astype(f32) * pa + gb_ref[...].astype(f32) * pb
    h = x_ref[...] + jnp.dot(merged.astype(bf16), wo_ref[...], preferred_element_type=f32)
    h_ref[...] = h

    r = lax.rsqrt(jnp.mean(h * h, axis=-1, keepdims=True) + EPS)
    uf = h * r * nf_ref[...]
    u_hi = uf.astype(bf16)
    for c, chunk in enumerate(_pack_rows(uf)):
        u_ref[c] = chunk
    u_lo = (uf - u_hi.astype(f32)).astype(bf16)
    logits = (jnp.dot(u_hi, rhi_ref[...], preferred_element_type=f32)
              + jnp.dot(u_lo, rhi_ref[...], preferred_element_type=f32)
              + jnp.dot(u_hi, rlo_ref[...], preferred_element_type=f32))
    scores = jax.nn.sigmoid(logits.T[:N_EXPERTS, :])
    sel = scores + bias_ref[...]
    tm = sel.shape[1]

    i8 = lax.broadcasted_iota(jnp.int32, (EXPERTS_PER_GROUP, tm), 0)
    neg_inf = jnp.float32(-jnp.inf)
    gscore = jnp.zeros((N_GROUPS, tm), f32)
    for g in range(N_GROUPS):
        blk = sel[g * EXPERTS_PER_GROUP:(g + 1) * EXPERTS_PER_GROUP, :]
        top1 = jnp.max(blk, axis=0, keepdims=True)
        first = _first_index_of_max(blk, i8, EXPERTS_PER_GROUP)
        top2 = jnp.max(jnp.where(i8 == first, neg_inf, blk), axis=0, keepdims=True)
        gscore = jnp.where(i8 == g, top1 + top2, gscore)
    gkeep = jnp.zeros((N_GROUPS, tm), jnp.bool_)
    cur = gscore
    for _ in range(TOPK_GROUPS):
        pick = i8 == _first_index_of_max(cur, i8, N_GROUPS)
        gkeep = jnp.logical_or(gkeep, pick)
        cur = jnp.where(pick, neg_inf, cur)
    cur = jnp.concatenate(
        [jnp.where(gkeep[g:g + 1, :], sel[g * EXPERTS_PER_GROUP:(g + 1) * EXPERTS_PER_GROUP, :], NEG_BIG)
         for g in range(N_GROUPS)], axis=0)

    ie = lax.broadcasted_iota(jnp.int32, (N_EXPERTS, tm), 0)
    ik = lax.broadcasted_iota(jnp.int32, (TOP_K, tm), 0)
    idx_out = jnp.zeros((TOP_K, tm), jnp.int32)
    w_out = jnp.zeros((TOP_K, tm), f32)
    chosen = jnp.zeros((N_EXPERTS, tm), jnp.bool_)
    for k in range(TOP_K):
        fi = _first_index_of_max(cur, ie, N_EXPERTS)
        pick = ie == fi
        wk = jnp.sum(jnp.where(pick, scores, 0.0), axis=0, keepdims=True)
        idx_out = jnp.where(ik == k, fi, idx_out)
        w_out = jnp.where(ik == k, wk, w_out)
        chosen = jnp.logical_or(chosen, pick)
        cur = jnp.where(pick, neg_inf, cur)
    w_out = w_out / jnp.sum(w_out, axis=0, keepdims=True) * ROUTE_SCALE
    for j in range(tm // SC_WINDOW):
        idx_ref[j] = idx_out[:, j * SC_WINDOW:(j + 1) * SC_WINDOW]
    wt_ref[...] = w_out

    sel01 = jnp.where(chosen, 1.0, 0.0)
    before = jnp.dot(sel01.astype(bf16), tri_ref[...], preferred_element_type=f32) + run_ref[:, 0:1]
    rank_out = jnp.zeros((TOP_K, tm), f32)
    for k in range(TOP_K):
        rk = jnp.sum(jnp.where(ie == idx_out[k:k + 1, :], before, 0.0), axis=0, keepdims=True)
        rank_out = jnp.where(ik == k, rk, rank_out)
    rank_i = rank_out.astype(jnp.int32)
    for j in range(tm // SC_WINDOW):
        rank_ref[j] = rank_i[:, j * SC_WINDOW:(j + 1) * SC_WINDOW]
    run_ref[...] = run_ref[...] + jnp.sum(sel01, axis=1, keepdims=True)
    cnt_ref[...] = run_ref[...].astype(jnp.int32)


def _merge_route(x2, ya, outs, lses, sga, sgb, prm):
    T = x2.shape[0]
    tm = TOKEN_TILE
    row = lambda i: (i, 0)
    col = lambda i: (0, i)
    const = lambda i: (0, 0)
    tok = lambda w: pl.BlockSpec((tm, w), row)
    full = lambda a: pl.BlockSpec(a.shape, const)
    win = pl.BlockSpec((tm // SC_WINDOW, TOP_K, SC_WINDOW), lambda i: (i, 0, 0))
    return pl.pallas_call(
        _merge_route_kernel,
        grid=(T // tm,),
        in_specs=[tok(D_MODEL), tok(A_WIDTH), tok(B_WIDTH), tok(B_WIDTH), tok(B_WIDTH),
                  tok(B_WIDTH), tok(B_WIDTH), tok(B_WIDTH), tok(D_MODEL), tok(D_MODEL),
                  full(prm["w_branch_a"]), full(prm["w_branch_b"]), full(prm["w_out"]), full(prm["norm_ffn"]),
                  full(prm["router_hi"]), full(prm["router_lo"]), full(prm["router_bias"]), full(prm["tri"])],
        out_specs=[tok(D_MODEL), pl.BlockSpec((ROW_CHUNKS, tm, LANES), lambda i: (0, i, 0)),
                   win, pl.BlockSpec((TOP_K, tm), col), win,
                   pl.BlockSpec((N_EXPERTS, LANES), const)],
        out_shape=[jax.ShapeDtypeStruct((T, D_MODEL), f32), jax.ShapeDtypeStruct((ROW_CHUNKS, T, LANES), jnp.uint32),
                   jax.ShapeDtypeStruct((T // SC_WINDOW, TOP_K, SC_WINDOW), jnp.int32),
                   jax.ShapeDtypeStruct((TOP_K, T), f32),
                   jax.ShapeDtypeStruct((T // SC_WINDOW, TOP_K, SC_WINDOW), jnp.int32),
                   jax.ShapeDtypeStruct((N_EXPERTS, LANES), jnp.int32)],
        scratch_shapes=[pltpu.VMEM((N_EXPERTS, LANES), f32)],
        compiler_params=_cparams(("arbitrary",)),
    )(x2, ya, *outs, *lses, sga, sgb,
      prm["w_branch_a"], prm["w_branch_b"], prm["w_out"], prm["norm_ffn"],
      prm["router_hi"], prm["router_lo"], prm["router_bias"], prm["tri"])


def _experts_kernel(be_ref, nu_ref, nv_ref, x_ref, wgu_ref, wd_ref, y_ref):
    del be_ref
    i = pl.program_id(0)

    @pl.when(i < nu_ref[0])
    def _():
        lo, hi = _unpack_rows([x_ref[c] for c in range(ROW_CHUNKS)])
        live = lax.broadcasted_iota(jnp.int32, lo.shape, 0) < nv_ref[i]
        x = jnp.concatenate([jnp.where(live, lo, 0.0), jnp.where(live, hi, 0.0)], axis=1).astype(bf16)
        gu = jnp.dot(x, wgu_ref[0], preferred_element_type=f32)
        g, u = gu[:, :D_EXPERT], gu[:, D_EXPERT:]
        a = g * jax.nn.sigmoid(g) * u
        y = jnp.dot(a.astype(bf16), wd_ref[0], preferred_element_type=f32)
        for c, chunk in enumerate(_pack_rows(y)):
            y_ref[c] = chunk

    @pl.when(i >= nu_ref[0])
    def _():
        y_ref[...] = jnp.zeros_like(y_ref)


def _experts(xs, blk_e, n_used, n_valid, prm):
    n_rows = xs.shape[1]
    bm = MOE_ROWS
    rows = lambda i, be, nu, nv: (0, jnp.minimum(i, nu[0] - 1), 0)
    return pl.pallas_call(
        _experts_kernel,
        grid_spec=pltpu.PrefetchScalarGridSpec(
            num_scalar_prefetch=3,
            grid=(n_rows // bm,),
            in_specs=[
                pl.BlockSpec((ROW_CHUNKS, bm, LANES), rows),
                pl.BlockSpec((1, D_MODEL, 2 * D_EXPERT), lambda i, be, nu, nv: (be[i], 0, 0)),
                pl.BlockSpec((1, D_EXPERT, D_MODEL), lambda i, be, nu, nv: (be[i], 0, 0)),
            ],
            out_specs=pl.BlockSpec((ROW_CHUNKS, bm, LANES), lambda i, be, nu, nv: (0, i, 0)),
        ),
        out_shape=jax.ShapeDtypeStruct((ROW_CHUNKS, n_rows, LANES), jnp.uint32),
        compiler_params=_cparams(("arbitrary",)),
    )(blk_e, n_used, n_valid, xs, prm["expert_gu"], prm["expert_down"])


def _sc_mesh():
    return plsc.VectorSubcoreMesh(core_axis_name="c", subcore_axis_name="s")


def _sc_dispatch(u2p, dest, n_rows):
    T = u2p.shape[1]
    per_worker = T // SC_WORKERS
    n_it = per_worker // SC_WINDOW

    @functools.partial(
        pl.kernel, mesh=_sc_mesh(),
        out_type=jax.ShapeDtypeStruct((ROW_CHUNKS, n_rows, LANES), jnp.uint32),
        scratch_types=[pltpu.VMEM((TOP_K, SC_WINDOW), jnp.int32),
                       pltpu.VMEM((ROW_CHUNKS, SC_WINDOW, LANES), jnp.uint32),
                       pltpu.SemaphoreType.DMA, pltpu.SemaphoreType.DMA],
    )
    def k(u_hbm, dest_hbm, xs_hbm, idx_v, rows_v, sem_in, sem_out):
        wid = lax.axis_index("s") * SC_CORES + lax.axis_index("c")

        @pl.loop(0, n_it)
        def _(it):
            t0 = wid * per_worker + it * SC_WINDOW
            pltpu.sync_copy(dest_hbm.at[wid * n_it + it], idx_v)
            loads = [pltpu.async_copy(u_hbm.at[c].at[pl.ds(t0, SC_WINDOW)], rows_v.at[c], sem_in)
                     for c in range(ROW_CHUNKS)]
            for cp in loads:
                cp.wait()
            stores = [pltpu.async_copy(rows_v.at[c], xs_hbm.at[c].at[idx_v.at[kk]], sem_out)
                      for kk in range(TOP_K) for c in range(ROW_CHUNKS)]
            for cp in stores:
                cp.wait()

    return k(u2p, dest)


def _sc_combine(ys, dest):
    n_windows = dest.shape[0]
    T = n_windows * SC_WINDOW
    per_worker = T // SC_WORKERS
    n_it = per_worker // SC_WINDOW

    @functools.partial(
        pl.kernel, mesh=_sc_mesh(),
        out_type=jax.ShapeDtypeStruct((TOP_K, ROW_CHUNKS, T, LANES), jnp.uint32),
        scratch_types=[pltpu.VMEM((TOP_K, SC_WINDOW), jnp.int32),
                       pltpu.VMEM((SC_SLABS, SC_WINDOW, LANES), jnp.uint32),
                       pltpu.SemaphoreType.DMA, pltpu.SemaphoreType.DMA],
    )
    def k(ys_hbm, dest_hbm, yg_hbm, idx_v, rows_v, sem_in, sem_out):
        wid = lax.axis_index("s") * SC_CORES + lax.axis_index("c")

        @pl.loop(0, n_it)
        def _(it):
            t0 = wid * per_worker + it * SC_WINDOW
            pltpu.sync_copy(dest_hbm.at[wid * n_it + it], idx_v)
            for c in range(ROW_CHUNKS):
                for k0 in range(0, TOP_K, SC_SLABS):
                    loads = [pltpu.async_copy(ys_hbm.at[c].at[idx_v.at[k0 + j]], rows_v.at[j], sem_in)
                             for j in range(SC_SLABS)]
                    for cp in loads:
                        cp.wait()
                    stores = [pltpu.async_copy(rows_v.at[j], yg_hbm.at[k0 + j].at[c].at[pl.ds(t0, SC_WINDOW)], sem_out)
                              for j in range(SC_SLABS)]
                    for cp in stores:
                        cp.wait()

    return k(ys, dest)


def _final_kernel(h_ref, u_ref, yg_ref, wt_ref, p_ref, sgu_ref, sd_ref, pg_ref, pp_ref, o_ref):
    wt = wt_ref[...]
    r_lo = r_hi = None
    for k in range(TOP_K):
        lo, hi = _unpack_rows([yg_ref[k, c] for c in range(ROW_CHUNKS)])
        wk = wt[:, k:k + 1]
        r_lo = lo * wk if k == 0 else r_lo + lo * wk
        r_hi = hi * wk if k == 0 else r_hi + hi * wk
    routed = jnp.concatenate([r_lo, r_hi], axis=1)
    u_lo, u_hi = _unpack_rows([u_ref[c] for c in range(ROW_CHUNKS)])
    un = jnp.concatenate([u_lo, u_hi], axis=1).astype(bf16)
    gu = jnp.dot(un, sgu_ref[...], preferred_element_type=f32)
    g, u = gu[:, :D_SHARED], gu[:, D_SHARED:]
    shared = jnp.dot((g * jax.nn.sigmoid(g) * u).astype(bf16), sd_ref[...], preferred_element_type=f32)
    h = h_ref[...] + (routed + shared)
    gate = jax.nn.sigmoid(jnp.dot(h.astype(bf16), pg_ref[...], preferred_element_type=f32))
    emb = jnp.dot(p_ref[...].astype(bf16), pp_ref[...], preferred_element_type=f32)
    o_ref[...] = h + gate * emb


def _final(h1, u2p, yg, wts_t, p2, prm):
    T = h1.shape[0]
    tm = TOKEN_TILE
    row = lambda i: (i, 0)
    const = lambda i: (0, 0)
    full = lambda a: pl.BlockSpec(a.shape, const)
    return pl.pallas_call(
        _final_kernel,
        grid=(T // tm,),
        in_specs=[pl.BlockSpec((tm, D_MODEL), row),
                  pl.BlockSpec((ROW_CHUNKS, tm, LANES), lambda i: (0, i, 0)),
                  pl.BlockSpec((TOP_K, ROW_CHUNKS, tm, LANES), lambda i: (0, 0, i, 0)),
                  pl.BlockSpec((tm, TOP_K), row),
                  pl.BlockSpec((tm, PLE_DIM), row),
                  full(prm["shared_gu"]), full(prm["shared_down"]), full(prm["ple_gate"]), full(prm["ple_proj"])],
        out_specs=pl.BlockSpec((tm, D_MODEL), row),
        out_shape=jax.ShapeDtypeStruct((T, D_MODEL), f32),
        compiler_params=_cparams(("parallel",)),
    )(h1, u2p, yg, wts_t, p2, prm["shared_gu"], prm["shared_down"], prm["ple_gate"], prm["ple_proj"])


def _deinterleave(n_heads):
    base = jnp.concatenate([jnp.arange(0, HEAD_DIM, 2), jnp.arange(1, HEAD_DIM, 2)])
    return (jnp.arange(n_heads)[:, None] * HEAD_DIM + base[None, :]).reshape(-1)


def _rope_tables(ang):
    c, s = jnp.cos(ang), jnp.sin(ang)
    return jnp.tile(jnp.concatenate([c, c], axis=-1), (1, 2)), jnp.tile(jnp.concatenate([-s, s], axis=-1), (1, 2))


def _axial_angles(S):
    rows = S // GRID_W
    row = jnp.repeat(jnp.arange(rows, dtype=f32), GRID_W)
    col = jnp.tile(jnp.arange(GRID_W, dtype=f32), rows)
    half = HEAD_DIM // 2
    inv = ROPE_THETA ** (-jnp.arange(0, half, 2, dtype=f32) / half)
    return jnp.concatenate([row[:, None] * inv, col[:, None] * inv], axis=-1)


def _linear_angles(S):
    t = jnp.arange(S, dtype=f32)
    inv = ROPE_THETA ** (-jnp.arange(0, HEAD_DIM, 2, dtype=f32) / HEAD_DIM)
    return t[:, None] * inv


def _prepare(seq_lens, norm_mix, w_in, q_norm_a, k_norm_a, q_norm_b, k_norm_b, w_branch_a, w_branch_b, w_out,
             norm_ffn, router_w, router_bias, expert_gate, expert_up, expert_down, shared_gate, shared_up,
             shared_down, ple_proj, ple_gate):
    cuts = [A_WIDTH, A_WIDTH + A_KV_WIDTH, A_WIDTH + 2 * A_KV_WIDTH,
            A_WIDTH + 2 * A_KV_WIDTH + B_WIDTH, A_WIDTH + 2 * A_KV_WIDTH + 2 * B_WIDTH,
            A_WIDTH + 2 * A_KV_WIDTH + 3 * B_WIDTH, A_WIDTH + 2 * A_KV_WIDTH + 3 * B_WIDTH + D_MODEL]
    wqa, wka, wva, wqb, wkb, wvb, wga, wgb = jnp.split(w_in, cuts, axis=-1)
    wqa = wqa[:, _deinterleave(A_Q_HEADS)]
    wka = wka[:, _deinterleave(A_KV_HEADS)]
    wqb = wqb[:, _deinterleave(B_HEADS)]
    wkb = wkb[:, _deinterleave(B_HEADS)]
    k0, k1 = wka[:, :HEAD_DIM], wka[:, HEAD_DIM:]
    v0, v1 = wva[:, :HEAD_DIM], wva[:, HEAD_DIM:]
    vz = jnp.zeros_like(v0)
    w_all = jnp.concatenate([wqa, k0, k0, k1, k1, v0, vz, v1, vz, wqb, wkb, wvb, wga, wgb], axis=-1).astype(bf16)
    perm = _deinterleave(1)
    scale = HEAD_DIM ** -0.5
    tile = lambda g, n: jnp.tile(g[perm], n)[None, :]
    router_pad = jnp.pad(router_w, ((0, 0), (0, LANES - N_EXPERTS)))
    router_hi = router_pad.astype(bf16)
    tm = TOKEN_TILE
    prm = {
        "norm_mix": norm_mix[None, :], "w_in": w_all,
        "gqa": tile(q_norm_a, A_Q_HEADS) * scale, "gka": tile(k_norm_a, KA_COLS // HEAD_DIM),
        "gqb": tile(q_norm_b, B_HEADS) * scale, "gkb": tile(k_norm_b, B_HEADS),
        "bd": jnp.kron(jnp.eye(LANES // HEAD_DIM, dtype=f32), jnp.ones((HEAD_DIM, HEAD_DIM), f32)).astype(bf16),
        "w_branch_a": w_branch_a.astype(bf16), "w_branch_b": w_branch_b.astype(bf16), "w_out": w_out.astype(bf16),
        "norm_ffn": norm_ffn[None, :],
        "router_hi": router_hi, "router_lo": (router_pad - router_hi.astype(f32)).astype(bf16),
        "router_bias": router_bias[:, None],
        "tri": (jnp.arange(tm)[:, None] < jnp.arange(tm)[None, :]).astype(bf16),
        "expert_gu": jnp.concatenate([expert_gate, expert_up], axis=-1).astype(bf16),
        "expert_down": expert_down.astype(bf16),
        "shared_gu": jnp.concatenate([shared_gate, shared_up], axis=-1).astype(bf16),
        "shared_down": shared_down.astype(bf16),
        "ple_gate": ple_gate.astype(bf16), "ple_proj": ple_proj.astype(bf16),
        "cos_a": {}, "sin_a": {}, "cos_b": {}, "sin_b": {},
    }
    for S in seq_lens:
        prm["cos_a"][S], prm["sin_a"][S] = _rope_tables(_axial_angles(S))
        prm["cos_b"][S], prm["sin_b"][S] = _rope_tables(_linear_angles(S))
    return prm


def _dispatch_plan(idx, rank, counts, T):
    bm = MOE_ROWS
    n_rows = (T * TOP_K + N_EXPERTS * bm) // bm * bm
    n_blocks = n_rows // bm
    padded = (counts + bm - 1) // bm * bm
    pad_end = jnp.cumsum(padded)
    pad_start = pad_end - padded
    dest = pad_start[idx] + rank
    blk_start = jnp.arange(n_blocks, dtype=jnp.int32) * bm
    blk_e = jnp.minimum(jnp.searchsorted(pad_end, blk_start, side="right"), N_EXPERTS - 1).astype(jnp.int32)
    n_used = (pad_end[-1:] // bm).astype(jnp.int32)
    n_valid = jnp.clip(counts[blk_e] - (blk_start - pad_start[blk_e]), 0, bm).astype(jnp.int32)
    return dest, n_rows, blk_e, n_used, n_valid


def _layer(x, p, prm):
    B, S, _ = x.shape
    T = B * S
    x2 = x.reshape(T, D_MODEL)
    qa, ka, va, qb, kb, vb, sga, sgb = _in_proj(x2, S, prm)
    shape3 = lambda a: a.reshape(B, S, a.shape[-1])
    ya = _flash(shape3(qa), shape3(ka), shape3(va)).reshape(T, A_WIDTH)
    outs, lses = [], []
    for _, dil in WINDOW_DILATIONS:
        o_d, lse_d = _dilated(shape3(qb), shape3(kb), shape3(vb), dil)
        outs.append(o_d)
        lses.append(lse_d)
    h1, u2p, idx, wts, rank, cnt = _merge_route(x2, ya, outs, lses, sga, sgb, prm)
    dest, n_rows, blk_e, n_used, n_valid = _dispatch_plan(idx, rank, cnt[:, 0], T)
    xs = _sc_dispatch(u2p, dest, n_rows)
    ys = _experts(xs, blk_e, n_used, n_valid, prm)
    yg = _sc_combine(ys, dest)
    out = _final(h1, u2p, yg, wts.T, p.reshape(T, PLE_DIM), prm)
    return out.reshape(B, S, D_MODEL)


def kernel(x_prompt, x_sample, p_prompt, p_sample, norm_mix, w_in, q_norm_a, k_norm_a, q_norm_b, k_norm_b,
           w_branch_a, w_branch_b, w_out, norm_ffn, router_w, router_bias, expert_gate, expert_up, expert_down,
           shared_gate, shared_up, shared_down, ple_proj, ple_gate):
    weights = (norm_mix, w_in, q_norm_a, k_norm_a, q_norm_b, k_norm_b, w_branch_a, w_branch_b, w_out, norm_ffn,
               router_w, router_bias, expert_gate, expert_up, expert_down, shared_gate, shared_up, shared_down,
               ple_proj, ple_gate)
    depth = norm_mix.shape[0]
    seq_lens = sorted({x_prompt.shape[1], x_sample.shape[1]})
    hp, hs = x_prompt, x_sample
    for i in range(depth):
        prm = _prepare(seq_lens, *[w[i] for w in weights])
        hp = _layer(hp, p_prompt[i], prm)
        hs = _layer(hs, p_sample[i], prm)
    return hp, hs
```

```python
import functools

import jax
import jax.numpy as jnp
from jax import lax
from jax.experimental import pallas as pl
from jax.experimental.pallas import tpu as pltpu
from jax.experimental.pallas import tpu_sc as plsc

D_MODEL = 1024
HEAD_DIM = 64
A_Q_HEADS = 8
A_KV_HEADS = 2
B_HEADS = 6
GRID_W = 64
ROPE_THETA = 10000.0
WINDOW_DILATIONS = ((128, 1), (512, 4), (2048, 16))
N_EXPERTS = 64
TOP_K = 8
N_GROUPS = 8
TOPK_GROUPS = 4
D_EXPERT = 256
D_SHARED = 256
ROUTE_SCALE = 2.5
PLE_DIM = 256
NEG_BIG = -1e30
EPS = 1e-6
LOG2_E = 1.4426950408889634

A_WIDTH = A_Q_HEADS * HEAD_DIM
A_KV_WIDTH = A_KV_HEADS * HEAD_DIM
B_WIDTH = B_HEADS * HEAD_DIM
GROUP_Q = A_Q_HEADS // A_KV_HEADS
EXPERTS_PER_GROUP = N_EXPERTS // N_GROUPS

LANES = 128
HALF_WINDOW = 64
KEY_SPAN = 256

KA_COLS = 2 * LANES
VA_COLS = 2 * LANES
OFF_QA = 0
OFF_KA = OFF_QA + A_WIDTH
OFF_VA = OFF_KA + KA_COLS
OFF_QB = OFF_VA + VA_COLS
OFF_KB = OFF_QB + B_WIDTH
OFF_VB = OFF_KB + B_WIDTH
OFF_GA = OFF_VB + B_WIDTH
OFF_GB = OFF_GA + D_MODEL
PROJ_COLS = OFF_GB + D_MODEL

TOKEN_TILE = 512
FLASH_TQ = 256
FLASH_TK = 1024
DIL_QBLOCK = 128
MOE_ROWS = 256

VMEM_LIMIT = 52 * 1024 * 1024

HALF_MODEL = D_MODEL // 2
ROW_CHUNKS = HALF_MODEL // LANES
SC_CORES = 2
SC_SUBCORES = 16
SC_WORKERS = SC_CORES * SC_SUBCORES
SC_WINDOW = 128
SC_SLABS = 4

bf16 = jnp.bfloat16
f32 = jnp.float32


def _cparams(sem):
    return pltpu.CompilerParams(dimension_semantics=sem, vmem_limit_bytes=VMEM_LIMIT)


def _pack_rows(v):
    lo = lax.bitcast_convert_type(v[:, :HALF_MODEL].astype(bf16).astype(f32), jnp.uint32) >> 16
    hi = lax.bitcast_convert_type(v[:, HALF_MODEL:].astype(bf16).astype(f32), jnp.uint32) & jnp.uint32(0xFFFF0000)
    w = lo | hi
    return [w[:, c * LANES:(c + 1) * LANES] for c in range(ROW_CHUNKS)]


def _unpack_rows(chunks):
    w = jnp.concatenate(chunks, axis=1)
    lo = lax.bitcast_convert_type(w << 16, f32)
    hi = lax.bitcast_convert_type(w & jnp.uint32(0xFFFF0000), f32)
    return lo, hi


def _in_proj_kernel(x_ref, nm_ref, w_ref, gqa_ref, gka_ref, gqb_ref, gkb_ref,
                    ca_ref, sa_ref, cb_ref, sb_ref, bd_ref,
                    qa_ref, ka_ref, va_ref, qb_ref, kb_ref, vb_ref, ga_ref, gb_ref):
    x = x_ref[...]
    r = lax.rsqrt(jnp.mean(x * x, axis=-1, keepdims=True) + EPS)
    u = (x * r * nm_ref[...]).astype(bf16)
    rows = x.shape[0]
    lane = lax.broadcasted_iota(jnp.int32, (rows, LANES), 1)
    first_half = (lane % HEAD_DIM) < (HEAD_DIM // 2)

    def proj(off, width):
        return jnp.dot(u, w_ref[:, off:off + width], preferred_element_type=f32)

    def norm_rope(off, width, g_ref, c_ref, s_ref, out_ref):
        z = proj(off, width)
        c = c_ref[...]
        s = s_ref[...]
        for c0 in range(0, width, LANES):
            zc = z[:, c0:c0 + LANES]
            ms = jnp.dot((zc * zc).astype(bf16), bd_ref[...], preferred_element_type=f32) * (1.0 / HEAD_DIM)
            zn = zc * lax.rsqrt(ms + EPS) * g_ref[:, c0:c0 + LANES]
            sw = jnp.where(first_half, pltpu.roll(zn, LANES - HEAD_DIM // 2, 1), pltpu.roll(zn, HEAD_DIM // 2, 1))
            out_ref[:, c0:c0 + LANES] = (zn * c + sw * s).astype(out_ref.dtype)

    norm_rope(OFF_QA, A_WIDTH, gqa_ref, ca_ref, sa_ref, qa_ref)
    norm_rope(OFF_KA, KA_COLS, gka_ref, ca_ref, sa_ref, ka_ref)
    va = proj(OFF_VA, VA_COLS)
    va_ref[...] = jnp.concatenate(
        [jnp.where(lane < HEAD_DIM, va[:, c0:c0 + LANES], 1.0) for c0 in range(0, VA_COLS, LANES)],
        axis=1).astype(va_ref.dtype)
    norm_rope(OFF_QB, B_WIDTH, gqb_ref, cb_ref, sb_ref, qb_ref)
    norm_rope(OFF_KB, B_WIDTH, gkb_ref, cb_ref, sb_ref, kb_ref)
    vb_ref[...] = proj(OFF_VB, B_WIDTH).astype(vb_ref.dtype)
    ga_ref[...] = jax.nn.sigmoid(proj(OFF_GA, D_MODEL)).astype(ga_ref.dtype)
    gb_ref[...] = jax.nn.sigmoid(proj(OFF_GB, D_MODEL)).astype(gb_ref.dtype)


def _in_proj(x2, S, prm):
    T = x2.shape[0]
    tm = TOKEN_TILE
    n_pos = S // tm
    row = lambda i: (i, 0)
    pos = lambda i: (i % n_pos, 0)
    const = lambda i: (0, 0)
    widths = (A_WIDTH, KA_COLS, VA_COLS, B_WIDTH, B_WIDTH, B_WIDTH, D_MODEL, D_MODEL)
    return pl.pallas_call(
        _in_proj_kernel,
        grid=(T // tm,),
        in_specs=[
            pl.BlockSpec((tm, D_MODEL), row),
            pl.BlockSpec((1, D_MODEL), const),
            pl.BlockSpec((D_MODEL, PROJ_COLS), const),
            pl.BlockSpec((1, A_WIDTH), const),
            pl.BlockSpec((1, KA_COLS), const),
            pl.BlockSpec((1, B_WIDTH), const),
            pl.BlockSpec((1, B_WIDTH), const),
            pl.BlockSpec((tm, LANES), pos),
            pl.BlockSpec((tm, LANES), pos),
            pl.BlockSpec((tm, LANES), pos),
            pl.BlockSpec((tm, LANES), pos),
            pl.BlockSpec((LANES, LANES), const),
        ],
        out_specs=[pl.BlockSpec((tm, w), row) for w in widths],
        out_shape=[jax.ShapeDtypeStruct((T, w), bf16) for w in widths],
        compiler_params=_cparams(("parallel",)),
    )(x2, prm["norm_mix"], prm["w_in"], prm["gqa"], prm["gka"], prm["gqb"], prm["gkb"],
      prm["cos_a"][S], prm["sin_a"][S], prm["cos_b"][S], prm["sin_b"][S], prm["bd"])


def _flash_kernel(q_ref, k_ref, v_ref, o_ref, s_ref, *, tq, tk, seq):
    lane = lax.broadcasted_iota(jnp.int32, (tq, LANES), 1)
    lo = lane < HEAD_DIM
    q = q_ref[0]
    q01, q23 = q[:, :LANES], q[:, LANES:]
    zero = jnp.zeros_like(q01)
    qs = [jnp.where(lo, q01, zero), jnp.where(lo, zero, q01), jnp.where(lo, q23, zero), jnp.where(lo, zero, q23)]
    n_chunks = seq // tk

    def keys(chunk):
        return k_ref[0, pl.ds(pl.multiple_of(chunk * tk, tk), tk), :]

    def score(h, kk):
        return lax.dot_general(qs[h], kk, (((1,), (1,)), ((), ())), preferred_element_type=f32)

    def attend(chunk, slot, ms, accs, next_chunk):
        vv = v_ref[0, pl.ds(pl.multiple_of(chunk * tk, tk), tk), :]
        kk = None if next_chunk is None else keys(next_chunk)
        new_ms, new_accs = [], []
        for h in range(GROUP_Q):
            if kk is not None:
                s_ref[1 - slot, h] = score(h, kk)
            s = s_ref[slot, h]
            m_new = jnp.maximum(ms[h], jnp.max(s, axis=-1, keepdims=True))
            p = jnp.exp2((s - m_new).astype(bf16))
            alpha = jnp.exp2(ms[h] - m_new)
            new_accs.append(accs[h] * alpha + jnp.dot(p, vv, preferred_element_type=f32))
            new_ms.append(m_new)
        return tuple(new_ms), tuple(new_accs)

    kk0 = keys(0)
    for h in range(GROUP_Q):
        s_ref[0, h] = score(h, kk0)

    def body(jj, carry):
        ms, accs = attend(2 * jj, 0, *carry, 2 * jj + 1)
        return attend(2 * jj + 1, 1, ms, accs, 2 * jj + 2)

    m0 = tuple(jnp.full((tq, 1), NEG_BIG, f32) for _ in range(GROUP_Q))
    acc0 = tuple(jnp.zeros((tq, LANES), f32) for _ in range(GROUP_Q))
    ms, accs = lax.fori_loop(0, n_chunks // 2 - 1, body, (m0, acc0))
    ms, accs = attend(n_chunks - 2, 0, ms, accs, n_chunks - 1)
    _, accs = attend(n_chunks - 1, 1, ms, accs, None)
    heads = [acc / jnp.where(lo, pltpu.roll(acc, HEAD_DIM, 1), 1.0) for acc in accs]
    out01 = jnp.where(lo, heads[0], pltpu.roll(heads[1], HEAD_DIM, 1))
    out23 = jnp.where(lo, heads[2], pltpu.roll(heads[3], HEAD_DIM, 1))
    o_ref[0] = jnp.concatenate([out01, out23], axis=1).astype(o_ref.dtype)


def _flash(qa, ka, va):
    B, S, _ = qa.shape
    tq, tk = FLASH_TQ, FLASH_TK
    return pl.pallas_call(
        functools.partial(_flash_kernel, tq=tq, tk=tk, seq=S),
        grid=(B, A_KV_HEADS, S // tq),
        in_specs=[
            pl.BlockSpec((1, tq, GROUP_Q * HEAD_DIM), lambda b, g, i: (b, i, g)),
            pl.BlockSpec((1, S, LANES), lambda b, g, i: (b, 0, g)),
            pl.BlockSpec((1, S, LANES), lambda b, g, i: (b, 0, g)),
        ],
        out_specs=pl.BlockSpec((1, tq, GROUP_Q * HEAD_DIM), lambda b, g, i: (b, i, g)),
        out_shape=jax.ShapeDtypeStruct((B, S, A_WIDTH), bf16),
        scratch_shapes=[pltpu.VMEM((2, GROUP_Q, tq, tk), f32)],
        compiler_params=_cparams(("parallel", "parallel", "arbitrary")),
    )(qa, ka, va)


def _dilated_kernel(q_ref, k_ref, v_ref, o_ref, lse_ref, *, length, tms):
    i = pl.program_id(2)
    lane = lax.broadcasted_iota(jnp.int32, (DIL_QBLOCK, LANES), 1)
    lo = lane < HEAD_DIM
    for sb in range(tms // DIL_QBLOCK):
        m0 = i * tms + sb * DIL_QBLOCK
        start = pl.multiple_of(jnp.clip(m0 - HALF_WINDOW, 0, length - KEY_SPAN), HALF_WINDOW)
        qpos = m0 + lax.broadcasted_iota(jnp.int32, (DIL_QBLOCK, KEY_SPAN), 0)
        kpos = start + lax.broadcasted_iota(jnp.int32, (DIL_QBLOCK, KEY_SPAN), 1)
        valid = jnp.abs(kpos - qpos) <= HALF_WINDOW
        r0 = sb * DIL_QBLOCK
        for c in range(B_WIDTH // LANES):
            cs = slice(c * LANES, (c + 1) * LANES)
            qp = q_ref[0, r0:r0 + DIL_QBLOCK, cs]
            kp = k_ref[0, pl.ds(start, KEY_SPAN), cs]
            vp = v_ref[0, pl.ds(start, KEY_SPAN), cs]
            zero = jnp.zeros_like(qp)
            outs, lses = [], []
            for half in range(2):
                qm = jnp.where(lo, qp, zero) if half == 0 else jnp.where(lo, zero, qp)
                s = lax.dot_general(qm, kp, (((1,), (1,)), ((), ())), preferred_element_type=f32)
                s = jnp.where(valid, s, NEG_BIG)
                m = jnp.max(s, axis=-1, keepdims=True)
                e = jnp.exp(s - m)
                l = jnp.sum(e, axis=-1, keepdims=True)
                outs.append(jnp.dot((e / l).astype(bf16), vp, preferred_element_type=f32))
                lses.append(m + jnp.log(l))
            o_ref[0, r0:r0 + DIL_QBLOCK, cs] = jnp.where(lo, outs[0], outs[1]).astype(o_ref.dtype)
            lse_ref[0, r0:r0 + DIL_QBLOCK, cs] = jnp.where(lo, lses[0], lses[1])


def _dilated(qb, kb, vb, dil):
    B, S, _ = qb.shape
    length = S // dil
    tms = min(length, 512)
    view = lambda a: a.reshape(B, length, dil * B_WIDTH)
    o, lse = pl.pallas_call(
        functools.partial(_dilated_kernel, length=length, tms=tms),
        grid=(B, dil, length // tms),
        in_specs=[
            pl.BlockSpec((1, tms, B_WIDTH), lambda b, r, i: (b, i, r)),
            pl.BlockSpec((1, length, B_WIDTH), lambda b, r, i: (b, 0, r)),
            pl.BlockSpec((1, length, B_WIDTH), lambda b, r, i: (b, 0, r)),
        ],
        out_specs=[
            pl.BlockSpec((1, tms, B_WIDTH), lambda b, r, i: (b, i, r)),
            pl.BlockSpec((1, tms, B_WIDTH), lambda b, r, i: (b, i, r)),
        ],
        out_shape=[
            jax.ShapeDtypeStruct((B, length, dil * B_WIDTH), bf16),
            jax.ShapeDtypeStruct((B, length, dil * B_WIDTH), f32),
        ],
        compiler_params=_cparams(("parallel", "parallel", "arbitrary")),
    )(view(qb), view(kb), view(vb))
    return o.reshape(B * S, B_WIDTH), lse.reshape(B * S, B_WIDTH)


def _first_index_of_max(cur, idx, n):
    mx = jnp.max(cur, axis=0, keepdims=True)
    return jnp.min(jnp.where(cur == mx, idx, n), axis=0, keepdims=True)


def _merge_route_kernel(x_ref, ya_ref, o1_ref, o2_ref, o3_ref, l1_ref, l2_ref, l3_ref, ga_ref, gb_ref,
                        wa_ref, wb_ref, wo_ref, nf_ref, rhi_ref, rlo_ref, bias_ref, tri_ref,
                        h_ref, u_ref, idx_ref, wt_ref, rank_ref, cnt_ref, run_ref):
    step = pl.program_id(0)

    @pl.when(step == 0)
    def _():
        run_ref[...] = jnp.zeros_like(run_ref)

    l1, l2, l3 = l1_ref[...], l2_ref[...], l3_ref[...]
    lm = jnp.maximum(jnp.maximum(l1, l2), l3)
    e1, e2, e3 = jnp.exp(l1 - lm), jnp.exp(l2 - lm), jnp.exp(l3 - lm)
    yb = (e1 * o1_ref[...].astype(f32) + e2 * o2_ref[...].astype(f32) + e3 * o3_ref[...].astype(f32)) / (e1 + e2 + e3)

    pa = jnp.dot(ya_ref[...], wa_ref[...], preferred_element_type=f32)
    pb = jnp.dot(yb.astype(bf16), wb_ref[...], preferred_element_type=f32)
    merged = ga_ref[...].astype(f32) * pa + gb_ref[...].astype(f32) * pb
    h = x_ref[...] + jnp.dot(merged.astype(bf16), wo_ref[...], preferred_element_type=f32)
    h_ref[...] = h

    r = lax.rsqrt(jnp.mean(h * h, axis=-1, keepdims=True) + EPS)
    uf = h * r * nf_ref[...]
    u_hi = uf.astype(bf16)
    for c, chunk in enumerate(_pack_rows(uf)):
        u_ref[c] = chunk
    u_lo = (uf - u_hi.astype(f32)).astype(bf16)
    logits = (jnp.dot(u_hi, rhi_ref[...], preferred_element_type=f32)
              + jnp.dot(u_lo, rhi_ref[...], preferred_element_type=f32)
              + jnp.dot(u_hi, rlo_ref[...], preferred_element_type=f32))
    scores = jax.nn.sigmoid(logits.T[:N_EXPERTS, :])
    sel = scores + bias_ref[...]
    tm = sel.shape[1]

    i8 = lax.broadcasted_iota(jnp.int32, (EXPERTS_PER_GROUP, tm), 0)
    neg_inf = jnp.float32(-jnp.inf)
    gscore = jnp.zeros((N_GROUPS, tm), f32)
    for g in range(N_GROUPS):
        blk = sel[g * EXPERTS_PER_GROUP:(g + 1) * EXPERTS_PER_GROUP, :]
        top1 = jnp.max(blk, axis=0, keepdims=True)
        first = _first_index_of_max(blk, i8, EXPERTS_PER_GROUP)
        top2 = jnp.max(jnp.where(i8 == first, neg_inf, blk), axis=0, keepdims=True)
        gscore = jnp.where(i8 == g, top1 + top2, gscore)
    gkeep = jnp.zeros((N_GROUPS, tm), jnp.bool_)
    cur = gscore
    for _ in range(TOPK_GROUPS):
        pick = i8 == _first_index_of_max(cur, i8, N_GROUPS)
        gkeep = jnp.logical_or(gkeep, pick)
        cur = jnp.where(pick, neg_inf, cur)
    cur = jnp.concatenate(
        [jnp.where(gkeep[g:g + 1, :], sel[g * EXPERTS_PER_GROUP:(g + 1) * EXPERTS_PER_GROUP, :], NEG_BIG)
         for g in range(N_GROUPS)], axis=0)

    ie = lax.broadcasted_iota(jnp.int32, (N_EXPERTS, tm), 0)
    ik = lax.broadcasted_iota(jnp.int32, (TOP_K, tm), 0)
    idx_out = jnp.zeros((TOP_K, tm), jnp.int32)
    w_out = jnp.zeros((TOP_K, tm), f32)
    chosen = jnp.zeros((N_EXPERTS, tm), jnp.bool_)
    for k in range(TOP_K):
        fi = _first_index_of_max(cur, ie, N_EXPERTS)
        pick = ie == fi
        wk = jnp.sum(jnp.where(pick, scores, 0.0), axis=0, keepdims=True)
        idx_out = jnp.where(ik == k, fi, idx_out)
        w_out = jnp.where(ik == k, wk, w_out)
        chosen = jnp.logical_or(chosen, pick)
        cur = jnp.where(pick, neg_inf, cur)
    w_out = w_out / jnp.sum(w_out, axis=0, keepdims=True) * ROUTE_SCALE
    for j in range(tm // SC_WINDOW):
        idx_ref[j] = idx_out[:, j * SC_WINDOW:(j + 1) * SC_WINDOW]
    wt_ref[...] = w_out

    sel01 = jnp.where(chosen, 1.0, 0.0)
    before = jnp.dot(sel01.astype(bf16), tri_ref[...], preferred_element_type=f32) + run_ref[:, 0:1]
    rank_out = jnp.zeros((TOP_K, tm), f32)
    for k in range(TOP_K):
        rk = jnp.sum(jnp.where(ie == idx_out[k:k + 1, :], before, 0.0), axis=0, keepdims=True)
        rank_out = jnp.where(ik == k, rk, rank_out)
    rank_i = rank_out.astype(jnp.int32)
    for j in range(tm // SC_WINDOW):
        rank_ref[j] = rank_i[:, j * SC_WINDOW:(j + 1) * SC_WINDOW]
    run_ref[...] = run_ref[...] + jnp.sum(sel01, axis=1, keepdims=True)
    cnt_ref[...] = run_ref[...].astype(jnp.int32)


def _merge_route(x2, ya, outs, lses, sga, sgb, prm):
    T = x2.shape[0]
    tm = TOKEN_TILE
    row = lambda i: (i, 0)
    col = lambda i: (0, i)
    const = lambda i: (0, 0)
    tok = lambda w: pl.BlockSpec((tm, w), row)
    full = lambda a: pl.BlockSpec(a.shape, const)
    win = pl.BlockSpec((tm // SC_WINDOW, TOP_K, SC_WINDOW), lambda i: (i, 0, 0))
    return pl.pallas_call(
        _merge_route_kernel,
        grid=(T // tm,),
        in_specs=[tok(D_MODEL), tok(A_WIDTH), tok(B_WIDTH), tok(B_WIDTH), tok(B_WIDTH),
                  tok(B_WIDTH), tok(B_WIDTH), tok(B_WIDTH), tok(D_MODEL), tok(D_MODEL),
                  full(prm["w_branch_a"]), full(prm["w_branch_b"]), full(prm["w_out"]), full(prm["norm_ffn"]),
                  full(prm["router_hi"]), full(prm["router_lo"]), full(prm["router_bias"]), full(prm["tri"])],
        out_specs=[tok(D_MODEL), pl.BlockSpec((ROW_CHUNKS, tm, LANES), lambda i: (0, i, 0)),
                   win, pl.BlockSpec((TOP_K, tm), col), win,
                   pl.BlockSpec((N_EXPERTS, LANES), const)],
        out_shape=[jax.ShapeDtypeStruct((T, D_MODEL), f32), jax.ShapeDtypeStruct((ROW_CHUNKS, T, LANES), jnp.uint32),
                   jax.ShapeDtypeStruct((T // SC_WINDOW, TOP_K, SC_WINDOW), jnp.int32),
                   jax.ShapeDtypeStruct((TOP_K, T), f32),
                   jax.ShapeDtypeStruct((T // SC_WINDOW, TOP_K, SC_WINDOW), jnp.int32),
                   jax.ShapeDtypeStruct((N_EXPERTS, LANES), jnp.int32)],
        scratch_shapes=[pltpu.VMEM((N_EXPERTS, LANES), f32)],
        compiler_params=_cparams(("arbitrary",)),
    )(x2, ya, *outs, *lses, sga, sgb,
      prm["w_branch_a"], prm["w_branch_b"], prm["w_out"], prm["norm_ffn"],
      prm["router_hi"], prm["router_lo"], prm["router_bias"], prm["tri"])


def _experts_kernel(be_ref, nu_ref, nv_ref, x_ref, wgu_ref, wd_ref, y_ref):
    del be_ref
    i = pl.program_id(0)

    @pl.when(i < nu_ref[0])
    def _():
        lo, hi = _unpack_rows([x_ref[c] for c in range(ROW_CHUNKS)])
        live = lax.broadcasted_iota(jnp.int32, lo.shape, 0) < nv_ref[i]
        x = jnp.concatenate([jnp.where(live, lo, 0.0), jnp.where(live, hi, 0.0)], axis=1).astype(bf16)
        gu = jnp.dot(x, wgu_ref[0], preferred_element_type=f32)
        g, u = gu[:, :D_EXPERT], gu[:, D_EXPERT:]
        a = g * jax.nn.sigmoid(g) * u
        y = jnp.dot(a.astype(bf16), wd_ref[0], preferred_element_type=f32)
        for c, chunk in enumerate(_pack_rows(y)):
            y_ref[c] = chunk

    @pl.when(i >= nu_ref[0])
    def _():
        y_ref[...] = jnp.zeros_like(y_ref)


def _experts(xs, blk_e, n_used, n_valid, prm):
    n_rows = xs.shape[1]
    bm = MOE_ROWS
    rows = lambda i, be, nu, nv: (0, jnp.minimum(i, nu[0] - 1), 0)
    return pl.pallas_call(
        _experts_kernel,
        grid_spec=pltpu.PrefetchScalarGridSpec(
            num_scalar_prefetch=3,
            grid=(n_rows // bm,),
            in_specs=[
                pl.BlockSpec((ROW_CHUNKS, bm, LANES), rows),
                pl.BlockSpec((1, D_MODEL, 2 * D_EXPERT), lambda i, be, nu, nv: (be[i], 0, 0)),
                pl.BlockSpec((1, D_EXPERT, D_MODEL), lambda i, be, nu, nv: (be[i], 0, 0)),
            ],
            out_specs=pl.BlockSpec((ROW_CHUNKS, bm, LANES), lambda i, be, nu, nv: (0, i, 0)),
        ),
        out_shape=jax.ShapeDtypeStruct((ROW_CHUNKS, n_rows, LANES), jnp.uint32),
        compiler_params=_cparams(("arbitrary",)),
    )(blk_e, n_used, n_valid, xs, prm["expert_gu"], prm["expert_down"])


def _sc_mesh():
    return plsc.VectorSubcoreMesh(core_axis_name="c", subcore_axis_name="s")


def _sc_dispatch(u2p, dest, n_rows):
    T = u2p.shape[1]
    per_worker = T // SC_WORKERS
    n_it = per_worker // SC_WINDOW

    @functools.partial(
        pl.kernel, mesh=_sc_mesh(),
        out_type=jax.ShapeDtypeStruct((ROW_CHUNKS, n_rows, LANES), jnp.uint32),
        scratch_types=[pltpu.VMEM((TOP_K, SC_WINDOW), jnp.int32),
                       pltpu.VMEM((ROW_CHUNKS, SC_WINDOW, LANES), jnp.uint32),
                       pltpu.SemaphoreType.DMA, pltpu.SemaphoreType.DMA],
    )
    def k(u_hbm, dest_hbm, xs_hbm, idx_v, rows_v, sem_in, sem_out):
        wid = lax.axis_index("s") * SC_CORES + lax.axis_index("c")

        @pl.loop(0, n_it)
        def _(it):
            t0 = wid * per_worker + it * SC_WINDOW
            pltpu.sync_copy(dest_hbm.at[wid * n_it + it], idx_v)
            loads = [pltpu.async_copy(u_hbm.at[c].at[pl.ds(t0, SC_WINDOW)], rows_v.at[c], sem_in)
                     for c in range(ROW_CHUNKS)]
            for cp in loads:
                cp.wait()
            stores = [pltpu.async_copy(rows_v.at[c], xs_hbm.at[c].at[idx_v.at[kk]], sem_out)
                      for kk in range(TOP_K) for c in range(ROW_CHUNKS)]
            for cp in stores:
                cp.wait()

    return k(u2p, dest)


def _sc_combine(ys, dest):
    n_windows = dest.shape[0]
    T = n_windows * SC_WINDOW
    per_worker = T // SC_WORKERS
    n_it = per_worker // SC_WINDOW

    @functools.partial(
        pl.kernel, mesh=_sc_mesh(),
        out_type=jax.ShapeDtypeStruct((TOP_K, ROW_CHUNKS, T, LANES), jnp.uint32),
        scratch_types=[pltpu.VMEM((TOP_K, SC_WINDOW), jnp.int32),
                       pltpu.VMEM((SC_SLABS, SC_WINDOW, LANES), jnp.uint32),
                       pltpu.SemaphoreType.DMA, pltpu.SemaphoreType.DMA],
    )
    def k(ys_hbm, dest_hbm, yg_hbm, idx_v, rows_v, sem_in, sem_out):
        wid = lax.axis_index("s") * SC_CORES + lax.axis_index("c")

        @pl.loop(0, n_it)
        def _(it):
            t0 = wid * per_worker + it * SC_WINDOW
            pltpu.sync_copy(dest_hbm.at[wid * n_it + it], idx_v)
            for c in range(ROW_CHUNKS):
                for k0 in range(0, TOP_K, SC_SLABS):
                    loads = [pltpu.async_copy(ys_hbm.at[c].at[idx_v.at[k0 + j]], rows_v.at[j], sem_in)
                             for j in range(SC_SLABS)]
                    for cp in loads:
                        cp.wait()
                    stores = [pltpu.async_copy(rows_v.at[j], yg_hbm.at[k0 + j].at[c].at[pl.ds(t0, SC_WINDOW)], sem_out)
                              for j in range(SC_SLABS)]
                    for cp in stores:
                        cp.wait()

    return k(ys, dest)


def _final_kernel(h_ref, u_ref, yg_ref, wt_ref, p_ref, sgu_ref, sd_ref, pg_ref, pp_ref, o_ref):
    wt = wt_ref[...]
    r_lo = r_hi = None
    for k in range(TOP_K):
        lo, hi = _unpack_rows([yg_ref[k, c] for c in range(ROW_CHUNKS)])
        wk = wt[:, k:k + 1]
        r_lo = lo * wk if k == 0 else r_lo + lo * wk
        r_hi = hi * wk if k == 0 else r_hi + hi * wk
    routed = jnp.concatenate([r_lo, r_hi], axis=1)
    u_lo, u_hi = _unpack_rows([u_ref[c] for c in range(ROW_CHUNKS)])
    un = jnp.concatenate([u_lo, u_hi], axis=1).astype(bf16)
    gu = jnp.dot(un, sgu_ref[...], preferred_element_type=f32)
    g, u = gu[:, :D_SHARED], gu[:, D_SHARED:]
    shared = jnp.dot((g * jax.nn.sigmoid(g) * u).astype(bf16), sd_ref[...], preferred_element_type=f32)
    h = h_ref[...] + (routed + shared)
    gate = jax.nn.sigmoid(jnp.dot(h.astype(bf16), pg_ref[...], preferred_element_type=f32))
    emb = jnp.dot(p_ref[...].astype(bf16), pp_ref[...], preferred_element_type=f32)
    o_ref[...] = h + gate * emb


def _final(h1, u2p, yg, wts_t, p2, prm):
    T = h1.shape[0]
    tm = TOKEN_TILE
    row = lambda i: (i, 0)
    const = lambda i: (0, 0)
    full = lambda a: pl.BlockSpec(a.shape, const)
    return pl.pallas_call(
        _final_kernel,
        grid=(T // tm,),
        in_specs=[pl.BlockSpec((tm, D_MODEL), row),
                  pl.BlockSpec((ROW_CHUNKS, tm, LANES), lambda i: (0, i, 0)),
                  pl.BlockSpec((TOP_K, ROW_CHUNKS, tm, LANES), lambda i: (0, 0, i, 0)),
                  pl.BlockSpec((tm, TOP_K), row),
                  pl.BlockSpec((tm, PLE_DIM), row),
                  full(prm["shared_gu"]), full(prm["shared_down"]), full(prm["ple_gate"]), full(prm["ple_proj"])],
        out_specs=pl.BlockSpec((tm, D_MODEL), row),
        out_shape=jax.ShapeDtypeStruct((T, D_MODEL), f32),
        compiler_params=_cparams(("parallel",)),
    )(h1, u2p, yg, wts_t, p2, prm["shared_gu"], prm["shared_down"], prm["ple_gate"], prm["ple_proj"])


def _deinterleave(n_heads):
    base = jnp.concatenate([jnp.arange(0, HEAD_DIM, 2), jnp.arange(1, HEAD_DIM, 2)])
    return (jnp.arange(n_heads)[:, None] * HEAD_DIM + base[None, :]).reshape(-1)


def _rope_tables(ang):
    c, s = jnp.cos(ang), jnp.sin(ang)
    return jnp.tile(jnp.concatenate([c, c], axis=-1), (1, 2)), jnp.tile(jnp.concatenate([-s, s], axis=-1), (1, 2))


def _axial_angles(S):
    rows = S // GRID_W
    row = jnp.repeat(jnp.arange(rows, dtype=f32), GRID_W)
    col = jnp.tile(jnp.arange(GRID_W, dtype=f32), rows)
    half = HEAD_DIM // 2
    inv = ROPE_THETA ** (-jnp.arange(0, half, 2, dtype=f32) / half)
    return jnp.concatenate([row[:, None] * inv, col[:, None] * inv], axis=-1)


def _linear_angles(S):
    t = jnp.arange(S, dtype=f32)
    inv = ROPE_THETA ** (-jnp.arange(0, HEAD_DIM, 2, dtype=f32) / HEAD_DIM)
    return t[:, None] * inv


def _prepare(seq_lens, norm_mix, w_in, q_norm_a, k_norm_a, q_norm_b, k_norm_b, w_branch_a, w_branch_b, w_out,
             norm_ffn, router_w, router_bias, expert_gate, expert_up, expert_down, shared_gate, shared_up,
             shared_down, ple_proj, ple_gate):
    cuts = [A_WIDTH, A_WIDTH + A_KV_WIDTH, A_WIDTH + 2 * A_KV_WIDTH,
            A_WIDTH + 2 * A_KV_WIDTH + B_WIDTH, A_WIDTH + 2 * A_KV_WIDTH + 2 * B_WIDTH,
            A_WIDTH + 2 * A_KV_WIDTH + 3 * B_WIDTH, A_WIDTH + 2 * A_KV_WIDTH + 3 * B_WIDTH + D_MODEL]
    wqa, wka, wva, wqb, wkb, wvb, wga, wgb = jnp.split(w_in, cuts, axis=-1)
    wqa = wqa[:, _deinterleave(A_Q_HEADS)]
    wka = wka[:, _deinterleave(A_KV_HEADS)]
    wqb = wqb[:, _deinterleave(B_HEADS)]
    wkb = wkb[:, _deinterleave(B_HEADS)]
    k0, k1 = wka[:, :HEAD_DIM], wka[:, HEAD_DIM:]
    v0, v1 = wva[:, :HEAD_DIM], wva[:, HEAD_DIM:]
    vz = jnp.zeros_like(v0)
    w_all = jnp.concatenate([wqa, k0, k0, k1, k1, v0, vz, v1, vz, wqb, wkb, wvb, wga, wgb], axis=-1).astype(bf16)
    perm = _deinterleave(1)
    scale = HEAD_DIM ** -0.5
    tile = lambda g, n: jnp.tile(g[perm], n)[None, :]
    router_pad = jnp.pad(router_w, ((0, 0), (0, LANES - N_EXPERTS)))
    router_hi = router_pad.astype(bf16)
    tm = TOKEN_TILE
    prm = {
        "norm_mix": norm_mix[None, :], "w_in": w_all,
        "gqa": tile(q_norm_a, A_Q_HEADS) * (scale * LOG2_E), "gka": tile(k_norm_a, KA_COLS // HEAD_DIM),
        "gqb": tile(q_norm_b, B_HEADS) * scale, "gkb": tile(k_norm_b, B_HEADS),
        "bd": jnp.kron(jnp.eye(LANES // HEAD_DIM, dtype=f32), jnp.ones((HEAD_DIM, HEAD_DIM), f32)).astype(bf16),
        "w_branch_a": w_branch_a.astype(bf16), "w_branch_b": w_branch_b.astype(bf16), "w_out": w_out.astype(bf16),
        "norm_ffn": norm_ffn[None, :],
        "router_hi": router_hi, "router_lo": (router_pad - router_hi.astype(f32)).astype(bf16),
        "router_bias": router_bias[:, None],
        "tri": (jnp.arange(tm)[:, None] < jnp.arange(tm)[None, :]).astype(bf16),
        "expert_gu": jnp.concatenate([expert_gate, expert_up], axis=-1).astype(bf16),
        "expert_down": expert_down.astype(bf16),
        "shared_gu": jnp.concatenate([shared_gate, shared_up], axis=-1).astype(bf16),
        "shared_down": shared_down.astype(bf16),
        "ple_gate": ple_gate.astype(bf16), "ple_proj": ple_proj.astype(bf16),
        "cos_a": {}, "sin_a": {}, "cos_b": {}, "sin_b": {},
    }
    for S in seq_lens:
        prm["cos_a"][S], prm["sin_a"][S] = _rope_tables(_axial_angles(S))
        prm["cos_b"][S], prm["sin_b"][S] = _rope_tables(_linear_angles(S))
    return prm


def _dispatch_plan(idx, rank, counts, T):
    bm = MOE_ROWS
    n_rows = (T * TOP_K + N_EXPERTS * bm) // bm * bm
    n_blocks = n_rows // bm
    padded = (counts + bm - 1) // bm * bm
    pad_end = jnp.cumsum(padded)
    pad_start = pad_end - padded
    dest = pad_start[idx] + rank
    blk_start = jnp.arange(n_blocks, dtype=jnp.int32) * bm
    blk_e = jnp.minimum(jnp.sum(pad_end[None, :] <= blk_start[:, None], axis=1), N_EXPERTS - 1).astype(jnp.int32)
    n_used = (pad_end[-1:] // bm).astype(jnp.int32)
    n_valid = jnp.clip(counts[blk_e] - (blk_start - pad_start[blk_e]), 0, bm).astype(jnp.int32)
    return dest, n_rows, blk_e, n_used, n_valid


def _layer(x, p, prm):
    B, S, _ = x.shape
    T = B * S
    x2 = x.reshape(T, D_MODEL)
    qa, ka, va, qb, kb, vb, sga, sgb = _in_proj(x2, S, prm)
    shape3 = lambda a: a.reshape(B, S, a.shape[-1])
    ya = _flash(shape3(qa), shape3(ka), shape3(va)).reshape(T, A_WIDTH)
    outs, lses = [], []
    for _, dil in WINDOW_DILATIONS:
        o_d, lse_d = _dilated(shape3(qb), shape3(kb), shape3(vb), dil)
        outs.append(o_d)
        lses.append(lse_d)
    h1, u2p, idx, wts, rank, cnt = _merge_route(x2, ya, outs, lses, sga, sgb, prm)
    dest, n_rows, blk_e, n_used, n_valid = _dispatch_plan(idx, rank, cnt[:, 0], T)
    xs = _sc_dispatch(u2p, dest, n_rows)
    ys = _experts(xs, blk_e, n_used, n_valid, prm)
    yg = _sc_combine(ys, dest)
    out = _final(h1, u2p, yg, wts.T, p.reshape(T, PLE_DIM), prm)
    return out.reshape(B, S, D_MODEL)


def kernel(x_prompt, x_sample, p_prompt, p_sample, norm_mix, w_in, q_norm_a, k_norm_a, q_norm_b, k_norm_b,
           w_branch_a, w_branch_b, w_out, norm_ffn, router_w, router_bias, expert_gate, expert_up, expert_down,
           shared_gate, shared_up, shared_down, ple_proj, ple_gate):
    weights = (norm_mix, w_in, q_norm_a, k_norm_a, q_norm_b, k_norm_b, w_branch_a, w_branch_b, w_out, norm_ffn,
               router_w, router_bias, expert_gate, expert_up, expert_down, shared_gate, shared_up, shared_down,
               ple_proj, ple_gate)
    depth = norm_mix.shape[0]
    seq_lens = sorted({x_prompt.shape[1], x_sample.shape[1]})
    hp, hs = x_prompt, x_sample
    for i in range(depth):
        prm = _prepare(seq_lens, *[w[i] for w in weights])
        hp = _layer(hp, p_prompt[i], prm)
        hs = _layer(hs, p_sample[i], prm)
    return hp, hs
```

```python
import functools

import jax
import jax.numpy as jnp
from jax import lax
from jax.experimental import pallas as pl
from jax.experimental.pallas import tpu as pltpu
from jax.experimental.pallas import tpu_sc as plsc

D_MODEL = 1024
HEAD_DIM = 64
A_Q_HEADS = 8
A_KV_HEADS = 2
B_HEADS = 6
GRID_W = 64
ROPE_THETA = 10000.0
WINDOW_DILATIONS = ((128, 1), (512, 4), (2048, 16))
N_EXPERTS = 64
TOP_K = 8
N_GROUPS = 8
TOPK_GROUPS = 4
D_EXPERT = 256
D_SHARED = 256
ROUTE_SCALE = 2.5
PLE_DIM = 256
NEG_BIG = -1e30
EPS = 1e-6
LOG2_E = 1.4426950408889634

A_WIDTH = A_Q_HEADS * HEAD_DIM
A_KV_WIDTH = A_KV_HEADS * HEAD_DIM
B_WIDTH = B_HEADS * HEAD_DIM
GROUP_Q = A_Q_HEADS // A_KV_HEADS
EXPERTS_PER_GROUP = N_EXPERTS // N_GROUPS

LANES = 128
HALF_WINDOW = 64
KEY_SPAN = 256

KA_COLS = 2 * LANES
VA_COLS = 2 * LANES
OFF_QA = 0
OFF_KA = OFF_QA + A_WIDTH
OFF_VA = OFF_KA + KA_COLS
OFF_QB = OFF_VA + VA_COLS
OFF_KB = OFF_QB + B_WIDTH
OFF_VB = OFF_KB + B_WIDTH
OFF_GA = OFF_VB + B_WIDTH
OFF_GB = OFF_GA + D_MODEL
PROJ_COLS = OFF_GB + D_MODEL

TOKEN_TILE = 512
FLASH_TQ = 256
FLASH_TK = 1024
DIL_QBLOCK = 128
DIL_STEP_TOKENS = 2048
MOE_ROWS = 1024
MOE_SUB_ROWS = 256

VMEM_LIMIT = 52 * 1024 * 1024

HALF_MODEL = D_MODEL // 2
ROW_CHUNKS = HALF_MODEL // LANES
SC_CORES = 2
SC_SUBCORES = 16
SC_WORKERS = SC_CORES * SC_SUBCORES
SC_WINDOW = 128
SC_SLABS = 4

bf16 = jnp.bfloat16
f32 = jnp.float32


def _cparams(sem):
    return pltpu.CompilerParams(dimension_semantics=sem, vmem_limit_bytes=VMEM_LIMIT)


def _pack_rows(v):
    lo = lax.bitcast_convert_type(v[:, :HALF_MODEL].astype(bf16).astype(f32), jnp.uint32) >> 16
    hi = lax.bitcast_convert_type(v[:, HALF_MODEL:].astype(bf16).astype(f32), jnp.uint32) & jnp.uint32(0xFFFF0000)
    w = lo | hi
    return [w[:, c * LANES:(c + 1) * LANES] for c in range(ROW_CHUNKS)]


def _unpack_rows(chunks):
    w = jnp.concatenate(chunks, axis=1)
    lo = lax.bitcast_convert_type(w << 16, f32)
    hi = lax.bitcast_convert_type(w & jnp.uint32(0xFFFF0000), f32)
    return lo, hi


def _in_proj_kernel(x_ref, nm_ref, w_ref, gqa_ref, gka_ref, gqb_ref, gkb_ref,
                    ca_ref, sa_ref, cb_ref, sb_ref, bd_ref,
                    qa_ref, ka_ref, va_ref, qb1_ref, qb4_ref, qb16_ref, kb1_ref, kb4_ref, kb16_ref,
                    vb1_ref, vb4_ref, vb16_ref, ga_ref, gb_ref, st_ref):
    x = x_ref[...]
    r = lax.rsqrt(jnp.mean(x * x, axis=-1, keepdims=True) + EPS)
    u = (x * r * nm_ref[...]).astype(bf16)
    rows = x.shape[0]
    lane = lax.broadcasted_iota(jnp.int32, (rows, LANES), 1)
    first_half = (lane % HEAD_DIM) < (HEAD_DIM // 2)

    def proj(off, width):
        return jnp.dot(u, w_ref[:, off:off + width], preferred_element_type=f32)

    def norm_rope(off, width, g_ref, c_ref, s_ref):
        z = proj(off, width)
        c = c_ref[...]
        s = s_ref[...]
        chunks = []
        for c0 in range(0, width, LANES):
            zc = z[:, c0:c0 + LANES]
            ms = jnp.dot((zc * zc).astype(bf16), bd_ref[...], preferred_element_type=f32) * (1.0 / HEAD_DIM)
            zn = zc * lax.rsqrt(ms + EPS) * g_ref[:, c0:c0 + LANES]
            sw = jnp.where(first_half, pltpu.roll(zn, LANES - HEAD_DIM // 2, 1), pltpu.roll(zn, HEAD_DIM // 2, 1))
            chunks.append(zn * c + sw * s)
        return chunks

    def store(chunks, out_ref):
        for j, ch in enumerate(chunks):
            out_ref[:, j * LANES:(j + 1) * LANES] = ch.astype(out_ref.dtype)

    def store_dilated(chunks, out_refs):
        for j, ch in enumerate(chunks):
            st_ref[j] = ch
        for (_, dil), out_ref in zip(WINDOW_DILATIONS, out_refs):
            for r in range(dil):
                for j in range(len(chunks)):
                    src = pl.ds(r, rows // dil, stride=dil) if dil > 1 else pl.ds(0, rows)
                    out_ref[r, :, j * LANES:(j + 1) * LANES] = st_ref[j, src, :].astype(out_ref.dtype)

    store(norm_rope(OFF_QA, A_WIDTH, gqa_ref, ca_ref, sa_ref), qa_ref)
    store(norm_rope(OFF_KA, KA_COLS, gka_ref, ca_ref, sa_ref), ka_ref)
    va = proj(OFF_VA, VA_COLS)
    va_ref[...] = jnp.concatenate(
        [jnp.where(lane < HEAD_DIM, va[:, c0:c0 + LANES], 1.0) for c0 in range(0, VA_COLS, LANES)],
        axis=1).astype(va_ref.dtype)
    store_dilated(norm_rope(OFF_QB, B_WIDTH, gqb_ref, cb_ref, sb_ref), (qb1_ref, qb4_ref, qb16_ref))
    store_dilated(norm_rope(OFF_KB, B_WIDTH, gkb_ref, cb_ref, sb_ref), (kb1_ref, kb4_ref, kb16_ref))
    vb = proj(OFF_VB, B_WIDTH)
    store_dilated([vb[:, c0:c0 + LANES] for c0 in range(0, B_WIDTH, LANES)], (vb1_ref, vb4_ref, vb16_ref))
    ga_ref[...] = jax.nn.sigmoid(proj(OFF_GA, D_MODEL)).astype(ga_ref.dtype)
    gb_ref[...] = jax.nn.sigmoid(proj(OFF_GB, D_MODEL)).astype(gb_ref.dtype)


def _in_proj(x2, S, prm):
    T = x2.shape[0]
    tm = TOKEN_TILE
    n_pos = S // tm
    row = lambda i: (i, 0)
    pos = lambda i: (i % n_pos, 0)
    const = lambda i: (0, 0)
    dils = [dil for _, dil in WINDOW_DILATIONS]
    flat = lambda w: (pl.BlockSpec((tm, w), row), jax.ShapeDtypeStruct((T, w), bf16))
    strided = lambda d: (pl.BlockSpec((d, tm // d, B_WIDTH), lambda i: (0, i, 0)),
                         jax.ShapeDtypeStruct((d, T // d, B_WIDTH), bf16))
    outs = ([flat(A_WIDTH), flat(KA_COLS), flat(VA_COLS)] + 3 * [strided(d) for d in dils]
            + [flat(D_MODEL), flat(D_MODEL)])
    return pl.pallas_call(
        _in_proj_kernel,
        grid=(T // tm,),
        in_specs=[
            pl.BlockSpec((tm, D_MODEL), row),
            pl.BlockSpec((1, D_MODEL), const),
            pl.BlockSpec((D_MODEL, PROJ_COLS), const),
            pl.BlockSpec((1, A_WIDTH), const),
            pl.BlockSpec((1, KA_COLS), const),
            pl.BlockSpec((1, B_WIDTH), const),
            pl.BlockSpec((1, B_WIDTH), const),
            pl.BlockSpec((tm, LANES), pos),
            pl.BlockSpec((tm, LANES), pos),
            pl.BlockSpec((tm, LANES), pos),
            pl.BlockSpec((tm, LANES), pos),
            pl.BlockSpec((LANES, LANES), const),
        ],
        out_specs=[spec for spec, _ in outs],
        out_shape=[shape for _, shape in outs],
        scratch_shapes=[pltpu.VMEM((B_WIDTH // LANES, tm, LANES), f32)],
        compiler_params=_cparams(("parallel",)),
        name="in_proj",
    )(x2, prm["norm_mix"], prm["w_in"], prm["gqa"], prm["gka"], prm["gqb"], prm["gkb"],
      prm["cos_a"][S], prm["sin_a"][S], prm["cos_b"][S], prm["sin_b"][S], prm["bd"])


def _flash_kernel(q_ref, k_ref, v_ref, o_ref, s_ref, *, tq, tk, seq):
    lane = lax.broadcasted_iota(jnp.int32, (tq, LANES), 1)
    lo = lane < HEAD_DIM
    q = q_ref[0]
    q01, q23 = q[:, :LANES], q[:, LANES:]
    zero = jnp.zeros_like(q01)
    qs = [jnp.where(lo, q01, zero), jnp.where(lo, zero, q01), jnp.where(lo, q23, zero), jnp.where(lo, zero, q23)]
    n_chunks = seq // tk

    def keys(chunk):
        return k_ref[0, pl.ds(pl.multiple_of(chunk * tk, tk), tk), :]

    def score(h, kk):
        return lax.dot_general(qs[h], kk, (((1,), (1,)), ((), ())), preferred_element_type=f32)

    def attend(chunk, slot, ms, accs, next_chunk):
        vv = v_ref[0, pl.ds(pl.multiple_of(chunk * tk, tk), tk), :]
        kk = None if next_chunk is None else keys(next_chunk)
        new_ms, new_accs = [], []
        for h in range(GROUP_Q):
            if kk is not None:
                s_ref[1 - slot, h] = score(h, kk)
            s = s_ref[slot, h]
            m_new = jnp.maximum(ms[h], jnp.max(s, axis=-1, keepdims=True))
            p = jnp.exp2((s - m_new).astype(bf16))
            alpha = jnp.exp2(ms[h] - m_new)
            new_accs.append(accs[h] * alpha + jnp.dot(p, vv, preferred_element_type=f32))
            new_ms.append(m_new)
        return tuple(new_ms), tuple(new_accs)

    kk0 = keys(0)
    for h in range(GROUP_Q):
        s_ref[0, h] = score(h, kk0)

    def body(jj, carry):
        ms, accs = attend(2 * jj, 0, *carry, 2 * jj + 1)
        return attend(2 * jj + 1, 1, ms, accs, 2 * jj + 2)

    m0 = tuple(jnp.full((tq, 1), NEG_BIG, f32) for _ in range(GROUP_Q))
    acc0 = tuple(jnp.zeros((tq, LANES), f32) for _ in range(GROUP_Q))
    ms, accs = lax.fori_loop(0, n_chunks // 2 - 1, body, (m0, acc0))
    ms, accs = attend(n_chunks - 2, 0, ms, accs, n_chunks - 1)
    _, accs = attend(n_chunks - 1, 1, ms, accs, None)
    heads = [acc / jnp.where(lo, pltpu.roll(acc, HEAD_DIM, 1), 1.0) for acc in accs]
    out01 = jnp.where(lo, heads[0], pltpu.roll(heads[1], HEAD_DIM, 1))
    out23 = jnp.where(lo, heads[2], pltpu.roll(heads[3], HEAD_DIM, 1))
    o_ref[0] = jnp.concatenate([out01, out23], axis=1).astype(o_ref.dtype)


def _flash(qa, ka, va):
    B, S, _ = qa.shape
    tq, tk = FLASH_TQ, FLASH_TK
    return pl.pallas_call(
        functools.partial(_flash_kernel, tq=tq, tk=tk, seq=S),
        grid=(B, A_KV_HEADS, S // tq),
        in_specs=[
            pl.BlockSpec((1, tq, GROUP_Q * HEAD_DIM), lambda b, g, i: (b, i, g)),
            pl.BlockSpec((1, S, LANES), lambda b, g, i: (b, 0, g)),
            pl.BlockSpec((1, S, LANES), lambda b, g, i: (b, 0, g)),
        ],
        out_specs=pl.BlockSpec((1, tq, GROUP_Q * HEAD_DIM), lambda b, g, i: (b, i, g)),
        out_shape=jax.ShapeDtypeStruct((B, S, A_WIDTH), bf16),
        scratch_shapes=[pltpu.VMEM((2, GROUP_Q, tq, tk), f32)],
        compiler_params=_cparams(("parallel", "parallel", "arbitrary")),
        name="flash_gqa",
    )(qa, ka, va)


def _dilated_kernel(q_ref, k_ref, v_ref, o_ref, lse_ref, so_ref, sl_ref, *, length, tms, dil):
    i = pl.program_id(1)
    lane = lax.broadcasted_iota(jnp.int32, (DIL_QBLOCK, LANES), 1)
    lo = lane < HEAD_DIM
    n_pairs = B_WIDTH // LANES

    def unit(u, carry):
        sb, r = u // dil, u % dil
        r0 = pl.multiple_of(sb * DIL_QBLOCK, DIL_QBLOCK)
        m0 = i * tms + r0
        start = pl.multiple_of(jnp.clip(m0 - HALF_WINDOW, 0, length - KEY_SPAN), HALF_WINDOW)
        qpos = m0 + lax.broadcasted_iota(jnp.int32, (DIL_QBLOCK, KEY_SPAN), 0)
        kpos = start + lax.broadcasted_iota(jnp.int32, (DIL_QBLOCK, KEY_SPAN), 1)
        valid = jnp.abs(kpos - qpos) <= HALF_WINDOW
        rows = pl.ds(r0 * dil + r, DIL_QBLOCK, stride=dil) if dil > 1 else pl.ds(r0, DIL_QBLOCK)
        scores = []
        for c in range(n_pairs):
            cs = slice(c * LANES, (c + 1) * LANES)
            qp = q_ref[r, 0, pl.ds(r0, DIL_QBLOCK), cs]
            kp = k_ref[r, 0, pl.ds(start, KEY_SPAN), cs]
            zero = jnp.zeros_like(qp)
            for half in range(2):
                qm = jnp.where(lo, qp, zero) if half == 0 else jnp.where(lo, zero, qp)
                scores.append(lax.dot_general(qm, kp, (((1,), (1,)), ((), ())), preferred_element_type=f32))
        for c in range(n_pairs):
            vp = v_ref[r, 0, pl.ds(start, KEY_SPAN), c * LANES:(c + 1) * LANES]
            outs, lses = [], []
            for half in range(2):
                s = jnp.where(valid, scores[2 * c + half], NEG_BIG)
                m = jnp.max(s, axis=-1, keepdims=True)
                e = jnp.exp2(s - m)
                l = jnp.sum(e, axis=-1, keepdims=True)
                outs.append(jnp.dot(e.astype(bf16), vp, preferred_element_type=f32) * (1.0 / l))
                lses.append(m + jnp.log2(l))
            so_ref[c, rows, :] = jnp.where(lo, outs[0], outs[1])
            sl_ref[c, rows, :] = jnp.where(lo, lses[0], lses[1])
        return carry

    lax.fori_loop(0, (tms // DIL_QBLOCK) * dil, unit, 0, unroll=4)
    o_ref[0] = jnp.concatenate([so_ref[c] for c in range(n_pairs)], axis=1).astype(o_ref.dtype)
    lse_ref[0] = jnp.concatenate([sl_ref[c] for c in range(n_pairs)], axis=1)


def _dilated(qb, kb, vb, B, S, dil):
    length = S // dil
    tms = DIL_STEP_TOKENS // dil
    view = lambda a: a.reshape(dil, B, length, B_WIDTH)
    o, lse = pl.pallas_call(
        functools.partial(_dilated_kernel, length=length, tms=tms, dil=dil),
        grid=(B, length // tms),
        in_specs=[
            pl.BlockSpec((dil, 1, tms, B_WIDTH), lambda b, i: (0, b, i, 0)),
            pl.BlockSpec((dil, 1, length, B_WIDTH), lambda b, i: (0, b, 0, 0)),
            pl.BlockSpec((dil, 1, length, B_WIDTH), lambda b, i: (0, b, 0, 0)),
        ],
        out_specs=[
            pl.BlockSpec((1, tms * dil, B_WIDTH), lambda b, i: (b, i, 0)),
            pl.BlockSpec((1, tms * dil, B_WIDTH), lambda b, i: (b, i, 0)),
        ],
        out_shape=[
            jax.ShapeDtypeStruct((B, S, B_WIDTH), bf16),
            jax.ShapeDtypeStruct((B, S, B_WIDTH), f32),
        ],
        scratch_shapes=[pltpu.VMEM((B_WIDTH // LANES, tms * dil, LANES), f32),
                        pltpu.VMEM((B_WIDTH // LANES, tms * dil, LANES), f32)],
        compiler_params=_cparams(("parallel", "arbitrary")),
        name=f"dilated_{dil}",
    )(view(qb), view(kb), view(vb))
    return o.reshape(B * S, B_WIDTH), lse.reshape(B * S, B_WIDTH)


def _first_index_of_max(cur, idx, n):
    mx = jnp.max(cur, axis=0, keepdims=True)
    return jnp.min(jnp.where(cur == mx, idx, n), axis=0, keepdims=True)


def _merge_route_kernel(x_ref, ya_ref, o1_ref, o2_ref, o3_ref, l1_ref, l2_ref, l3_ref, ga_ref, gb_ref,
                        wa_ref, wb_ref, wo_ref, nf_ref, rhi_ref, rlo_ref, bias_ref, tri_ref,
                        h_ref, u_ref, idx_ref, wt_ref, rank_ref, cnt_ref, run_ref):
    step = pl.program_id(0)

    @pl.when(step == 0)
    def _():
        run_ref[...] = jnp.zeros_like(run_ref)

    l1, l2, l3 = l1_ref[...], l2_ref[...], l3_ref[...]
    lm = jnp.maximum(jnp.maximum(l1, l2), l3)
    e1, e2, e3 = jnp.exp2(l1 - lm), jnp.exp2(l2 - lm), jnp.exp2(l3 - lm)
    yb = (e1 * o1_ref[...].astype(f32) + e2 * o2_ref[...].astype(f32) + e3 * o3_ref[...].astype(f32)) / (e1 + e2 + e3)

    pa = jnp.dot(ya_ref[...], wa_ref[...], preferred_element_type=f32)
    pb = jnp.dot(yb.astype(bf16), wb_ref[...], preferred_element_type=f32)
    merged = ga_ref[...].astype(f32) * pa + gb_ref[...].astype(f32) * pb
    h = x_ref[...] + jnp.dot(merged.astype(bf16), wo_ref[...], preferred_element_type=f32)
    h_ref[...] = h

    r = lax.rsqrt(jnp.mean(h * h, axis=-1, keepdims=True) + EPS)
    uf = h * r * nf_ref[...]
    u_hi = uf.astype(bf16)
    for c, chunk in enumerate(_pack_rows(uf)):
        u_ref[c] = chunk
    u_lo = (uf - u_hi.astype(f32)).astype(bf16)
    logits = (jnp.dot(u_hi, rhi_ref[...], preferred_element_type=f32)
              + jnp.dot(u_lo, rhi_ref[...], preferred_element_type=f32)
              + jnp.dot(u_hi, rlo_ref[...], preferred_element_type=f32))
    scores = jax.nn.sigmoid(logits.T[:N_EXPERTS, :])
    sel = scores + bias_ref[...]
    tm = sel.shape[1]

    i8 = lax.broadcasted_iota(jnp.int32, (EXPERTS_PER_GROUP, tm), 0)
    neg_inf = jnp.float32(-jnp.inf)
    gscore = jnp.zeros((N_GROUPS, tm), f32)
    for g in range(N_GROUPS):
        blk = sel[g * EXPERTS_PER_GROUP:(g + 1) * EXPERTS_PER_GROUP, :]
        top1 = jnp.max(blk, axis=0, keepdims=True)
        first = _first_index_of_max(blk, i8, EXPERTS_PER_GROUP)
        top2 = jnp.max(jnp.where(i8 == first, neg_inf, blk), axis=0, keepdims=True)
        gscore = jnp.where(i8 == g, top1 + top2, gscore)
    gkeep = jnp.zeros((N_GROUPS, tm), jnp.bool_)
    cur = gscore
    for _ in range(TOPK_GROUPS):
        pick = i8 == _first_index_of_max(cur, i8, N_GROUPS)
        gkeep = jnp.logical_or(gkeep, pick)
        cur = jnp.where(pick, neg_inf, cur)
    cur = jnp.concatenate(
        [jnp.where(gkeep[g:g + 1, :], sel[g * EXPERTS_PER_GROUP:(g + 1) * EXPERTS_PER_GROUP, :], NEG_BIG)
         for g in range(N_GROUPS)], axis=0)

    ie = lax.broadcasted_iota(jnp.int32, (N_EXPERTS, tm), 0)
    ik = lax.broadcasted_iota(jnp.int32, (TOP_K, tm), 0)
    idx_out = jnp.zeros((TOP_K, tm), jnp.int32)
    w_out = jnp.zeros((TOP_K, tm), f32)
    chosen = jnp.zeros((N_EXPERTS, tm), jnp.bool_)
    for k in range(TOP_K):
        fi = _first_index_of_max(cur, ie, N_EXPERTS)
        pick = ie == fi
        wk = jnp.sum(jnp.where(pick, scores, 0.0), axis=0, keepdims=True)
        idx_out = jnp.where(ik == k, fi, idx_out)
        w_out = jnp.where(ik == k, wk, w_out)
        chosen = jnp.logical_or(chosen, pick)
        cur = jnp.where(pick, neg_inf, cur)
    w_out = w_out / jnp.sum(w_out, axis=0, keepdims=True) * ROUTE_SCALE
    for j in range(tm // SC_WINDOW):
        idx_ref[j] = idx_out[:, j * SC_WINDOW:(j + 1) * SC_WINDOW]
    wt_ref[...] = w_out

    sel01 = jnp.where(chosen, 1.0, 0.0)
    before = jnp.dot(sel01.astype(bf16), tri_ref[...], preferred_element_type=f32) + run_ref[:, 0:1]
    rank_out = jnp.zeros((TOP_K, tm), f32)
    for k in range(TOP_K):
        rk = jnp.sum(jnp.where(ie == idx_out[k:k + 1, :], before, 0.0), axis=0, keepdims=True)
        rank_out = jnp.where(ik == k, rk, rank_out)
    rank_i = rank_out.astype(jnp.int32)
    for j in range(tm // SC_WINDOW):
        rank_ref[j] = rank_i[:, j * SC_WINDOW:(j + 1) * SC_WINDOW]
    run_ref[...] = run_ref[...] + jnp.sum(sel01, axis=1, keepdims=True)
    cnt_ref[...] = run_ref[...].astype(jnp.int32)


def _merge_route(x2, ya, outs, lses, sga, sgb, prm):
    T = x2.shape[0]
    tm = TOKEN_TILE
    row = lambda i: (i, 0)
    col = lambda i: (0, i)
    const = lambda i: (0, 0)
    tok = lambda w: pl.BlockSpec((tm, w), row)
    full = lambda a: pl.BlockSpec(a.shape, const)
    win = pl.BlockSpec((tm // SC_WINDOW, TOP_K, SC_WINDOW), lambda i: (i, 0, 0))
    return pl.pallas_call(
        _merge_route_kernel,
        grid=(T // tm,),
        in_specs=[tok(D_MODEL), tok(A_WIDTH), tok(B_WIDTH), tok(B_WIDTH), tok(B_WIDTH),
                  tok(B_WIDTH), tok(B_WIDTH), tok(B_WIDTH), tok(D_MODEL), tok(D_MODEL),
                  full(prm["w_branch_a"]), full(prm["w_branch_b"]), full(prm["w_out"]), full(prm["norm_ffn"]),
                  full(prm["router_hi"]), full(prm["router_lo"]), full(prm["router_bias"]), full(prm["tri"])],
        out_specs=[tok(D_MODEL), pl.BlockSpec((ROW_CHUNKS, tm, LANES), lambda i: (0, i, 0)),
                   win, pl.BlockSpec((TOP_K, tm), col), win,
                   pl.BlockSpec((N_EXPERTS, LANES), const)],
        out_shape=[jax.ShapeDtypeStruct((T, D_MODEL), f32), jax.ShapeDtypeStruct((ROW_CHUNKS, T, LANES), jnp.uint32),
                   jax.ShapeDtypeStruct((T // SC_WINDOW, TOP_K, SC_WINDOW), jnp.int32),
                   jax.ShapeDtypeStruct((TOP_K, T), f32),
                   jax.ShapeDtypeStruct((T // SC_WINDOW, TOP_K, SC_WINDOW), jnp.int32),
                   jax.ShapeDtypeStruct((N_EXPERTS, LANES), jnp.int32)],
        scratch_shapes=[pltpu.VMEM((N_EXPERTS, LANES), f32)],
        compiler_params=_cparams(("arbitrary",)),
        name="merge_route",
    )(x2, ya, *outs, *lses, sga, sgb,
      prm["w_branch_a"], prm["w_branch_b"], prm["w_out"], prm["norm_ffn"],
      prm["router_hi"], prm["router_lo"], prm["router_bias"], prm["tri"])


def _experts_kernel(be_ref, nu_ref, nv_ref, x_ref, wgu_ref, wd_ref, y_ref):
    del be_ref
    i = pl.program_id(0)

    @pl.when(i < nu_ref[0])
    def _():
        n_sub = x_ref.shape[1] // MOE_SUB_ROWS

        def up(j):
            r0 = j * MOE_SUB_ROWS
            lo, hi = _unpack_rows([x_ref[c, r0:r0 + MOE_SUB_ROWS, :] for c in range(ROW_CHUNKS)])
            live = r0 + lax.broadcasted_iota(jnp.int32, lo.shape, 0) < nv_ref[i]
            x = jnp.concatenate([jnp.where(live, lo, 0.0), jnp.where(live, hi, 0.0)], axis=1).astype(bf16)
            return jnp.dot(x, wgu_ref[0], preferred_element_type=f32)

        def down(j, gu):
            g, u = gu[:, :D_EXPERT], gu[:, D_EXPERT:]
            a = g * jax.nn.sigmoid(g) * u
            y = jnp.dot(a.astype(bf16), wd_ref[0], preferred_element_type=f32)
            r0 = j * MOE_SUB_ROWS
            for c, chunk in enumerate(_pack_rows(y)):
                y_ref[c, r0:r0 + MOE_SUB_ROWS, :] = chunk

        gu_prev = up(0)
        for j in range(1, n_sub):
            gu_next = up(j)
            down(j - 1, gu_prev)
            gu_prev = gu_next
        down(n_sub - 1, gu_prev)

    @pl.when(i >= nu_ref[0])
    def _():
        y_ref[...] = jnp.zeros_like(y_ref)


def _experts(xs, blk_e, n_used, n_valid, prm):
    n_rows = xs.shape[1]
    bm = MOE_ROWS
    rows = lambda i, be, nu, nv: (0, jnp.minimum(i, nu[0] - 1), 0)
    return pl.pallas_call(
        _experts_kernel,
        grid_spec=pltpu.PrefetchScalarGridSpec(
            num_scalar_prefetch=3,
            grid=(n_rows // bm,),
            in_specs=[
                pl.BlockSpec((ROW_CHUNKS, bm, LANES), rows),
                pl.BlockSpec((1, D_MODEL, 2 * D_EXPERT), lambda i, be, nu, nv: (be[i], 0, 0)),
                pl.BlockSpec((1, D_EXPERT, D_MODEL), lambda i, be, nu, nv: (be[i], 0, 0)),
            ],
            out_specs=pl.BlockSpec((ROW_CHUNKS, bm, LANES), lambda i, be, nu, nv: (0, i, 0)),
        ),
        out_shape=jax.ShapeDtypeStruct((ROW_CHUNKS, n_rows, LANES), jnp.uint32),
        compiler_params=_cparams(("arbitrary",)),
        name="experts",
    )(blk_e, n_used, n_valid, xs, prm["expert_gu"], prm["expert_down"])


def _sc_mesh():
    return plsc.VectorSubcoreMesh(core_axis_name="c", subcore_axis_name="s")


def _sc_dispatch(u2p, dest, n_rows):
    T = u2p.shape[1]
    per_worker = T // SC_WORKERS
    n_it = per_worker // SC_WINDOW

    @functools.partial(
        pl.kernel, mesh=_sc_mesh(),
        out_type=jax.ShapeDtypeStruct((ROW_CHUNKS, n_rows, LANES), jnp.uint32),
        scratch_types=[pltpu.VMEM((TOP_K, SC_WINDOW), jnp.int32),
                       pltpu.VMEM((ROW_CHUNKS, SC_WINDOW, LANES), jnp.uint32),
                       pltpu.SemaphoreType.DMA, pltpu.SemaphoreType.DMA],
    )
    def k(u_hbm, dest_hbm, xs_hbm, idx_v, rows_v, sem_in, sem_out):
        wid = lax.axis_index("s") * SC_CORES + lax.axis_index("c")

        @pl.loop(0, n_it)
        def _(it):
            t0 = wid * per_worker + it * SC_WINDOW
            pltpu.sync_copy(dest_hbm.at[wid * n_it + it], idx_v)
            loads = [pltpu.async_copy(u_hbm.at[c].at[pl.ds(t0, SC_WINDOW)], rows_v.at[c], sem_in)
                     for c in range(ROW_CHUNKS)]
            for cp in loads:
                cp.wait()
            stores = [pltpu.async_copy(rows_v.at[c], xs_hbm.at[c].at[idx_v.at[kk]], sem_out)
                      for kk in range(TOP_K) for c in range(ROW_CHUNKS)]
            for cp in stores:
                cp.wait()

    return k(u2p, dest)


def _sc_combine(ys, dest):
    n_windows = dest.shape[0]
    T = n_windows * SC_WINDOW
    per_worker = T // SC_WORKERS
    n_it = per_worker // SC_WINDOW

    @functools.partial(
        pl.kernel, mesh=_sc_mesh(),
        out_type=jax.ShapeDtypeStruct((TOP_K, ROW_CHUNKS, T, LANES), jnp.uint32),
        scratch_types=[pltpu.VMEM((TOP_K, SC_WINDOW), jnp.int32),
                       pltpu.VMEM((SC_SLABS, SC_WINDOW, LANES), jnp.uint32),
                       pltpu.SemaphoreType.DMA, pltpu.SemaphoreType.DMA],
    )
    def k(ys_hbm, dest_hbm, yg_hbm, idx_v, rows_v, sem_in, sem_out):
        wid = lax.axis_index("s") * SC_CORES + lax.axis_index("c")

        @pl.loop(0, n_it)
        def _(it):
            t0 = wid * per_worker + it * SC_WINDOW
            pltpu.sync_copy(dest_hbm.at[wid * n_it + it], idx_v)
            for c in range(ROW_CHUNKS):
                for k0 in range(0, TOP_K, SC_SLABS):
                    loads = [pltpu.async_copy(ys_hbm.at[c].at[idx_v.at[k0 + j]], rows_v.at[j], sem_in)
                             for j in range(SC_SLABS)]
                    for cp in loads:
                        cp.wait()
                    stores = [pltpu.async_copy(rows_v.at[j], yg_hbm.at[k0 + j].at[c].at[pl.ds(t0, SC_WINDOW)], sem_out)
                              for j in range(SC_SLABS)]
                    for cp in stores:
                        cp.wait()

    return k(ys, dest)


def _final_kernel(h_ref, u_ref, yg_ref, wt_ref, p_ref, sgu_ref, sd_ref, pg_ref, pp_ref, o_ref):
    wt = wt_ref[...]
    r_lo = r_hi = None
    for k in range(TOP_K):
        lo, hi = _unpack_rows([yg_ref[k, c] for c in range(ROW_CHUNKS)])
        wk = wt[:, k:k + 1]
        r_lo = lo * wk if k == 0 else r_lo + lo * wk
        r_hi = hi * wk if k == 0 else r_hi + hi * wk
    routed = jnp.concatenate([r_lo, r_hi], axis=1)
    u_lo, u_hi = _unpack_rows([u_ref[c] for c in range(ROW_CHUNKS)])
    un = jnp.concatenate([u_lo, u_hi], axis=1).astype(bf16)
    gu = jnp.dot(un, sgu_ref[...], preferred_element_type=f32)
    g, u = gu[:, :D_SHARED], gu[:, D_SHARED:]
    shared = jnp.dot((g * jax.nn.sigmoid(g) * u).astype(bf16), sd_ref[...], preferred_element_type=f32)
    h = h_ref[...] + (routed + shared)
    gate = jax.nn.sigmoid(jnp.dot(h.astype(bf16), pg_ref[...], preferred_element_type=f32))
    emb = jnp.dot(p_ref[...].astype(bf16), pp_ref[...], preferred_element_type=f32)
    o_ref[...] = h + gate * emb


def _final(h1, u2p, yg, wts_t, p2, prm):
    T = h1.shape[0]
    tm = TOKEN_TILE
    row = lambda i: (i, 0)
    const = lambda i: (0, 0)
    full = lambda a: pl.BlockSpec(a.shape, const)
    return pl.pallas_call(
        _final_kernel,
        grid=(T // tm,),
        in_specs=[pl.BlockSpec((tm, D_MODEL), row),
                  pl.BlockSpec((ROW_CHUNKS, tm, LANES), lambda i: (0, i, 0)),
                  pl.BlockSpec((TOP_K, ROW_CHUNKS, tm, LANES), lambda i: (0, 0, i, 0)),
                  pl.BlockSpec((tm, TOP_K), row),
                  pl.BlockSpec((tm, PLE_DIM), row),
                  full(prm["shared_gu"]), full(prm["shared_down"]), full(prm["ple_gate"]), full(prm["ple_proj"])],
        out_specs=pl.BlockSpec((tm, D_MODEL), row),
        out_shape=jax.ShapeDtypeStruct((T, D_MODEL), f32),
        compiler_params=_cparams(("parallel",)),
        name="final",
    )(h1, u2p, yg, wts_t, p2, prm["shared_gu"], prm["shared_down"], prm["ple_gate"], prm["ple_proj"])


def _deinterleave(n_heads):
    base = jnp.concatenate([jnp.arange(0, HEAD_DIM, 2), jnp.arange(1, HEAD_DIM, 2)])
    return (jnp.arange(n_heads)[:, None] * HEAD_DIM + base[None, :]).reshape(-1)


def _rope_tables(ang):
    c, s = jnp.cos(ang), jnp.sin(ang)
    return jnp.tile(jnp.concatenate([c, c], axis=-1), (1, 2)), jnp.tile(jnp.concatenate([-s, s], axis=-1), (1, 2))


def _axial_angles(S):
    rows = S // GRID_W
    row = jnp.repeat(jnp.arange(rows, dtype=f32), GRID_W)
    col = jnp.tile(jnp.arange(GRID_W, dtype=f32), rows)
    half = HEAD_DIM // 2
    inv = ROPE_THETA ** (-jnp.arange(0, half, 2, dtype=f32) / half)
    return jnp.concatenate([row[:, None] * inv, col[:, None] * inv], axis=-1)


def _linear_angles(S):
    t = jnp.arange(S, dtype=f32)
    inv = ROPE_THETA ** (-jnp.arange(0, HEAD_DIM, 2, dtype=f32) / HEAD_DIM)
    return t[:, None] * inv


def _prepare(seq_lens, norm_mix, w_in, q_norm_a, k_norm_a, q_norm_b, k_norm_b, w_branch_a, w_branch_b, w_out,
             norm_ffn, router_w, router_bias, expert_gate, expert_up, expert_down, shared_gate, shared_up,
             shared_down, ple_proj, ple_gate):
    cuts = [A_WIDTH, A_WIDTH + A_KV_WIDTH, A_WIDTH + 2 * A_KV_WIDTH,
            A_WIDTH + 2 * A_KV_WIDTH + B_WIDTH, A_WIDTH + 2 * A_KV_WIDTH + 2 * B_WIDTH,
            A_WIDTH + 2 * A_KV_WIDTH + 3 * B_WIDTH, A_WIDTH + 2 * A_KV_WIDTH + 3 * B_WIDTH + D_MODEL]
    wqa, wka, wva, wqb, wkb, wvb, wga, wgb = jnp.split(w_in, cuts, axis=-1)
    wqa = wqa[:, _deinterleave(A_Q_HEADS)]
    wka = wka[:, _deinterleave(A_KV_HEADS)]
    wqb = wqb[:, _deinterleave(B_HEADS)]
    wkb = wkb[:, _deinterleave(B_HEADS)]
    k0, k1 = wka[:, :HEAD_DIM], wka[:, HEAD_DIM:]
    v0, v1 = wva[:, :HEAD_DIM], wva[:, HEAD_DIM:]
    vz = jnp.zeros_like(v0)
    w_all = jnp.concatenate([wqa, k0, k0, k1, k1, v0, vz, v1, vz, wqb, wkb, wvb, wga, wgb], axis=-1).astype(bf16)
    perm = _deinterleave(1)
    scale = HEAD_DIM ** -0.5
    tile = lambda g, n: jnp.tile(g[perm], n)[None, :]
    router_pad = jnp.pad(router_w, ((0, 0), (0, LANES - N_EXPERTS)))
    router_hi = router_pad.astype(bf16)
    tm = TOKEN_TILE
    prm = {
        "norm_mix": norm_mix[None, :], "w_in": w_all,
        "gqa": tile(q_norm_a, A_Q_HEADS) * (scale * LOG2_E), "gka": tile(k_norm_a, KA_COLS // HEAD_DIM),
        "gqb": tile(q_norm_b, B_HEADS) * (scale * LOG2_E), "gkb": tile(k_norm_b, B_HEADS),
        "bd": jnp.kron(jnp.eye(LANES // HEAD_DIM, dtype=f32), jnp.ones((HEAD_DIM, HEAD_DIM), f32)).astype(bf16),
        "w_branch_a": w_branch_a.astype(bf16), "w_branch_b": w_branch_b.astype(bf16), "w_out": w_out.astype(bf16),
        "norm_ffn": norm_ffn[None, :],
        "router_hi": router_hi, "router_lo": (router_pad - router_hi.astype(f32)).astype(bf16),
        "router_bias": router_bias[:, None],
        "tri": (jnp.arange(tm)[:, None] < jnp.arange(tm)[None, :]).astype(bf16),
        "expert_gu": jnp.concatenate([expert_gate, expert_up], axis=-1).astype(bf16),
        "expert_down": expert_down.astype(bf16),
        "shared_gu": jnp.concatenate([shared_gate, shared_up], axis=-1).astype(bf16),
        "shared_down": shared_down.astype(bf16),
        "ple_gate": ple_gate.astype(bf16), "ple_proj": ple_proj.astype(bf16),
        "cos_a": {}, "sin_a": {}, "cos_b": {}, "sin_b": {},
    }
    for S in seq_lens:
        prm["cos_a"][S], prm["sin_a"][S] = _rope_tables(_axial_angles(S))
        prm["cos_b"][S], prm["sin_b"][S] = _rope_tables(_linear_angles(S))
    return prm


def _dispatch_plan(idx, rank, counts, T):
    bm = MOE_ROWS
    n_rows = (T * TOP_K + N_EXPERTS * bm) // bm * bm
    n_blocks = n_rows // bm
    padded = (counts + bm - 1) // bm * bm
    pad_end = jnp.cumsum(padded)
    pad_start = pad_end - padded
    dest = pad_start[idx] + rank
    blk_start = jnp.arange(n_blocks, dtype=jnp.int32) * bm
    blk_e = jnp.minimum(jnp.sum(pad_end[None, :] <= blk_start[:, None], axis=1), N_EXPERTS - 1).astype(jnp.int32)
    n_used = (pad_end[-1:] // bm).astype(jnp.int32)
    n_valid = jnp.clip(counts[blk_e] - (blk_start - pad_start[blk_e]), 0, bm).astype(jnp.int32)
    return dest, n_rows, blk_e, n_used, n_valid


def _layer(x, p, prm):
    B, S, _ = x.shape
    T = B * S
    x2 = x.reshape(T, D_MODEL)
    qa, ka, va, *qkv_b, sga, sgb = _in_proj(x2, S, prm)
    shape3 = lambda a: a.reshape(B, S, a.shape[-1])
    ya = _flash(shape3(qa), shape3(ka), shape3(va)).reshape(T, A_WIDTH)
    n_dil = len(WINDOW_DILATIONS)
    outs, lses = [], []
    for j, (_, dil) in enumerate(WINDOW_DILATIONS):
        o_d, lse_d = _dilated(qkv_b[j], qkv_b[n_dil + j], qkv_b[2 * n_dil + j], B, S, dil)
        outs.append(o_d)
        lses.append(lse_d)
    h1, u2p, idx, wts, rank, cnt = _merge_route(x2, ya, outs, lses, sga, sgb, prm)
    dest, n_rows, blk_e, n_used, n_valid = _dispatch_plan(idx, rank, cnt[:, 0], T)
    xs = _sc_dispatch(u2p, dest, n_rows)
    ys = _experts(xs, blk_e, n_used, n_valid, prm)
    yg = _sc_combine(ys, dest)
    out = _final(h1, u2p, yg, wts.T, p.reshape(T, PLE_DIM), prm)
    return out.reshape(B, S, D_MODEL)


def kernel(x_prompt, x_sample, p_prompt, p_sample, norm_mix, w_in, q_norm_a, k_norm_a, q_norm_b, k_norm_b,
           w_branch_a, w_branch_b, w_out, norm_ffn, router_w, router_bias, expert_gate, expert_up, expert_down,
           shared_gate, shared_up, shared_down, ple_proj, ple_gate):
    weights = (norm_mix, w_in, q_norm_a, k_norm_a, q_norm_b, k_norm_b, w_branch_a, w_branch_b, w_out, norm_ffn,
               router_w, router_bias, expert_gate, expert_up, expert_down, shared_gate, shared_up, shared_down,
               ple_proj, ple_gate)
    depth = norm_mix.shape[0]
    seq_lens = sorted({x_prompt.shape[1], x_sample.shape[1]})
    hp, hs = x_prompt, x_sample
    for i in range(depth):
        prm = _prepare(seq_lens, *[w[i] for w in weights])
        hp = _layer(hp, p_prompt[i], prm)
        hs = _layer(hs, p_sample[i], prm)
    return hp, hs
```

```python
import functools

import jax
import jax.numpy as jnp
from jax import lax
from jax.experimental import pallas as pl
from jax.experimental.pallas import tpu as pltpu
from jax.experimental.pallas import tpu_sc as plsc

D_MODEL = 1024
HEAD_DIM = 64
A_Q_HEADS = 8
A_KV_HEADS = 2
B_HEADS = 6
GRID_W = 64
ROPE_THETA = 10000.0
WINDOW_DILATIONS = ((128, 1), (512, 4), (2048, 16))
N_EXPERTS = 64
TOP_K = 8
N_GROUPS = 8
TOPK_GROUPS = 4
D_EXPERT = 256
D_SHARED = 256
ROUTE_SCALE = 2.5
PLE_DIM = 256
NEG_BIG = -1e30
EPS = 1e-6
LOG2_E = 1.4426950408889634

A_WIDTH = A_Q_HEADS * HEAD_DIM
A_KV_WIDTH = A_KV_HEADS * HEAD_DIM
B_WIDTH = B_HEADS * HEAD_DIM
GROUP_Q = A_Q_HEADS // A_KV_HEADS
EXPERTS_PER_GROUP = N_EXPERTS // N_GROUPS

LANES = 128
NORM_COLS = 256
HALF_WINDOW = 64
KEY_SPAN = 256

KA_COLS = 2 * LANES
VA_COLS = 2 * LANES
OFF_QA = 0
OFF_KA = OFF_QA + A_WIDTH
OFF_VA = OFF_KA + KA_COLS
OFF_QB = OFF_VA + VA_COLS
OFF_KB = OFF_QB + B_WIDTH
OFF_VB = OFF_KB + B_WIDTH
OFF_GA = OFF_VB + B_WIDTH
OFF_GB = OFF_GA + D_MODEL
PROJ_COLS = OFF_GB + D_MODEL

TOKEN_TILE = 512
MERGE_SUB = 256
FLASH_TQ = 256
FLASH_TK = 1024
DIL_QBLOCK = 128
DIL_STEP_TOKENS = 2048
MOE_ROWS = 1024
MOE_SUB_ROWS = 256

VMEM_LIMIT = 52 * 1024 * 1024

HALF_MODEL = D_MODEL // 2
ROW_CHUNKS = HALF_MODEL // LANES
SC_CORES = 2
SC_SUBCORES = 16
SC_WORKERS = SC_CORES * SC_SUBCORES
SC_WINDOW = 128
SC_SLABS = 4

bf16 = jnp.bfloat16
f32 = jnp.float32


def _cparams(sem):
    return pltpu.CompilerParams(dimension_semantics=sem, vmem_limit_bytes=VMEM_LIMIT)


def _pack_rows(v):
    lo = lax.bitcast_convert_type(v[:, :HALF_MODEL].astype(bf16).astype(f32), jnp.uint32) >> 16
    hi = lax.bitcast_convert_type(v[:, HALF_MODEL:].astype(bf16).astype(f32), jnp.uint32) & jnp.uint32(0xFFFF0000)
    w = lo | hi
    return [w[:, c * LANES:(c + 1) * LANES] for c in range(ROW_CHUNKS)]


def _unpack_rows(chunks):
    w = jnp.concatenate(chunks, axis=1)
    lo = lax.bitcast_convert_type(w << 16, f32)
    hi = lax.bitcast_convert_type(w & jnp.uint32(0xFFFF0000), f32)
    return lo, hi


def _in_proj_kernel(x_ref, nm_ref, w_ref, gqa_ref, gka_ref, gqb_ref, gkb_ref,
                    ca_ref, sa_ref, cb_ref, sb_ref, bd_ref,
                    qa_ref, ka_ref, va_ref, qb1_ref, qb4_ref, qb16_ref, kb1_ref, kb4_ref, kb16_ref,
                    vb1_ref, vb4_ref, vb16_ref, ga_ref, gb_ref, st_ref):
    x = x_ref[...]
    r = lax.rsqrt(jnp.mean(x * x, axis=-1, keepdims=True) + EPS)
    u = (x * r * nm_ref[...]).astype(bf16)
    rows = x.shape[0]
    lane = lax.broadcasted_iota(jnp.int32, (rows, LANES), 1)
    first_half = (lane % HEAD_DIM) < (HEAD_DIM // 2)

    def proj(off, width):
        return jnp.dot(u, w_ref[:, off:off + width], preferred_element_type=f32)

    def norm_rope(off, width, g_ref, c_ref, s_ref):
        z = proj(off, width)
        c = c_ref[...]
        s = s_ref[...]
        chunks = []
        for c0 in range(0, width, NORM_COLS):
            cw = min(NORM_COLS, width - c0)
            zc = z[:, c0:c0 + cw]
            ms = jnp.dot((zc * zc).astype(bf16), bd_ref[:cw, :cw], preferred_element_type=f32) * (1.0 / HEAD_DIM)
            zn = zc * lax.rsqrt(ms + EPS) * g_ref[:, c0:c0 + cw]
            for h0 in range(0, cw, LANES):
                zh = zn[:, h0:h0 + LANES]
                sw = jnp.where(first_half, pltpu.roll(zh, LANES - HEAD_DIM // 2, 1), pltpu.roll(zh, HEAD_DIM // 2, 1))
                chunks.append(zh * c + sw * s)
        return chunks

    def store(chunks, out_ref):
        for j, ch in enumerate(chunks):
            out_ref[:, j * LANES:(j + 1) * LANES] = ch.astype(out_ref.dtype)

    def store_dilated(chunks, out_refs):
        for j, ch in enumerate(chunks):
            st_ref[j] = ch
        for (_, dil), out_ref in zip(WINDOW_DILATIONS, out_refs):
            for r in range(dil):
                for j in range(len(chunks)):
                    src = pl.ds(r, rows // dil, stride=dil) if dil > 1 else pl.ds(0, rows)
                    out_ref[r, :, j * LANES:(j + 1) * LANES] = st_ref[j, src, :].astype(out_ref.dtype)

    store(norm_rope(OFF_QA, A_WIDTH, gqa_ref, ca_ref, sa_ref), qa_ref)
    store(norm_rope(OFF_KA, KA_COLS, gka_ref, ca_ref, sa_ref), ka_ref)
    va = proj(OFF_VA, VA_COLS)
    va_ref[...] = jnp.concatenate(
        [jnp.where(lane < HEAD_DIM, va[:, c0:c0 + LANES], 1.0) for c0 in range(0, VA_COLS, LANES)],
        axis=1).astype(va_ref.dtype)
    store_dilated(norm_rope(OFF_QB, B_WIDTH, gqb_ref, cb_ref, sb_ref), (qb1_ref, qb4_ref, qb16_ref))
    store_dilated(norm_rope(OFF_KB, B_WIDTH, gkb_ref, cb_ref, sb_ref), (kb1_ref, kb4_ref, kb16_ref))
    vb = proj(OFF_VB, B_WIDTH)
    store_dilated([vb[:, c0:c0 + LANES] for c0 in range(0, B_WIDTH, LANES)], (vb1_ref, vb4_ref, vb16_ref))
    ga_ref[...] = jax.nn.sigmoid(proj(OFF_GA, D_MODEL)).astype(ga_ref.dtype)
    gb_ref[...] = jax.nn.sigmoid(proj(OFF_GB, D_MODEL)).astype(gb_ref.dtype)


def _in_proj(x2, S, prm):
    T = x2.shape[0]
    tm = TOKEN_TILE
    n_pos = S // tm
    row = lambda i: (i, 0)
    pos = lambda i: (i % n_pos, 0)
    const = lambda i: (0, 0)
    dils = [dil for _, dil in WINDOW_DILATIONS]
    flat = lambda w: (pl.BlockSpec((tm, w), row), jax.ShapeDtypeStruct((T, w), bf16))
    strided = lambda d: (pl.BlockSpec((d, tm // d, B_WIDTH), lambda i: (0, i, 0)),
                         jax.ShapeDtypeStruct((d, T // d, B_WIDTH), bf16))
    outs = ([flat(A_WIDTH), flat(KA_COLS), flat(VA_COLS)] + 3 * [strided(d) for d in dils]
            + [flat(D_MODEL), flat(D_MODEL)])
    return pl.pallas_call(
        _in_proj_kernel,
        grid=(T // tm,),
        in_specs=[
            pl.BlockSpec((tm, D_MODEL), row),
            pl.BlockSpec((1, D_MODEL), const),
            pl.BlockSpec((D_MODEL, PROJ_COLS), const),
            pl.BlockSpec((1, A_WIDTH), const),
            pl.BlockSpec((1, KA_COLS), const),
            pl.BlockSpec((1, B_WIDTH), const),
            pl.BlockSpec((1, B_WIDTH), const),
            pl.BlockSpec((tm, LANES), pos),
            pl.BlockSpec((tm, LANES), pos),
            pl.BlockSpec((tm, LANES), pos),
            pl.BlockSpec((tm, LANES), pos),
            pl.BlockSpec((NORM_COLS, NORM_COLS), const),
        ],
        out_specs=[spec for spec, _ in outs],
        out_shape=[shape for _, shape in outs],
        scratch_shapes=[pltpu.VMEM((B_WIDTH // LANES, tm, LANES), f32)],
        compiler_params=_cparams(("parallel",)),
        name="in_proj",
    )(x2, prm["norm_mix"], prm["w_in"], prm["gqa"], prm["gka"], prm["gqb"], prm["gkb"],
      prm["cos_a"][S], prm["sin_a"][S], prm["cos_b"][S], prm["sin_b"][S], prm["bd"])


def _flash_kernel(q_ref, k_ref, v_ref, o_ref, s_ref, *, tq, tk, seq):
    lane = lax.broadcasted_iota(jnp.int32, (tq, LANES), 1)
    lo = lane < HEAD_DIM
    q = q_ref[0]
    q01, q23 = q[:, :LANES], q[:, LANES:]
    zero = jnp.zeros_like(q01)
    qs = [jnp.where(lo, q01, zero), jnp.where(lo, zero, q01), jnp.where(lo, q23, zero), jnp.where(lo, zero, q23)]
    n_chunks = seq // tk

    def keys(chunk):
        return k_ref[0, pl.ds(pl.multiple_of(chunk * tk, tk), tk), :]

    def score(h, kk):
        return lax.dot_general(qs[h], kk, (((1,), (1,)), ((), ())), preferred_element_type=f32)

    def attend(chunk, slot, ms, accs, next_chunk):
        vv = v_ref[0, pl.ds(pl.multiple_of(chunk * tk, tk), tk), :]
        kk = None if next_chunk is None else keys(next_chunk)
        new_ms, new_accs = [], []
        for h in range(GROUP_Q):
            if kk is not None:
                s_ref[1 - slot, h] = score(h, kk)
            s = s_ref[slot, h]
            m_new = jnp.maximum(ms[h], jnp.max(s, axis=-1, keepdims=True))
            p = jnp.exp2((s - m_new).astype(bf16))
            alpha = jnp.exp2(ms[h] - m_new)
            new_accs.append(accs[h] * alpha + jnp.dot(p, vv, preferred_element_type=f32))
            new_ms.append(m_new)
        return tuple(new_ms), tuple(new_accs)

    kk0 = keys(0)
    for h in range(GROUP_Q):
        s_ref[0, h] = score(h, kk0)

    def body(jj, carry):
        ms, accs = attend(2 * jj, 0, *carry, 2 * jj + 1)
        return attend(2 * jj + 1, 1, ms, accs, 2 * jj + 2)

    m0 = tuple(jnp.full((tq, 1), NEG_BIG, f32) for _ in range(GROUP_Q))
    acc0 = tuple(jnp.zeros((tq, LANES), f32) for _ in range(GROUP_Q))
    ms, accs = lax.fori_loop(0, n_chunks // 2 - 1, body, (m0, acc0))
    ms, accs = attend(n_chunks - 2, 0, ms, accs, n_chunks - 1)
    _, accs = attend(n_chunks - 1, 1, ms, accs, None)
    heads = [acc / jnp.where(lo, pltpu.roll(acc, HEAD_DIM, 1), 1.0) for acc in accs]
    out01 = jnp.where(lo, heads[0], pltpu.roll(heads[1], HEAD_DIM, 1))
    out23 = jnp.where(lo, heads[2], pltpu.roll(heads[3], HEAD_DIM, 1))
    o_ref[0] = jnp.concatenate([out01, out23], axis=1).astype(o_ref.dtype)


def _flash(qa, ka, va):
    B, S, _ = qa.shape
    tq, tk = FLASH_TQ, FLASH_TK
    return pl.pallas_call(
        functools.partial(_flash_kernel, tq=tq, tk=tk, seq=S),
        grid=(B, A_KV_HEADS, S // tq),
        in_specs=[
            pl.BlockSpec((1, tq, GROUP_Q * HEAD_DIM), lambda b, g, i: (b, i, g)),
            pl.BlockSpec((1, S, LANES), lambda b, g, i: (b, 0, g)),
            pl.BlockSpec((1, S, LANES), lambda b, g, i: (b, 0, g)),
        ],
        out_specs=pl.BlockSpec((1, tq, GROUP_Q * HEAD_DIM), lambda b, g, i: (b, i, g)),
        out_shape=jax.ShapeDtypeStruct((B, S, A_WIDTH), bf16),
        scratch_shapes=[pltpu.VMEM((2, GROUP_Q, tq, tk), f32)],
        compiler_params=_cparams(("parallel", "parallel", "arbitrary")),
        name="flash_gqa",
    )(qa, ka, va)


def _dilated_kernel(q_ref, k_ref, v_ref, o_ref, lse_ref, so_ref, sl_ref, *, length, tms, dil):
    i = pl.program_id(1)
    lane = lax.broadcasted_iota(jnp.int32, (DIL_QBLOCK, LANES), 1)
    lo = lane < HEAD_DIM
    n_pairs = B_WIDTH // LANES

    def unit(u, carry):
        sb, r = u // dil, u % dil
        r0 = pl.multiple_of(sb * DIL_QBLOCK, DIL_QBLOCK)
        m0 = i * tms + r0
        start = pl.multiple_of(jnp.clip(m0 - HALF_WINDOW, 0, length - KEY_SPAN), HALF_WINDOW)
        qpos = m0 + lax.broadcasted_iota(jnp.int32, (DIL_QBLOCK, KEY_SPAN), 0)
        kpos = start + lax.broadcasted_iota(jnp.int32, (DIL_QBLOCK, KEY_SPAN), 1)
        valid = jnp.abs(kpos - qpos) <= HALF_WINDOW
        rows = pl.ds(r0 * dil + r, DIL_QBLOCK, stride=dil) if dil > 1 else pl.ds(r0, DIL_QBLOCK)
        scores = []
        for c in range(n_pairs):
            cs = slice(c * LANES, (c + 1) * LANES)
            qp = q_ref[r, 0, pl.ds(r0, DIL_QBLOCK), cs]
            kp = k_ref[r, 0, pl.ds(start, KEY_SPAN), cs]
            zero = jnp.zeros_like(qp)
            for half in range(2):
                qm = jnp.where(lo, qp, zero) if half == 0 else jnp.where(lo, zero, qp)
                scores.append(lax.dot_general(qm, kp, (((1,), (1,)), ((), ())), preferred_element_type=f32))
        for c in range(n_pairs):
            vp = v_ref[r, 0, pl.ds(start, KEY_SPAN), c * LANES:(c + 1) * LANES]
            outs, lses = [], []
            for half in range(2):
                s = jnp.where(valid, scores[2 * c + half], NEG_BIG)
                m = jnp.max(s, axis=-1, keepdims=True)
                e = jnp.exp2(s - m)
                l = jnp.sum(e, axis=-1, keepdims=True)
                outs.append(jnp.dot(e.astype(bf16), vp, preferred_element_type=f32) * (1.0 / l))
                lses.append(m + jnp.log2(l))
            so_ref[c, rows, :] = jnp.where(lo, outs[0], outs[1])
            sl_ref[c, rows, :] = jnp.where(lo, lses[0], lses[1])
        return carry

    lax.fori_loop(0, (tms // DIL_QBLOCK) * dil, unit, 0, unroll=4)
    o_ref[0] = jnp.concatenate([so_ref[c] for c in range(n_pairs)], axis=1).astype(o_ref.dtype)
    lse_ref[0] = jnp.concatenate([sl_ref[c] for c in range(n_pairs)], axis=1)


def _dilated(qb, kb, vb, B, S, dil):
    length = S // dil
    tms = DIL_STEP_TOKENS // dil
    view = lambda a: a.reshape(dil, B, length, B_WIDTH)
    o, lse = pl.pallas_call(
        functools.partial(_dilated_kernel, length=length, tms=tms, dil=dil),
        grid=(B, length // tms),
        in_specs=[
            pl.BlockSpec((dil, 1, tms, B_WIDTH), lambda b, i: (0, b, i, 0)),
            pl.BlockSpec((dil, 1, length, B_WIDTH), lambda b, i: (0, b, 0, 0)),
            pl.BlockSpec((dil, 1, length, B_WIDTH), lambda b, i: (0, b, 0, 0)),
        ],
        out_specs=[
            pl.BlockSpec((1, tms * dil, B_WIDTH), lambda b, i: (b, i, 0)),
            pl.BlockSpec((1, tms * dil, B_WIDTH), lambda b, i: (b, i, 0)),
        ],
        out_shape=[
            jax.ShapeDtypeStruct((B, S, B_WIDTH), bf16),
            jax.ShapeDtypeStruct((B, S, B_WIDTH), f32),
        ],
        scratch_shapes=[pltpu.VMEM((B_WIDTH // LANES, tms * dil, LANES), f32),
                        pltpu.VMEM((B_WIDTH // LANES, tms * dil, LANES), f32)],
        compiler_params=_cparams(("parallel", "arbitrary")),
        name=f"dilated_{dil}",
    )(view(qb), view(kb), view(vb))
    return o.reshape(B * S, B_WIDTH), lse.reshape(B * S, B_WIDTH)


def _first_index_of_max(cur, idx, n):
    mx = jnp.max(cur, axis=0, keepdims=True)
    return jnp.min(jnp.where(cur == mx, idx, n), axis=0, keepdims=True)


def _merge_route_kernel(x_ref, ya_ref, o1_ref, o2_ref, o3_ref, l1_ref, l2_ref, l3_ref, ga_ref, gb_ref,
                        wa_ref, wb_ref, wo_ref, nf_ref, rhi_ref, rlo_ref, bias_ref, tri_ref,
                        h_ref, u_ref, idx_ref, wt_ref, rank_ref, cnt_ref, run_ref):
    step = pl.program_id(0)

    @pl.when(step == 0)
    def _():
        run_ref[...] = jnp.zeros_like(run_ref)

    tm = x_ref.shape[0]
    sub = tri_ref.shape[0]

    def dense(j):
        rs = slice(j * sub, (j + 1) * sub)
        l1, l2, l3 = l1_ref[rs, :], l2_ref[rs, :], l3_ref[rs, :]
        lm = jnp.maximum(jnp.maximum(l1, l2), l3)
        e1, e2, e3 = jnp.exp2(l1 - lm), jnp.exp2(l2 - lm), jnp.exp2(l3 - lm)
        yb = ((e1 * o1_ref[rs, :].astype(f32) + e2 * o2_ref[rs, :].astype(f32) + e3 * o3_ref[rs, :].astype(f32))
              / (e1 + e2 + e3))
        pa = jnp.dot(ya_ref[rs, :], wa_ref[...], preferred_element_type=f32)
        pb = jnp.dot(yb.astype(bf16), wb_ref[...], preferred_element_type=f32)
        merged = ga_ref[rs, :].astype(f32) * pa + gb_ref[rs, :].astype(f32) * pb
        h = x_ref[rs, :] + jnp.dot(merged.astype(bf16), wo_ref[...], preferred_element_type=f32)
        h_ref[rs, :] = h
        r = lax.rsqrt(jnp.mean(h * h, axis=-1, keepdims=True) + EPS)
        uf = h * r * nf_ref[...]
        u_hi = uf.astype(bf16)
        for c, chunk in enumerate(_pack_rows(uf)):
            u_ref[c, rs, :] = chunk
        u_lo = (uf - u_hi.astype(f32)).astype(bf16)
        return (jnp.dot(u_hi, rhi_ref[...], preferred_element_type=f32)
                + jnp.dot(u_lo, rhi_ref[...], preferred_element_type=f32)
                + jnp.dot(u_hi, rlo_ref[...], preferred_element_type=f32))

    def route(j, logits):
        scores = jax.nn.sigmoid(logits.T[:N_EXPERTS, :])
        sel = scores + bias_ref[...]
        i8 = lax.broadcasted_iota(jnp.int32, (EXPERTS_PER_GROUP, sub), 0)
        neg_inf = jnp.float32(-jnp.inf)
        gscore = jnp.zeros((N_GROUPS, sub), f32)
        for g in range(N_GROUPS):
            blk = sel[g * EXPERTS_PER_GROUP:(g + 1) * EXPERTS_PER_GROUP, :]
            top1 = jnp.max(blk, axis=0, keepdims=True)
            first = _first_index_of_max(blk, i8, EXPERTS_PER_GROUP)
            top2 = jnp.max(jnp.where(i8 == first, neg_inf, blk), axis=0, keepdims=True)
            gscore = jnp.where(i8 == g, top1 + top2, gscore)
        gkeep = jnp.zeros((N_GROUPS, sub), jnp.bool_)
        cur = gscore
        for _ in range(TOPK_GROUPS):
            pick = i8 == _first_index_of_max(cur, i8, N_GROUPS)
            gkeep = jnp.logical_or(gkeep, pick)
            cur = jnp.where(pick, neg_inf, cur)
        cur = jnp.concatenate(
            [jnp.where(gkeep[g:g + 1, :], sel[g * EXPERTS_PER_GROUP:(g + 1) * EXPERTS_PER_GROUP, :], NEG_BIG)
             for g in range(N_GROUPS)], axis=0)

        ie = lax.broadcasted_iota(jnp.int32, (N_EXPERTS, sub), 0)
        ik = lax.broadcasted_iota(jnp.int32, (TOP_K, sub), 0)
        idx_out = jnp.zeros((TOP_K, sub), jnp.int32)
        w_out = jnp.zeros((TOP_K, sub), f32)
        chosen = jnp.zeros((N_EXPERTS, sub), jnp.bool_)
        for k in range(TOP_K):
            fi = _first_index_of_max(cur, ie, N_EXPERTS)
            pick = ie == fi
            wk = jnp.sum(jnp.where(pick, scores, 0.0), axis=0, keepdims=True)
            idx_out = jnp.where(ik == k, fi, idx_out)
            w_out = jnp.where(ik == k, wk, w_out)
            chosen = jnp.logical_or(chosen, pick)
            cur = jnp.where(pick, neg_inf, cur)
        w_out = w_out / jnp.sum(w_out, axis=0, keepdims=True) * ROUTE_SCALE
        wt_ref[:, j * sub:(j + 1) * sub] = w_out

        sel01 = jnp.where(chosen, 1.0, 0.0)
        before = jnp.dot(sel01.astype(bf16), tri_ref[...], preferred_element_type=f32) + run_ref[:, 0:1]
        rank_out = jnp.zeros((TOP_K, sub), f32)
        for k in range(TOP_K):
            rk = jnp.sum(jnp.where(ie == idx_out[k:k + 1, :], before, 0.0), axis=0, keepdims=True)
            rank_out = jnp.where(ik == k, rk, rank_out)
        rank_i = rank_out.astype(jnp.int32)
        for w in range(sub // SC_WINDOW):
            win = j * (sub // SC_WINDOW) + w
            idx_ref[win] = idx_out[:, w * SC_WINDOW:(w + 1) * SC_WINDOW]
            rank_ref[win] = rank_i[:, w * SC_WINDOW:(w + 1) * SC_WINDOW]
        run_ref[...] = run_ref[...] + jnp.sum(sel01, axis=1, keepdims=True)

    logits = [dense(j) for j in range(tm // sub)]
    for j in range(tm // sub):
        route(j, logits[j])
    cnt_ref[...] = run_ref[...].astype(jnp.int32)


def _merge_route(x2, ya, outs, lses, sga, sgb, prm):
    T = x2.shape[0]
    tm = TOKEN_TILE
    row = lambda i: (i, 0)
    col = lambda i: (0, i)
    const = lambda i: (0, 0)
    tok = lambda w: pl.BlockSpec((tm, w), row)
    full = lambda a: pl.BlockSpec(a.shape, const)
    win = pl.BlockSpec((tm // SC_WINDOW, TOP_K, SC_WINDOW), lambda i: (i, 0, 0))
    return pl.pallas_call(
        _merge_route_kernel,
        grid=(T // tm,),
        in_specs=[tok(D_MODEL), tok(A_WIDTH), tok(B_WIDTH), tok(B_WIDTH), tok(B_WIDTH),
                  tok(B_WIDTH), tok(B_WIDTH), tok(B_WIDTH), tok(D_MODEL), tok(D_MODEL),
                  full(prm["w_branch_a"]), full(prm["w_branch_b"]), full(prm["w_out"]), full(prm["norm_ffn"]),
                  full(prm["router_hi"]), full(prm["router_lo"]), full(prm["router_bias"]), full(prm["tri"])],
        out_specs=[tok(D_MODEL), pl.BlockSpec((ROW_CHUNKS, tm, LANES), lambda i: (0, i, 0)),
                   win, pl.BlockSpec((TOP_K, tm), col), win,
                   pl.BlockSpec((N_EXPERTS, LANES), const)],
        out_shape=[jax.ShapeDtypeStruct((T, D_MODEL), f32), jax.ShapeDtypeStruct((ROW_CHUNKS, T, LANES), jnp.uint32),
                   jax.ShapeDtypeStruct((T // SC_WINDOW, TOP_K, SC_WINDOW), jnp.int32),
                   jax.ShapeDtypeStruct((TOP_K, T), f32),
                   jax.ShapeDtypeStruct((T // SC_WINDOW, TOP_K, SC_WINDOW), jnp.int32),
                   jax.ShapeDtypeStruct((N_EXPERTS, LANES), jnp.int32)],
        scratch_shapes=[pltpu.VMEM((N_EXPERTS, LANES), f32)],
        compiler_params=_cparams(("arbitrary",)),
        name="merge_route",
    )(x2, ya, *outs, *lses, sga, sgb,
      prm["w_branch_a"], prm["w_branch_b"], prm["w_out"], prm["norm_ffn"],
      prm["router_hi"], prm["router_lo"], prm["router_bias"], prm["tri"])


def _experts_kernel(be_ref, nu_ref, nv_ref, x_ref, wgu_ref, wd_ref, y_ref):
    del be_ref
    i = pl.program_id(0)

    @pl.when(i < nu_ref[0])
    def _():
        n_sub = x_ref.shape[1] // MOE_SUB_ROWS

        def up(j):
            r0 = j * MOE_SUB_ROWS
            lo, hi = _unpack_rows([x_ref[c, r0:r0 + MOE_SUB_ROWS, :] for c in range(ROW_CHUNKS)])
            live = r0 + lax.broadcasted_iota(jnp.int32, lo.shape, 0) < nv_ref[i]
            x = jnp.concatenate([jnp.where(live, lo, 0.0), jnp.where(live, hi, 0.0)], axis=1).astype(bf16)
            return jnp.dot(x, wgu_ref[0], preferred_element_type=f32)

        def down(j, gu):
            g, u = gu[:, :D_EXPERT], gu[:, D_EXPERT:]
            a = g * jax.nn.sigmoid(g) * u
            y = jnp.dot(a.astype(bf16), wd_ref[0], preferred_element_type=f32)
            r0 = j * MOE_SUB_ROWS
            for c, chunk in enumerate(_pack_rows(y)):
                y_ref[c, r0:r0 + MOE_SUB_ROWS, :] = chunk

        gu_prev = up(0)
        for j in range(1, n_sub):
            gu_next = up(j)
            down(j - 1, gu_prev)
            gu_prev = gu_next
        down(n_sub - 1, gu_prev)

    @pl.when(i >= nu_ref[0])
    def _():
        y_ref[...] = jnp.zeros_like(y_ref)


def _experts(xs, blk_e, n_used, n_valid, prm):
    n_rows = xs.shape[1]
    bm = MOE_ROWS
    rows = lambda i, be, nu, nv: (0, jnp.minimum(i, nu[0] - 1), 0)
    return pl.pallas_call(
        _experts_kernel,
        grid_spec=pltpu.PrefetchScalarGridSpec(
            num_scalar_prefetch=3,
            grid=(n_rows // bm,),
            in_specs=[
                pl.BlockSpec((ROW_CHUNKS, bm, LANES), rows),
                pl.BlockSpec((1, D_MODEL, 2 * D_EXPERT), lambda i, be, nu, nv: (be[i], 0, 0)),
                pl.BlockSpec((1, D_EXPERT, D_MODEL), lambda i, be, nu, nv: (be[i], 0, 0)),
            ],
            out_specs=pl.BlockSpec((ROW_CHUNKS, bm, LANES), lambda i, be, nu, nv: (0, i, 0)),
        ),
        out_shape=jax.ShapeDtypeStruct((ROW_CHUNKS, n_rows, LANES), jnp.uint32),
        compiler_params=_cparams(("arbitrary",)),
        name="experts",
    )(blk_e, n_used, n_valid, xs, prm["expert_gu"], prm["expert_down"])


def _sc_mesh():
    return plsc.VectorSubcoreMesh(core_axis_name="c", subcore_axis_name="s")


def _sc_dispatch(u2p, dest, n_rows):
    T = u2p.shape[1]
    per_worker = T // SC_WORKERS
    n_it = per_worker // SC_WINDOW

    @functools.partial(
        pl.kernel, mesh=_sc_mesh(),
        out_type=jax.ShapeDtypeStruct((ROW_CHUNKS, n_rows, LANES), jnp.uint32),
        scratch_types=[pltpu.VMEM((TOP_K, SC_WINDOW), jnp.int32),
                       pltpu.VMEM((ROW_CHUNKS, SC_WINDOW, LANES), jnp.uint32),
                       pltpu.SemaphoreType.DMA, pltpu.SemaphoreType.DMA],
    )
    def k(u_hbm, dest_hbm, xs_hbm, idx_v, rows_v, sem_in, sem_out):
        wid = lax.axis_index("s") * SC_CORES + lax.axis_index("c")

        @pl.loop(0, n_it)
        def _(it):
            t0 = wid * per_worker + it * SC_WINDOW
            pltpu.sync_copy(dest_hbm.at[wid * n_it + it], idx_v)
            loads = [pltpu.async_copy(u_hbm.at[c].at[pl.ds(t0, SC_WINDOW)], rows_v.at[c], sem_in)
                     for c in range(ROW_CHUNKS)]
            for cp in loads:
                cp.wait()
            stores = [pltpu.async_copy(rows_v.at[c], xs_hbm.at[c].at[idx_v.at[kk]], sem_out)
                      for kk in range(TOP_K) for c in range(ROW_CHUNKS)]
            for cp in stores:
                cp.wait()

    return k(u2p, dest)


def _sc_combine(ys, dest):
    n_windows = dest.shape[0]
    T = n_windows * SC_WINDOW
    per_worker = T // SC_WORKERS
    n_it = per_worker // SC_WINDOW

    @functools.partial(
        pl.kernel, mesh=_sc_mesh(),
        out_type=jax.ShapeDtypeStruct((TOP_K, ROW_CHUNKS, T, LANES), jnp.uint32),
        scratch_types=[pltpu.VMEM((TOP_K, SC_WINDOW), jnp.int32),
                       pltpu.VMEM((SC_SLABS, SC_WINDOW, LANES), jnp.uint32),
                       pltpu.SemaphoreType.DMA, pltpu.SemaphoreType.DMA],
    )
    def k(ys_hbm, dest_hbm, yg_hbm, idx_v, rows_v, sem_in, sem_out):
        wid = lax.axis_index("s") * SC_CORES + lax.axis_index("c")

        @pl.loop(0, n_it)
        def _(it):
            t0 = wid * per_worker + it * SC_WINDOW
            pltpu.sync_copy(dest_hbm.at[wid * n_it + it], idx_v)
            for c in range(ROW_CHUNKS):
                for k0 in range(0, TOP_K, SC_SLABS):
                    loads = [pltpu.async_copy(ys_hbm.at[c].at[idx_v.at[k0 + j]], rows_v.at[j], sem_in)
                             for j in range(SC_SLABS)]
                    for cp in loads:
                        cp.wait()
                    stores = [pltpu.async_copy(rows_v.at[j], yg_hbm.at[k0 + j].at[c].at[pl.ds(t0, SC_WINDOW)], sem_out)
                              for j in range(SC_SLABS)]
                    for cp in stores:
                        cp.wait()

    return k(ys, dest)


def _final_kernel(h_ref, u_ref, yg_ref, wt_ref, p_ref, sgu_ref, sd_ref, pg_ref, pp_ref, o_ref):
    wt = wt_ref[...]
    r_lo = r_hi = None
    for k in range(TOP_K):
        lo, hi = _unpack_rows([yg_ref[k, c] for c in range(ROW_CHUNKS)])
        wk = wt[:, k:k + 1]
        r_lo = lo * wk if k == 0 else r_lo + lo * wk
        r_hi = hi * wk if k == 0 else r_hi + hi * wk
    routed = jnp.concatenate([r_lo, r_hi], axis=1)
    u_lo, u_hi = _unpack_rows([u_ref[c] for c in range(ROW_CHUNKS)])
    un = jnp.concatenate([u_lo, u_hi], axis=1).astype(bf16)
    gu = jnp.dot(un, sgu_ref[...], preferred_element_type=f32)
    g, u = gu[:, :D_SHARED], gu[:, D_SHARED:]
    shared = jnp.dot((g * jax.nn.sigmoid(g) * u).astype(bf16), sd_ref[...], preferred_element_type=f32)
    h = h_ref[...] + (routed + shared)
    gate = jax.nn.sigmoid(jnp.dot(h.astype(bf16), pg_ref[...], preferred_element_type=f32))
    emb = jnp.dot(p_ref[...].astype(bf16), pp_ref[...], preferred_element_type=f32)
    o_ref[...] = h + gate * emb


def _final(h1, u2p, yg, wts_t, p2, prm):
    T = h1.shape[0]
    tm = TOKEN_TILE
    row = lambda i: (i, 0)
    const = lambda i: (0, 0)
    full = lambda a: pl.BlockSpec(a.shape, const)
    return pl.pallas_call(
        _final_kernel,
        grid=(T // tm,),
        in_specs=[pl.BlockSpec((tm, D_MODEL), row),
                  pl.BlockSpec((ROW_CHUNKS, tm, LANES), lambda i: (0, i, 0)),
                  pl.BlockSpec((TOP_K, ROW_CHUNKS, tm, LANES), lambda i: (0, 0, i, 0)),
                  pl.BlockSpec((tm, TOP_K), row),
                  pl.BlockSpec((tm, PLE_DIM), row),
                  full(prm["shared_gu"]), full(prm["shared_down"]), full(prm["ple_gate"]), full(prm["ple_proj"])],
        out_specs=pl.BlockSpec((tm, D_MODEL), row),
        out_shape=jax.ShapeDtypeStruct((T, D_MODEL), f32),
        compiler_params=_cparams(("parallel",)),
        name="final",
    )(h1, u2p, yg, wts_t, p2, prm["shared_gu"], prm["shared_down"], prm["ple_gate"], prm["ple_proj"])


def _deinterleave(n_heads):
    base = jnp.concatenate([jnp.arange(0, HEAD_DIM, 2), jnp.arange(1, HEAD_DIM, 2)])
    return (jnp.arange(n_heads)[:, None] * HEAD_DIM + base[None, :]).reshape(-1)


def _rope_tables(ang):
    c, s = jnp.cos(ang), jnp.sin(ang)
    return jnp.tile(jnp.concatenate([c, c], axis=-1), (1, 2)), jnp.tile(jnp.concatenate([-s, s], axis=-1), (1, 2))


def _axial_angles(S):
    rows = S // GRID_W
    row = jnp.repeat(jnp.arange(rows, dtype=f32), GRID_W)
    col = jnp.tile(jnp.arange(GRID_W, dtype=f32), rows)
    half = HEAD_DIM // 2
    inv = ROPE_THETA ** (-jnp.arange(0, half, 2, dtype=f32) / half)
    return jnp.concatenate([row[:, None] * inv, col[:, None] * inv], axis=-1)


def _linear_angles(S):
    t = jnp.arange(S, dtype=f32)
    inv = ROPE_THETA ** (-jnp.arange(0, HEAD_DIM, 2, dtype=f32) / HEAD_DIM)
    return t[:, None] * inv


def _prepare(seq_lens, norm_mix, w_in, q_norm_a, k_norm_a, q_norm_b, k_norm_b, w_branch_a, w_branch_b, w_out,
             norm_ffn, router_w, router_bias, expert_gate, expert_up, expert_down, shared_gate, shared_up,
             shared_down, ple_proj, ple_gate):
    cuts = [A_WIDTH, A_WIDTH + A_KV_WIDTH, A_WIDTH + 2 * A_KV_WIDTH,
            A_WIDTH + 2 * A_KV_WIDTH + B_WIDTH, A_WIDTH + 2 * A_KV_WIDTH + 2 * B_WIDTH,
            A_WIDTH + 2 * A_KV_WIDTH + 3 * B_WIDTH, A_WIDTH + 2 * A_KV_WIDTH + 3 * B_WIDTH + D_MODEL]
    wqa, wka, wva, wqb, wkb, wvb, wga, wgb = jnp.split(w_in, cuts, axis=-1)
    wqa = wqa[:, _deinterleave(A_Q_HEADS)]
    wka = wka[:, _deinterleave(A_KV_HEADS)]
    wqb = wqb[:, _deinterleave(B_HEADS)]
    wkb = wkb[:, _deinterleave(B_HEADS)]
    k0, k1 = wka[:, :HEAD_DIM], wka[:, HEAD_DIM:]
    v0, v1 = wva[:, :HEAD_DIM], wva[:, HEAD_DIM:]
    vz = jnp.zeros_like(v0)
    w_all = jnp.concatenate([wqa, k0, k0, k1, k1, v0, vz, v1, vz, wqb, wkb, wvb, wga, wgb], axis=-1).astype(bf16)
    perm = _deinterleave(1)
    scale = HEAD_DIM ** -0.5
    tile = lambda g, n: jnp.tile(g[perm], n)[None, :]
    router_pad = jnp.pad(router_w, ((0, 0), (0, LANES - N_EXPERTS)))
    router_hi = router_pad.astype(bf16)
    prm = {
        "norm_mix": norm_mix[None, :], "w_in": w_all,
        "gqa": tile(q_norm_a, A_Q_HEADS) * (scale * LOG2_E), "gka": tile(k_norm_a, KA_COLS // HEAD_DIM),
        "gqb": tile(q_norm_b, B_HEADS) * (scale * LOG2_E), "gkb": tile(k_norm_b, B_HEADS),
        "bd": jnp.kron(jnp.eye(NORM_COLS // HEAD_DIM, dtype=f32), jnp.ones((HEAD_DIM, HEAD_DIM), f32)).astype(bf16),
        "w_branch_a": w_branch_a.astype(bf16), "w_branch_b": w_branch_b.astype(bf16), "w_out": w_out.astype(bf16),
        "norm_ffn": norm_ffn[None, :],
        "router_hi": router_hi, "router_lo": (router_pad - router_hi.astype(f32)).astype(bf16),
        "router_bias": router_bias[:, None],
        "tri": (jnp.arange(MERGE_SUB)[:, None] < jnp.arange(MERGE_SUB)[None, :]).astype(bf16),
        "expert_gu": jnp.concatenate([expert_gate, expert_up], axis=-1).astype(bf16),
        "expert_down": expert_down.astype(bf16),
        "shared_gu": jnp.concatenate([shared_gate, shared_up], axis=-1).astype(bf16),
        "shared_down": shared_down.astype(bf16),
        "ple_gate": ple_gate.astype(bf16), "ple_proj": ple_proj.astype(bf16),
        "cos_a": {}, "sin_a": {}, "cos_b": {}, "sin_b": {},
    }
    for S in seq_lens:
        prm["cos_a"][S], prm["sin_a"][S] = _rope_tables(_axial_angles(S))
        prm["cos_b"][S], prm["sin_b"][S] = _rope_tables(_linear_angles(S))
    return prm


def _dispatch_plan(idx, rank, counts, T):
    bm = MOE_ROWS
    n_rows = (T * TOP_K + N_EXPERTS * bm) // bm * bm
    n_blocks = n_rows // bm
    padded = (counts + bm - 1) // bm * bm
    pad_end = jnp.cumsum(padded)
    pad_start = pad_end - padded
    dest = pad_start[idx] + rank
    blk_start = jnp.arange(n_blocks, dtype=jnp.int32) * bm
    blk_e = jnp.minimum(jnp.sum(pad_end[None, :] <= blk_start[:, None], axis=1), N_EXPERTS - 1).astype(jnp.int32)
    n_used = (pad_end[-1:] // bm).astype(jnp.int32)
    n_valid = jnp.clip(counts[blk_e] - (blk_start - pad_start[blk_e]), 0, bm).astype(jnp.int32)
    return dest, n_rows, blk_e, n_used, n_valid


def _layer(x, p, prm):
    B, S, _ = x.shape
    T = B * S
    x2 = x.reshape(T, D_MODEL)
    qa, ka, va, *qkv_b, sga, sgb = _in_proj(x2, S, prm)
    shape3 = lambda a: a.reshape(B, S, a.shape[-1])
    ya = _flash(shape3(qa), shape3(ka), shape3(va)).reshape(T, A_WIDTH)
    n_dil = len(WINDOW_DILATIONS)
    outs, lses = [], []
    for j, (_, dil) in enumerate(WINDOW_DILATIONS):
        o_d, lse_d = _dilated(qkv_b[j], qkv_b[n_dil + j], qkv_b[2 * n_dil + j], B, S, dil)
        outs.append(o_d)
        lses.append(lse_d)
    h1, u2p, idx, wts, rank, cnt = _merge_route(x2, ya, outs, lses, sga, sgb, prm)
    dest, n_rows, blk_e, n_used, n_valid = _dispatch_plan(idx, rank, cnt[:, 0], T)
    xs = _sc_dispatch(u2p, dest, n_rows)
    ys = _experts(xs, blk_e, n_used, n_valid, prm)
    yg = _sc_combine(ys, dest)
    out = _final(h1, u2p, yg, wts.T, p.reshape(T, PLE_DIM), prm)
    return out.reshape(B, S, D_MODEL)


def kernel(x_prompt, x_sample, p_prompt, p_sample, norm_mix, w_in, q_norm_a, k_norm_a, q_norm_b, k_norm_b,
           w_branch_a, w_branch_b, w_out, norm_ffn, router_w, router_bias, expert_gate, expert_up, expert_down,
           shared_gate, shared_up, shared_down, ple_proj, ple_gate):
    weights = (norm_mix, w_in, q_norm_a, k_norm_a, q_norm_b, k_norm_b, w_branch_a, w_branch_b, w_out, norm_ffn,
               router_w, router_bias, expert_gate, expert_up, expert_down, shared_gate, shared_up, shared_down,
               ple_proj, ple_gate)
    depth = norm_mix.shape[0]
    seq_lens = sorted({x_prompt.shape[1], x_sample.shape[1]})
    hp, hs = x_prompt, x_sample
    for i in range(depth):
        prm = _prepare(seq_lens, *[w[i] for w in weights])
        hp = _layer(hp, p_prompt[i], prm)
        hs = _layer(hs, p_sample[i], prm)
    return hp, hs
```

```python
import functools

import jax
import jax.numpy as jnp
from jax import lax
from jax.experimental import pallas as pl
from jax.experimental.pallas import tpu as pltpu
from jax.experimental.pallas import tpu_sc as plsc

D_MODEL = 1024
HEAD_DIM = 64
A_Q_HEADS = 8
A_KV_HEADS = 2
B_HEADS = 6
GRID_W = 64
ROPE_THETA = 10000.0
WINDOW_DILATIONS = ((128, 1), (512, 4), (2048, 16))
N_EXPERTS = 64
TOP_K = 8
N_GROUPS = 8
TOPK_GROUPS = 4
D_EXPERT = 256
D_SHARED = 256
ROUTE_SCALE = 2.5
PLE_DIM = 256
NEG_BIG = -1e30
EPS = 1e-6
LOG2_E = 1.4426950408889634

A_WIDTH = A_Q_HEADS * HEAD_DIM
A_KV_WIDTH = A_KV_HEADS * HEAD_DIM
B_WIDTH = B_HEADS * HEAD_DIM
GROUP_Q = A_Q_HEADS // A_KV_HEADS
EXPERTS_PER_GROUP = N_EXPERTS // N_GROUPS

LANES = 128
NORM_COLS = 256
HALF_WINDOW = 64
KEY_SPAN = 256

KA_COLS = 2 * LANES
VA_COLS = 2 * LANES
OFF_QA = 0
OFF_KA = OFF_QA + A_WIDTH
OFF_VA = OFF_KA + KA_COLS
OFF_QB = OFF_VA + VA_COLS
OFF_KB = OFF_QB + B_WIDTH
OFF_VB = OFF_KB + B_WIDTH
OFF_GA = OFF_VB + B_WIDTH
OFF_GB = OFF_GA + D_MODEL
PROJ_COLS = OFF_GB + D_MODEL

TOKEN_TILE = 512
MERGE_SUB = 256
FLASH_TQ = 256
FLASH_TK = 1024
DIL_QBLOCK = 128
DIL_STEP_TOKENS = 2048
MOE_ROWS = 1024
MOE_SUB_ROWS = 256

VMEM_LIMIT = 52 * 1024 * 1024

HALF_MODEL = D_MODEL // 2
ROW_CHUNKS = HALF_MODEL // LANES
SC_CORES = 2
SC_SUBCORES = 16
SC_WORKERS = SC_CORES * SC_SUBCORES
SC_WINDOW = 128
SC_SLABS = 4

bf16 = jnp.bfloat16
f32 = jnp.float32


def _cparams(sem):
    return pltpu.CompilerParams(dimension_semantics=sem, vmem_limit_bytes=VMEM_LIMIT)


def _pack_rows(v):
    lo = lax.bitcast_convert_type(v[:, :HALF_MODEL].astype(bf16).astype(f32), jnp.uint32) >> 16
    hi = lax.bitcast_convert_type(v[:, HALF_MODEL:].astype(bf16).astype(f32), jnp.uint32) & jnp.uint32(0xFFFF0000)
    w = lo | hi
    return [w[:, c * LANES:(c + 1) * LANES] for c in range(ROW_CHUNKS)]


def _unpack_rows(chunks):
    w = jnp.concatenate(chunks, axis=1)
    lo = lax.bitcast_convert_type(w << 16, f32)
    hi = lax.bitcast_convert_type(w & jnp.uint32(0xFFFF0000), f32)
    return lo, hi


def _in_proj_kernel(x_ref, nm_ref, w_ref, gqa_ref, gka_ref, gqb_ref, gkb_ref,
                    ca_ref, sa_ref, cb_ref, sb_ref, bd_ref,
                    qa_ref, ka_ref, va_ref, qb1_ref, qb4_ref, qb16_ref, kb1_ref, kb4_ref, kb16_ref,
                    vb1_ref, vb4_ref, vb16_ref, ga_ref, gb_ref, st_ref):
    x = x_ref[...]
    r = lax.rsqrt(jnp.mean(x * x, axis=-1, keepdims=True) + EPS)
    u = (x * r * nm_ref[...]).astype(bf16)
    rows = x.shape[0]
    lane = lax.broadcasted_iota(jnp.int32, (rows, LANES), 1)
    first_half = (lane % HEAD_DIM) < (HEAD_DIM // 2)

    def proj(off, width):
        return jnp.dot(u, w_ref[:, off:off + width], preferred_element_type=f32)

    def norm_rope(off, width, g_ref, c_ref, s_ref):
        z = proj(off, width)
        c = c_ref[...]
        s = s_ref[...]
        chunks = []
        for c0 in range(0, width, NORM_COLS):
            cw = min(NORM_COLS, width - c0)
            zc = z[:, c0:c0 + cw]
            ms = jnp.dot((zc * zc).astype(bf16), bd_ref[:cw, :cw], preferred_element_type=f32) * (1.0 / HEAD_DIM)
            zn = zc * lax.rsqrt(ms + EPS) * g_ref[:, c0:c0 + cw]
            for h0 in range(0, cw, LANES):
                zh = zn[:, h0:h0 + LANES]
                sw = jnp.where(first_half, pltpu.roll(zh, LANES - HEAD_DIM // 2, 1), pltpu.roll(zh, HEAD_DIM // 2, 1))
                chunks.append(zh * c + sw * s)
        return chunks

    def store(chunks, out_ref):
        for j, ch in enumerate(chunks):
            out_ref[:, j * LANES:(j + 1) * LANES] = ch.astype(out_ref.dtype)

    def store_dilated(chunks, out_refs):
        for j, ch in enumerate(chunks):
            st_ref[j] = ch
        for (_, dil), out_ref in zip(WINDOW_DILATIONS, out_refs):
            for r in range(dil):
                for j in range(len(chunks)):
                    src = pl.ds(r, rows // dil, stride=dil) if dil > 1 else pl.ds(0, rows)
                    out_ref[r, :, j * LANES:(j + 1) * LANES] = st_ref[j, src, :].astype(out_ref.dtype)

    store(norm_rope(OFF_QA, A_WIDTH, gqa_ref, ca_ref, sa_ref), qa_ref)
    store(norm_rope(OFF_KA, KA_COLS, gka_ref, ca_ref, sa_ref), ka_ref)
    va = proj(OFF_VA, VA_COLS)
    va_ref[...] = jnp.concatenate(
        [jnp.where(lane < HEAD_DIM, va[:, c0:c0 + LANES], 1.0) for c0 in range(0, VA_COLS, LANES)],
        axis=1).astype(va_ref.dtype)
    store_dilated(norm_rope(OFF_QB, B_WIDTH, gqb_ref, cb_ref, sb_ref), (qb1_ref, qb4_ref, qb16_ref))
    store_dilated(norm_rope(OFF_KB, B_WIDTH, gkb_ref, cb_ref, sb_ref), (kb1_ref, kb4_ref, kb16_ref))
    vb = proj(OFF_VB, B_WIDTH)
    store_dilated([vb[:, c0:c0 + LANES] for c0 in range(0, B_WIDTH, LANES)], (vb1_ref, vb4_ref, vb16_ref))
    ga_ref[...] = jax.nn.sigmoid(proj(OFF_GA, D_MODEL)).astype(ga_ref.dtype)
    gb_ref[...] = jax.nn.sigmoid(proj(OFF_GB, D_MODEL)).astype(gb_ref.dtype)


def _in_proj(x2, S, prm):
    T = x2.shape[0]
    tm = TOKEN_TILE
    n_pos = S // tm
    row = lambda i: (i, 0)
    pos = lambda i: (i % n_pos, 0)
    const = lambda i: (0, 0)
    dils = [dil for _, dil in WINDOW_DILATIONS]
    flat = lambda w: (pl.BlockSpec((tm, w), row), jax.ShapeDtypeStruct((T, w), bf16))
    strided = lambda d: (pl.BlockSpec((d, tm // d, B_WIDTH), lambda i: (0, i, 0)),
                         jax.ShapeDtypeStruct((d, T // d, B_WIDTH), bf16))
    outs = ([flat(A_WIDTH), flat(KA_COLS), flat(VA_COLS)] + 3 * [strided(d) for d in dils]
            + [flat(D_MODEL), flat(D_MODEL)])
    return pl.pallas_call(
        _in_proj_kernel,
        grid=(T // tm,),
        in_specs=[
            pl.BlockSpec((tm, D_MODEL), row),
            pl.BlockSpec((1, D_MODEL), const),
            pl.BlockSpec((D_MODEL, PROJ_COLS), const),
            pl.BlockSpec((1, A_WIDTH), const),
            pl.BlockSpec((1, KA_COLS), const),
            pl.BlockSpec((1, B_WIDTH), const),
            pl.BlockSpec((1, B_WIDTH), const),
            pl.BlockSpec((tm, LANES), pos),
            pl.BlockSpec((tm, LANES), pos),
            pl.BlockSpec((tm, LANES), pos),
            pl.BlockSpec((tm, LANES), pos),
            pl.BlockSpec((NORM_COLS, NORM_COLS), const),
        ],
        out_specs=[spec for spec, _ in outs],
        out_shape=[shape for _, shape in outs],
        scratch_shapes=[pltpu.VMEM((B_WIDTH // LANES, tm, LANES), f32)],
        compiler_params=_cparams(("parallel",)),
        name="in_proj",
    )(x2, prm["norm_mix"], prm["w_in"], prm["gqa"], prm["gka"], prm["gqb"], prm["gkb"],
      prm["cos_a"][S], prm["sin_a"][S], prm["cos_b"][S], prm["sin_b"][S], prm["bd"])


def _flash_kernel(q_ref, k_ref, v_ref, o_ref, s_ref, *, tq, tk, seq):
    lane = lax.broadcasted_iota(jnp.int32, (tq, LANES), 1)
    lo = lane < HEAD_DIM
    q = q_ref[0]
    q01, q23 = q[:, :LANES], q[:, LANES:]
    zero = jnp.zeros_like(q01)
    qs = [jnp.where(lo, q01, zero), jnp.where(lo, zero, q01), jnp.where(lo, q23, zero), jnp.where(lo, zero, q23)]
    n_chunks = seq // tk

    def keys(chunk):
        return k_ref[0, pl.ds(pl.multiple_of(chunk * tk, tk), tk), :]

    def score(h, kk):
        return lax.dot_general(qs[h], kk, (((1,), (1,)), ((), ())), preferred_element_type=f32)

    def attend(chunk, slot, ms, accs, next_chunk):
        vv = v_ref[0, pl.ds(pl.multiple_of(chunk * tk, tk), tk), :]
        kk = None if next_chunk is None else keys(next_chunk)
        new_ms, new_accs = [], []
        for h in range(GROUP_Q):
            if kk is not None:
                s_ref[1 - slot, h] = score(h, kk)
            s = s_ref[slot, h]
            m_new = jnp.maximum(ms[h], jnp.max(s, axis=-1, keepdims=True))
            p = jnp.exp2((s - m_new).astype(bf16))
            alpha = jnp.exp2(ms[h] - m_new)
            new_accs.append(accs[h] * alpha + jnp.dot(p, vv, preferred_element_type=f32))
            new_ms.append(m_new)
        return tuple(new_ms), tuple(new_accs)

    kk0 = keys(0)
    for h in range(GROUP_Q):
        s_ref[0, h] = score(h, kk0)

    def body(jj, carry):
        ms, accs = attend(2 * jj, 0, *carry, 2 * jj + 1)
        return attend(2 * jj + 1, 1, ms, accs, 2 * jj + 2)

    m0 = tuple(jnp.full((tq, 1), NEG_BIG, f32) for _ in range(GROUP_Q))
    acc0 = tuple(jnp.zeros((tq, LANES), f32) for _ in range(GROUP_Q))
    ms, accs = lax.fori_loop(0, n_chunks // 2 - 1, body, (m0, acc0), unroll=True)
    ms, accs = attend(n_chunks - 2, 0, ms, accs, n_chunks - 1)
    _, accs = attend(n_chunks - 1, 1, ms, accs, None)
    heads = [acc / jnp.where(lo, pltpu.roll(acc, HEAD_DIM, 1), 1.0) for acc in accs]
    out01 = jnp.where(lo, heads[0], pltpu.roll(heads[1], HEAD_DIM, 1))
    out23 = jnp.where(lo, heads[2], pltpu.roll(heads[3], HEAD_DIM, 1))
    o_ref[0] = jnp.concatenate([out01, out23], axis=1).astype(o_ref.dtype)


def _flash(qa, ka, va):
    B, S, _ = qa.shape
    tq, tk = FLASH_TQ, FLASH_TK
    return pl.pallas_call(
        functools.partial(_flash_kernel, tq=tq, tk=tk, seq=S),
        grid=(B, A_KV_HEADS, S // tq),
        in_specs=[
            pl.BlockSpec((1, tq, GROUP_Q * HEAD_DIM), lambda b, g, i: (b, i, g)),
            pl.BlockSpec((1, S, LANES), lambda b, g, i: (b, 0, g)),
            pl.BlockSpec((1, S, LANES), lambda b, g, i: (b, 0, g)),
        ],
        out_specs=pl.BlockSpec((1, tq, GROUP_Q * HEAD_DIM), lambda b, g, i: (b, i, g)),
        out_shape=jax.ShapeDtypeStruct((B, S, A_WIDTH), bf16),
        scratch_shapes=[pltpu.VMEM((2, GROUP_Q, tq, tk), f32)],
        compiler_params=_cparams(("parallel", "parallel", "arbitrary")),
        name="flash_gqa",
    )(qa, ka, va)


def _dilated_kernel(q_ref, k_ref, v_ref, o_ref, lse_ref, so_ref, sl_ref, *, length, tms, dil):
    i = pl.program_id(1)
    lane = lax.broadcasted_iota(jnp.int32, (DIL_QBLOCK, LANES), 1)
    lo = lane < HEAD_DIM
    n_pairs = B_WIDTH // LANES

    def unit(u, carry):
        sb, r = u // dil, u % dil
        r0 = pl.multiple_of(sb * DIL_QBLOCK, DIL_QBLOCK)
        m0 = i * tms + r0
        start = pl.multiple_of(jnp.clip(m0 - HALF_WINDOW, 0, length - KEY_SPAN), HALF_WINDOW)
        qpos = m0 + lax.broadcasted_iota(jnp.int32, (DIL_QBLOCK, KEY_SPAN), 0)
        kpos = start + lax.broadcasted_iota(jnp.int32, (DIL_QBLOCK, KEY_SPAN), 1)
        valid = jnp.abs(kpos - qpos) <= HALF_WINDOW
        rows = pl.ds(r0 * dil + r, DIL_QBLOCK, stride=dil) if dil > 1 else pl.ds(r0, DIL_QBLOCK)
        scores = []
        for c in range(n_pairs):
            cs = slice(c * LANES, (c + 1) * LANES)
            qp = q_ref[r, 0, pl.ds(r0, DIL_QBLOCK), cs]
            kp = k_ref[r, 0, pl.ds(start, KEY_SPAN), cs]
            zero = jnp.zeros_like(qp)
            for half in range(2):
                qm = jnp.where(lo, qp, zero) if half == 0 else jnp.where(lo, zero, qp)
                scores.append(lax.dot_general(qm, kp, (((1,), (1,)), ((), ())), preferred_element_type=f32))
        for c in range(n_pairs):
            vp = v_ref[r, 0, pl.ds(start, KEY_SPAN), c * LANES:(c + 1) * LANES]
            outs, lses = [], []
            for half in range(2):
                s = jnp.where(valid, scores[2 * c + half], NEG_BIG)
                m = jnp.max(s, axis=-1, keepdims=True)
                e = jnp.exp2(s - m)
                l = jnp.sum(e, axis=-1, keepdims=True)
                outs.append(jnp.dot(e.astype(bf16), vp, preferred_element_type=f32) * (1.0 / l))
                lses.append(m + jnp.log2(l))
            so_ref[c, rows, :] = jnp.where(lo, outs[0], outs[1])
            sl_ref[c, rows, :] = jnp.where(lo, lses[0], lses[1])
        return carry

    lax.fori_loop(0, (tms // DIL_QBLOCK) * dil, unit, 0, unroll=4)
    o_ref[0] = jnp.concatenate([so_ref[c] for c in range(n_pairs)], axis=1).astype(o_ref.dtype)
    lse_ref[0] = jnp.concatenate([sl_ref[c] for c in range(n_pairs)], axis=1)


def _dilated(qb, kb, vb, B, S, dil):
    length = S // dil
    tms = DIL_STEP_TOKENS // dil
    view = lambda a: a.reshape(dil, B, length, B_WIDTH)
    o, lse = pl.pallas_call(
        functools.partial(_dilated_kernel, length=length, tms=tms, dil=dil),
        grid=(B, length // tms),
        in_specs=[
            pl.BlockSpec((dil, 1, tms, B_WIDTH), lambda b, i: (0, b, i, 0)),
            pl.BlockSpec((dil, 1, length, B_WIDTH), lambda b, i: (0, b, 0, 0)),
            pl.BlockSpec((dil, 1, length, B_WIDTH), lambda b, i: (0, b, 0, 0)),
        ],
        out_specs=[
            pl.BlockSpec((1, tms * dil, B_WIDTH), lambda b, i: (b, i, 0)),
            pl.BlockSpec((1, tms * dil, B_WIDTH), lambda b, i: (b, i, 0)),
        ],
        out_shape=[
            jax.ShapeDtypeStruct((B, S, B_WIDTH), bf16),
            jax.ShapeDtypeStruct((B, S, B_WIDTH), f32),
        ],
        scratch_shapes=[pltpu.VMEM((B_WIDTH // LANES, tms * dil, LANES), f32),
                        pltpu.VMEM((B_WIDTH // LANES, tms * dil, LANES), f32)],
        compiler_params=_cparams(("parallel", "arbitrary")),
        name=f"dilated_{dil}",
    )(view(qb), view(kb), view(vb))
    return o.reshape(B * S, B_WIDTH), lse.reshape(B * S, B_WIDTH)


def _first_index_of_max(cur, idx, n):
    mx = jnp.max(cur, axis=0, keepdims=True)
    return jnp.min(jnp.where(cur == mx, idx, n), axis=0, keepdims=True)


def _merge_route_kernel(x_ref, ya_ref, o1_ref, o2_ref, o3_ref, l1_ref, l2_ref, l3_ref, ga_ref, gb_ref,
                        wa_ref, wb_ref, wo_ref, nf_ref, rhi_ref, rlo_ref, bias_ref, tri_ref,
                        h_ref, u_ref, idx_ref, wt_ref, rank_ref, cnt_ref, run_ref):
    step = pl.program_id(0)

    @pl.when(step == 0)
    def _():
        run_ref[...] = jnp.zeros_like(run_ref)

    tm = x_ref.shape[0]
    sub = tri_ref.shape[0]

    def dense(j):
        rs = slice(j * sub, (j + 1) * sub)
        l1, l2, l3 = l1_ref[rs, :], l2_ref[rs, :], l3_ref[rs, :]
        lm = jnp.maximum(jnp.maximum(l1, l2), l3)
        e1, e2, e3 = jnp.exp2(l1 - lm), jnp.exp2(l2 - lm), jnp.exp2(l3 - lm)
        yb = ((e1 * o1_ref[rs, :].astype(f32) + e2 * o2_ref[rs, :].astype(f32) + e3 * o3_ref[rs, :].astype(f32))
              / (e1 + e2 + e3))
        pa = jnp.dot(ya_ref[rs, :], wa_ref[...], preferred_element_type=f32)
        pb = jnp.dot(yb.astype(bf16), wb_ref[...], preferred_element_type=f32)
        merged = ga_ref[rs, :].astype(f32) * pa + gb_ref[rs, :].astype(f32) * pb
        h = x_ref[rs, :] + jnp.dot(merged.astype(bf16), wo_ref[...], preferred_element_type=f32)
        h_ref[rs, :] = h
        r = lax.rsqrt(jnp.mean(h * h, axis=-1, keepdims=True) + EPS)
        uf = h * r * nf_ref[...]
        u_hi = uf.astype(bf16)
        for c, chunk in enumerate(_pack_rows(uf)):
            u_ref[c, rs, :] = chunk
        u_lo = (uf - u_hi.astype(f32)).astype(bf16)
        return (jnp.dot(u_hi, rhi_ref[...], preferred_element_type=f32)
                + jnp.dot(u_lo, rhi_ref[...], preferred_element_type=f32)
                + jnp.dot(u_hi, rlo_ref[...], preferred_element_type=f32))

    def route(j, logits):
        scores = jax.nn.sigmoid(logits.T[:N_EXPERTS, :])
        sel = scores + bias_ref[...]
        i8 = lax.broadcasted_iota(jnp.int32, (EXPERTS_PER_GROUP, sub), 0)
        neg_inf = jnp.float32(-jnp.inf)
        gscore = jnp.zeros((N_GROUPS, sub), f32)
        for g in range(N_GROUPS):
            blk = sel[g * EXPERTS_PER_GROUP:(g + 1) * EXPERTS_PER_GROUP, :]
            top1 = jnp.max(blk, axis=0, keepdims=True)
            first = _first_index_of_max(blk, i8, EXPERTS_PER_GROUP)
            top2 = jnp.max(jnp.where(i8 == first, neg_inf, blk), axis=0, keepdims=True)
            gscore = jnp.where(i8 == g, top1 + top2, gscore)
        gkeep = jnp.zeros((N_GROUPS, sub), jnp.bool_)
        cur = gscore
        for _ in range(TOPK_GROUPS):
            pick = i8 == _first_index_of_max(cur, i8, N_GROUPS)
            gkeep = jnp.logical_or(gkeep, pick)
            cur = jnp.where(pick, neg_inf, cur)
        cur = jnp.concatenate(
            [jnp.where(gkeep[g:g + 1, :], sel[g * EXPERTS_PER_GROUP:(g + 1) * EXPERTS_PER_GROUP, :], NEG_BIG)
             for g in range(N_GROUPS)], axis=0)

        ie = lax.broadcasted_iota(jnp.int32, (N_EXPERTS, sub), 0)
        ik = lax.broadcasted_iota(jnp.int32, (TOP_K, sub), 0)
        idx_out = jnp.zeros((TOP_K, sub), jnp.int32)
        w_out = jnp.zeros((TOP_K, sub), f32)
        chosen = jnp.zeros((N_EXPERTS, sub), jnp.bool_)
        for k in range(TOP_K):
            fi = _first_index_of_max(cur, ie, N_EXPERTS)
            pick = ie == fi
            wk = jnp.sum(jnp.where(pick, scores, 0.0), axis=0, keepdims=True)
            idx_out = jnp.where(ik == k, fi, idx_out)
            w_out = jnp.where(ik == k, wk, w_out)
            chosen = jnp.logical_or(chosen, pick)
            cur = jnp.where(pick, neg_inf, cur)
        w_out = w_out / jnp.sum(w_out, axis=0, keepdims=True) * ROUTE_SCALE
        wt_ref[:, j * sub:(j + 1) * sub] = w_out

        sel01 = jnp.where(chosen, 1.0, 0.0)
        before = jnp.dot(sel01.astype(bf16), tri_ref[...], preferred_element_type=f32) + run_ref[:, 0:1]
        rank_out = jnp.zeros((TOP_K, sub), f32)
        for k in range(TOP_K):
            rk = jnp.sum(jnp.where(ie == idx_out[k:k + 1, :], before, 0.0), axis=0, keepdims=True)
            rank_out = jnp.where(ik == k, rk, rank_out)
        rank_i = rank_out.astype(jnp.int32)
        for w in range(sub // SC_WINDOW):
            win = j * (sub // SC_WINDOW) + w
            idx_ref[win] = idx_out[:, w * SC_WINDOW:(w + 1) * SC_WINDOW]
            rank_ref[win] = rank_i[:, w * SC_WINDOW:(w + 1) * SC_WINDOW]
        run_ref[...] = run_ref[...] + jnp.sum(sel01, axis=1, keepdims=True)

    logits = [dense(j) for j in range(tm // sub)]
    for j in range(tm // sub):
        route(j, logits[j])
    cnt_ref[...] = run_ref[...].astype(jnp.int32)


def _merge_route(x2, ya, outs, lses, sga, sgb, prm):
    T = x2.shape[0]
    tm = TOKEN_TILE
    row = lambda i: (i, 0)
    col = lambda i: (0, i)
    const = lambda i: (0, 0)
    tok = lambda w: pl.BlockSpec((tm, w), row)
    full = lambda a: pl.BlockSpec(a.shape, const)
    win = pl.BlockSpec((tm // SC_WINDOW, TOP_K, SC_WINDOW), lambda i: (i, 0, 0))
    return pl.pallas_call(
        _merge_route_kernel,
        grid=(T // tm,),
        in_specs=[tok(D_MODEL), tok(A_WIDTH), tok(B_WIDTH), tok(B_WIDTH), tok(B_WIDTH),
                  tok(B_WIDTH), tok(B_WIDTH), tok(B_WIDTH), tok(D_MODEL), tok(D_MODEL),
                  full(prm["w_branch_a"]), full(prm["w_branch_b"]), full(prm["w_out"]), full(prm["norm_ffn"]),
                  full(prm["router_hi"]), full(prm["router_lo"]), full(prm["router_bias"]), full(prm["tri"])],
        out_specs=[tok(D_MODEL), pl.BlockSpec((ROW_CHUNKS, tm, LANES), lambda i: (0, i, 0)),
                   win, pl.BlockSpec((TOP_K, tm), col), win,
                   pl.BlockSpec((N_EXPERTS, LANES), const)],
        out_shape=[jax.ShapeDtypeStruct((T, D_MODEL), f32), jax.ShapeDtypeStruct((ROW_CHUNKS, T, LANES), jnp.uint32),
                   jax.ShapeDtypeStruct((T // SC_WINDOW, TOP_K, SC_WINDOW), jnp.int32),
                   jax.ShapeDtypeStruct((TOP_K, T), f32),
                   jax.ShapeDtypeStruct((T // SC_WINDOW, TOP_K, SC_WINDOW), jnp.int32),
                   jax.ShapeDtypeStruct((N_EXPERTS, LANES), jnp.int32)],
        scratch_shapes=[pltpu.VMEM((N_EXPERTS, LANES), f32)],
        compiler_params=_cparams(("arbitrary",)),
        name="merge_route",
    )(x2, ya, *outs, *lses, sga, sgb,
      prm["w_branch_a"], prm["w_branch_b"], prm["w_out"], prm["norm_ffn"],
      prm["router_hi"], prm["router_lo"], prm["router_bias"], prm["tri"])


def _experts_kernel(be_ref, nu_ref, nv_ref, x_ref, wgu_ref, wd_ref, y_ref):
    del be_ref
    i = pl.program_id(0)

    @pl.when(i < nu_ref[0])
    def _():
        n_sub = x_ref.shape[1] // MOE_SUB_ROWS

        def up(j):
            r0 = j * MOE_SUB_ROWS
            lo, hi = _unpack_rows([x_ref[c, r0:r0 + MOE_SUB_ROWS, :] for c in range(ROW_CHUNKS)])
            live = r0 + lax.broadcasted_iota(jnp.int32, lo.shape, 0) < nv_ref[i]
            x = jnp.concatenate([jnp.where(live, lo, 0.0), jnp.where(live, hi, 0.0)], axis=1).astype(bf16)
            return jnp.dot(x, wgu_ref[0], preferred_element_type=f32)

        def down(j, gu):
            g, u = gu[:, :D_EXPERT], gu[:, D_EXPERT:]
            a = g * jax.nn.sigmoid(g) * u
            y = jnp.dot(a.astype(bf16), wd_ref[0], preferred_element_type=f32)
            r0 = j * MOE_SUB_ROWS
            for c, chunk in enumerate(_pack_rows(y)):
                y_ref[c, r0:r0 + MOE_SUB_ROWS, :] = chunk

        gu_prev = up(0)
        for j in range(1, n_sub):
            gu_next = up(j)
            down(j - 1, gu_prev)
            gu_prev = gu_next
        down(n_sub - 1, gu_prev)

    @pl.when(i >= nu_ref[0])
    def _():
        y_ref[...] = jnp.zeros_like(y_ref)


def _experts(xs, blk_e, n_used, n_valid, prm):
    n_rows = xs.shape[1]
    bm = MOE_ROWS
    rows = lambda i, be, nu, nv: (0, jnp.minimum(i, nu[0] - 1), 0)
    return pl.pallas_call(
        _experts_kernel,
        grid_spec=pltpu.PrefetchScalarGridSpec(
            num_scalar_prefetch=3,
            grid=(n_rows // bm,),
            in_specs=[
                pl.BlockSpec((ROW_CHUNKS, bm, LANES), rows),
                pl.BlockSpec((1, D_MODEL, 2 * D_EXPERT), lambda i, be, nu, nv: (be[i], 0, 0)),
                pl.BlockSpec((1, D_EXPERT, D_MODEL), lambda i, be, nu, nv: (be[i], 0, 0)),
            ],
            out_specs=pl.BlockSpec((ROW_CHUNKS, bm, LANES), lambda i, be, nu, nv: (0, i, 0)),
        ),
        out_shape=jax.ShapeDtypeStruct((ROW_CHUNKS, n_rows, LANES), jnp.uint32),
        compiler_params=_cparams(("arbitrary",)),
        name="experts",
    )(blk_e, n_used, n_valid, xs, prm["expert_gu"], prm["expert_down"])


def _sc_mesh():
    return plsc.VectorSubcoreMesh(core_axis_name="c", subcore_axis_name="s")


def _sc_dispatch(u2p, dest, n_rows):
    T = u2p.shape[1]
    per_worker = T // SC_WORKERS
    n_it = per_worker // SC_WINDOW

    @functools.partial(
        pl.kernel, mesh=_sc_mesh(),
        out_type=jax.ShapeDtypeStruct((ROW_CHUNKS, n_rows, LANES), jnp.uint32),
        scratch_types=[pltpu.VMEM((TOP_K, SC_WINDOW), jnp.int32),
                       pltpu.VMEM((ROW_CHUNKS, SC_WINDOW, LANES), jnp.uint32),
                       pltpu.SemaphoreType.DMA, pltpu.SemaphoreType.DMA],
    )
    def k(u_hbm, dest_hbm, xs_hbm, idx_v, rows_v, sem_in, sem_out):
        wid = lax.axis_index("s") * SC_CORES + lax.axis_index("c")

        @pl.loop(0, n_it)
        def _(it):
            t0 = wid * per_worker + it * SC_WINDOW
            pltpu.sync_copy(dest_hbm.at[wid * n_it + it], idx_v)
            loads = [pltpu.async_copy(u_hbm.at[c].at[pl.ds(t0, SC_WINDOW)], rows_v.at[c], sem_in)
                     for c in range(ROW_CHUNKS)]
            for cp in loads:
                cp.wait()
            stores = [pltpu.async_copy(rows_v.at[c], xs_hbm.at[c].at[idx_v.at[kk]], sem_out)
                      for kk in range(TOP_K) for c in range(ROW_CHUNKS)]
            for cp in stores:
                cp.wait()

    return k(u2p, dest)


def _sc_combine(ys, dest):
    n_windows = dest.shape[0]
    T = n_windows * SC_WINDOW
    per_worker = T // SC_WORKERS
    n_it = per_worker // SC_WINDOW

    @functools.partial(
        pl.kernel, mesh=_sc_mesh(),
        out_type=jax.ShapeDtypeStruct((TOP_K, ROW_CHUNKS, T, LANES), jnp.uint32),
        scratch_types=[pltpu.VMEM((TOP_K, SC_WINDOW), jnp.int32),
                       pltpu.VMEM((SC_SLABS, SC_WINDOW, LANES), jnp.uint32),
                       pltpu.SemaphoreType.DMA, pltpu.SemaphoreType.DMA],
    )
    def k(ys_hbm, dest_hbm, yg_hbm, idx_v, rows_v, sem_in, sem_out):
        wid = lax.axis_index("s") * SC_CORES + lax.axis_index("c")

        @pl.loop(0, n_it)
        def _(it):
            t0 = wid * per_worker + it * SC_WINDOW
            pltpu.sync_copy(dest_hbm.at[wid * n_it + it], idx_v)
            for c in range(ROW_CHUNKS):
                for k0 in range(0, TOP_K, SC_SLABS):
                    loads = [pltpu.async_copy(ys_hbm.at[c].at[idx_v.at[k0 + j]], rows_v.at[j], sem_in)
                             for j in range(SC_SLABS)]
                    for cp in loads:
                        cp.wait()
                    stores = [pltpu.async_copy(rows_v.at[j], yg_hbm.at[k0 + j].at[c].at[pl.ds(t0, SC_WINDOW)], sem_out)
                              for j in range(SC_SLABS)]
                    for cp in stores:
                        cp.wait()

    return k(ys, dest)


def _final_kernel(h_ref, u_ref, yg_ref, wt_ref, p_ref, sgu_ref, sd_ref, pg_ref, pp_ref, o_ref):
    wl = wt_ref[...]
    wt = jnp.concatenate([wl, jnp.zeros((LANES - TOP_K, wl.shape[1]), f32)], axis=0).T
    r_lo = r_hi = None
    for k in range(TOP_K):
        lo, hi = _unpack_rows([yg_ref[k, c] for c in range(ROW_CHUNKS)])
        wk = wt[:, k:k + 1]
        r_lo = lo * wk if k == 0 else r_lo + lo * wk
        r_hi = hi * wk if k == 0 else r_hi + hi * wk
    routed = jnp.concatenate([r_lo, r_hi], axis=1)
    u_lo, u_hi = _unpack_rows([u_ref[c] for c in range(ROW_CHUNKS)])
    un = jnp.concatenate([u_lo, u_hi], axis=1).astype(bf16)
    gu = jnp.dot(un, sgu_ref[...], preferred_element_type=f32)
    g, u = gu[:, :D_SHARED], gu[:, D_SHARED:]
    shared = jnp.dot((g * jax.nn.sigmoid(g) * u).astype(bf16), sd_ref[...], preferred_element_type=f32)
    h = h_ref[...] + (routed + shared)
    gate = jax.nn.sigmoid(jnp.dot(h.astype(bf16), pg_ref[...], preferred_element_type=f32))
    emb = jnp.dot(p_ref[...].astype(bf16), pp_ref[...], preferred_element_type=f32)
    o_ref[...] = h + gate * emb


def _final(h1, u2p, yg, wts, p2, prm):
    T = h1.shape[0]
    tm = TOKEN_TILE
    row = lambda i: (i, 0)
    const = lambda i: (0, 0)
    full = lambda a: pl.BlockSpec(a.shape, const)
    return pl.pallas_call(
        _final_kernel,
        grid=(T // tm,),
        in_specs=[pl.BlockSpec((tm, D_MODEL), row),
                  pl.BlockSpec((ROW_CHUNKS, tm, LANES), lambda i: (0, i, 0)),
                  pl.BlockSpec((TOP_K, ROW_CHUNKS, tm, LANES), lambda i: (0, 0, i, 0)),
                  pl.BlockSpec((TOP_K, tm), lambda i: (0, i)),
                  pl.BlockSpec((tm, PLE_DIM), row),
                  full(prm["shared_gu"]), full(prm["shared_down"]), full(prm["ple_gate"]), full(prm["ple_proj"])],
        out_specs=pl.BlockSpec((tm, D_MODEL), row),
        out_shape=jax.ShapeDtypeStruct((T, D_MODEL), f32),
        compiler_params=_cparams(("parallel",)),
        name="final",
    )(h1, u2p, yg, wts, p2, prm["shared_gu"], prm["shared_down"], prm["ple_gate"], prm["ple_proj"])


def _deinterleave(n_heads):
    base = jnp.concatenate([jnp.arange(0, HEAD_DIM, 2), jnp.arange(1, HEAD_DIM, 2)])
    return (jnp.arange(n_heads)[:, None] * HEAD_DIM + base[None, :]).reshape(-1)


def _rope_tables(ang):
    c, s = jnp.cos(ang), jnp.sin(ang)
    return jnp.tile(jnp.concatenate([c, c], axis=-1), (1, 2)), jnp.tile(jnp.concatenate([-s, s], axis=-1), (1, 2))


def _axial_angles(S):
    rows = S // GRID_W
    row = jnp.repeat(jnp.arange(rows, dtype=f32), GRID_W)
    col = jnp.tile(jnp.arange(GRID_W, dtype=f32), rows)
    half = HEAD_DIM // 2
    inv = ROPE_THETA ** (-jnp.arange(0, half, 2, dtype=f32) / half)
    return jnp.concatenate([row[:, None] * inv, col[:, None] * inv], axis=-1)


def _linear_angles(S):
    t = jnp.arange(S, dtype=f32)
    inv = ROPE_THETA ** (-jnp.arange(0, HEAD_DIM, 2, dtype=f32) / HEAD_DIM)
    return t[:, None] * inv


def _prepare(seq_lens, norm_mix, w_in, q_norm_a, k_norm_a, q_norm_b, k_norm_b, w_branch_a, w_branch_b, w_out,
             norm_ffn, router_w, router_bias, expert_gate, expert_up, expert_down, shared_gate, shared_up,
             shared_down, ple_proj, ple_gate):
    cuts = [A_WIDTH, A_WIDTH + A_KV_WIDTH, A_WIDTH + 2 * A_KV_WIDTH,
            A_WIDTH + 2 * A_KV_WIDTH + B_WIDTH, A_WIDTH + 2 * A_KV_WIDTH + 2 * B_WIDTH,
            A_WIDTH + 2 * A_KV_WIDTH + 3 * B_WIDTH, A_WIDTH + 2 * A_KV_WIDTH + 3 * B_WIDTH + D_MODEL]
    wqa, wka, wva, wqb, wkb, wvb, wga, wgb = jnp.split(w_in, cuts, axis=-1)
    wqa = wqa[:, _deinterleave(A_Q_HEADS)]
    wka = wka[:, _deinterleave(A_KV_HEADS)]
    wqb = wqb[:, _deinterleave(B_HEADS)]
    wkb = wkb[:, _deinterleave(B_HEADS)]
    k0, k1 = wka[:, :HEAD_DIM], wka[:, HEAD_DIM:]
    v0, v1 = wva[:, :HEAD_DIM], wva[:, HEAD_DIM:]
    vz = jnp.zeros_like(v0)
    w_all = jnp.concatenate([wqa, k0, k0, k1, k1, v0, vz, v1, vz, wqb, wkb, wvb, wga, wgb], axis=-1).astype(bf16)
    perm = _deinterleave(1)
    scale = HEAD_DIM ** -0.5
    tile = lambda g, n: jnp.tile(g[perm], n)[None, :]
    router_pad = jnp.pad(router_w, ((0, 0), (0, LANES - N_EXPERTS)))
    router_hi = router_pad.astype(bf16)
    prm = {
        "norm_mix": norm_mix[None, :], "w_in": w_all,
        "gqa": tile(q_norm_a, A_Q_HEADS) * (scale * LOG2_E), "gka": tile(k_norm_a, KA_COLS // HEAD_DIM),
        "gqb": tile(q_norm_b, B_HEADS) * (scale * LOG2_E), "gkb": tile(k_norm_b, B_HEADS),
        "bd": jnp.kron(jnp.eye(NORM_COLS // HEAD_DIM, dtype=f32), jnp.ones((HEAD_DIM, HEAD_DIM), f32)).astype(bf16),
        "w_branch_a": w_branch_a.astype(bf16), "w_branch_b": w_branch_b.astype(bf16), "w_out": w_out.astype(bf16),
        "norm_ffn": norm_ffn[None, :],
        "router_hi": router_hi, "router_lo": (router_pad - router_hi.astype(f32)).astype(bf16),
        "router_bias": router_bias[:, None],
        "tri": (jnp.arange(MERGE_SUB)[:, None] < jnp.arange(MERGE_SUB)[None, :]).astype(bf16),
        "expert_gu": jnp.concatenate([expert_gate, expert_up], axis=-1).astype(bf16),
        "expert_down": expert_down.astype(bf16),
        "shared_gu": jnp.concatenate([shared_gate, shared_up], axis=-1).astype(bf16),
        "shared_down": shared_down.astype(bf16),
        "ple_gate": ple_gate.astype(bf16), "ple_proj": ple_proj.astype(bf16),
        "cos_a": {}, "sin_a": {}, "cos_b": {}, "sin_b": {},
    }
    for S in seq_lens:
        prm["cos_a"][S], prm["sin_a"][S] = _rope_tables(_axial_angles(S))
        prm["cos_b"][S], prm["sin_b"][S] = _rope_tables(_linear_angles(S))
    return prm


def _dispatch_plan(idx, rank, counts, T):
    bm = MOE_ROWS
    n_rows = (T * TOP_K + N_EXPERTS * bm) // bm * bm
    n_blocks = n_rows // bm
    padded = (counts + bm - 1) // bm * bm
    pad_end = jnp.cumsum(padded)
    pad_start = pad_end - padded
    dest = pad_start[idx] + rank
    blk_start = jnp.arange(n_blocks, dtype=jnp.int32) * bm
    blk_e = jnp.minimum(jnp.sum(pad_end[None, :] <= blk_start[:, None], axis=1), N_EXPERTS - 1).astype(jnp.int32)
    n_used = (pad_end[-1:] // bm).astype(jnp.int32)
    n_valid = jnp.clip(counts[blk_e] - (blk_start - pad_start[blk_e]), 0, bm).astype(jnp.int32)
    return dest, n_rows, blk_e, n_used, n_valid


def _layer(x, p, prm):
    B, S, _ = x.shape
    T = B * S
    x2 = x.reshape(T, D_MODEL)
    qa, ka, va, *qkv_b, sga, sgb = _in_proj(x2, S, prm)
    shape3 = lambda a: a.reshape(B, S, a.shape[-1])
    ya = _flash(shape3(qa), shape3(ka), shape3(va)).reshape(T, A_WIDTH)
    n_dil = len(WINDOW_DILATIONS)
    outs, lses = [], []
    for j, (_, dil) in enumerate(WINDOW_DILATIONS):
        o_d, lse_d = _dilated(qkv_b[j], qkv_b[n_dil + j], qkv_b[2 * n_dil + j], B, S, dil)
        outs.append(o_d)
        lses.append(lse_d)
    h1, u2p, idx, wts, rank, cnt = _merge_route(x2, ya, outs, lses, sga, sgb, prm)
    dest, n_rows, blk_e, n_used, n_valid = _dispatch_plan(idx, rank, cnt[:, 0], T)
    xs = _sc_dispatch(u2p, dest, n_rows)
    ys = _experts(xs, blk_e, n_used, n_valid, prm)
    yg = _sc_combine(ys, dest)
    out = _final(h1, u2p, yg, wts, p.reshape(T, PLE_DIM), prm)
    return out.reshape(B, S, D_MODEL)


def kernel(x_prompt, x_sample, p_prompt, p_sample, norm_mix, w_in, q_norm_a, k_norm_a, q_norm_b, k_norm_b,
           w_branch_a, w_branch_b, w_out, norm_ffn, router_w, router_bias, expert_gate, expert_up, expert_down,
           shared_gate, shared_up, shared_down, ple_proj, ple_gate):
    weights = (norm_mix, w_in, q_norm_a, k_norm_a, q_norm_b, k_norm_b, w_branch_a, w_branch_b, w_out, norm_ffn,
               router_w, router_bias, expert_gate, expert_up, expert_down, shared_gate, shared_up, shared_down,
               ple_proj, ple_gate)
    depth = norm_mix.shape[0]
    seq_lens = sorted({x_prompt.shape[1], x_sample.shape[1]})
    hp, hs = x_prompt, x_sample
    for i in range(depth):
        prm = _prepare(seq_lens, *[w[i] for w in weights])
        hp = _layer(hp, p_prompt[i], prm)
        hs = _layer(hs, p_sample[i], prm)
    return hp, hs
```

```python
import functools

import jax
import jax.numpy as jnp
from jax import lax
from jax.experimental import pallas as pl
from jax.experimental.pallas import tpu as pltpu
from jax.experimental.pallas import tpu_sc as plsc

D_MODEL = 1024
HEAD_DIM = 64
A_Q_HEADS = 8
A_KV_HEADS = 2
B_HEADS = 6
GRID_W = 64
ROPE_THETA = 10000.0
WINDOW_DILATIONS = ((128, 1), (512, 4), (2048, 16))
N_EXPERTS = 64
TOP_K = 8
N_GROUPS = 8
TOPK_GROUPS = 4
D_EXPERT = 256
D_SHARED = 256
ROUTE_SCALE = 2.5
PLE_DIM = 256
NEG_BIG = -1e30
EPS = 1e-6
LOG2_E = 1.4426950408889634

A_WIDTH = A_Q_HEADS * HEAD_DIM
A_KV_WIDTH = A_KV_HEADS * HEAD_DIM
B_WIDTH = B_HEADS * HEAD_DIM
GROUP_Q = A_Q_HEADS // A_KV_HEADS
EXPERTS_PER_GROUP = N_EXPERTS // N_GROUPS

LANES = 128
NORM_COLS = 256
HALF_WINDOW = 64
KEY_SPAN = 256

KA_COLS = 2 * LANES
VA_COLS = 2 * LANES
OFF_QA = 0
OFF_KA = OFF_QA + A_WIDTH
OFF_VA = OFF_KA + KA_COLS
OFF_QB = OFF_VA + VA_COLS
OFF_KB = OFF_QB + B_WIDTH
OFF_VB = OFF_KB + B_WIDTH
OFF_GA = OFF_VB + B_WIDTH
OFF_GB = OFF_GA + D_MODEL
PROJ_COLS = OFF_GB + D_MODEL

TOKEN_TILE = 512
MERGE_SUB = 256
FLASH_TQ = 256
FLASH_TK = 1024
DIL_QBLOCK = 128
DIL_STEP_TOKENS = 2048
MOE_ROWS = 1024
MOE_SUB_ROWS = 256

VMEM_LIMIT = 52 * 1024 * 1024

HALF_MODEL = D_MODEL // 2
ROW_CHUNKS = HALF_MODEL // LANES
SC_CORES = 2
SC_SUBCORES = 16
SC_WORKERS = SC_CORES * SC_SUBCORES
SC_WINDOW = 128
SC_SLABS = 4

bf16 = jnp.bfloat16
f32 = jnp.float32


def _cparams(sem):
    return pltpu.CompilerParams(dimension_semantics=sem, vmem_limit_bytes=VMEM_LIMIT)


def _pack_rows(v):
    lo = lax.bitcast_convert_type(v[:, :HALF_MODEL].astype(bf16).astype(f32), jnp.uint32) >> 16
    hi = lax.bitcast_convert_type(v[:, HALF_MODEL:].astype(bf16).astype(f32), jnp.uint32) & jnp.uint32(0xFFFF0000)
    w = lo | hi
    return [w[:, c * LANES:(c + 1) * LANES] for c in range(ROW_CHUNKS)]


def _unpack_rows(chunks):
    w = jnp.concatenate(chunks, axis=1)
    lo = lax.bitcast_convert_type(w << 16, f32)
    hi = lax.bitcast_convert_type(w & jnp.uint32(0xFFFF0000), f32)
    return lo, hi


def _in_proj_kernel(x_ref, nm_ref, w_ref, gqa_ref, gka_ref, gqb_ref, gkb_ref,
                    ca_ref, sa_ref, cb_ref, sb_ref, bd_ref,
                    qa_ref, ka_ref, va_ref, qb1_ref, qb4_ref, qb16_ref, kb1_ref, kb4_ref, kb16_ref,
                    vb1_ref, vb4_ref, vb16_ref, ga_ref, gb_ref, st_ref):
    x = x_ref[...]
    r = lax.rsqrt(jnp.mean(x * x, axis=-1, keepdims=True) + EPS)
    u = (x * r * nm_ref[...]).astype(bf16)
    rows = x.shape[0]
    lane = lax.broadcasted_iota(jnp.int32, (rows, LANES), 1)
    first_half = (lane % HEAD_DIM) < (HEAD_DIM // 2)

    def proj(off, width):
        return jnp.dot(u, w_ref[:, off:off + width], preferred_element_type=f32)

    def norm_rope(off, width, g_ref, c_ref, s_ref):
        z = proj(off, width)
        c = c_ref[...]
        s = s_ref[...]
        chunks = []
        for c0 in range(0, width, NORM_COLS):
            cw = min(NORM_COLS, width - c0)
            zc = z[:, c0:c0 + cw]
            ms = jnp.dot((zc * zc).astype(bf16), bd_ref[:cw, :cw], preferred_element_type=f32) * (1.0 / HEAD_DIM)
            zn = zc * lax.rsqrt(ms + EPS) * g_ref[:, c0:c0 + cw]
            for h0 in range(0, cw, LANES):
                zh = zn[:, h0:h0 + LANES]
                sw = jnp.where(first_half, pltpu.roll(zh, LANES - HEAD_DIM // 2, 1), pltpu.roll(zh, HEAD_DIM // 2, 1))
                chunks.append(zh * c + sw * s)
        return chunks

    def store(chunks, out_ref):
        for j, ch in enumerate(chunks):
            out_ref[:, j * LANES:(j + 1) * LANES] = ch.astype(out_ref.dtype)

    def store_dilated(chunks, out_refs):
        for j, ch in enumerate(chunks):
            st_ref[j] = ch
        for (_, dil), out_ref in zip(WINDOW_DILATIONS, out_refs):
            for r in range(dil):
                for j in range(len(chunks)):
                    src = pl.ds(r, rows // dil, stride=dil) if dil > 1 else pl.ds(0, rows)
                    out_ref[r, :, j * LANES:(j + 1) * LANES] = st_ref[j, src, :].astype(out_ref.dtype)

    store(norm_rope(OFF_QA, A_WIDTH, gqa_ref, ca_ref, sa_ref), qa_ref)
    store(norm_rope(OFF_KA, KA_COLS, gka_ref, ca_ref, sa_ref), ka_ref)
    va = proj(OFF_VA, VA_COLS)
    va_ref[...] = jnp.concatenate(
        [jnp.where(lane < HEAD_DIM, va[:, c0:c0 + LANES], 1.0) for c0 in range(0, VA_COLS, LANES)],
        axis=1).astype(va_ref.dtype)
    store_dilated(norm_rope(OFF_QB, B_WIDTH, gqb_ref, cb_ref, sb_ref), (qb1_ref, qb4_ref, qb16_ref))
    store_dilated(norm_rope(OFF_KB, B_WIDTH, gkb_ref, cb_ref, sb_ref), (kb1_ref, kb4_ref, kb16_ref))
    vb = proj(OFF_VB, B_WIDTH)
    store_dilated([vb[:, c0:c0 + LANES] for c0 in range(0, B_WIDTH, LANES)], (vb1_ref, vb4_ref, vb16_ref))
    ga_ref[...] = jax.nn.sigmoid(proj(OFF_GA, D_MODEL)).astype(ga_ref.dtype)
    gb_ref[...] = jax.nn.sigmoid(proj(OFF_GB, D_MODEL)).astype(gb_ref.dtype)


def _in_proj(x2, S, prm):
    T = x2.shape[0]
    tm = TOKEN_TILE
    n_pos = S // tm
    row = lambda i: (i, 0)
    pos = lambda i: (i % n_pos, 0)
    const = lambda i: (0, 0)
    dils = [dil for _, dil in WINDOW_DILATIONS]
    flat = lambda w: (pl.BlockSpec((tm, w), row), jax.ShapeDtypeStruct((T, w), bf16))
    strided = lambda d: (pl.BlockSpec((d, tm // d, B_WIDTH), lambda i: (0, i, 0)),
                         jax.ShapeDtypeStruct((d, T // d, B_WIDTH), bf16))
    outs = ([flat(A_WIDTH), flat(KA_COLS), flat(VA_COLS)] + 3 * [strided(d) for d in dils]
            + [flat(D_MODEL), flat(D_MODEL)])
    return pl.pallas_call(
        _in_proj_kernel,
        grid=(T // tm,),
        in_specs=[
            pl.BlockSpec((tm, D_MODEL), row),
            pl.BlockSpec((1, D_MODEL), const),
            pl.BlockSpec((D_MODEL, PROJ_COLS), const),
            pl.BlockSpec((1, A_WIDTH), const),
            pl.BlockSpec((1, KA_COLS), const),
            pl.BlockSpec((1, B_WIDTH), const),
            pl.BlockSpec((1, B_WIDTH), const),
            pl.BlockSpec((tm, LANES), pos),
            pl.BlockSpec((tm, LANES), pos),
            pl.BlockSpec((tm, LANES), pos),
            pl.BlockSpec((tm, LANES), pos),
            pl.BlockSpec((NORM_COLS, NORM_COLS), const),
        ],
        out_specs=[spec for spec, _ in outs],
        out_shape=[shape for _, shape in outs],
        scratch_shapes=[pltpu.VMEM((B_WIDTH // LANES, tm, LANES), f32)],
        compiler_params=_cparams(("parallel",)),
        name="in_proj",
    )(x2, prm["norm_mix"], prm["w_in"], prm["gqa"], prm["gka"], prm["gqb"], prm["gkb"],
      prm["cos_a"][S], prm["sin_a"][S], prm["cos_b"][S], prm["sin_b"][S], prm["bd"])


def _flash_kernel(q_ref, k_ref, v_ref, o_ref, s_ref, *, tq, tk, seq):
    lane = lax.broadcasted_iota(jnp.int32, (tq, LANES), 1)
    lo = lane < HEAD_DIM
    q = q_ref[0]
    q01, q23 = q[:, :LANES], q[:, LANES:]
    zero = jnp.zeros_like(q01)
    qs = [jnp.where(lo, q01, zero), jnp.where(lo, zero, q01), jnp.where(lo, q23, zero), jnp.where(lo, zero, q23)]
    n_chunks = seq // tk

    def keys(chunk):
        return k_ref[0, pl.ds(pl.multiple_of(chunk * tk, tk), tk), :]

    def score(h, kk):
        return lax.dot_general(qs[h], kk, (((1,), (1,)), ((), ())), preferred_element_type=f32)

    def attend(chunk, slot, ms, accs, next_chunk):
        vv = v_ref[0, pl.ds(pl.multiple_of(chunk * tk, tk), tk), :]
        kk = None if next_chunk is None else keys(next_chunk)
        new_ms, new_accs = [], []
        for h in range(GROUP_Q):
            if kk is not None:
                s_ref[1 - slot, h] = score(h, kk)
            s = s_ref[slot, h]
            m_new = jnp.maximum(ms[h], jnp.max(s, axis=-1, keepdims=True))
            p = jnp.exp2((s - m_new).astype(bf16))
            alpha = jnp.exp2(ms[h] - m_new)
            new_accs.append(accs[h] * alpha + jnp.dot(p, vv, preferred_element_type=f32))
            new_ms.append(m_new)
        return tuple(new_ms), tuple(new_accs)

    kk0 = keys(0)
    for h in range(GROUP_Q):
        s_ref[0, h] = score(h, kk0)

    def body(jj, carry):
        ms, accs = attend(2 * jj, 0, *carry, 2 * jj + 1)
        return attend(2 * jj + 1, 1, ms, accs, 2 * jj + 2)

    m0 = tuple(jnp.full((tq, 1), NEG_BIG, f32) for _ in range(GROUP_Q))
    acc0 = tuple(jnp.zeros((tq, LANES), f32) for _ in range(GROUP_Q))
    ms, accs = lax.fori_loop(0, n_chunks // 2 - 1, body, (m0, acc0), unroll=True)
    ms, accs = attend(n_chunks - 2, 0, ms, accs, n_chunks - 1)
    _, accs = attend(n_chunks - 1, 1, ms, accs, None)
    heads = [acc / jnp.where(lo, pltpu.roll(acc, HEAD_DIM, 1), 1.0) for acc in accs]
    out01 = jnp.where(lo, heads[0], pltpu.roll(heads[1], HEAD_DIM, 1))
    out23 = jnp.where(lo, heads[2], pltpu.roll(heads[3], HEAD_DIM, 1))
    o_ref[0] = jnp.concatenate([out01, out23], axis=1).astype(o_ref.dtype)


def _flash(qa, ka, va):
    B, S, _ = qa.shape
    tq, tk = FLASH_TQ, FLASH_TK
    return pl.pallas_call(
        functools.partial(_flash_kernel, tq=tq, tk=tk, seq=S),
        grid=(B, A_KV_HEADS, S // tq),
        in_specs=[
            pl.BlockSpec((1, tq, GROUP_Q * HEAD_DIM), lambda b, g, i: (b, i, g)),
            pl.BlockSpec((1, S, LANES), lambda b, g, i: (b, 0, g)),
            pl.BlockSpec((1, S, LANES), lambda b, g, i: (b, 0, g)),
        ],
        out_specs=pl.BlockSpec((1, tq, GROUP_Q * HEAD_DIM), lambda b, g, i: (b, i, g)),
        out_shape=jax.ShapeDtypeStruct((B, S, A_WIDTH), bf16),
        scratch_shapes=[pltpu.VMEM((2, GROUP_Q, tq, tk), f32)],
        compiler_params=_cparams(("parallel", "parallel", "arbitrary")),
        name="flash_gqa",
    )(qa, ka, va)


def _dilated_kernel(q_ref, k_ref, v_ref, o_ref, lse_ref, so_ref, sl_ref, *, length, tms, dil):
    i = pl.program_id(1)
    lane = lax.broadcasted_iota(jnp.int32, (DIL_QBLOCK, LANES), 1)
    lo = lane < HEAD_DIM
    n_pairs = B_WIDTH // LANES

    def unit(u, carry):
        sb, r = u // dil, u % dil
        r0 = pl.multiple_of(sb * DIL_QBLOCK, DIL_QBLOCK)
        m0 = i * tms + r0
        start = pl.multiple_of(jnp.clip(m0 - HALF_WINDOW, 0, length - KEY_SPAN), HALF_WINDOW)
        qpos = m0 + lax.broadcasted_iota(jnp.int32, (DIL_QBLOCK, KEY_SPAN), 0)
        kpos = start + lax.broadcasted_iota(jnp.int32, (DIL_QBLOCK, KEY_SPAN), 1)
        valid = jnp.abs(kpos - qpos) <= HALF_WINDOW
        rows = pl.ds(r0 * dil + r, DIL_QBLOCK, stride=dil) if dil > 1 else pl.ds(r0, DIL_QBLOCK)
        scores = []
        for c in range(n_pairs):
            cs = slice(c * LANES, (c + 1) * LANES)
            qp = q_ref[r, 0, pl.ds(r0, DIL_QBLOCK), cs]
            kp = k_ref[r, 0, pl.ds(start, KEY_SPAN), cs]
            zero = jnp.zeros_like(qp)
            for half in range(2):
                qm = jnp.where(lo, qp, zero) if half == 0 else jnp.where(lo, zero, qp)
                scores.append(lax.dot_general(qm, kp, (((1,), (1,)), ((), ())), preferred_element_type=f32))
        for c in range(n_pairs):
            vp = v_ref[r, 0, pl.ds(start, KEY_SPAN), c * LANES:(c + 1) * LANES]
            outs, lses = [], []
            for half in range(2):
                s = jnp.where(valid, scores[2 * c + half], NEG_BIG)
                m = jnp.max(s, axis=-1, keepdims=True)
                e = jnp.exp2(s - m)
                l = jnp.sum(e, axis=-1, keepdims=True)
                outs.append(jnp.dot(e.astype(bf16), vp, preferred_element_type=f32) * (1.0 / l))
                lses.append(m + jnp.log2(l))
            so_ref[c, rows, :] = jnp.where(lo, outs[0], outs[1])
            sl_ref[c, rows, :] = jnp.where(lo, lses[0], lses[1])
        return carry

    lax.fori_loop(0, (tms // DIL_QBLOCK) * dil, unit, 0, unroll=4)
    o_ref[0] = jnp.concatenate([so_ref[c] for c in range(n_pairs)], axis=1).astype(o_ref.dtype)
    lse_ref[0] = jnp.concatenate([sl_ref[c] for c in range(n_pairs)], axis=1)


def _dilated(qb, kb, vb, B, S, dil):
    length = S // dil
    tms = DIL_STEP_TOKENS // dil
    view = lambda a: a.reshape(dil, B, length, B_WIDTH)
    o, lse = pl.pallas_call(
        functools.partial(_dilated_kernel, length=length, tms=tms, dil=dil),
        grid=(B, length // tms),
        in_specs=[
            pl.BlockSpec((dil, 1, tms, B_WIDTH), lambda b, i: (0, b, i, 0)),
            pl.BlockSpec((dil, 1, length, B_WIDTH), lambda b, i: (0, b, 0, 0)),
            pl.BlockSpec((dil, 1, length, B_WIDTH), lambda b, i: (0, b, 0, 0)),
        ],
        out_specs=[
            pl.BlockSpec((1, tms * dil, B_WIDTH), lambda b, i: (b, i, 0)),
            pl.BlockSpec((1, tms * dil, B_WIDTH), lambda b, i: (b, i, 0)),
        ],
        out_shape=[
            jax.ShapeDtypeStruct((B, S, B_WIDTH), bf16),
            jax.ShapeDtypeStruct((B, S, B_WIDTH), f32),
        ],
        scratch_shapes=[pltpu.VMEM((B_WIDTH // LANES, tms * dil, LANES), f32),
                        pltpu.VMEM((B_WIDTH // LANES, tms * dil, LANES), f32)],
        compiler_params=_cparams(("parallel", "arbitrary")),
        name=f"dilated_{dil}",
    )(view(qb), view(kb), view(vb))
    return o.reshape(B * S, B_WIDTH), lse.reshape(B * S, B_WIDTH)


def _first_index_of_max(cur, idx, n):
    mx = jnp.max(cur, axis=0, keepdims=True)
    return jnp.min(jnp.where(cur == mx, idx, n), axis=0, keepdims=True)


def _merge_route_kernel(x_ref, ya_ref, o1_ref, o2_ref, o3_ref, l1_ref, l2_ref, l3_ref, ga_ref, gb_ref,
                        wa_ref, wb_ref, wo_ref, nf_ref, rhi_ref, rlo_ref, bias_ref, tri_ref,
                        h_ref, u_ref, idx_ref, wt_ref, rank_ref, cnt_ref, run_ref):
    step = pl.program_id(0)

    @pl.when(step == 0)
    def _():
        run_ref[...] = jnp.zeros_like(run_ref)

    tm = x_ref.shape[0]
    sub = tri_ref.shape[0]

    def dense(j):
        rs = slice(j * sub, (j + 1) * sub)
        l1, l2, l3 = l1_ref[rs, :], l2_ref[rs, :], l3_ref[rs, :]
        lm = jnp.maximum(jnp.maximum(l1, l2), l3)
        e1, e2, e3 = jnp.exp2(l1 - lm), jnp.exp2(l2 - lm), jnp.exp2(l3 - lm)
        yb = ((e1 * o1_ref[rs, :].astype(f32) + e2 * o2_ref[rs, :].astype(f32) + e3 * o3_ref[rs, :].astype(f32))
              / (e1 + e2 + e3))
        pa = jnp.dot(ya_ref[rs, :], wa_ref[...], preferred_element_type=f32)
        pb = jnp.dot(yb.astype(bf16), wb_ref[...], preferred_element_type=f32)
        merged = ga_ref[rs, :].astype(f32) * pa + gb_ref[rs, :].astype(f32) * pb
        h = x_ref[rs, :] + jnp.dot(merged.astype(bf16), wo_ref[...], preferred_element_type=f32)
        h_ref[rs, :] = h
        r = lax.rsqrt(jnp.mean(h * h, axis=-1, keepdims=True) + EPS)
        uf = h * r * nf_ref[...]
        u_hi = uf.astype(bf16)
        for c, chunk in enumerate(_pack_rows(uf)):
            u_ref[c, rs, :] = chunk
        u_lo = (uf - u_hi.astype(f32)).astype(bf16)
        return (jnp.dot(u_hi, rhi_ref[...], preferred_element_type=f32)
                + jnp.dot(u_lo, rhi_ref[...], preferred_element_type=f32)
                + jnp.dot(u_hi, rlo_ref[...], preferred_element_type=f32))

    def route(j, logits):
        scores = jax.nn.sigmoid(logits.T[:N_EXPERTS, :])
        sel = scores + bias_ref[...]
        i8 = lax.broadcasted_iota(jnp.int32, (EXPERTS_PER_GROUP, sub), 0)
        neg_inf = jnp.float32(-jnp.inf)
        gscore = jnp.zeros((N_GROUPS, sub), f32)
        for g in range(N_GROUPS):
            blk = sel[g * EXPERTS_PER_GROUP:(g + 1) * EXPERTS_PER_GROUP, :]
            top1 = jnp.max(blk, axis=0, keepdims=True)
            first = _first_index_of_max(blk, i8, EXPERTS_PER_GROUP)
            top2 = jnp.max(jnp.where(i8 == first, neg_inf, blk), axis=0, keepdims=True)
            gscore = jnp.where(i8 == g, top1 + top2, gscore)
        gkeep = jnp.zeros((N_GROUPS, sub), jnp.bool_)
        cur = gscore
        for _ in range(TOPK_GROUPS):
            pick = i8 == _first_index_of_max(cur, i8, N_GROUPS)
            gkeep = jnp.logical_or(gkeep, pick)
            cur = jnp.where(pick, neg_inf, cur)
        cur = jnp.concatenate(
            [jnp.where(gkeep[g:g + 1, :], sel[g * EXPERTS_PER_GROUP:(g + 1) * EXPERTS_PER_GROUP, :], NEG_BIG)
             for g in range(N_GROUPS)], axis=0)

        ie = lax.broadcasted_iota(jnp.int32, (N_EXPERTS, sub), 0)
        ik = lax.broadcasted_iota(jnp.int32, (TOP_K, sub), 0)
        idx_out = jnp.zeros((TOP_K, sub), jnp.int32)
        w_out = jnp.zeros((TOP_K, sub), f32)
        chosen = jnp.zeros((N_EXPERTS, sub), jnp.bool_)
        for k in range(TOP_K):
            fi = _first_index_of_max(cur, ie, N_EXPERTS)
            pick = ie == fi
            wk = jnp.sum(jnp.where(pick, scores, 0.0), axis=0, keepdims=True)
            idx_out = jnp.where(ik == k, fi, idx_out)
            w_out = jnp.where(ik == k, wk, w_out)
            chosen = jnp.logical_or(chosen, pick)
            cur = jnp.where(pick, neg_inf, cur)
        w_out = w_out / jnp.sum(w_out, axis=0, keepdims=True) * ROUTE_SCALE
        wt_ref[:, j * sub:(j + 1) * sub] = w_out

        sel01 = jnp.where(chosen, 1.0, 0.0)
        before = jnp.dot(sel01.astype(bf16), tri_ref[...], preferred_element_type=f32) + run_ref[:, 0:1]
        rank_out = jnp.zeros((TOP_K, sub), f32)
        for k in range(TOP_K):
            rk = jnp.sum(jnp.where(ie == idx_out[k:k + 1, :], before, 0.0), axis=0, keepdims=True)
            rank_out = jnp.where(ik == k, rk, rank_out)
        rank_i = rank_out.astype(jnp.int32)
        for w in range(sub // SC_WINDOW):
            win = j * (sub // SC_WINDOW) + w
            idx_ref[win] = idx_out[:, w * SC_WINDOW:(w + 1) * SC_WINDOW]
            rank_ref[win] = rank_i[:, w * SC_WINDOW:(w + 1) * SC_WINDOW]
        run_ref[...] = run_ref[...] + jnp.sum(sel01, axis=1, keepdims=True)

    logits = [dense(j) for j in range(tm // sub)]
    for j in range(tm // sub):
        route(j, logits[j])
    cnt_ref[...] = run_ref[...].astype(jnp.int32)


def _merge_route(x2, ya, outs, lses, sga, sgb, prm):
    T = x2.shape[0]
    tm = TOKEN_TILE
    row = lambda i: (i, 0)
    col = lambda i: (0, i)
    const = lambda i: (0, 0)
    tok = lambda w: pl.BlockSpec((tm, w), row)
    full = lambda a: pl.BlockSpec(a.shape, const)
    win = pl.BlockSpec((tm // SC_WINDOW, TOP_K, SC_WINDOW), lambda i: (i, 0, 0))
    return pl.pallas_call(
        _merge_route_kernel,
        grid=(T // tm,),
        in_specs=[tok(D_MODEL), tok(A_WIDTH), tok(B_WIDTH), tok(B_WIDTH), tok(B_WIDTH),
                  tok(B_WIDTH), tok(B_WIDTH), tok(B_WIDTH), tok(D_MODEL), tok(D_MODEL),
                  full(prm["w_branch_a"]), full(prm["w_branch_b"]), full(prm["w_out"]), full(prm["norm_ffn"]),
                  full(prm["router_hi"]), full(prm["router_lo"]), full(prm["router_bias"]), full(prm["tri"])],
        out_specs=[tok(D_MODEL), pl.BlockSpec((ROW_CHUNKS, tm, LANES), lambda i: (0, i, 0)),
                   win, pl.BlockSpec((TOP_K, tm), col), win,
                   pl.BlockSpec((N_EXPERTS, LANES), const)],
        out_shape=[jax.ShapeDtypeStruct((T, D_MODEL), f32), jax.ShapeDtypeStruct((ROW_CHUNKS, T, LANES), jnp.uint32),
                   jax.ShapeDtypeStruct((T // SC_WINDOW, TOP_K, SC_WINDOW), jnp.int32),
                   jax.ShapeDtypeStruct((TOP_K, T), f32),
                   jax.ShapeDtypeStruct((T // SC_WINDOW, TOP_K, SC_WINDOW), jnp.int32),
                   jax.ShapeDtypeStruct((N_EXPERTS, LANES), jnp.int32)],
        scratch_shapes=[pltpu.VMEM((N_EXPERTS, LANES), f32)],
        compiler_params=_cparams(("arbitrary",)),
        name="merge_route",
    )(x2, ya, *outs, *lses, sga, sgb,
      prm["w_branch_a"], prm["w_branch_b"], prm["w_out"], prm["norm_ffn"],
      prm["router_hi"], prm["router_lo"], prm["router_bias"], prm["tri"])


def _experts_kernel(be_ref, nu_ref, nv_ref, x_ref, wg_ref, wu_ref, wd_ref, y_ref, wgu_s, wd_s):
    i = pl.program_id(0)

    @pl.when(jnp.logical_or(i == 0, be_ref[i] != be_ref[jnp.maximum(i - 1, 0)]))
    def _():
        wgu_s[:, :D_EXPERT] = wg_ref[0].astype(bf16)
        wgu_s[:, D_EXPERT:] = wu_ref[0].astype(bf16)
        wd_s[...] = wd_ref[0].astype(bf16)

    @pl.when(i < nu_ref[0])
    def _():
        n_sub = x_ref.shape[1] // MOE_SUB_ROWS

        def up(j):
            r0 = j * MOE_SUB_ROWS
            lo, hi = _unpack_rows([x_ref[c, r0:r0 + MOE_SUB_ROWS, :] for c in range(ROW_CHUNKS)])
            live = r0 + lax.broadcasted_iota(jnp.int32, lo.shape, 0) < nv_ref[i]
            x = jnp.concatenate([jnp.where(live, lo, 0.0), jnp.where(live, hi, 0.0)], axis=1).astype(bf16)
            return jnp.dot(x, wgu_s[...], preferred_element_type=f32)

        def down(j, gu):
            g, u = gu[:, :D_EXPERT], gu[:, D_EXPERT:]
            a = g * jax.nn.sigmoid(g) * u
            y = jnp.dot(a.astype(bf16), wd_s[...], preferred_element_type=f32)
            r0 = j * MOE_SUB_ROWS
            for c, chunk in enumerate(_pack_rows(y)):
                y_ref[c, r0:r0 + MOE_SUB_ROWS, :] = chunk

        gu_prev = up(0)
        for j in range(1, n_sub):
            gu_next = up(j)
            down(j - 1, gu_prev)
            gu_prev = gu_next
        down(n_sub - 1, gu_prev)

    @pl.when(i >= nu_ref[0])
    def _():
        y_ref[...] = jnp.zeros_like(y_ref)


def _experts(xs, blk_e, n_used, n_valid, prm):
    n_rows = xs.shape[1]
    bm = MOE_ROWS
    rows = lambda i, be, nu, nv: (0, jnp.minimum(i, nu[0] - 1), 0)
    return pl.pallas_call(
        _experts_kernel,
        grid_spec=pltpu.PrefetchScalarGridSpec(
            num_scalar_prefetch=3,
            grid=(n_rows // bm,),
            in_specs=[
                pl.BlockSpec((ROW_CHUNKS, bm, LANES), rows),
                pl.BlockSpec((1, D_MODEL, D_EXPERT), lambda i, be, nu, nv: (be[i], 0, 0)),
                pl.BlockSpec((1, D_MODEL, D_EXPERT), lambda i, be, nu, nv: (be[i], 0, 0)),
                pl.BlockSpec((1, D_EXPERT, D_MODEL), lambda i, be, nu, nv: (be[i], 0, 0)),
            ],
            out_specs=pl.BlockSpec((ROW_CHUNKS, bm, LANES), lambda i, be, nu, nv: (0, i, 0)),
            scratch_shapes=[pltpu.VMEM((D_MODEL, 2 * D_EXPERT), bf16), pltpu.VMEM((D_EXPERT, D_MODEL), bf16)],
        ),
        out_shape=jax.ShapeDtypeStruct((ROW_CHUNKS, n_rows, LANES), jnp.uint32),
        compiler_params=_cparams(("arbitrary",)),
        name="experts",
    )(blk_e, n_used, n_valid, xs, prm["expert_gate"], prm["expert_up"], prm["expert_down"])


def _sc_mesh():
    return plsc.VectorSubcoreMesh(core_axis_name="c", subcore_axis_name="s")


def _sc_dispatch(u2p, dest, n_rows):
    T = u2p.shape[1]
    per_worker = T // SC_WORKERS
    n_it = per_worker // SC_WINDOW

    @functools.partial(
        pl.kernel, mesh=_sc_mesh(),
        out_type=jax.ShapeDtypeStruct((ROW_CHUNKS, n_rows, LANES), jnp.uint32),
        scratch_types=[pltpu.VMEM((TOP_K, SC_WINDOW), jnp.int32),
                       pltpu.VMEM((ROW_CHUNKS, SC_WINDOW, LANES), jnp.uint32),
                       pltpu.SemaphoreType.DMA, pltpu.SemaphoreType.DMA],
    )
    def k(u_hbm, dest_hbm, xs_hbm, idx_v, rows_v, sem_in, sem_out):
        wid = lax.axis_index("s") * SC_CORES + lax.axis_index("c")

        @pl.loop(0, n_it)
        def _(it):
            t0 = wid * per_worker + it * SC_WINDOW
            pltpu.sync_copy(dest_hbm.at[wid * n_it + it], idx_v)
            loads = [pltpu.async_copy(u_hbm.at[c].at[pl.ds(t0, SC_WINDOW)], rows_v.at[c], sem_in)
                     for c in range(ROW_CHUNKS)]
            for cp in loads:
                cp.wait()
            stores = [pltpu.async_copy(rows_v.at[c], xs_hbm.at[c].at[idx_v.at[kk]], sem_out)
                      for kk in range(TOP_K) for c in range(ROW_CHUNKS)]
            for cp in stores:
                cp.wait()

    return k(u2p, dest)


def _sc_combine(ys, dest):
    n_windows = dest.shape[0]
    T = n_windows * SC_WINDOW
    per_worker = T // SC_WORKERS
    n_it = per_worker // SC_WINDOW

    @functools.partial(
        pl.kernel, mesh=_sc_mesh(),
        out_type=jax.ShapeDtypeStruct((TOP_K, ROW_CHUNKS, T, LANES), jnp.uint32),
        scratch_types=[pltpu.VMEM((TOP_K, SC_WINDOW), jnp.int32),
                       pltpu.VMEM((SC_SLABS, SC_WINDOW, LANES), jnp.uint32),
                       pltpu.SemaphoreType.DMA, pltpu.SemaphoreType.DMA],
    )
    def k(ys_hbm, dest_hbm, yg_hbm, idx_v, rows_v, sem_in, sem_out):
        wid = lax.axis_index("s") * SC_CORES + lax.axis_index("c")

        @pl.loop(0, n_it)
        def _(it):
            t0 = wid * per_worker + it * SC_WINDOW
            pltpu.sync_copy(dest_hbm.at[wid * n_it + it], idx_v)
            for c in range(ROW_CHUNKS):
                for k0 in range(0, TOP_K, SC_SLABS):
                    loads = [pltpu.async_copy(ys_hbm.at[c].at[idx_v.at[k0 + j]], rows_v.at[j], sem_in)
                             for j in range(SC_SLABS)]
                    for cp in loads:
                        cp.wait()
                    stores = [pltpu.async_copy(rows_v.at[j], yg_hbm.at[k0 + j].at[c].at[pl.ds(t0, SC_WINDOW)], sem_out)
                              for j in range(SC_SLABS)]
                    for cp in stores:
                        cp.wait()

    return k(ys, dest)


def _final_kernel(h_ref, u_ref, yg_ref, wt_ref, p_ref, sgu_ref, sd_ref, pg_ref, pp_ref, o_ref):
    wl = wt_ref[...]
    wt = jnp.concatenate([wl, jnp.zeros((LANES - TOP_K, wl.shape[1]), f32)], axis=0).T
    r_lo = r_hi = None
    for k in range(TOP_K):
        lo, hi = _unpack_rows([yg_ref[k, c] for c in range(ROW_CHUNKS)])
        wk = wt[:, k:k + 1]
        r_lo = lo * wk if k == 0 else r_lo + lo * wk
        r_hi = hi * wk if k == 0 else r_hi + hi * wk
    routed = jnp.concatenate([r_lo, r_hi], axis=1)
    u_lo, u_hi = _unpack_rows([u_ref[c] for c in range(ROW_CHUNKS)])
    un = jnp.concatenate([u_lo, u_hi], axis=1).astype(bf16)
    gu = jnp.dot(un, sgu_ref[...], preferred_element_type=f32)
    g, u = gu[:, :D_SHARED], gu[:, D_SHARED:]
    shared = jnp.dot((g * jax.nn.sigmoid(g) * u).astype(bf16), sd_ref[...], preferred_element_type=f32)
    h = h_ref[...] + (routed + shared)
    gate = jax.nn.sigmoid(jnp.dot(h.astype(bf16), pg_ref[...], preferred_element_type=f32))
    emb = jnp.dot(p_ref[...].astype(bf16), pp_ref[...], preferred_element_type=f32)
    o_ref[...] = h + gate * emb


def _final(h1, u2p, yg, wts, p2, prm):
    T = h1.shape[0]
    tm = TOKEN_TILE
    row = lambda i: (i, 0)
    const = lambda i: (0, 0)
    full = lambda a: pl.BlockSpec(a.shape, const)
    return pl.pallas_call(
        _final_kernel,
        grid=(T // tm,),
        in_specs=[pl.BlockSpec((tm, D_MODEL), row),
                  pl.BlockSpec((ROW_CHUNKS, tm, LANES), lambda i: (0, i, 0)),
                  pl.BlockSpec((TOP_K, ROW_CHUNKS, tm, LANES), lambda i: (0, 0, i, 0)),
                  pl.BlockSpec((TOP_K, tm), lambda i: (0, i)),
                  pl.BlockSpec((tm, PLE_DIM), row),
                  full(prm["shared_gu"]), full(prm["shared_down"]), full(prm["ple_gate"]), full(prm["ple_proj"])],
        out_specs=pl.BlockSpec((tm, D_MODEL), row),
        out_shape=jax.ShapeDtypeStruct((T, D_MODEL), f32),
        compiler_params=_cparams(("parallel",)),
        name="final",
    )(h1, u2p, yg, wts, p2, prm["shared_gu"], prm["shared_down"], prm["ple_gate"], prm["ple_proj"])


def _deinterleave(n_heads):
    base = jnp.concatenate([jnp.arange(0, HEAD_DIM, 2), jnp.arange(1, HEAD_DIM, 2)])
    return (jnp.arange(n_heads)[:, None] * HEAD_DIM + base[None, :]).reshape(-1)


def _rope_tables(ang):
    c, s = jnp.cos(ang), jnp.sin(ang)
    return jnp.tile(jnp.concatenate([c, c], axis=-1), (1, 2)), jnp.tile(jnp.concatenate([-s, s], axis=-1), (1, 2))


def _axial_angles(S):
    rows = S // GRID_W
    row = jnp.repeat(jnp.arange(rows, dtype=f32), GRID_W)
    col = jnp.tile(jnp.arange(GRID_W, dtype=f32), rows)
    half = HEAD_DIM // 2
    inv = ROPE_THETA ** (-jnp.arange(0, half, 2, dtype=f32) / half)
    return jnp.concatenate([row[:, None] * inv, col[:, None] * inv], axis=-1)


def _linear_angles(S):
    t = jnp.arange(S, dtype=f32)
    inv = ROPE_THETA ** (-jnp.arange(0, HEAD_DIM, 2, dtype=f32) / HEAD_DIM)
    return t[:, None] * inv


def _prepare(seq_lens, norm_mix, w_in, q_norm_a, k_norm_a, q_norm_b, k_norm_b, w_branch_a, w_branch_b, w_out,
             norm_ffn, router_w, router_bias, expert_gate, expert_up, expert_down, shared_gate, shared_up,
             shared_down, ple_proj, ple_gate):
    cuts = [A_WIDTH, A_WIDTH + A_KV_WIDTH, A_WIDTH + 2 * A_KV_WIDTH,
            A_WIDTH + 2 * A_KV_WIDTH + B_WIDTH, A_WIDTH + 2 * A_KV_WIDTH + 2 * B_WIDTH,
            A_WIDTH + 2 * A_KV_WIDTH + 3 * B_WIDTH, A_WIDTH + 2 * A_KV_WIDTH + 3 * B_WIDTH + D_MODEL]
    wqa, wka, wva, wqb, wkb, wvb, wga, wgb = jnp.split(w_in, cuts, axis=-1)
    wqa = wqa[:, _deinterleave(A_Q_HEADS)]
    wka = wka[:, _deinterleave(A_KV_HEADS)]
    wqb = wqb[:, _deinterleave(B_HEADS)]
    wkb = wkb[:, _deinterleave(B_HEADS)]
    k0, k1 = wka[:, :HEAD_DIM], wka[:, HEAD_DIM:]
    v0, v1 = wva[:, :HEAD_DIM], wva[:, HEAD_DIM:]
    vz = jnp.zeros_like(v0)
    w_all = jnp.concatenate([wqa, k0, k0, k1, k1, v0, vz, v1, vz, wqb, wkb, wvb, wga, wgb], axis=-1).astype(bf16)
    perm = _deinterleave(1)
    scale = HEAD_DIM ** -0.5
    tile = lambda g, n: jnp.tile(g[perm], n)[None, :]
    router_pad = jnp.pad(router_w, ((0, 0), (0, LANES - N_EXPERTS)))
    router_hi = router_pad.astype(bf16)
    prm = {
        "norm_mix": norm_mix[None, :], "w_in": w_all,
        "gqa": tile(q_norm_a, A_Q_HEADS) * (scale * LOG2_E), "gka": tile(k_norm_a, KA_COLS // HEAD_DIM),
        "gqb": tile(q_norm_b, B_HEADS) * (scale * LOG2_E), "gkb": tile(k_norm_b, B_HEADS),
        "bd": jnp.kron(jnp.eye(NORM_COLS // HEAD_DIM, dtype=f32), jnp.ones((HEAD_DIM, HEAD_DIM), f32)).astype(bf16),
        "w_branch_a": w_branch_a.astype(bf16), "w_branch_b": w_branch_b.astype(bf16), "w_out": w_out.astype(bf16),
        "norm_ffn": norm_ffn[None, :],
        "router_hi": router_hi, "router_lo": (router_pad - router_hi.astype(f32)).astype(bf16),
        "router_bias": router_bias[:, None],
        "tri": (jnp.arange(MERGE_SUB)[:, None] < jnp.arange(MERGE_SUB)[None, :]).astype(bf16),
        "expert_gate": expert_gate, "expert_up": expert_up, "expert_down": expert_down,
        "shared_gu": jnp.concatenate([shared_gate, shared_up], axis=-1).astype(bf16),
        "shared_down": shared_down.astype(bf16),
        "ple_gate": ple_gate.astype(bf16), "ple_proj": ple_proj.astype(bf16),
        "cos_a": {}, "sin_a": {}, "cos_b": {}, "sin_b": {},
    }
    for S in seq_lens:
        prm["cos_a"][S], prm["sin_a"][S] = _rope_tables(_axial_angles(S))
        prm["cos_b"][S], prm["sin_b"][S] = _rope_tables(_linear_angles(S))
    return prm


def _dispatch_plan(idx, rank, counts, T):
    bm = MOE_ROWS
    n_rows = (T * TOP_K + N_EXPERTS * bm) // bm * bm
    n_blocks = n_rows // bm
    padded = (counts + bm - 1) // bm * bm
    pad_end = jnp.cumsum(padded)
    pad_start = pad_end - padded
    dest = pad_start[idx] + rank
    blk_start = jnp.arange(n_blocks, dtype=jnp.int32) * bm
    blk_e = jnp.minimum(jnp.sum(pad_end[None, :] <= blk_start[:, None], axis=1), N_EXPERTS - 1).astype(jnp.int32)
    n_used = (pad_end[-1:] // bm).astype(jnp.int32)
    own = blk_e[:, None] == jnp.arange(N_EXPERTS, dtype=jnp.int32)[None, :]
    real_end = jnp.sum(jnp.where(own, (pad_start + counts)[None, :], 0), axis=1)
    n_valid = jnp.clip(real_end - blk_start, 0, bm).astype(jnp.int32)
    return dest, n_rows, blk_e, n_used, n_valid


def _layer(x, p, prm):
    B, S, _ = x.shape
    T = B * S
    x2 = x.reshape(T, D_MODEL)
    qa, ka, va, *qkv_b, sga, sgb = _in_proj(x2, S, prm)
    shape3 = lambda a: a.reshape(B, S, a.shape[-1])
    ya = _flash(shape3(qa), shape3(ka), shape3(va)).reshape(T, A_WIDTH)
    n_dil = len(WINDOW_DILATIONS)
    outs, lses = [], []
    for j, (_, dil) in enumerate(WINDOW_DILATIONS):
        o_d, lse_d = _dilated(qkv_b[j], qkv_b[n_dil + j], qkv_b[2 * n_dil + j], B, S, dil)
        outs.append(o_d)
        lses.append(lse_d)
    h1, u2p, idx, wts, rank, cnt = _merge_route(x2, ya, outs, lses, sga, sgb, prm)
    dest, n_rows, blk_e, n_used, n_valid = _dispatch_plan(idx, rank, cnt[:, 0], T)
    xs = _sc_dispatch(u2p, dest, n_rows)
    ys = _experts(xs, blk_e, n_used, n_valid, prm)
    yg = _sc_combine(ys, dest)
    out = _final(h1, u2p, yg, wts, p.reshape(T, PLE_DIM), prm)
    return out.reshape(B, S, D_MODEL)


def kernel(x_prompt, x_sample, p_prompt, p_sample, norm_mix, w_in, q_norm_a, k_norm_a, q_norm_b, k_norm_b,
           w_branch_a, w_branch_b, w_out, norm_ffn, router_w, router_bias, expert_gate, expert_up, expert_down,
           shared_gate, shared_up, shared_down, ple_proj, ple_gate):
    weights = (norm_mix, w_in, q_norm_a, k_norm_a, q_norm_b, k_norm_b, w_branch_a, w_branch_b, w_out, norm_ffn,
               router_w, router_bias, expert_gate, expert_up, expert_down, shared_gate, shared_up, shared_down,
               ple_proj, ple_gate)
    depth = norm_mix.shape[0]
    seq_lens = sorted({x_prompt.shape[1], x_sample.shape[1]})
    hp, hs = x_prompt, x_sample
    for i in range(depth):
        prm = _prepare(seq_lens, *[w[i] for w in weights])
        hp = _layer(hp, p_prompt[i], prm)
        hs = _layer(hs, p_sample[i], prm)
    return hp, hs
```

```python
import functools

import jax
import jax.numpy as jnp
from jax import lax
from jax.experimental import pallas as pl
from jax.experimental.pallas import tpu as pltpu
from jax.experimental.pallas import tpu_sc as plsc

D_MODEL = 1024
HEAD_DIM = 64
A_Q_HEADS = 8
A_KV_HEADS = 2
B_HEADS = 6
GRID_W = 64
ROPE_THETA = 10000.0
WINDOW_DILATIONS = ((128, 1), (512, 4), (2048, 16))
N_EXPERTS = 64
TOP_K = 8
N_GROUPS = 8
TOPK_GROUPS = 4
D_EXPERT = 256
D_SHARED = 256
ROUTE_SCALE = 2.5
PLE_DIM = 256
NEG_BIG = -1e30
EPS = 1e-6
LOG2_E = 1.4426950408889634

A_WIDTH = A_Q_HEADS * HEAD_DIM
A_KV_WIDTH = A_KV_HEADS * HEAD_DIM
B_WIDTH = B_HEADS * HEAD_DIM
GROUP_Q = A_Q_HEADS // A_KV_HEADS
EXPERTS_PER_GROUP = N_EXPERTS // N_GROUPS

LANES = 128
NORM_COLS = 256
HALF_WINDOW = 64
KEY_SPAN = 256

KA_COLS = 2 * LANES
VA_COLS = 2 * LANES
OFF_QA = 0
OFF_KA = OFF_QA + A_WIDTH
OFF_VA = OFF_KA + KA_COLS
OFF_QB = OFF_VA + VA_COLS
OFF_KB = OFF_QB + B_WIDTH
OFF_VB = OFF_KB + B_WIDTH
OFF_GA = OFF_VB + B_WIDTH
OFF_GB = OFF_GA + D_MODEL
PROJ_COLS = OFF_GB + D_MODEL

TOKEN_TILE = 512
MERGE_SUB = 256
FLASH_TQ = 256
FLASH_TK = 1024
DIL_QBLOCK = 128
DIL_STEP_TOKENS = 2048
GROUP_SPLIT = 2
MOE_ROWS = 1024
MOE_SUB_ROWS = 256

VMEM_LIMIT = 52 * 1024 * 1024

HALF_MODEL = D_MODEL // 2
ROW_CHUNKS = HALF_MODEL // LANES
SC_CORES = 2
SC_SUBCORES = 16
SC_WORKERS = SC_CORES * SC_SUBCORES
SC_WINDOW = 128
SC_SLABS = 4

bf16 = jnp.bfloat16
f32 = jnp.float32


def _cparams(sem):
    return pltpu.CompilerParams(dimension_semantics=sem, vmem_limit_bytes=VMEM_LIMIT)


def _pack_rows(v):
    lo = lax.bitcast_convert_type(v[:, :HALF_MODEL].astype(bf16).astype(f32), jnp.uint32) >> 16
    hi = lax.bitcast_convert_type(v[:, HALF_MODEL:].astype(bf16).astype(f32), jnp.uint32) & jnp.uint32(0xFFFF0000)
    w = lo | hi
    return [w[:, c * LANES:(c + 1) * LANES] for c in range(ROW_CHUNKS)]


def _unpack_rows(chunks):
    w = jnp.concatenate(chunks, axis=1)
    lo = lax.bitcast_convert_type(w << 16, f32)
    hi = lax.bitcast_convert_type(w & jnp.uint32(0xFFFF0000), f32)
    return lo, hi


def _in_proj_kernel(x_ref, nm_ref, w_ref, gqa_ref, gka_ref, gqb_ref, gkb_ref,
                    ca_ref, sa_ref, cb_ref, sb_ref, bd_ref,
                    qa_ref, ka_ref, va_ref, qb1_ref, qb4_ref, qb16_ref, kb1_ref, kb4_ref, kb16_ref,
                    vb1_ref, vb4_ref, vb16_ref, ga_ref, gb_ref, st_ref):
    x = x_ref[...]
    r = lax.rsqrt(jnp.mean(x * x, axis=-1, keepdims=True) + EPS)
    u = (x * r * nm_ref[...]).astype(bf16)
    rows = x.shape[0]
    lane = lax.broadcasted_iota(jnp.int32, (rows, LANES), 1)
    first_half = (lane % HEAD_DIM) < (HEAD_DIM // 2)

    def proj(off, width):
        return jnp.dot(u, w_ref[:, off:off + width], preferred_element_type=f32)

    def norm_rope(off, width, g_ref, c_ref, s_ref):
        z = proj(off, width)
        c = c_ref[...]
        s = s_ref[...]
        chunks = []
        for c0 in range(0, width, NORM_COLS):
            cw = min(NORM_COLS, width - c0)
            zc = z[:, c0:c0 + cw]
            ms = jnp.dot((zc * zc).astype(bf16), bd_ref[:cw, :cw], preferred_element_type=f32) * (1.0 / HEAD_DIM)
            zn = zc * lax.rsqrt(ms + EPS) * g_ref[:, c0:c0 + cw]
            for h0 in range(0, cw, LANES):
                zh = zn[:, h0:h0 + LANES]
                sw = jnp.where(first_half, pltpu.roll(zh, LANES - HEAD_DIM // 2, 1), pltpu.roll(zh, HEAD_DIM // 2, 1))
                chunks.append(zh * c + sw * s)
        return chunks

    def store(chunks, out_ref):
        for j, ch in enumerate(chunks):
            out_ref[:, j * LANES:(j + 1) * LANES] = ch.astype(out_ref.dtype)

    def store_dilated(chunks, out_refs):
        for j, ch in enumerate(chunks):
            st_ref[j] = ch
        for (_, dil), out_ref in zip(WINDOW_DILATIONS, out_refs):
            for r in range(dil):
                for j in range(len(chunks)):
                    src = pl.ds(r, rows // dil, stride=dil) if dil > 1 else pl.ds(0, rows)
                    out_ref[r, :, j * LANES:(j + 1) * LANES] = st_ref[j, src, :].astype(out_ref.dtype)

    store(norm_rope(OFF_QA, A_WIDTH, gqa_ref, ca_ref, sa_ref), qa_ref)
    store(norm_rope(OFF_KA, KA_COLS, gka_ref, ca_ref, sa_ref), ka_ref)
    va = proj(OFF_VA, VA_COLS)
    va_ref[...] = jnp.concatenate(
        [jnp.where(lane < HEAD_DIM, va[:, c0:c0 + LANES], 1.0) for c0 in range(0, VA_COLS, LANES)],
        axis=1).astype(va_ref.dtype)
    store_dilated(norm_rope(OFF_QB, B_WIDTH, gqb_ref, cb_ref, sb_ref), (qb1_ref, qb4_ref, qb16_ref))
    store_dilated(norm_rope(OFF_KB, B_WIDTH, gkb_ref, cb_ref, sb_ref), (kb1_ref, kb4_ref, kb16_ref))
    vb = proj(OFF_VB, B_WIDTH)
    store_dilated([vb[:, c0:c0 + LANES] for c0 in range(0, B_WIDTH, LANES)], (vb1_ref, vb4_ref, vb16_ref))
    ga_ref[...] = jax.nn.sigmoid(proj(OFF_GA, D_MODEL)).astype(ga_ref.dtype)
    gb_ref[...] = jax.nn.sigmoid(proj(OFF_GB, D_MODEL)).astype(gb_ref.dtype)


def _in_proj(x2, S, prm, tile0, T):
    tm = TOKEN_TILE
    n_pos = S // tm
    row = lambda i: (i, 0)
    pos = lambda i: (i % n_pos, 0)
    const = lambda i: (0, 0)
    dils = [dil for _, dil in WINDOW_DILATIONS]
    flat = lambda w: (pl.BlockSpec((tm, w), row), jax.ShapeDtypeStruct((T, w), bf16))
    strided = lambda d: (pl.BlockSpec((d, tm // d, B_WIDTH), lambda i: (0, i, 0)),
                         jax.ShapeDtypeStruct((d, T // d, B_WIDTH), bf16))
    outs = ([flat(A_WIDTH), flat(KA_COLS), flat(VA_COLS)] + 3 * [strided(d) for d in dils]
            + [flat(D_MODEL), flat(D_MODEL)])
    return pl.pallas_call(
        _in_proj_kernel,
        grid=(T // tm,),
        in_specs=[
            pl.BlockSpec((tm, D_MODEL), lambda i: (i + tile0, 0)),
            pl.BlockSpec((1, D_MODEL), const),
            pl.BlockSpec((D_MODEL, PROJ_COLS), const),
            pl.BlockSpec((1, A_WIDTH), const),
            pl.BlockSpec((1, KA_COLS), const),
            pl.BlockSpec((1, B_WIDTH), const),
            pl.BlockSpec((1, B_WIDTH), const),
            pl.BlockSpec((tm, LANES), pos),
            pl.BlockSpec((tm, LANES), pos),
            pl.BlockSpec((tm, LANES), pos),
            pl.BlockSpec((tm, LANES), pos),
            pl.BlockSpec((NORM_COLS, NORM_COLS), const),
        ],
        out_specs=[spec for spec, _ in outs],
        out_shape=[shape for _, shape in outs],
        scratch_shapes=[pltpu.VMEM((B_WIDTH // LANES, tm, LANES), f32)],
        compiler_params=_cparams(("parallel",)),
        name="in_proj",
    )(x2, prm["norm_mix"], prm["w_in"], prm["gqa"], prm["gka"], prm["gqb"], prm["gkb"],
      prm["cos_a"][S], prm["sin_a"][S], prm["cos_b"][S], prm["sin_b"][S], prm["bd"])


def _flash_kernel(q_ref, k_ref, v_ref, o_ref, s_ref, *, tq, tk, seq):
    lane = lax.broadcasted_iota(jnp.int32, (tq, LANES), 1)
    lo = lane < HEAD_DIM
    q = q_ref[0]
    q01, q23 = q[:, :LANES], q[:, LANES:]
    zero = jnp.zeros_like(q01)
    qs = [jnp.where(lo, q01, zero), jnp.where(lo, zero, q01), jnp.where(lo, q23, zero), jnp.where(lo, zero, q23)]
    n_chunks = seq // tk

    def keys(chunk):
        return k_ref[0, pl.ds(pl.multiple_of(chunk * tk, tk), tk), :]

    def score(h, kk):
        return lax.dot_general(qs[h], kk, (((1,), (1,)), ((), ())), preferred_element_type=f32)

    def attend(chunk, slot, ms, accs, next_chunk):
        vv = v_ref[0, pl.ds(pl.multiple_of(chunk * tk, tk), tk), :]
        kk = None if next_chunk is None else keys(next_chunk)
        new_ms, new_accs = [], []
        for h in range(GROUP_Q):
            if kk is not None:
                s_ref[1 - slot, h] = score(h, kk)
            s = s_ref[slot, h]
            m_new = jnp.maximum(ms[h], jnp.max(s, axis=-1, keepdims=True))
            p = jnp.exp2((s - m_new).astype(bf16))
            alpha = jnp.exp2(ms[h] - m_new)
            new_accs.append(accs[h] * alpha + jnp.dot(p, vv, preferred_element_type=f32))
            new_ms.append(m_new)
        return tuple(new_ms), tuple(new_accs)

    kk0 = keys(0)
    for h in range(GROUP_Q):
        s_ref[0, h] = score(h, kk0)

    def body(jj, carry):
        ms, accs = attend(2 * jj, 0, *carry, 2 * jj + 1)
        return attend(2 * jj + 1, 1, ms, accs, 2 * jj + 2)

    m0 = tuple(jnp.full((tq, 1), NEG_BIG, f32) for _ in range(GROUP_Q))
    acc0 = tuple(jnp.zeros((tq, LANES), f32) for _ in range(GROUP_Q))
    ms, accs = lax.fori_loop(0, n_chunks // 2 - 1, body, (m0, acc0), unroll=True)
    ms, accs = attend(n_chunks - 2, 0, ms, accs, n_chunks - 1)
    _, accs = attend(n_chunks - 1, 1, ms, accs, None)
    heads = [acc / jnp.where(lo, pltpu.roll(acc, HEAD_DIM, 1), 1.0) for acc in accs]
    out01 = jnp.where(lo, heads[0], pltpu.roll(heads[1], HEAD_DIM, 1))
    out23 = jnp.where(lo, heads[2], pltpu.roll(heads[3], HEAD_DIM, 1))
    o_ref[0] = jnp.concatenate([out01, out23], axis=1).astype(o_ref.dtype)


def _flash(qa, ka, va):
    B, S, _ = qa.shape
    tq, tk = FLASH_TQ, FLASH_TK
    return pl.pallas_call(
        functools.partial(_flash_kernel, tq=tq, tk=tk, seq=S),
        grid=(B, A_KV_HEADS, S // tq),
        in_specs=[
            pl.BlockSpec((1, tq, GROUP_Q * HEAD_DIM), lambda b, g, i: (b, i, g)),
            pl.BlockSpec((1, S, LANES), lambda b, g, i: (b, 0, g)),
            pl.BlockSpec((1, S, LANES), lambda b, g, i: (b, 0, g)),
        ],
        out_specs=pl.BlockSpec((1, tq, GROUP_Q * HEAD_DIM), lambda b, g, i: (b, i, g)),
        out_shape=jax.ShapeDtypeStruct((B, S, A_WIDTH), bf16),
        scratch_shapes=[pltpu.VMEM((2, GROUP_Q, tq, tk), f32)],
        compiler_params=_cparams(("parallel", "parallel", "arbitrary")),
        name="flash_gqa",
    )(qa, ka, va)


def _dilated_kernel(q_ref, k_ref, v_ref, o_ref, lse_ref, so_ref, sl_ref, *, length, tms, dil):
    i = pl.program_id(1)
    lane = lax.broadcasted_iota(jnp.int32, (DIL_QBLOCK, LANES), 1)
    lo = lane < HEAD_DIM
    n_pairs = B_WIDTH // LANES

    def unit(u, carry):
        sb, r = u // dil, u % dil
        r0 = pl.multiple_of(sb * DIL_QBLOCK, DIL_QBLOCK)
        m0 = i * tms + r0
        start = pl.multiple_of(jnp.clip(m0 - HALF_WINDOW, 0, length - KEY_SPAN), HALF_WINDOW)
        qpos = m0 + lax.broadcasted_iota(jnp.int32, (DIL_QBLOCK, KEY_SPAN), 0)
        kpos = start + lax.broadcasted_iota(jnp.int32, (DIL_QBLOCK, KEY_SPAN), 1)
        valid = jnp.abs(kpos - qpos) <= HALF_WINDOW
        rows = pl.ds(r0 * dil + r, DIL_QBLOCK, stride=dil) if dil > 1 else pl.ds(r0, DIL_QBLOCK)
        scores = []
        for c in range(n_pairs):
            cs = slice(c * LANES, (c + 1) * LANES)
            qp = q_ref[r, 0, pl.ds(r0, DIL_QBLOCK), cs]
            kp = k_ref[r, 0, pl.ds(start, KEY_SPAN), cs]
            zero = jnp.zeros_like(qp)
            for half in range(2):
                qm = jnp.where(lo, qp, zero) if half == 0 else jnp.where(lo, zero, qp)
                scores.append(lax.dot_general(qm, kp, (((1,), (1,)), ((), ())), preferred_element_type=f32))
        for c in range(n_pairs):
            vp = v_ref[r, 0, pl.ds(start, KEY_SPAN), c * LANES:(c + 1) * LANES]
            outs, lses = [], []
            for half in range(2):
                s = jnp.where(valid, scores[2 * c + half], NEG_BIG)
                m = jnp.max(s, axis=-1, keepdims=True)
                e = jnp.exp2(s - m)
                l = jnp.sum(e, axis=-1, keepdims=True)
                outs.append(jnp.dot(e.astype(bf16), vp, preferred_element_type=f32) * (1.0 / l))
                lses.append(m + jnp.log2(l))
            so_ref[c, rows, :] = jnp.where(lo, outs[0], outs[1])
            sl_ref[c, rows, :] = jnp.where(lo, lses[0], lses[1])
        return carry

    lax.fori_loop(0, (tms // DIL_QBLOCK) * dil, unit, 0, unroll=4)
    o_ref[0] = jnp.concatenate([so_ref[c] for c in range(n_pairs)], axis=1).astype(o_ref.dtype)
    lse_ref[0] = jnp.concatenate([sl_ref[c] for c in range(n_pairs)], axis=1)


def _dilated(qb, kb, vb, B, S, dil):
    length = S // dil
    tms = DIL_STEP_TOKENS // dil
    view = lambda a: a.reshape(dil, B, length, B_WIDTH)
    o, lse = pl.pallas_call(
        functools.partial(_dilated_kernel, length=length, tms=tms, dil=dil),
        grid=(B, length // tms),
        in_specs=[
            pl.BlockSpec((dil, 1, tms, B_WIDTH), lambda b, i: (0, b, i, 0)),
            pl.BlockSpec((dil, 1, length, B_WIDTH), lambda b, i: (0, b, 0, 0)),
            pl.BlockSpec((dil, 1, length, B_WIDTH), lambda b, i: (0, b, 0, 0)),
        ],
        out_specs=[
            pl.BlockSpec((1, tms * dil, B_WIDTH), lambda b, i: (b, i, 0)),
            pl.BlockSpec((1, tms * dil, B_WIDTH), lambda b, i: (b, i, 0)),
        ],
        out_shape=[
            jax.ShapeDtypeStruct((B, S, B_WIDTH), bf16),
            jax.ShapeDtypeStruct((B, S, B_WIDTH), f32),
        ],
        scratch_shapes=[pltpu.VMEM((B_WIDTH // LANES, tms * dil, LANES), f32),
                        pltpu.VMEM((B_WIDTH // LANES, tms * dil, LANES), f32)],
        compiler_params=_cparams(("parallel", "arbitrary")),
        name=f"dilated_{dil}",
    )(view(qb), view(kb), view(vb))
    return o.reshape(B * S, B_WIDTH), lse.reshape(B * S, B_WIDTH)


def _first_index_of_max(cur, idx, n):
    mx = jnp.max(cur, axis=0, keepdims=True)
    return jnp.min(jnp.where(cur == mx, idx, n), axis=0, keepdims=True)


def _merge_route_kernel(x_ref, ya_ref, o1_ref, o2_ref, o3_ref, l1_ref, l2_ref, l3_ref, ga_ref, gb_ref,
                        wa_ref, wb_ref, wo_ref, nf_ref, rhi_ref, rlo_ref, bias_ref, tri_ref,
                        h_ref, u_ref, idx_ref, wt_ref, rank_ref, cnt_ref, run_ref):
    step = pl.program_id(0)

    @pl.when(step == 0)
    def _():
        run_ref[...] = jnp.zeros_like(run_ref)

    tm = x_ref.shape[0]
    sub = tri_ref.shape[0]

    def dense(j):
        rs = slice(j * sub, (j + 1) * sub)
        l1, l2, l3 = l1_ref[rs, :], l2_ref[rs, :], l3_ref[rs, :]
        lm = jnp.maximum(jnp.maximum(l1, l2), l3)
        e1, e2, e3 = jnp.exp2(l1 - lm), jnp.exp2(l2 - lm), jnp.exp2(l3 - lm)
        yb = ((e1 * o1_ref[rs, :].astype(f32) + e2 * o2_ref[rs, :].astype(f32) + e3 * o3_ref[rs, :].astype(f32))
              / (e1 + e2 + e3))
        pa = jnp.dot(ya_ref[rs, :], wa_ref[...], preferred_element_type=f32)
        pb = jnp.dot(yb.astype(bf16), wb_ref[...], preferred_element_type=f32)
        merged = ga_ref[rs, :].astype(f32) * pa + gb_ref[rs, :].astype(f32) * pb
        h = x_ref[rs, :] + jnp.dot(merged.astype(bf16), wo_ref[...], preferred_element_type=f32)
        h_ref[rs, :] = h
        r = lax.rsqrt(jnp.mean(h * h, axis=-1, keepdims=True) + EPS)
        uf = h * r * nf_ref[...]
        u_hi = uf.astype(bf16)
        for c, chunk in enumerate(_pack_rows(uf)):
            u_ref[c, rs, :] = chunk
        u_lo = (uf - u_hi.astype(f32)).astype(bf16)
        return (jnp.dot(u_hi, rhi_ref[...], preferred_element_type=f32)
                + jnp.dot(u_lo, rhi_ref[...], preferred_element_type=f32)
                + jnp.dot(u_hi, rlo_ref[...], preferred_element_type=f32))

    def route(j, logits):
        scores = jax.nn.sigmoid(logits.T[:N_EXPERTS, :])
        sel = scores + bias_ref[...]
        i8 = lax.broadcasted_iota(jnp.int32, (EXPERTS_PER_GROUP, sub), 0)
        neg_inf = jnp.float32(-jnp.inf)
        gscore = jnp.zeros((N_GROUPS, sub), f32)
        for g in range(N_GROUPS):
            blk = sel[g * EXPERTS_PER_GROUP:(g + 1) * EXPERTS_PER_GROUP, :]
            top1 = jnp.max(blk, axis=0, keepdims=True)
            first = _first_index_of_max(blk, i8, EXPERTS_PER_GROUP)
            top2 = jnp.max(jnp.where(i8 == first, neg_inf, blk), axis=0, keepdims=True)
            gscore = jnp.where(i8 == g, top1 + top2, gscore)
        gkeep = jnp.zeros((N_GROUPS, sub), jnp.bool_)
        cur = gscore
        for _ in range(TOPK_GROUPS):
            pick = i8 == _first_index_of_max(cur, i8, N_GROUPS)
            gkeep = jnp.logical_or(gkeep, pick)
            cur = jnp.where(pick, neg_inf, cur)
        cur = jnp.concatenate(
            [jnp.where(gkeep[g:g + 1, :], sel[g * EXPERTS_PER_GROUP:(g + 1) * EXPERTS_PER_GROUP, :], NEG_BIG)
             for g in range(N_GROUPS)], axis=0)

        ie = lax.broadcasted_iota(jnp.int32, (N_EXPERTS, sub), 0)
        ik = lax.broadcasted_iota(jnp.int32, (TOP_K, sub), 0)
        idx_out = jnp.zeros((TOP_K, sub), jnp.int32)
        w_out = jnp.zeros((TOP_K, sub), f32)
        chosen = jnp.zeros((N_EXPERTS, sub), jnp.bool_)
        for k in range(TOP_K):
            fi = _first_index_of_max(cur, ie, N_EXPERTS)
            pick = ie == fi
            wk = jnp.sum(jnp.where(pick, scores, 0.0), axis=0, keepdims=True)
            idx_out = jnp.where(ik == k, fi, idx_out)
            w_out = jnp.where(ik == k, wk, w_out)
            chosen = jnp.logical_or(chosen, pick)
            cur = jnp.where(pick, neg_inf, cur)
        w_out = w_out / jnp.sum(w_out, axis=0, keepdims=True) * ROUTE_SCALE
        wt_ref[:, j * sub:(j + 1) * sub] = w_out

        sel01 = jnp.where(chosen, 1.0, 0.0)
        before = jnp.dot(sel01.astype(bf16), tri_ref[...], preferred_element_type=f32) + run_ref[:, 0:1]
        rank_out = jnp.zeros((TOP_K, sub), f32)
        for k in range(TOP_K):
            rk = jnp.sum(jnp.where(ie == idx_out[k:k + 1, :], before, 0.0), axis=0, keepdims=True)
            rank_out = jnp.where(ik == k, rk, rank_out)
        rank_i = rank_out.astype(jnp.int32)
        for w in range(sub // SC_WINDOW):
            win = j * (sub // SC_WINDOW) + w
            idx_ref[win] = idx_out[:, w * SC_WINDOW:(w + 1) * SC_WINDOW]
            rank_ref[win] = rank_i[:, w * SC_WINDOW:(w + 1) * SC_WINDOW]
        run_ref[...] = run_ref[...] + jnp.sum(sel01, axis=1, keepdims=True)

    logits = [dense(j) for j in range(tm // sub)]
    for j in range(tm // sub):
        route(j, logits[j])
    cnt_ref[...] = run_ref[...].astype(jnp.int32)


def _merge_route(x2, ya, outs, lses, sga, sgb, prm, tile0):
    T = ya.shape[0]
    tm = TOKEN_TILE
    row = lambda i: (i, 0)
    col = lambda i: (0, i)
    const = lambda i: (0, 0)
    tok = lambda w: pl.BlockSpec((tm, w), row)
    full = lambda a: pl.BlockSpec(a.shape, const)
    win = pl.BlockSpec((tm // SC_WINDOW, TOP_K, SC_WINDOW), lambda i: (i, 0, 0))
    return pl.pallas_call(
        _merge_route_kernel,
        grid=(T // tm,),
        in_specs=[pl.BlockSpec((tm, D_MODEL), lambda i: (i + tile0, 0)), tok(A_WIDTH),
                  tok(B_WIDTH), tok(B_WIDTH), tok(B_WIDTH),
                  tok(B_WIDTH), tok(B_WIDTH), tok(B_WIDTH), tok(D_MODEL), tok(D_MODEL),
                  full(prm["w_branch_a"]), full(prm["w_branch_b"]), full(prm["w_out"]), full(prm["norm_ffn"]),
                  full(prm["router_hi"]), full(prm["router_lo"]), full(prm["router_bias"]), full(prm["tri"])],
        out_specs=[tok(D_MODEL), pl.BlockSpec((ROW_CHUNKS, tm, LANES), lambda i: (0, i, 0)),
                   win, pl.BlockSpec((TOP_K, tm), col), win,
                   pl.BlockSpec((N_EXPERTS, LANES), const)],
        out_shape=[jax.ShapeDtypeStruct((T, D_MODEL), f32), jax.ShapeDtypeStruct((ROW_CHUNKS, T, LANES), jnp.uint32),
                   jax.ShapeDtypeStruct((T // SC_WINDOW, TOP_K, SC_WINDOW), jnp.int32),
                   jax.ShapeDtypeStruct((TOP_K, T), f32),
                   jax.ShapeDtypeStruct((T // SC_WINDOW, TOP_K, SC_WINDOW), jnp.int32),
                   jax.ShapeDtypeStruct((N_EXPERTS, LANES), jnp.int32)],
        scratch_shapes=[pltpu.VMEM((N_EXPERTS, LANES), f32)],
        compiler_params=_cparams(("arbitrary",)),
        name="merge_route",
    )(x2, ya, *outs, *lses, sga, sgb,
      prm["w_branch_a"], prm["w_branch_b"], prm["w_out"], prm["norm_ffn"],
      prm["router_hi"], prm["router_lo"], prm["router_bias"], prm["tri"])


def _experts_kernel(be_ref, nu_ref, nv_ref, x_ref, wg_ref, wu_ref, wd_ref, y_ref, wgu_s, wd_s):
    i = pl.program_id(0)

    @pl.when(jnp.logical_or(i == 0, be_ref[i] != be_ref[jnp.maximum(i - 1, 0)]))
    def _():
        wgu_s[:, :D_EXPERT] = wg_ref[0].astype(bf16)
        wgu_s[:, D_EXPERT:] = wu_ref[0].astype(bf16)
        wd_s[...] = wd_ref[0].astype(bf16)

    @pl.when(i < nu_ref[0])
    def _():
        n_sub = x_ref.shape[1] // MOE_SUB_ROWS

        def up(j):
            r0 = j * MOE_SUB_ROWS
            lo, hi = _unpack_rows([x_ref[c, r0:r0 + MOE_SUB_ROWS, :] for c in range(ROW_CHUNKS)])
            live = r0 + lax.broadcasted_iota(jnp.int32, lo.shape, 0) < nv_ref[i]
            x = jnp.concatenate([jnp.where(live, lo, 0.0), jnp.where(live, hi, 0.0)], axis=1).astype(bf16)
            return jnp.dot(x, wgu_s[...], preferred_element_type=f32)

        def down(j, gu):
            g, u = gu[:, :D_EXPERT], gu[:, D_EXPERT:]
            a = g * jax.nn.sigmoid(g) * u
            y = jnp.dot(a.astype(bf16), wd_s[...], preferred_element_type=f32)
            r0 = j * MOE_SUB_ROWS
            for c, chunk in enumerate(_pack_rows(y)):
                y_ref[c, r0:r0 + MOE_SUB_ROWS, :] = chunk

        gu_prev = up(0)
        for j in range(1, n_sub):
            gu_next = up(j)
            down(j - 1, gu_prev)
            gu_prev = gu_next
        down(n_sub - 1, gu_prev)

    @pl.when(i >= nu_ref[0])
    def _():
        y_ref[...] = jnp.zeros_like(y_ref)


def _experts(xs, blk_e, n_used, n_valid, prm):
    n_rows = xs.shape[1]
    bm = MOE_ROWS
    rows = lambda i, be, nu, nv: (0, jnp.minimum(i, nu[0] - 1), 0)
    return pl.pallas_call(
        _experts_kernel,
        grid_spec=pltpu.PrefetchScalarGridSpec(
            num_scalar_prefetch=3,
            grid=(n_rows // bm,),
            in_specs=[
                pl.BlockSpec((ROW_CHUNKS, bm, LANES), rows),
                pl.BlockSpec((1, D_MODEL, D_EXPERT), lambda i, be, nu, nv: (be[i], 0, 0)),
                pl.BlockSpec((1, D_MODEL, D_EXPERT), lambda i, be, nu, nv: (be[i], 0, 0)),
                pl.BlockSpec((1, D_EXPERT, D_MODEL), lambda i, be, nu, nv: (be[i], 0, 0)),
            ],
            out_specs=pl.BlockSpec((ROW_CHUNKS, bm, LANES), lambda i, be, nu, nv: (0, i, 0)),
            scratch_shapes=[pltpu.VMEM((D_MODEL, 2 * D_EXPERT), bf16), pltpu.VMEM((D_EXPERT, D_MODEL), bf16)],
        ),
        out_shape=jax.ShapeDtypeStruct((ROW_CHUNKS, n_rows, LANES), jnp.uint32),
        compiler_params=_cparams(("arbitrary",)),
        name="experts",
    )(blk_e, n_used, n_valid, xs, prm["expert_gate"], prm["expert_up"], prm["expert_down"])


def _sc_mesh():
    return plsc.VectorSubcoreMesh(core_axis_name="c", subcore_axis_name="s")


def _sc_dispatch(u2p, dest, n_rows):
    T = u2p.shape[1]
    per_worker = T // SC_WORKERS
    n_it = per_worker // SC_WINDOW

    @functools.partial(
        pl.kernel, mesh=_sc_mesh(),
        out_type=jax.ShapeDtypeStruct((ROW_CHUNKS, n_rows, LANES), jnp.uint32),
        scratch_types=[pltpu.VMEM((TOP_K, SC_WINDOW), jnp.int32),
                       pltpu.VMEM((ROW_CHUNKS, SC_WINDOW, LANES), jnp.uint32),
                       pltpu.SemaphoreType.DMA, pltpu.SemaphoreType.DMA],
    )
    def k(u_hbm, dest_hbm, xs_hbm, idx_v, rows_v, sem_in, sem_out):
        wid = lax.axis_index("s") * SC_CORES + lax.axis_index("c")

        @pl.loop(0, n_it)
        def _(it):
            t0 = wid * per_worker + it * SC_WINDOW
            pltpu.sync_copy(dest_hbm.at[wid * n_it + it], idx_v)
            loads = [pltpu.async_copy(u_hbm.at[c].at[pl.ds(t0, SC_WINDOW)], rows_v.at[c], sem_in)
                     for c in range(ROW_CHUNKS)]
            for cp in loads:
                cp.wait()
            stores = [pltpu.async_copy(rows_v.at[c], xs_hbm.at[c].at[idx_v.at[kk]], sem_out)
                      for kk in range(TOP_K) for c in range(ROW_CHUNKS)]
            for cp in stores:
                cp.wait()

    return k(u2p, dest)


def _sc_combine(ys, dest):
    n_windows = dest.shape[0]
    T = n_windows * SC_WINDOW
    per_worker = T // SC_WORKERS
    n_it = per_worker // SC_WINDOW

    @functools.partial(
        pl.kernel, mesh=_sc_mesh(),
        out_type=jax.ShapeDtypeStruct((TOP_K, ROW_CHUNKS, T, LANES), jnp.uint32),
        scratch_types=[pltpu.VMEM((TOP_K, SC_WINDOW), jnp.int32),
                       pltpu.VMEM((SC_SLABS, SC_WINDOW, LANES), jnp.uint32),
                       pltpu.SemaphoreType.DMA, pltpu.SemaphoreType.DMA],
    )
    def k(ys_hbm, dest_hbm, yg_hbm, idx_v, rows_v, sem_in, sem_out):
        wid = lax.axis_index("s") * SC_CORES + lax.axis_index("c")

        @pl.loop(0, n_it)
        def _(it):
            t0 = wid * per_worker + it * SC_WINDOW
            pltpu.sync_copy(dest_hbm.at[wid * n_it + it], idx_v)
            for c in range(ROW_CHUNKS):
                for k0 in range(0, TOP_K, SC_SLABS):
                    loads = [pltpu.async_copy(ys_hbm.at[c].at[idx_v.at[k0 + j]], rows_v.at[j], sem_in)
                             for j in range(SC_SLABS)]
                    for cp in loads:
                        cp.wait()
                    stores = [pltpu.async_copy(rows_v.at[j], yg_hbm.at[k0 + j].at[c].at[pl.ds(t0, SC_WINDOW)], sem_out)
                              for j in range(SC_SLABS)]
                    for cp in stores:
                        cp.wait()

    return k(ys, dest)


def _final_kernel(h_ref, u_ref, yg_ref, wt_ref, p_ref, sgu_ref, sd_ref, pg_ref, pp_ref, *out_refs):
    o_ref = out_refs[-1]
    wl = wt_ref[...]
    wt = jnp.concatenate([wl, jnp.zeros((LANES - TOP_K, wl.shape[1]), f32)], axis=0).T
    r_lo = r_hi = None
    for k in range(TOP_K):
        lo, hi = _unpack_rows([yg_ref[k, c] for c in range(ROW_CHUNKS)])
        wk = wt[:, k:k + 1]
        r_lo = lo * wk if k == 0 else r_lo + lo * wk
        r_hi = hi * wk if k == 0 else r_hi + hi * wk
    routed = jnp.concatenate([r_lo, r_hi], axis=1)
    u_lo, u_hi = _unpack_rows([u_ref[c] for c in range(ROW_CHUNKS)])
    un = jnp.concatenate([u_lo, u_hi], axis=1).astype(bf16)
    gu = jnp.dot(un, sgu_ref[...], preferred_element_type=f32)
    g, u = gu[:, :D_SHARED], gu[:, D_SHARED:]
    shared = jnp.dot((g * jax.nn.sigmoid(g) * u).astype(bf16), sd_ref[...], preferred_element_type=f32)
    h = h_ref[...] + (routed + shared)
    gate = jax.nn.sigmoid(jnp.dot(h.astype(bf16), pg_ref[...], preferred_element_type=f32))
    emb = jnp.dot(p_ref[...].astype(bf16), pp_ref[...], preferred_element_type=f32)
    o_ref[...] = h + gate * emb


def _final(h1, u2p, yg, wts, p2, prm, tile0, out_prev):
    T = h1.shape[0]
    tm = TOKEN_TILE
    row = lambda i: (i, 0)
    off = lambda i: (i + tile0, 0)
    const = lambda i: (0, 0)
    full = lambda a: pl.BlockSpec(a.shape, const)
    in_specs = [pl.BlockSpec((tm, D_MODEL), row),
                pl.BlockSpec((ROW_CHUNKS, tm, LANES), lambda i: (0, i, 0)),
                pl.BlockSpec((TOP_K, ROW_CHUNKS, tm, LANES), lambda i: (0, 0, i, 0)),
                pl.BlockSpec((TOP_K, tm), lambda i: (0, i)),
                pl.BlockSpec((tm, PLE_DIM), off),
                full(prm["shared_gu"]), full(prm["shared_down"]), full(prm["ple_gate"]), full(prm["ple_proj"])]
    args = [h1, u2p, yg, wts, p2, prm["shared_gu"], prm["shared_down"], prm["ple_gate"], prm["ple_proj"]]
    aliases = {}
    if out_prev is not None:
        in_specs.append(pl.BlockSpec(memory_space=pl.ANY))
        args.append(out_prev)
        aliases = {len(args) - 1: 0}
    return pl.pallas_call(
        _final_kernel,
        grid=(T // tm,),
        in_specs=in_specs,
        out_specs=pl.BlockSpec((tm, D_MODEL), off),
        out_shape=jax.ShapeDtypeStruct((p2.shape[0], D_MODEL), f32),
        input_output_aliases=aliases,
        compiler_params=_cparams(("parallel",)),
        name="final",
    )(*args)


def _deinterleave(n_heads):
    base = jnp.concatenate([jnp.arange(0, HEAD_DIM, 2), jnp.arange(1, HEAD_DIM, 2)])
    return (jnp.arange(n_heads)[:, None] * HEAD_DIM + base[None, :]).reshape(-1)


def _rope_tables(ang):
    c, s = jnp.cos(ang), jnp.sin(ang)
    return jnp.tile(jnp.concatenate([c, c], axis=-1), (1, 2)), jnp.tile(jnp.concatenate([-s, s], axis=-1), (1, 2))


def _axial_angles(S):
    rows = S // GRID_W
    row = jnp.repeat(jnp.arange(rows, dtype=f32), GRID_W)
    col = jnp.tile(jnp.arange(GRID_W, dtype=f32), rows)
    half = HEAD_DIM // 2
    inv = ROPE_THETA ** (-jnp.arange(0, half, 2, dtype=f32) / half)
    return jnp.concatenate([row[:, None] * inv, col[:, None] * inv], axis=-1)


def _linear_angles(S):
    t = jnp.arange(S, dtype=f32)
    inv = ROPE_THETA ** (-jnp.arange(0, HEAD_DIM, 2, dtype=f32) / HEAD_DIM)
    return t[:, None] * inv


def _prepare(seq_lens, norm_mix, w_in, q_norm_a, k_norm_a, q_norm_b, k_norm_b, w_branch_a, w_branch_b, w_out,
             norm_ffn, router_w, router_bias, expert_gate, expert_up, expert_down, shared_gate, shared_up,
             shared_down, ple_proj, ple_gate):
    cuts = [A_WIDTH, A_WIDTH + A_KV_WIDTH, A_WIDTH + 2 * A_KV_WIDTH,
            A_WIDTH + 2 * A_KV_WIDTH + B_WIDTH, A_WIDTH + 2 * A_KV_WIDTH + 2 * B_WIDTH,
            A_WIDTH + 2 * A_KV_WIDTH + 3 * B_WIDTH, A_WIDTH + 2 * A_KV_WIDTH + 3 * B_WIDTH + D_MODEL]
    wqa, wka, wva, wqb, wkb, wvb, wga, wgb = jnp.split(w_in, cuts, axis=-1)
    wqa = wqa[:, _deinterleave(A_Q_HEADS)]
    wka = wka[:, _deinterleave(A_KV_HEADS)]
    wqb = wqb[:, _deinterleave(B_HEADS)]
    wkb = wkb[:, _deinterleave(B_HEADS)]
    k0, k1 = wka[:, :HEAD_DIM], wka[:, HEAD_DIM:]
    v0, v1 = wva[:, :HEAD_DIM], wva[:, HEAD_DIM:]
    vz = jnp.zeros_like(v0)
    w_all = jnp.concatenate([wqa, k0, k0, k1, k1, v0, vz, v1, vz, wqb, wkb, wvb, wga, wgb], axis=-1).astype(bf16)
    perm = _deinterleave(1)
    scale = HEAD_DIM ** -0.5
    tile = lambda g, n: jnp.tile(g[perm], n)[None, :]
    router_pad = jnp.pad(router_w, ((0, 0), (0, LANES - N_EXPERTS)))
    router_hi = router_pad.astype(bf16)
    prm = {
        "norm_mix": norm_mix[None, :], "w_in": w_all,
        "gqa": tile(q_norm_a, A_Q_HEADS) * (scale * LOG2_E), "gka": tile(k_norm_a, KA_COLS // HEAD_DIM),
        "gqb": tile(q_norm_b, B_HEADS) * (scale * LOG2_E), "gkb": tile(k_norm_b, B_HEADS),
        "bd": jnp.kron(jnp.eye(NORM_COLS // HEAD_DIM, dtype=f32), jnp.ones((HEAD_DIM, HEAD_DIM), f32)).astype(bf16),
        "w_branch_a": w_branch_a.astype(bf16), "w_branch_b": w_branch_b.astype(bf16), "w_out": w_out.astype(bf16),
        "norm_ffn": norm_ffn[None, :],
        "router_hi": router_hi, "router_lo": (router_pad - router_hi.astype(f32)).astype(bf16),
        "router_bias": router_bias[:, None],
        "tri": (jnp.arange(MERGE_SUB)[:, None] < jnp.arange(MERGE_SUB)[None, :]).astype(bf16),
        "expert_gate": expert_gate, "expert_up": expert_up, "expert_down": expert_down,
        "shared_gu": jnp.concatenate([shared_gate, shared_up], axis=-1).astype(bf16),
        "shared_down": shared_down.astype(bf16),
        "ple_gate": ple_gate.astype(bf16), "ple_proj": ple_proj.astype(bf16),
        "cos_a": {}, "sin_a": {}, "cos_b": {}, "sin_b": {},
    }
    for S in seq_lens:
        prm["cos_a"][S], prm["sin_a"][S] = _rope_tables(_axial_angles(S))
        prm["cos_b"][S], prm["sin_b"][S] = _rope_tables(_linear_angles(S))
    return prm


def _dispatch_plan(idx, rank, counts, T):
    bm = MOE_ROWS
    n_rows = (T * TOP_K + N_EXPERTS * bm) // bm * bm
    n_blocks = n_rows // bm
    padded = (counts + bm - 1) // bm * bm
    pad_end = jnp.cumsum(padded)
    pad_start = pad_end - padded
    dest = pad_start[idx] + rank
    blk_start = jnp.arange(n_blocks, dtype=jnp.int32) * bm
    blk_e = jnp.minimum(jnp.sum(pad_end[None, :] <= blk_start[:, None], axis=1), N_EXPERTS - 1).astype(jnp.int32)
    n_used = (pad_end[-1:] // bm).astype(jnp.int32)
    own = blk_e[:, None] == jnp.arange(N_EXPERTS, dtype=jnp.int32)[None, :]
    real_end = jnp.sum(jnp.where(own, (pad_start + counts)[None, :], 0), axis=1)
    n_valid = jnp.clip(real_end - blk_start, 0, bm).astype(jnp.int32)
    return dest, n_rows, blk_e, n_used, n_valid


def _layer(x, p, prm):
    nb = x.shape[0] // GROUP_SPLIT
    out = None
    for j in range(GROUP_SPLIT):
        out = _layer_slice(x, p, prm, j * nb, nb, out)
    return out.reshape(x.shape)


def _layer_slice(x, p, prm, b0, B, out_prev):
    _, S, _ = x.shape
    T = B * S
    tile0 = b0 * S // TOKEN_TILE
    x2 = x.reshape(-1, D_MODEL)
    qa, ka, va, *qkv_b, sga, sgb = _in_proj(x2, S, prm, tile0, T)
    shape3 = lambda a: a.reshape(B, S, a.shape[-1])
    ya = _flash(shape3(qa), shape3(ka), shape3(va)).reshape(T, A_WIDTH)
    n_dil = len(WINDOW_DILATIONS)
    outs, lses = [], []
    for j, (_, dil) in enumerate(WINDOW_DILATIONS):
        o_d, lse_d = _dilated(qkv_b[j], qkv_b[n_dil + j], qkv_b[2 * n_dil + j], B, S, dil)
        outs.append(o_d)
        lses.append(lse_d)
    h1, u2p, idx, wts, rank, cnt = _merge_route(x2, ya, outs, lses, sga, sgb, prm, tile0)
    dest, n_rows, blk_e, n_used, n_valid = _dispatch_plan(idx, rank, cnt[:, 0], T)
    xs = _sc_dispatch(u2p, dest, n_rows)
    ys = _experts(xs, blk_e, n_used, n_valid, prm)
    yg = _sc_combine(ys, dest)
    return _final(h1, u2p, yg, wts, p.reshape(-1, PLE_DIM), prm, tile0, out_prev)


def kernel(x_prompt, x_sample, p_prompt, p_sample, norm_mix, w_in, q_norm_a, k_norm_a, q_norm_b, k_norm_b,
           w_branch_a, w_branch_b, w_out, norm_ffn, router_w, router_bias, expert_gate, expert_up, expert_down,
           shared_gate, shared_up, shared_down, ple_proj, ple_gate):
    weights = (norm_mix, w_in, q_norm_a, k_norm_a, q_norm_b, k_norm_b, w_branch_a, w_branch_b, w_out, norm_ffn,
               router_w, router_bias, expert_gate, expert_up, expert_down, shared_gate, shared_up, shared_down,
               ple_proj, ple_gate)
    depth = norm_mix.shape[0]
    seq_lens = sorted({x_prompt.shape[1], x_sample.shape[1]})
    hp, hs = x_prompt, x_sample
    for i in range(depth):
        prm = _prepare(seq_lens, *[w[i] for w in weights])
        hp = _layer(hp, p_prompt[i], prm)
        hs = _layer(hs, p_sample[i], prm)
    return hp, hs
```

```python
import functools

import jax
import jax.numpy as jnp
from jax import lax
from jax.experimental import pallas as pl
from jax.experimental.pallas import tpu as pltpu
from jax.experimental.pallas import tpu_sc as plsc

D_MODEL = 1024
HEAD_DIM = 64
A_Q_HEADS = 8
A_KV_HEADS = 2
B_HEADS = 6
GRID_W = 64
ROPE_THETA = 10000.0
WINDOW_DILATIONS = ((128, 1), (512, 4), (2048, 16))
N_EXPERTS = 64
TOP_K = 8
N_GROUPS = 8
TOPK_GROUPS = 4
D_EXPERT = 256
D_SHARED = 256
ROUTE_SCALE = 2.5
PLE_DIM = 256
NEG_BIG = -1e30
EPS = 1e-6
LOG2_E = 1.4426950408889634

A_WIDTH = A_Q_HEADS * HEAD_DIM
A_KV_WIDTH = A_KV_HEADS * HEAD_DIM
B_WIDTH = B_HEADS * HEAD_DIM
GROUP_Q = A_Q_HEADS // A_KV_HEADS
EXPERTS_PER_GROUP = N_EXPERTS // N_GROUPS

LANES = 128
NORM_COLS = 256
HALF_WINDOW = 64
KEY_SPAN = 256

KA_COLS = 2 * LANES
VA_COLS = 2 * LANES
OFF_QA = 0
OFF_KA = OFF_QA + A_WIDTH
OFF_VA = OFF_KA + KA_COLS
OFF_QB = OFF_VA + VA_COLS
OFF_KB = OFF_QB + B_WIDTH
OFF_VB = OFF_KB + B_WIDTH
OFF_GA = OFF_VB + B_WIDTH
OFF_GB = OFF_GA + D_MODEL
PROJ_COLS = OFF_GB + D_MODEL

TOKEN_TILE = 512
MERGE_SUB = 256
FLASH_TQ = 256
FLASH_TK = 1024
DIL_QBLOCK = 128
DIL_STEP_TOKENS = 2048
MOE_ROWS = 1024
MOE_SUB_ROWS = 256

VMEM_LIMIT = 52 * 1024 * 1024

HALF_MODEL = D_MODEL // 2
ROW_CHUNKS = HALF_MODEL // LANES
SC_CORES = 2
SC_SUBCORES = 16
SC_WORKERS = SC_CORES * SC_SUBCORES
SC_WINDOW = 128
SC_SLABS = 4

bf16 = jnp.bfloat16
f32 = jnp.float32


def _cparams(sem):
    return pltpu.CompilerParams(dimension_semantics=sem, vmem_limit_bytes=VMEM_LIMIT)


def _pack_rows(v):
    lo = lax.bitcast_convert_type(v[:, :HALF_MODEL].astype(bf16).astype(f32), jnp.uint32) >> 16
    hi = lax.bitcast_convert_type(v[:, HALF_MODEL:].astype(bf16).astype(f32), jnp.uint32) & jnp.uint32(0xFFFF0000)
    w = lo | hi
    return [w[:, c * LANES:(c + 1) * LANES] for c in range(ROW_CHUNKS)]


def _unpack_rows(chunks):
    w = jnp.concatenate(chunks, axis=1)
    lo = lax.bitcast_convert_type(w << 16, f32)
    hi = lax.bitcast_convert_type(w & jnp.uint32(0xFFFF0000), f32)
    return lo, hi


def _in_proj_kernel(x_ref, nm_ref, w_ref, gqa_ref, gka_ref, gqb_ref, gkb_ref,
                    ca_ref, sa_ref, cb_ref, sb_ref, bd_ref,
                    qa_ref, ka_ref, va_ref, qb1_ref, qb4_ref, qb16_ref, kb1_ref, kb4_ref, kb16_ref,
                    vb1_ref, vb4_ref, vb16_ref, ga_ref, gb_ref, st_ref):
    x = x_ref[...]
    r = lax.rsqrt(jnp.mean(x * x, axis=-1, keepdims=True) + EPS)
    u = (x * r * nm_ref[...]).astype(bf16)
    rows = x.shape[0]
    lane = lax.broadcasted_iota(jnp.int32, (rows, LANES), 1)
    first_half = (lane % HEAD_DIM) < (HEAD_DIM // 2)

    def proj(off, width):
        return jnp.dot(u, w_ref[:, off:off + width], preferred_element_type=f32)

    def norm_rope(off, width, g_ref, c_ref, s_ref):
        z = proj(off, width)
        c = c_ref[...]
        s = s_ref[...]
        chunks = []
        for c0 in range(0, width, NORM_COLS):
            cw = min(NORM_COLS, width - c0)
            zc = z[:, c0:c0 + cw]
            ms = jnp.dot((zc * zc).astype(bf16), bd_ref[:cw, :cw], preferred_element_type=f32) * (1.0 / HEAD_DIM)
            zn = zc * lax.rsqrt(ms + EPS) * g_ref[:, c0:c0 + cw]
            for h0 in range(0, cw, LANES):
                zh = zn[:, h0:h0 + LANES]
                sw = jnp.where(first_half, pltpu.roll(zh, LANES - HEAD_DIM // 2, 1), pltpu.roll(zh, HEAD_DIM // 2, 1))
                chunks.append(zh * c + sw * s)
        return chunks

    def store(chunks, out_ref):
        for j, ch in enumerate(chunks):
            out_ref[:, j * LANES:(j + 1) * LANES] = ch.astype(out_ref.dtype)

    def store_dilated(chunks, out_refs):
        for j, ch in enumerate(chunks):
            st_ref[j] = ch
        for (_, dil), out_ref in zip(WINDOW_DILATIONS, out_refs):
            for r in range(dil):
                for j in range(len(chunks)):
                    src = pl.ds(r, rows // dil, stride=dil) if dil > 1 else pl.ds(0, rows)
                    out_ref[r, :, j * LANES:(j + 1) * LANES] = st_ref[j, src, :].astype(out_ref.dtype)

    store(norm_rope(OFF_QA, A_WIDTH, gqa_ref, ca_ref, sa_ref), qa_ref)
    store(norm_rope(OFF_KA, KA_COLS, gka_ref, ca_ref, sa_ref), ka_ref)
    va = proj(OFF_VA, VA_COLS)
    va_ref[...] = jnp.concatenate(
        [jnp.where(lane < HEAD_DIM, va[:, c0:c0 + LANES], 1.0) for c0 in range(0, VA_COLS, LANES)],
        axis=1).astype(va_ref.dtype)
    store_dilated(norm_rope(OFF_QB, B_WIDTH, gqb_ref, cb_ref, sb_ref), (qb1_ref, qb4_ref, qb16_ref))
    store_dilated(norm_rope(OFF_KB, B_WIDTH, gkb_ref, cb_ref, sb_ref), (kb1_ref, kb4_ref, kb16_ref))
    vb = proj(OFF_VB, B_WIDTH)
    store_dilated([vb[:, c0:c0 + LANES] for c0 in range(0, B_WIDTH, LANES)], (vb1_ref, vb4_ref, vb16_ref))
    ga_ref[...] = jax.nn.sigmoid(proj(OFF_GA, D_MODEL)).astype(ga_ref.dtype)
    gb_ref[...] = jax.nn.sigmoid(proj(OFF_GB, D_MODEL)).astype(gb_ref.dtype)


def _in_proj(x2, S, prm):
    T = x2.shape[0]
    tm = TOKEN_TILE
    n_pos = S // tm
    row = lambda i: (i, 0)
    pos = lambda i: (i % n_pos, 0)
    const = lambda i: (0, 0)
    dils = [dil for _, dil in WINDOW_DILATIONS]
    flat = lambda w: (pl.BlockSpec((tm, w), row), jax.ShapeDtypeStruct((T, w), bf16))
    strided = lambda d: (pl.BlockSpec((d, tm // d, B_WIDTH), lambda i: (0, i, 0)),
                         jax.ShapeDtypeStruct((d, T // d, B_WIDTH), bf16))
    outs = ([flat(A_WIDTH), flat(KA_COLS), flat(VA_COLS)] + 3 * [strided(d) for d in dils]
            + [flat(D_MODEL), flat(D_MODEL)])
    return pl.pallas_call(
        _in_proj_kernel,
        grid=(T // tm,),
        in_specs=[
            pl.BlockSpec((tm, D_MODEL), row),
            pl.BlockSpec((1, D_MODEL), const),
            pl.BlockSpec((D_MODEL, PROJ_COLS), const),
            pl.BlockSpec((1, A_WIDTH), const),
            pl.BlockSpec((1, KA_COLS), const),
            pl.BlockSpec((1, B_WIDTH), const),
            pl.BlockSpec((1, B_WIDTH), const),
            pl.BlockSpec((tm, LANES), pos),
            pl.BlockSpec((tm, LANES), pos),
            pl.BlockSpec((tm, LANES), pos),
            pl.BlockSpec((tm, LANES), pos),
            pl.BlockSpec((NORM_COLS, NORM_COLS), const),
        ],
        out_specs=[spec for spec, _ in outs],
        out_shape=[shape for _, shape in outs],
        scratch_shapes=[pltpu.VMEM((B_WIDTH // LANES, tm, LANES), f32)],
        compiler_params=_cparams(("parallel",)),
        name="in_proj",
    )(x2, prm["norm_mix"], prm["w_in"], prm["gqa"], prm["gka"], prm["gqb"], prm["gkb"],
      prm["cos_a"][S], prm["sin_a"][S], prm["cos_b"][S], prm["sin_b"][S], prm["bd"])


def _flash_kernel(q_ref, k_ref, v_ref, o_ref, s_ref, *, tq, tk, seq):
    lane = lax.broadcasted_iota(jnp.int32, (tq, LANES), 1)
    lo = lane < HEAD_DIM
    q = q_ref[0]
    q01, q23 = q[:, :LANES], q[:, LANES:]
    zero = jnp.zeros_like(q01)
    qs = [jnp.where(lo, q01, zero), jnp.where(lo, zero, q01), jnp.where(lo, q23, zero), jnp.where(lo, zero, q23)]
    n_chunks = seq // tk

    def keys(chunk):
        return k_ref[0, pl.ds(pl.multiple_of(chunk * tk, tk), tk), :]

    def score(h, kk):
        return lax.dot_general(qs[h], kk, (((1,), (1,)), ((), ())), preferred_element_type=f32)

    def attend(chunk, slot, ms, accs, next_chunk):
        vv = v_ref[0, pl.ds(pl.multiple_of(chunk * tk, tk), tk), :]
        kk = None if next_chunk is None else keys(next_chunk)
        new_ms, new_accs = [], []
        for h in range(GROUP_Q):
            if kk is not None:
                s_ref[1 - slot, h] = score(h, kk)
            s = s_ref[slot, h]
            m_new = jnp.maximum(ms[h], jnp.max(s, axis=-1, keepdims=True))
            p = jnp.exp2((s - m_new).astype(bf16))
            alpha = jnp.exp2(ms[h] - m_new)
            new_accs.append(accs[h] * alpha + jnp.dot(p, vv, preferred_element_type=f32))
            new_ms.append(m_new)
        return tuple(new_ms), tuple(new_accs)

    kk0 = keys(0)
    for h in range(GROUP_Q):
        s_ref[0, h] = score(h, kk0)

    def body(jj, carry):
        ms, accs = attend(2 * jj, 0, *carry, 2 * jj + 1)
        return attend(2 * jj + 1, 1, ms, accs, 2 * jj + 2)

    m0 = tuple(jnp.full((tq, 1), NEG_BIG, f32) for _ in range(GROUP_Q))
    acc0 = tuple(jnp.zeros((tq, LANES), f32) for _ in range(GROUP_Q))
    ms, accs = lax.fori_loop(0, n_chunks // 2 - 1, body, (m0, acc0), unroll=True)
    ms, accs = attend(n_chunks - 2, 0, ms, accs, n_chunks - 1)
    _, accs = attend(n_chunks - 1, 1, ms, accs, None)
    heads = [acc / jnp.where(lo, pltpu.roll(acc, HEAD_DIM, 1), 1.0) for acc in accs]
    out01 = jnp.where(lo, heads[0], pltpu.roll(heads[1], HEAD_DIM, 1))
    out23 = jnp.where(lo, heads[2], pltpu.roll(heads[3], HEAD_DIM, 1))
    o_ref[0] = jnp.concatenate([out01, out23], axis=1).astype(o_ref.dtype)


def _flash(qa, ka, va):
    B, S, _ = qa.shape
    tq, tk = FLASH_TQ, FLASH_TK
    return pl.pallas_call(
        functools.partial(_flash_kernel, tq=tq, tk=tk, seq=S),
        grid=(B, A_KV_HEADS, S // tq),
        in_specs=[
            pl.BlockSpec((1, tq, GROUP_Q * HEAD_DIM), lambda b, g, i: (b, i, g)),
            pl.BlockSpec((1, S, LANES), lambda b, g, i: (b, 0, g)),
            pl.BlockSpec((1, S, LANES), lambda b, g, i: (b, 0, g)),
        ],
        out_specs=pl.BlockSpec((1, tq, GROUP_Q * HEAD_DIM), lambda b, g, i: (b, i, g)),
        out_shape=jax.ShapeDtypeStruct((B, S, A_WIDTH), bf16),
        scratch_shapes=[pltpu.VMEM((2, GROUP_Q, tq, tk), f32)],
        compiler_params=_cparams(("parallel", "parallel", "arbitrary")),
        name="flash_gqa",
    )(qa, ka, va)


def _dilated_kernel(q_ref, k_ref, v_ref, o_ref, lse_ref, so_ref, sl_ref, *, length, tms, dil):
    i = pl.program_id(1)
    lane = lax.broadcasted_iota(jnp.int32, (DIL_QBLOCK, LANES), 1)
    lo = lane < HEAD_DIM
    n_pairs = B_WIDTH // LANES
    ones = jnp.ones((KEY_SPAN, LANES), bf16)

    def unit(u, carry):
        sb, r = u // dil, u % dil
        r0 = pl.multiple_of(sb * DIL_QBLOCK, DIL_QBLOCK)
        m0 = i * tms + r0
        start = pl.multiple_of(jnp.clip(m0 - HALF_WINDOW, 0, length - KEY_SPAN), HALF_WINDOW)
        qpos = m0 + lax.broadcasted_iota(jnp.int32, (DIL_QBLOCK, KEY_SPAN), 0)
        kpos = start + lax.broadcasted_iota(jnp.int32, (DIL_QBLOCK, KEY_SPAN), 1)
        valid = jnp.abs(kpos - qpos) <= HALF_WINDOW
        rows = pl.ds(r0 * dil + r, DIL_QBLOCK, stride=dil) if dil > 1 else pl.ds(r0, DIL_QBLOCK)
        scores = []
        for c in range(n_pairs):
            cs = slice(c * LANES, (c + 1) * LANES)
            qp = q_ref[r, 0, pl.ds(r0, DIL_QBLOCK), cs]
            kp = k_ref[r, 0, pl.ds(start, KEY_SPAN), cs]
            zero = jnp.zeros_like(qp)
            for half in range(2):
                qm = jnp.where(lo, qp, zero) if half == 0 else jnp.where(lo, zero, qp)
                scores.append(lax.dot_general(qm, kp, (((1,), (1,)), ((), ())), preferred_element_type=f32))
        for c in range(n_pairs):
            vp_ones = jnp.concatenate([v_ref[r, 0, pl.ds(start, KEY_SPAN), c * LANES:(c + 1) * LANES], ones], axis=1)
            ols, ms = [], []
            for half in range(2):
                s = jnp.where(valid, scores[2 * c + half], NEG_BIG)
                m = jnp.max(s, axis=-1, keepdims=True)
                e = jnp.exp2((s - m).astype(bf16))
                ols.append(jnp.dot(e, vp_ones, preferred_element_type=f32))
                ms.append(m)
            num = jnp.where(lo, ols[0][:, :LANES], ols[1][:, :LANES])
            den = jnp.where(lo, ols[0][:, LANES:], ols[1][:, LANES:])
            so_ref[c, rows, :] = num / den
            sl_ref[c, rows, :] = jnp.where(lo, ms[0], ms[1]) + jnp.log2(den)
        return carry

    lax.fori_loop(0, (tms // DIL_QBLOCK) * dil, unit, 0, unroll=4)
    o_ref[0] = jnp.concatenate([so_ref[c] for c in range(n_pairs)], axis=1).astype(o_ref.dtype)
    lse_ref[0] = jnp.concatenate([sl_ref[c] for c in range(n_pairs)], axis=1)


def _dilated(qb, kb, vb, B, S, dil):
    length = S // dil
    tms = DIL_STEP_TOKENS // dil
    view = lambda a: a.reshape(dil, B, length, B_WIDTH)
    o, lse = pl.pallas_call(
        functools.partial(_dilated_kernel, length=length, tms=tms, dil=dil),
        grid=(B, length // tms),
        in_specs=[
            pl.BlockSpec((dil, 1, tms, B_WIDTH), lambda b, i: (0, b, i, 0)),
            pl.BlockSpec((dil, 1, length, B_WIDTH), lambda b, i: (0, b, 0, 0)),
            pl.BlockSpec((dil, 1, length, B_WIDTH), lambda b, i: (0, b, 0, 0)),
        ],
        out_specs=[
            pl.BlockSpec((1, tms * dil, B_WIDTH), lambda b, i: (b, i, 0)),
            pl.BlockSpec((1, tms * dil, B_WIDTH), lambda b, i: (b, i, 0)),
        ],
        out_shape=[
            jax.ShapeDtypeStruct((B, S, B_WIDTH), bf16),
            jax.ShapeDtypeStruct((B, S, B_WIDTH), f32),
        ],
        scratch_shapes=[pltpu.VMEM((B_WIDTH // LANES, tms * dil, LANES), f32),
                        pltpu.VMEM((B_WIDTH // LANES, tms * dil, LANES), f32)],
        compiler_params=_cparams(("parallel", "arbitrary")),
        name=f"dilated_{dil}",
    )(view(qb), view(kb), view(vb))
    return o.reshape(B * S, B_WIDTH), lse.reshape(B * S, B_WIDTH)


def _first_index_of_max(cur, idx, n):
    mx = jnp.max(cur, axis=0, keepdims=True)
    return jnp.min(jnp.where(cur == mx, idx, n), axis=0, keepdims=True)


def _merge_route_kernel(x_ref, ya_ref, o1_ref, o2_ref, o3_ref, l1_ref, l2_ref, l3_ref, ga_ref, gb_ref,
                        wa_ref, wb_ref, wo_ref, nf_ref, rhi_ref, rlo_ref, bias_ref, tri_ref,
                        h_ref, u_ref, idx_ref, wt_ref, rank_ref, cnt_ref, run_ref):
    step = pl.program_id(0)

    @pl.when(step == 0)
    def _():
        run_ref[...] = jnp.zeros_like(run_ref)

    tm = x_ref.shape[0]
    sub = tri_ref.shape[0]

    def dense(j):
        rs = slice(j * sub, (j + 1) * sub)
        l1, l2, l3 = l1_ref[rs, :], l2_ref[rs, :], l3_ref[rs, :]
        lm = jnp.maximum(jnp.maximum(l1, l2), l3)
        e1, e2, e3 = jnp.exp2(l1 - lm), jnp.exp2(l2 - lm), jnp.exp2(l3 - lm)
        yb = ((e1 * o1_ref[rs, :].astype(f32) + e2 * o2_ref[rs, :].astype(f32) + e3 * o3_ref[rs, :].astype(f32))
              / (e1 + e2 + e3))
        pa = jnp.dot(ya_ref[rs, :], wa_ref[...], preferred_element_type=f32)
        pb = jnp.dot(yb.astype(bf16), wb_ref[...], preferred_element_type=f32)
        merged = ga_ref[rs, :].astype(f32) * pa + gb_ref[rs, :].astype(f32) * pb
        h = x_ref[rs, :] + jnp.dot(merged.astype(bf16), wo_ref[...], preferred_element_type=f32)
        h_ref[rs, :] = h
        r = lax.rsqrt(jnp.mean(h * h, axis=-1, keepdims=True) + EPS)
        uf = h * r * nf_ref[...]
        u_hi = uf.astype(bf16)
        for c, chunk in enumerate(_pack_rows(uf)):
            u_ref[c, rs, :] = chunk
        u_lo = (uf - u_hi.astype(f32)).astype(bf16)
        return (jnp.dot(u_hi, rhi_ref[...], preferred_element_type=f32)
                + jnp.dot(u_lo, rhi_ref[...], preferred_element_type=f32)
                + jnp.dot(u_hi, rlo_ref[...], preferred_element_type=f32))

    def route(j, logits):
        scores = jax.nn.sigmoid(logits.T[:N_EXPERTS, :])
        sel = scores + bias_ref[...]
        i8 = lax.broadcasted_iota(jnp.int32, (EXPERTS_PER_GROUP, sub), 0)
        neg_inf = jnp.float32(-jnp.inf)
        gscore = jnp.zeros((N_GROUPS, sub), f32)
        for g in range(N_GROUPS):
            blk = sel[g * EXPERTS_PER_GROUP:(g + 1) * EXPERTS_PER_GROUP, :]
            top1 = jnp.max(blk, axis=0, keepdims=True)
            first = _first_index_of_max(blk, i8, EXPERTS_PER_GROUP)
            top2 = jnp.max(jnp.where(i8 == first, neg_inf, blk), axis=0, keepdims=True)
            gscore = jnp.where(i8 == g, top1 + top2, gscore)
        gkeep = jnp.zeros((N_GROUPS, sub), jnp.bool_)
        cur = gscore
        for _ in range(TOPK_GROUPS):
            pick = i8 == _first_index_of_max(cur, i8, N_GROUPS)
            gkeep = jnp.logical_or(gkeep, pick)
            cur = jnp.where(pick, neg_inf, cur)
        cur = jnp.concatenate(
            [jnp.where(gkeep[g:g + 1, :], sel[g * EXPERTS_PER_GROUP:(g + 1) * EXPERTS_PER_GROUP, :], NEG_BIG)
             for g in range(N_GROUPS)], axis=0)

        ie = lax.broadcasted_iota(jnp.int32, (N_EXPERTS, sub), 0)
        ik = lax.broadcasted_iota(jnp.int32, (TOP_K, sub), 0)
        idx_out = jnp.zeros((TOP_K, sub), jnp.int32)
        w_out = jnp.zeros((TOP_K, sub), f32)
        chosen = jnp.zeros((N_EXPERTS, sub), jnp.bool_)
        for k in range(TOP_K):
            fi = _first_index_of_max(cur, ie, N_EXPERTS)
            pick = ie == fi
            wk = jnp.sum(jnp.where(pick, scores, 0.0), axis=0, keepdims=True)
            idx_out = jnp.where(ik == k, fi, idx_out)
            w_out = jnp.where(ik == k, wk, w_out)
            chosen = jnp.logical_or(chosen, pick)
            cur = jnp.where(pick, neg_inf, cur)
        w_out = w_out / jnp.sum(w_out, axis=0, keepdims=True) * ROUTE_SCALE
        wt_ref[:, j * sub:(j + 1) * sub] = w_out

        sel01 = jnp.where(chosen, 1.0, 0.0)
        before = jnp.dot(sel01.astype(bf16), tri_ref[...], preferred_element_type=f32) + run_ref[:, 0:1]
        rank_out = jnp.zeros((TOP_K, sub), f32)
        for k in range(TOP_K):
            rk = jnp.sum(jnp.where(ie == idx_out[k:k + 1, :], before, 0.0), axis=0, keepdims=True)
            rank_out = jnp.where(ik == k, rk, rank_out)
        rank_i = rank_out.astype(jnp.int32)
        for w in range(sub // SC_WINDOW):
            win = j * (sub // SC_WINDOW) + w
            idx_ref[win] = idx_out[:, w * SC_WINDOW:(w + 1) * SC_WINDOW]
            rank_ref[win] = rank_i[:, w * SC_WINDOW:(w + 1) * SC_WINDOW]
        run_ref[...] = run_ref[...] + jnp.sum(sel01, axis=1, keepdims=True)

    logits = [dense(j) for j in range(tm // sub)]
    for j in range(tm // sub):
        route(j, logits[j])
    cnt_ref[...] = run_ref[...].astype(jnp.int32)


def _merge_route(x2, ya, outs, lses, sga, sgb, prm):
    T = x2.shape[0]
    tm = TOKEN_TILE
    row = lambda i: (i, 0)
    col = lambda i: (0, i)
    const = lambda i: (0, 0)
    tok = lambda w: pl.BlockSpec((tm, w), row)
    full = lambda a: pl.BlockSpec(a.shape, const)
    win = pl.BlockSpec((tm // SC_WINDOW, TOP_K, SC_WINDOW), lambda i: (i, 0, 0))
    return pl.pallas_call(
        _merge_route_kernel,
        grid=(T // tm,),
        in_specs=[tok(D_MODEL), tok(A_WIDTH), tok(B_WIDTH), tok(B_WIDTH), tok(B_WIDTH),
                  tok(B_WIDTH), tok(B_WIDTH), tok(B_WIDTH), tok(D_MODEL), tok(D_MODEL),
                  full(prm["w_branch_a"]), full(prm["w_branch_b"]), full(prm["w_out"]), full(prm["norm_ffn"]),
                  full(prm["router_hi"]), full(prm["router_lo"]), full(prm["router_bias"]), full(prm["tri"])],
        out_specs=[tok(D_MODEL), pl.BlockSpec((ROW_CHUNKS, tm, LANES), lambda i: (0, i, 0)),
                   win, pl.BlockSpec((TOP_K, tm), col), win,
                   pl.BlockSpec((N_EXPERTS, LANES), const)],
        out_shape=[jax.ShapeDtypeStruct((T, D_MODEL), f32), jax.ShapeDtypeStruct((ROW_CHUNKS, T, LANES), jnp.uint32),
                   jax.ShapeDtypeStruct((T // SC_WINDOW, TOP_K, SC_WINDOW), jnp.int32),
                   jax.ShapeDtypeStruct((TOP_K, T), f32),
                   jax.ShapeDtypeStruct((T // SC_WINDOW, TOP_K, SC_WINDOW), jnp.int32),
                   jax.ShapeDtypeStruct((N_EXPERTS, LANES), jnp.int32)],
        scratch_shapes=[pltpu.VMEM((N_EXPERTS, LANES), f32)],
        compiler_params=_cparams(("arbitrary",)),
        name="merge_route",
    )(x2, ya, *outs, *lses, sga, sgb,
      prm["w_branch_a"], prm["w_branch_b"], prm["w_out"], prm["norm_ffn"],
      prm["router_hi"], prm["router_lo"], prm["router_bias"], prm["tri"])


def _experts_kernel(be_ref, nu_ref, nv_ref, x_ref, wg_ref, wu_ref, wd_ref, y_ref, wgu_s, wd_s):
    i = pl.program_id(0)

    @pl.when(jnp.logical_or(i == 0, be_ref[i] != be_ref[jnp.maximum(i - 1, 0)]))
    def _():
        wgu_s[:, :D_EXPERT] = wg_ref[0].astype(bf16)
        wgu_s[:, D_EXPERT:] = wu_ref[0].astype(bf16)
        wd_s[...] = wd_ref[0].astype(bf16)

    @pl.when(i < nu_ref[0])
    def _():
        n_sub = x_ref.shape[1] // MOE_SUB_ROWS

        def up(j):
            r0 = j * MOE_SUB_ROWS
            lo, hi = _unpack_rows([x_ref[c, r0:r0 + MOE_SUB_ROWS, :] for c in range(ROW_CHUNKS)])
            live = r0 + lax.broadcasted_iota(jnp.int32, lo.shape, 0) < nv_ref[i]
            x = jnp.concatenate([jnp.where(live, lo, 0.0), jnp.where(live, hi, 0.0)], axis=1).astype(bf16)
            return jnp.dot(x, wgu_s[...], preferred_element_type=f32)

        def down(j, gu):
            g, u = gu[:, :D_EXPERT], gu[:, D_EXPERT:]
            a = g * jax.nn.sigmoid(g) * u
            y = jnp.dot(a.astype(bf16), wd_s[...], preferred_element_type=f32)
            r0 = j * MOE_SUB_ROWS
            for c, chunk in enumerate(_pack_rows(y)):
                y_ref[c, r0:r0 + MOE_SUB_ROWS, :] = chunk

        gu_prev = up(0)
        for j in range(1, n_sub):
            gu_next = up(j)
            down(j - 1, gu_prev)
            gu_prev = gu_next
        down(n_sub - 1, gu_prev)

    @pl.when(i >= nu_ref[0])
    def _():
        y_ref[...] = jnp.zeros_like(y_ref)


def _experts(xs, blk_e, n_used, n_valid, prm):
    n_rows = xs.shape[1]
    bm = MOE_ROWS
    rows = lambda i, be, nu, nv: (0, jnp.minimum(i, nu[0] - 1), 0)
    return pl.pallas_call(
        _experts_kernel,
        grid_spec=pltpu.PrefetchScalarGridSpec(
            num_scalar_prefetch=3,
            grid=(n_rows // bm,),
            in_specs=[
                pl.BlockSpec((ROW_CHUNKS, bm, LANES), rows),
                pl.BlockSpec((1, D_MODEL, D_EXPERT), lambda i, be, nu, nv: (be[i], 0, 0)),
                pl.BlockSpec((1, D_MODEL, D_EXPERT), lambda i, be, nu, nv: (be[i], 0, 0)),
                pl.BlockSpec((1, D_EXPERT, D_MODEL), lambda i, be, nu, nv: (be[i], 0, 0)),
            ],
            out_specs=pl.BlockSpec((ROW_CHUNKS, bm, LANES), lambda i, be, nu, nv: (0, i, 0)),
            scratch_shapes=[pltpu.VMEM((D_MODEL, 2 * D_EXPERT), bf16), pltpu.VMEM((D_EXPERT, D_MODEL), bf16)],
        ),
        out_shape=jax.ShapeDtypeStruct((ROW_CHUNKS, n_rows, LANES), jnp.uint32),
        compiler_params=_cparams(("arbitrary",)),
        name="experts",
    )(blk_e, n_used, n_valid, xs, prm["expert_gate"], prm["expert_up"], prm["expert_down"])


def _sc_mesh():
    return plsc.VectorSubcoreMesh(core_axis_name="c", subcore_axis_name="s")


def _sc_dispatch(u2p, dest, n_rows):
    T = u2p.shape[1]
    per_worker = T // SC_WORKERS
    n_it = per_worker // SC_WINDOW

    @functools.partial(
        pl.kernel, mesh=_sc_mesh(),
        out_type=jax.ShapeDtypeStruct((ROW_CHUNKS, n_rows, LANES), jnp.uint32),
        scratch_types=[pltpu.VMEM((TOP_K, SC_WINDOW), jnp.int32),
                       pltpu.VMEM((ROW_CHUNKS, SC_WINDOW, LANES), jnp.uint32),
                       pltpu.SemaphoreType.DMA, pltpu.SemaphoreType.DMA],
    )
    def k(u_hbm, dest_hbm, xs_hbm, idx_v, rows_v, sem_in, sem_out):
        wid = lax.axis_index("s") * SC_CORES + lax.axis_index("c")

        @pl.loop(0, n_it)
        def _(it):
            t0 = wid * per_worker + it * SC_WINDOW
            pltpu.sync_copy(dest_hbm.at[wid * n_it + it], idx_v)
            loads = [pltpu.async_copy(u_hbm.at[c].at[pl.ds(t0, SC_WINDOW)], rows_v.at[c], sem_in)
                     for c in range(ROW_CHUNKS)]
            for cp in loads:
                cp.wait()
            stores = [pltpu.async_copy(rows_v.at[c], xs_hbm.at[c].at[idx_v.at[kk]], sem_out)
                      for kk in range(TOP_K) for c in range(ROW_CHUNKS)]
            for cp in stores:
                cp.wait()

    return k(u2p, dest)


def _sc_combine(ys, dest):
    n_windows = dest.shape[0]
    T = n_windows * SC_WINDOW
    per_worker = T // SC_WORKERS
    n_it = per_worker // SC_WINDOW

    @functools.partial(
        pl.kernel, mesh=_sc_mesh(),
        out_type=jax.ShapeDtypeStruct((TOP_K, ROW_CHUNKS, T, LANES), jnp.uint32),
        scratch_types=[pltpu.VMEM((TOP_K, SC_WINDOW), jnp.int32),
                       pltpu.VMEM((SC_SLABS, SC_WINDOW, LANES), jnp.uint32),
                       pltpu.SemaphoreType.DMA, pltpu.SemaphoreType.DMA],
    )
    def k(ys_hbm, dest_hbm, yg_hbm, idx_v, rows_v, sem_in, sem_out):
        wid = lax.axis_index("s") * SC_CORES + lax.axis_index("c")

        @pl.loop(0, n_it)
        def _(it):
            t0 = wid * per_worker + it * SC_WINDOW
            pltpu.sync_copy(dest_hbm.at[wid * n_it + it], idx_v)
            for c in range(ROW_CHUNKS):
                for k0 in range(0, TOP_K, SC_SLABS):
                    loads = [pltpu.async_copy(ys_hbm.at[c].at[idx_v.at[k0 + j]], rows_v.at[j], sem_in)
                             for j in range(SC_SLABS)]
                    for cp in loads:
                        cp.wait()
                    stores = [pltpu.async_copy(rows_v.at[j], yg_hbm.at[k0 + j].at[c].at[pl.ds(t0, SC_WINDOW)], sem_out)
                              for j in range(SC_SLABS)]
                    for cp in stores:
                        cp.wait()

    return k(ys, dest)


def _final_kernel(h_ref, u_ref, yg_ref, wt_ref, p_ref, sgu_ref, sd_ref, pg_ref, pp_ref, o_ref):
    wl = wt_ref[...]
    wt = jnp.concatenate([wl, jnp.zeros((LANES - TOP_K, wl.shape[1]), f32)], axis=0).T
    r_lo = r_hi = None
    for k in range(TOP_K):
        lo, hi = _unpack_rows([yg_ref[k, c] for c in range(ROW_CHUNKS)])
        wk = wt[:, k:k + 1]
        r_lo = lo * wk if k == 0 else r_lo + lo * wk
        r_hi = hi * wk if k == 0 else r_hi + hi * wk
    routed = jnp.concatenate([r_lo, r_hi], axis=1)
    u_lo, u_hi = _unpack_rows([u_ref[c] for c in range(ROW_CHUNKS)])
    un = jnp.concatenate([u_lo, u_hi], axis=1).astype(bf16)
    gu = jnp.dot(un, sgu_ref[...], preferred_element_type=f32)
    g, u = gu[:, :D_SHARED], gu[:, D_SHARED:]
    shared = jnp.dot((g * jax.nn.sigmoid(g) * u).astype(bf16), sd_ref[...], preferred_element_type=f32)
    h = h_ref[...] + (routed + shared)
    gate = jax.nn.sigmoid(jnp.dot(h.astype(bf16), pg_ref[...], preferred_element_type=f32))
    emb = jnp.dot(p_ref[...].astype(bf16), pp_ref[...], preferred_element_type=f32)
    o_ref[...] = h + gate * emb


def _final(h1, u2p, yg, wts, p2, prm):
    T = h1.shape[0]
    tm = TOKEN_TILE
    row = lambda i: (i, 0)
    const = lambda i: (0, 0)
    full = lambda a: pl.BlockSpec(a.shape, const)
    return pl.pallas_call(
        _final_kernel,
        grid=(T // tm,),
        in_specs=[pl.BlockSpec((tm, D_MODEL), row),
                  pl.BlockSpec((ROW_CHUNKS, tm, LANES), lambda i: (0, i, 0)),
                  pl.BlockSpec((TOP_K, ROW_CHUNKS, tm, LANES), lambda i: (0, 0, i, 0)),
                  pl.BlockSpec((TOP_K, tm), lambda i: (0, i)),
                  pl.BlockSpec((tm, PLE_DIM), row),
                  full(prm["shared_gu"]), full(prm["shared_down"]), full(prm["ple_gate"]), full(prm["ple_proj"])],
        out_specs=pl.BlockSpec((tm, D_MODEL), row),
        out_shape=jax.ShapeDtypeStruct((T, D_MODEL), f32),
        compiler_params=_cparams(("parallel",)),
        name="final",
    )(h1, u2p, yg, wts, p2, prm["shared_gu"], prm["shared_down"], prm["ple_gate"], prm["ple_proj"])


def _deinterleave(n_heads):
    base = jnp.concatenate([jnp.arange(0, HEAD_DIM, 2), jnp.arange(1, HEAD_DIM, 2)])
    return (jnp.arange(n_heads)[:, None] * HEAD_DIM + base[None, :]).reshape(-1)


def _rope_tables(ang):
    c, s = jnp.cos(ang), jnp.sin(ang)
    return jnp.tile(jnp.concatenate([c, c], axis=-1), (1, 2)), jnp.tile(jnp.concatenate([-s, s], axis=-1), (1, 2))


def _axial_angles(S):
    rows = S // GRID_W
    row = jnp.repeat(jnp.arange(rows, dtype=f32), GRID_W)
    col = jnp.tile(jnp.arange(GRID_W, dtype=f32), rows)
    half = HEAD_DIM // 2
    inv = ROPE_THETA ** (-jnp.arange(0, half, 2, dtype=f32) / half)
    return jnp.concatenate([row[:, None] * inv, col[:, None] * inv], axis=-1)


def _linear_angles(S):
    t = jnp.arange(S, dtype=f32)
    inv = ROPE_THETA ** (-jnp.arange(0, HEAD_DIM, 2, dtype=f32) / HEAD_DIM)
    return t[:, None] * inv


def _prepare(seq_lens, norm_mix, w_in, q_norm_a, k_norm_a, q_norm_b, k_norm_b, w_branch_a, w_branch_b, w_out,
             norm_ffn, router_w, router_bias, expert_gate, expert_up, expert_down, shared_gate, shared_up,
             shared_down, ple_proj, ple_gate):
    cuts = [A_WIDTH, A_WIDTH + A_KV_WIDTH, A_WIDTH + 2 * A_KV_WIDTH,
            A_WIDTH + 2 * A_KV_WIDTH + B_WIDTH, A_WIDTH + 2 * A_KV_WIDTH + 2 * B_WIDTH,
            A_WIDTH + 2 * A_KV_WIDTH + 3 * B_WIDTH, A_WIDTH + 2 * A_KV_WIDTH + 3 * B_WIDTH + D_MODEL]
    wqa, wka, wva, wqb, wkb, wvb, wga, wgb = jnp.split(w_in, cuts, axis=-1)
    wqa = wqa[:, _deinterleave(A_Q_HEADS)]
    wka = wka[:, _deinterleave(A_KV_HEADS)]
    wqb = wqb[:, _deinterleave(B_HEADS)]
    wkb = wkb[:, _deinterleave(B_HEADS)]
    k0, k1 = wka[:, :HEAD_DIM], wka[:, HEAD_DIM:]
    v0, v1 = wva[:, :HEAD_DIM], wva[:, HEAD_DIM:]
    vz = jnp.zeros_like(v0)
    w_all = jnp.concatenate([wqa, k0, k0, k1, k1, v0, vz, v1, vz, wqb, wkb, wvb, wga, wgb], axis=-1).astype(bf16)
    perm = _deinterleave(1)
    scale = HEAD_DIM ** -0.5
    tile = lambda g, n: jnp.tile(g[perm], n)[None, :]
    router_pad = jnp.pad(router_w, ((0, 0), (0, LANES - N_EXPERTS)))
    router_hi = router_pad.astype(bf16)
    prm = {
        "norm_mix": norm_mix[None, :], "w_in": w_all,
        "gqa": tile(q_norm_a, A_Q_HEADS) * (scale * LOG2_E), "gka": tile(k_norm_a, KA_COLS // HEAD_DIM),
        "gqb": tile(q_norm_b, B_HEADS) * (scale * LOG2_E), "gkb": tile(k_norm_b, B_HEADS),
        "bd": jnp.kron(jnp.eye(NORM_COLS // HEAD_DIM, dtype=f32), jnp.ones((HEAD_DIM, HEAD_DIM), f32)).astype(bf16),
        "w_branch_a": w_branch_a.astype(bf16), "w_branch_b": w_branch_b.astype(bf16), "w_out": w_out.astype(bf16),
        "norm_ffn": norm_ffn[None, :],
        "router_hi": router_hi, "router_lo": (router_pad - router_hi.astype(f32)).astype(bf16),
        "router_bias": router_bias[:, None],
        "tri": (jnp.arange(MERGE_SUB)[:, None] < jnp.arange(MERGE_SUB)[None, :]).astype(bf16),
        "expert_gate": expert_gate, "expert_up": expert_up, "expert_down": expert_down,
        "shared_gu": jnp.concatenate([shared_gate, shared_up], axis=-1).astype(bf16),
        "shared_down": shared_down.astype(bf16),
        "ple_gate": ple_gate.astype(bf16), "ple_proj": ple_proj.astype(bf16),
        "cos_a": {}, "sin_a": {}, "cos_b": {}, "sin_b": {},
    }
    for S in seq_lens:
        prm["cos_a"][S], prm["sin_a"][S] = _rope_tables(_axial_angles(S))
        prm["cos_b"][S], prm["sin_b"][S] = _rope_tables(_linear_angles(S))
    return prm


def _dispatch_plan(idx, rank, counts, T):
    bm = MOE_ROWS
    n_rows = (T * TOP_K + N_EXPERTS * bm) // bm * bm
    n_blocks = n_rows // bm
    padded = (counts + bm - 1) // bm * bm
    pad_end = jnp.cumsum(padded)
    pad_start = pad_end - padded
    dest = pad_start[idx] + rank
    blk_start = jnp.arange(n_blocks, dtype=jnp.int32) * bm
    blk_e = jnp.minimum(jnp.sum(pad_end[None, :] <= blk_start[:, None], axis=1), N_EXPERTS - 1).astype(jnp.int32)
    n_used = (pad_end[-1:] // bm).astype(jnp.int32)
    own = blk_e[:, None] == jnp.arange(N_EXPERTS, dtype=jnp.int32)[None, :]
    real_end = jnp.sum(jnp.where(own, (pad_start + counts)[None, :], 0), axis=1)
    n_valid = jnp.clip(real_end - blk_start, 0, bm).astype(jnp.int32)
    return dest, n_rows, blk_e, n_used, n_valid


def _layer(x, p, prm):
    B, S, _ = x.shape
    T = B * S
    x2 = x.reshape(T, D_MODEL)
    qa, ka, va, *qkv_b, sga, sgb = _in_proj(x2, S, prm)
    shape3 = lambda a: a.reshape(B, S, a.shape[-1])
    ya = _flash(shape3(qa), shape3(ka), shape3(va)).reshape(T, A_WIDTH)
    n_dil = len(WINDOW_DILATIONS)
    outs, lses = [], []
    for j, (_, dil) in enumerate(WINDOW_DILATIONS):
        o_d, lse_d = _dilated(qkv_b[j], qkv_b[n_dil + j], qkv_b[2 * n_dil + j], B, S, dil)
        outs.append(o_d)
        lses.append(lse_d)
    h1, u2p, idx, wts, rank, cnt = _merge_route(x2, ya, outs, lses, sga, sgb, prm)
    dest, n_rows, blk_e, n_used, n_valid = _dispatch_plan(idx, rank, cnt[:, 0], T)
    xs = _sc_dispatch(u2p, dest, n_rows)
    ys = _experts(xs, blk_e, n_used, n_valid, prm)
    yg = _sc_combine(ys, dest)
    out = _final(h1, u2p, yg, wts, p.reshape(T, PLE_DIM), prm)
    return out.reshape(B, S, D_MODEL)


def kernel(x_prompt, x_sample, p_prompt, p_sample, norm_mix, w_in, q_norm_a, k_norm_a, q_norm_b, k_norm_b,
           w_branch_a, w_branch_b, w_out, norm_ffn, router_w, router_bias, expert_gate, expert_up, expert_down,
           shared_gate, shared_up, shared_down, ple_proj, ple_gate):
    weights = (norm_mix, w_in, q_norm_a, k_norm_a, q_norm_b, k_norm_b, w_branch_a, w_branch_b, w_out, norm_ffn,
               router_w, router_bias, expert_gate, expert_up, expert_down, shared_gate, shared_up, shared_down,
               ple_proj, ple_gate)
    depth = norm_mix.shape[0]
    seq_lens = sorted({x_prompt.shape[1], x_sample.shape[1]})
    hp, hs = x_prompt, x_sample
    for i in range(depth):
        prm = _prepare(seq_lens, *[w[i] for w in weights])
        hp = _layer(hp, p_prompt[i], prm)
        hs = _layer(hs, p_sample[i], prm)
    return hp, hs
```

```python
import functools

import jax
import jax.numpy as jnp
from jax import lax
from jax.experimental import pallas as pl
from jax.experimental.pallas import tpu as pltpu
from jax.experimental.pallas import tpu_sc as plsc

D_MODEL = 1024
HEAD_DIM = 64
A_Q_HEADS = 8
A_KV_HEADS = 2
B_HEADS = 6
GRID_W = 64
ROPE_THETA = 10000.0
WINDOW_DILATIONS = ((128, 1), (512, 4), (2048, 16))
N_EXPERTS = 64
TOP_K = 8
N_GROUPS = 8
TOPK_GROUPS = 4
D_EXPERT = 256
D_SHARED = 256
ROUTE_SCALE = 2.5
PLE_DIM = 256
NEG_BIG = -1e30
EPS = 1e-6
LOG2_E = 1.4426950408889634

A_WIDTH = A_Q_HEADS * HEAD_DIM
A_KV_WIDTH = A_KV_HEADS * HEAD_DIM
B_WIDTH = B_HEADS * HEAD_DIM
GROUP_Q = A_Q_HEADS // A_KV_HEADS
EXPERTS_PER_GROUP = N_EXPERTS // N_GROUPS

LANES = 128
NORM_COLS = 256
HALF_WINDOW = 64
KEY_SPAN = 256

KA_COLS = 2 * LANES
VA_COLS = 2 * LANES
OFF_QA = 0
OFF_KA = OFF_QA + A_WIDTH
OFF_VA = OFF_KA + KA_COLS
OFF_QB = OFF_VA + VA_COLS
OFF_KB = OFF_QB + B_WIDTH
OFF_VB = OFF_KB + B_WIDTH
OFF_GA = OFF_VB + B_WIDTH
OFF_GB = OFF_GA + D_MODEL
PROJ_COLS = OFF_GB + D_MODEL

TOKEN_TILE = 512
MERGE_SUB = 256
FLASH_TQ = 256
FLASH_TK = 1024
FLASH_ITEMS = 8
DIL_QBLOCK = 128
DIL_STEP_TOKENS = 2048
MOE_ROWS = 1024
MOE_SUB_ROWS = 256

VMEM_LIMIT = 52 * 1024 * 1024

HALF_MODEL = D_MODEL // 2
ROW_CHUNKS = HALF_MODEL // LANES
SC_CORES = 2
SC_SUBCORES = 16
SC_WORKERS = SC_CORES * SC_SUBCORES
SC_WINDOW = 128
SC_SLABS = 4

bf16 = jnp.bfloat16
f32 = jnp.float32


def _cparams(sem):
    return pltpu.CompilerParams(dimension_semantics=sem, vmem_limit_bytes=VMEM_LIMIT)


def _pack_rows(v):
    lo = lax.bitcast_convert_type(v[:, :HALF_MODEL].astype(bf16).astype(f32), jnp.uint32) >> 16
    hi = lax.bitcast_convert_type(v[:, HALF_MODEL:].astype(bf16).astype(f32), jnp.uint32) & jnp.uint32(0xFFFF0000)
    w = lo | hi
    return [w[:, c * LANES:(c + 1) * LANES] for c in range(ROW_CHUNKS)]


def _unpack_rows(chunks):
    w = jnp.concatenate(chunks, axis=1)
    lo = lax.bitcast_convert_type(w << 16, f32)
    hi = lax.bitcast_convert_type(w & jnp.uint32(0xFFFF0000), f32)
    return lo, hi


def _in_proj_kernel(x_ref, nm_ref, w_ref, gqa_ref, gka_ref, gqb_ref, gkb_ref,
                    ca_ref, sa_ref, cb_ref, sb_ref, bd_ref,
                    qa_ref, ka_ref, va_ref, qb1_ref, qb4_ref, qb16_ref, kb1_ref, kb4_ref, kb16_ref,
                    vb1_ref, vb4_ref, vb16_ref, ga_ref, gb_ref, st_ref):
    x = x_ref[...]
    r = lax.rsqrt(jnp.mean(x * x, axis=-1, keepdims=True) + EPS)
    u = (x * r * nm_ref[...]).astype(bf16)
    rows = x.shape[0]
    lane = lax.broadcasted_iota(jnp.int32, (rows, LANES), 1)
    first_half = (lane % HEAD_DIM) < (HEAD_DIM // 2)

    def proj(off, width):
        return jnp.dot(u, w_ref[:, off:off + width], preferred_element_type=f32)

    def norm_rope(off, width, g_ref, c_ref, s_ref):
        z = proj(off, width)
        c = c_ref[...]
        s = s_ref[...]
        chunks = []
        for c0 in range(0, width, NORM_COLS):
            cw = min(NORM_COLS, width - c0)
            zc = z[:, c0:c0 + cw]
            ms = jnp.dot((zc * zc).astype(bf16), bd_ref[:cw, :cw], preferred_element_type=f32) * (1.0 / HEAD_DIM)
            zn = zc * lax.rsqrt(ms + EPS) * g_ref[:, c0:c0 + cw]
            for h0 in range(0, cw, LANES):
                zh = zn[:, h0:h0 + LANES]
                sw = jnp.where(first_half, pltpu.roll(zh, LANES - HEAD_DIM // 2, 1), pltpu.roll(zh, HEAD_DIM // 2, 1))
                chunks.append(zh * c + sw * s)
        return chunks

    def store(chunks, out_ref):
        for j, ch in enumerate(chunks):
            out_ref[:, j * LANES:(j + 1) * LANES] = ch.astype(out_ref.dtype)

    def store_dilated(chunks, out_refs):
        for j, ch in enumerate(chunks):
            st_ref[j] = ch
        for (_, dil), out_ref in zip(WINDOW_DILATIONS, out_refs):
            for r in range(dil):
                for j in range(len(chunks)):
                    src = pl.ds(r, rows // dil, stride=dil) if dil > 1 else pl.ds(0, rows)
                    out_ref[r, :, j * LANES:(j + 1) * LANES] = st_ref[j, src, :].astype(out_ref.dtype)

    store(norm_rope(OFF_QA, A_WIDTH, gqa_ref, ca_ref, sa_ref), qa_ref)
    store(norm_rope(OFF_KA, KA_COLS, gka_ref, ca_ref, sa_ref), ka_ref)
    va = proj(OFF_VA, VA_COLS)
    va_ref[...] = jnp.concatenate(
        [jnp.where(lane < HEAD_DIM, va[:, c0:c0 + LANES], 1.0) for c0 in range(0, VA_COLS, LANES)],
        axis=1).astype(va_ref.dtype)
    store_dilated(norm_rope(OFF_QB, B_WIDTH, gqb_ref, cb_ref, sb_ref), (qb1_ref, qb4_ref, qb16_ref))
    store_dilated(norm_rope(OFF_KB, B_WIDTH, gkb_ref, cb_ref, sb_ref), (kb1_ref, kb4_ref, kb16_ref))
    vb = proj(OFF_VB, B_WIDTH)
    store_dilated([vb[:, c0:c0 + LANES] for c0 in range(0, B_WIDTH, LANES)], (vb1_ref, vb4_ref, vb16_ref))
    ga_ref[...] = jax.nn.sigmoid(proj(OFF_GA, D_MODEL)).astype(ga_ref.dtype)
    gb_ref[...] = jax.nn.sigmoid(proj(OFF_GB, D_MODEL)).astype(gb_ref.dtype)


def _in_proj(x2, S, prm):
    T = x2.shape[0]
    tm = TOKEN_TILE
    n_pos = S // tm
    row = lambda i: (i, 0)
    pos = lambda i: (i % n_pos, 0)
    const = lambda i: (0, 0)
    dils = [dil for _, dil in WINDOW_DILATIONS]
    flat = lambda w: (pl.BlockSpec((tm, w), row), jax.ShapeDtypeStruct((T, w), bf16))
    strided = lambda d: (pl.BlockSpec((d, tm // d, B_WIDTH), lambda i: (0, i, 0)),
                         jax.ShapeDtypeStruct((d, T // d, B_WIDTH), bf16))
    outs = ([flat(A_WIDTH), flat(KA_COLS), flat(VA_COLS)] + 3 * [strided(d) for d in dils]
            + [flat(D_MODEL), flat(D_MODEL)])
    return pl.pallas_call(
        _in_proj_kernel,
        grid=(T // tm,),
        in_specs=[
            pl.BlockSpec((tm, D_MODEL), row),
            pl.BlockSpec((1, D_MODEL), const),
            pl.BlockSpec((D_MODEL, PROJ_COLS), const),
            pl.BlockSpec((1, A_WIDTH), const),
            pl.BlockSpec((1, KA_COLS), const),
            pl.BlockSpec((1, B_WIDTH), const),
            pl.BlockSpec((1, B_WIDTH), const),
            pl.BlockSpec((tm, LANES), pos),
            pl.BlockSpec((tm, LANES), pos),
            pl.BlockSpec((tm, LANES), pos),
            pl.BlockSpec((tm, LANES), pos),
            pl.BlockSpec((NORM_COLS, NORM_COLS), const),
        ],
        out_specs=[spec for spec, _ in outs],
        out_shape=[shape for _, shape in outs],
        scratch_shapes=[pltpu.VMEM((B_WIDTH // LANES, tm, LANES), f32)],
        compiler_params=_cparams(("parallel",)),
        name="in_proj",
    )(x2, prm["norm_mix"], prm["w_in"], prm["gqa"], prm["gka"], prm["gqb"], prm["gkb"],
      prm["cos_a"][S], prm["sin_a"][S], prm["cos_b"][S], prm["sin_b"][S], prm["bd"])


def _flash_kernel(q_ref, k_ref, v_ref, o_ref, s_ref, *, tq, tk, seq, nq):
    lane = lax.broadcasted_iota(jnp.int32, (tq, LANES), 1)
    lo = lane < HEAD_DIM
    n_chunks = seq // tk

    def masked_heads(b):
        q = q_ref[0, b * tq:(b + 1) * tq, :]
        q01, q23 = q[:, :LANES], q[:, LANES:]
        zero = jnp.zeros_like(q01)
        return [jnp.where(lo, q01, zero), jnp.where(lo, zero, q01), jnp.where(lo, q23, zero), jnp.where(lo, zero, q23)]

    def score(qh, chunk):
        kk = k_ref[0, chunk * tk:(chunk + 1) * tk, :]
        return lax.dot_general(qh, kk, (((1,), (1,)), ((), ())), preferred_element_type=f32)

    qs = [masked_heads(b) for b in range(nq)]
    items = [(b, c) for b in range(nq) for c in range(n_chunks)]
    for h in range(GROUP_Q):
        s_ref[0, h] = score(qs[0][h], 0)
    ms = accs = None
    for n, (b, chunk) in enumerate(items):
        slot = n % 2
        nxt = items[n + 1] if n + 1 < len(items) else None
        if chunk == 0:
            ms = [jnp.full((tq, 1), NEG_BIG, f32) for _ in range(GROUP_Q)]
            accs = [jnp.zeros((tq, LANES), f32) for _ in range(GROUP_Q)]
        vv = v_ref[0, chunk * tk:(chunk + 1) * tk, :]
        for h in range(GROUP_Q):
            if nxt is not None:
                s_ref[1 - slot, h] = score(qs[nxt[0]][h], nxt[1])
            s = s_ref[slot, h]
            m_new = jnp.maximum(ms[h], jnp.max(s, axis=-1, keepdims=True))
            p = jnp.exp2((s - m_new).astype(bf16))
            alpha = jnp.exp2(ms[h] - m_new)
            accs[h] = accs[h] * alpha + jnp.dot(p, vv, preferred_element_type=f32)
            ms[h] = m_new
        if chunk == n_chunks - 1:
            heads = [acc / jnp.where(lo, pltpu.roll(acc, HEAD_DIM, 1), 1.0) for acc in accs]
            out01 = jnp.where(lo, heads[0], pltpu.roll(heads[1], HEAD_DIM, 1))
            out23 = jnp.where(lo, heads[2], pltpu.roll(heads[3], HEAD_DIM, 1))
            o_ref[0, b * tq:(b + 1) * tq, :] = jnp.concatenate([out01, out23], axis=1).astype(o_ref.dtype)


def _flash(qa, ka, va):
    B, S, _ = qa.shape
    tq, tk = FLASH_TQ, FLASH_TK
    nq = max(1, FLASH_ITEMS // (S // tk))
    return pl.pallas_call(
        functools.partial(_flash_kernel, tq=tq, tk=tk, seq=S, nq=nq),
        grid=(B, A_KV_HEADS, S // (nq * tq)),
        in_specs=[
            pl.BlockSpec((1, nq * tq, GROUP_Q * HEAD_DIM), lambda b, g, i: (b, i, g)),
            pl.BlockSpec((1, S, LANES), lambda b, g, i: (b, 0, g)),
            pl.BlockSpec((1, S, LANES), lambda b, g, i: (b, 0, g)),
        ],
        out_specs=pl.BlockSpec((1, nq * tq, GROUP_Q * HEAD_DIM), lambda b, g, i: (b, i, g)),
        out_shape=jax.ShapeDtypeStruct((B, S, A_WIDTH), bf16),
        scratch_shapes=[pltpu.VMEM((2, GROUP_Q, tq, tk), f32)],
        compiler_params=_cparams(("parallel", "parallel", "arbitrary")),
        name="flash_gqa",
    )(qa, ka, va)


def _dilated_kernel(q_ref, k_ref, v_ref, o_ref, lse_ref, so_ref, sl_ref, *, length, tms, dil):
    i = pl.program_id(1)
    lane = lax.broadcasted_iota(jnp.int32, (DIL_QBLOCK, LANES), 1)
    lo = lane < HEAD_DIM
    n_pairs = B_WIDTH // LANES
    ones = jnp.ones((KEY_SPAN, LANES), bf16)

    def unit(u, carry):
        sb, r = u // dil, u % dil
        r0 = pl.multiple_of(sb * DIL_QBLOCK, DIL_QBLOCK)
        m0 = i * tms + r0
        start = pl.multiple_of(jnp.clip(m0 - HALF_WINDOW, 0, length - KEY_SPAN), HALF_WINDOW)
        qpos = m0 + lax.broadcasted_iota(jnp.int32, (DIL_QBLOCK, KEY_SPAN), 0)
        kpos = start + lax.broadcasted_iota(jnp.int32, (DIL_QBLOCK, KEY_SPAN), 1)
        valid = jnp.abs(kpos - qpos) <= HALF_WINDOW
        rows = pl.ds(r0 * dil + r, DIL_QBLOCK, stride=dil) if dil > 1 else pl.ds(r0, DIL_QBLOCK)
        scores = []
        for c in range(n_pairs):
            cs = slice(c * LANES, (c + 1) * LANES)
            qp = q_ref[r, 0, pl.ds(r0, DIL_QBLOCK), cs]
            kp = k_ref[r, 0, pl.ds(start, KEY_SPAN), cs]
            zero = jnp.zeros_like(qp)
            for half in range(2):
                qm = jnp.where(lo, qp, zero) if half == 0 else jnp.where(lo, zero, qp)
                scores.append(lax.dot_general(qm, kp, (((1,), (1,)), ((), ())), preferred_element_type=f32))
        for c in range(n_pairs):
            vp_ones = jnp.concatenate([v_ref[r, 0, pl.ds(start, KEY_SPAN), c * LANES:(c + 1) * LANES], ones], axis=1)
            ols, ms = [], []
            for half in range(2):
                s = jnp.where(valid, scores[2 * c + half], NEG_BIG)
                m = jnp.max(s, axis=-1, keepdims=True)
                e = jnp.exp2((s - m).astype(bf16))
                ols.append(jnp.dot(e, vp_ones, preferred_element_type=f32))
                ms.append(m)
            num = jnp.where(lo, ols[0][:, :LANES], ols[1][:, :LANES])
            den = jnp.where(lo, ols[0][:, LANES:], ols[1][:, LANES:])
            so_ref[c, rows, :] = num / den
            sl_ref[c, rows, :] = jnp.where(lo, ms[0], ms[1]) + jnp.log2(den)
        return carry

    lax.fori_loop(0, (tms // DIL_QBLOCK) * dil, unit, 0, unroll=4)
    o_ref[0] = jnp.concatenate([so_ref[c] for c in range(n_pairs)], axis=1).astype(o_ref.dtype)
    lse_ref[0] = jnp.concatenate([sl_ref[c] for c in range(n_pairs)], axis=1)


def _dilated(qb, kb, vb, B, S, dil):
    length = S // dil
    tms = DIL_STEP_TOKENS // dil
    view = lambda a: a.reshape(dil, B, length, B_WIDTH)
    o, lse = pl.pallas_call(
        functools.partial(_dilated_kernel, length=length, tms=tms, dil=dil),
        grid=(B, length // tms),
        in_specs=[
            pl.BlockSpec((dil, 1, tms, B_WIDTH), lambda b, i: (0, b, i, 0)),
            pl.BlockSpec((dil, 1, length, B_WIDTH), lambda b, i: (0, b, 0, 0)),
            pl.BlockSpec((dil, 1, length, B_WIDTH), lambda b, i: (0, b, 0, 0)),
        ],
        out_specs=[
            pl.BlockSpec((1, tms * dil, B_WIDTH), lambda b, i: (b, i, 0)),
            pl.BlockSpec((1, tms * dil, B_WIDTH), lambda b, i: (b, i, 0)),
        ],
        out_shape=[
            jax.ShapeDtypeStruct((B, S, B_WIDTH), bf16),
            jax.ShapeDtypeStruct((B, S, B_WIDTH), f32),
        ],
        scratch_shapes=[pltpu.VMEM((B_WIDTH // LANES, tms * dil, LANES), f32),
                        pltpu.VMEM((B_WIDTH // LANES, tms * dil, LANES), f32)],
        compiler_params=_cparams(("parallel", "arbitrary")),
        name=f"dilated_{dil}",
    )(view(qb), view(kb), view(vb))
    return o.reshape(B * S, B_WIDTH), lse.reshape(B * S, B_WIDTH)


def _first_index_of_max(cur, idx, n):
    mx = jnp.max(cur, axis=0, keepdims=True)
    return jnp.min(jnp.where(cur == mx, idx, n), axis=0, keepdims=True)


def _merge_route_kernel(x_ref, ya_ref, o1_ref, o2_ref, o3_ref, l1_ref, l2_ref, l3_ref, ga_ref, gb_ref,
                        wa_ref, wb_ref, wo_ref, nf_ref, rhi_ref, rlo_ref, bias_ref, tri_ref,
                        h_ref, u_ref, idx_ref, wt_ref, rank_ref, cnt_ref, run_ref):
    step = pl.program_id(0)

    @pl.when(step == 0)
    def _():
        run_ref[...] = jnp.zeros_like(run_ref)

    tm = x_ref.shape[0]
    sub = tri_ref.shape[0]

    def dense(j):
        rs = slice(j * sub, (j + 1) * sub)
        l1, l2, l3 = l1_ref[rs, :], l2_ref[rs, :], l3_ref[rs, :]
        lm = jnp.maximum(jnp.maximum(l1, l2), l3)
        e1, e2, e3 = jnp.exp2(l1 - lm), jnp.exp2(l2 - lm), jnp.exp2(l3 - lm)
        yb = ((e1 * o1_ref[rs, :].astype(f32) + e2 * o2_ref[rs, :].astype(f32) + e3 * o3_ref[rs, :].astype(f32))
              / (e1 + e2 + e3))
        pa = jnp.dot(ya_ref[rs, :], wa_ref[...], preferred_element_type=f32)
        pb = jnp.dot(yb.astype(bf16), wb_ref[...], preferred_element_type=f32)
        merged = ga_ref[rs, :].astype(f32) * pa + gb_ref[rs, :].astype(f32) * pb
        h = x_ref[rs, :] + jnp.dot(merged.astype(bf16), wo_ref[...], preferred_element_type=f32)
        h_ref[rs, :] = h
        r = lax.rsqrt(jnp.mean(h * h, axis=-1, keepdims=True) + EPS)
        uf = h * r * nf_ref[...]
        u_hi = uf.astype(bf16)
        for c, chunk in enumerate(_pack_rows(uf)):
            u_ref[c, rs, :] = chunk
        u_lo = (uf - u_hi.astype(f32)).astype(bf16)
        return (jnp.dot(u_hi, rhi_ref[...], preferred_element_type=f32)
                + jnp.dot(u_lo, rhi_ref[...], preferred_element_type=f32)
                + jnp.dot(u_hi, rlo_ref[...], preferred_element_type=f32))

    def route(j, logits):
        scores = jax.nn.sigmoid(logits.T[:N_EXPERTS, :])
        sel = scores + bias_ref[...]
        i8 = lax.broadcasted_iota(jnp.int32, (EXPERTS_PER_GROUP, sub), 0)
        neg_inf = jnp.float32(-jnp.inf)
        gscore = jnp.zeros((N_GROUPS, sub), f32)
        for g in range(N_GROUPS):
            blk = sel[g * EXPERTS_PER_GROUP:(g + 1) * EXPERTS_PER_GROUP, :]
            top1 = jnp.max(blk, axis=0, keepdims=True)
            first = _first_index_of_max(blk, i8, EXPERTS_PER_GROUP)
            top2 = jnp.max(jnp.where(i8 == first, neg_inf, blk), axis=0, keepdims=True)
            gscore = jnp.where(i8 == g, top1 + top2, gscore)
        gkeep = jnp.zeros((N_GROUPS, sub), jnp.bool_)
        cur = gscore
        for _ in range(TOPK_GROUPS):
            pick = i8 == _first_index_of_max(cur, i8, N_GROUPS)
            gkeep = jnp.logical_or(gkeep, pick)
            cur = jnp.where(pick, neg_inf, cur)
        cur = jnp.concatenate(
            [jnp.where(gkeep[g:g + 1, :], sel[g * EXPERTS_PER_GROUP:(g + 1) * EXPERTS_PER_GROUP, :], NEG_BIG)
             for g in range(N_GROUPS)], axis=0)

        ie = lax.broadcasted_iota(jnp.int32, (N_EXPERTS, sub), 0)
        ik = lax.broadcasted_iota(jnp.int32, (TOP_K, sub), 0)
        idx_out = jnp.zeros((TOP_K, sub), jnp.int32)
        w_out = jnp.zeros((TOP_K, sub), f32)
        chosen = jnp.zeros((N_EXPERTS, sub), jnp.bool_)
        for k in range(TOP_K):
            fi = _first_index_of_max(cur, ie, N_EXPERTS)
            pick = ie == fi
            wk = jnp.sum(jnp.where(pick, scores, 0.0), axis=0, keepdims=True)
            idx_out = jnp.where(ik == k, fi, idx_out)
            w_out = jnp.where(ik == k, wk, w_out)
            chosen = jnp.logical_or(chosen, pick)
            cur = jnp.where(pick, neg_inf, cur)
        w_out = w_out / jnp.sum(w_out, axis=0, keepdims=True) * ROUTE_SCALE
        wt_ref[:, j * sub:(j + 1) * sub] = w_out

        sel01 = jnp.where(chosen, 1.0, 0.0)
        before = jnp.dot(sel01.astype(bf16), tri_ref[...], preferred_element_type=f32) + run_ref[:, 0:1]
        rank_out = jnp.zeros((TOP_K, sub), f32)
        for k in range(TOP_K):
            rk = jnp.sum(jnp.where(ie == idx_out[k:k + 1, :], before, 0.0), axis=0, keepdims=True)
            rank_out = jnp.where(ik == k, rk, rank_out)
        rank_i = rank_out.astype(jnp.int32)
        for w in range(sub // SC_WINDOW):
            win = j * (sub // SC_WINDOW) + w
            idx_ref[win] = idx_out[:, w * SC_WINDOW:(w + 1) * SC_WINDOW]
            rank_ref[win] = rank_i[:, w * SC_WINDOW:(w + 1) * SC_WINDOW]
        run_ref[...] = run_ref[...] + jnp.sum(sel01, axis=1, keepdims=True)

    logits = [dense(j) for j in range(tm // sub)]
    for j in range(tm // sub):
        route(j, logits[j])
    cnt_ref[...] = run_ref[...].astype(jnp.int32)


def _merge_route(x2, ya, outs, lses, sga, sgb, prm):
    T = x2.shape[0]
    tm = TOKEN_TILE
    row = lambda i: (i, 0)
    col = lambda i: (0, i)
    const = lambda i: (0, 0)
    tok = lambda w: pl.BlockSpec((tm, w), row)
    full = lambda a: pl.BlockSpec(a.shape, const)
    win = pl.BlockSpec((tm // SC_WINDOW, TOP_K, SC_WINDOW), lambda i: (i, 0, 0))
    return pl.pallas_call(
        _merge_route_kernel,
        grid=(T // tm,),
        in_specs=[tok(D_MODEL), tok(A_WIDTH), tok(B_WIDTH), tok(B_WIDTH), tok(B_WIDTH),
                  tok(B_WIDTH), tok(B_WIDTH), tok(B_WIDTH), tok(D_MODEL), tok(D_MODEL),
                  full(prm["w_branch_a"]), full(prm["w_branch_b"]), full(prm["w_out"]), full(prm["norm_ffn"]),
                  full(prm["router_hi"]), full(prm["router_lo"]), full(prm["router_bias"]), full(prm["tri"])],
        out_specs=[tok(D_MODEL), pl.BlockSpec((ROW_CHUNKS, tm, LANES), lambda i: (0, i, 0)),
                   win, pl.BlockSpec((TOP_K, tm), col), win,
                   pl.BlockSpec((N_EXPERTS, LANES), const)],
        out_shape=[jax.ShapeDtypeStruct((T, D_MODEL), f32), jax.ShapeDtypeStruct((ROW_CHUNKS, T, LANES), jnp.uint32),
                   jax.ShapeDtypeStruct((T // SC_WINDOW, TOP_K, SC_WINDOW), jnp.int32),
                   jax.ShapeDtypeStruct((TOP_K, T), f32),
                   jax.ShapeDtypeStruct((T // SC_WINDOW, TOP_K, SC_WINDOW), jnp.int32),
                   jax.ShapeDtypeStruct((N_EXPERTS, LANES), jnp.int32)],
        scratch_shapes=[pltpu.VMEM((N_EXPERTS, LANES), f32)],
        compiler_params=_cparams(("arbitrary",)),
        name="merge_route",
    )(x2, ya, *outs, *lses, sga, sgb,
      prm["w_branch_a"], prm["w_branch_b"], prm["w_out"], prm["norm_ffn"],
      prm["router_hi"], prm["router_lo"], prm["router_bias"], prm["tri"])


def _experts_kernel(be_ref, nu_ref, nv_ref, x_ref, wg_ref, wu_ref, wd_ref, y_ref):
    del be_ref
    i = pl.program_id(0)

    @pl.when(i < nu_ref[0])
    def _():
        n_sub = x_ref.shape[1] // MOE_SUB_ROWS

        def up(j):
            r0 = j * MOE_SUB_ROWS
            lo, hi = _unpack_rows([x_ref[c, r0:r0 + MOE_SUB_ROWS, :] for c in range(ROW_CHUNKS)])
            live = r0 + lax.broadcasted_iota(jnp.int32, lo.shape, 0) < nv_ref[i]
            x = jnp.concatenate([jnp.where(live, lo, 0.0), jnp.where(live, hi, 0.0)], axis=1).astype(bf16)
            return (jnp.dot(x, wg_ref[0], preferred_element_type=f32),
                    jnp.dot(x, wu_ref[0], preferred_element_type=f32))

        def down(j, gu):
            g, u = gu
            a = g * jax.nn.sigmoid(g) * u
            y = jnp.dot(a.astype(bf16), wd_ref[0], preferred_element_type=f32)
            r0 = j * MOE_SUB_ROWS
            for c, chunk in enumerate(_pack_rows(y)):
                y_ref[c, r0:r0 + MOE_SUB_ROWS, :] = chunk

        gu_prev = up(0)
        for j in range(1, n_sub):
            gu_next = up(j)
            down(j - 1, gu_prev)
            gu_prev = gu_next
        down(n_sub - 1, gu_prev)

    @pl.when(i >= nu_ref[0])
    def _():
        y_ref[...] = jnp.zeros_like(y_ref)


def _experts(xs, blk_e, n_used, n_valid, prm):
    n_rows = xs.shape[1]
    bm = MOE_ROWS
    rows = lambda i, be, nu, nv: (0, jnp.minimum(i, nu[0] - 1), 0)
    return pl.pallas_call(
        _experts_kernel,
        grid_spec=pltpu.PrefetchScalarGridSpec(
            num_scalar_prefetch=3,
            grid=(n_rows // bm,),
            in_specs=[
                pl.BlockSpec((ROW_CHUNKS, bm, LANES), rows),
                pl.BlockSpec((1, D_MODEL, D_EXPERT), lambda i, be, nu, nv: (be[i], 0, 0)),
                pl.BlockSpec((1, D_MODEL, D_EXPERT), lambda i, be, nu, nv: (be[i], 0, 0)),
                pl.BlockSpec((1, D_EXPERT, D_MODEL), lambda i, be, nu, nv: (be[i], 0, 0)),
            ],
            out_specs=pl.BlockSpec((ROW_CHUNKS, bm, LANES), lambda i, be, nu, nv: (0, i, 0)),
        ),
        out_shape=jax.ShapeDtypeStruct((ROW_CHUNKS, n_rows, LANES), jnp.uint32),
        compiler_params=_cparams(("arbitrary",)),
        name="experts",
    )(blk_e, n_used, n_valid, xs, prm["expert_gate"], prm["expert_up"], prm["expert_down"])


def _sc_mesh():
    return plsc.VectorSubcoreMesh(core_axis_name="c", subcore_axis_name="s")


def _sc_dispatch(u2p, dest, n_rows):
    T = u2p.shape[1]
    per_worker = T // SC_WORKERS
    n_it = per_worker // SC_WINDOW

    @functools.partial(
        pl.kernel, mesh=_sc_mesh(),
        out_type=jax.ShapeDtypeStruct((ROW_CHUNKS, n_rows, LANES), jnp.uint32),
        scratch_types=[pltpu.VMEM((TOP_K, SC_WINDOW), jnp.int32),
                       pltpu.VMEM((ROW_CHUNKS, SC_WINDOW, LANES), jnp.uint32),
                       pltpu.SemaphoreType.DMA, pltpu.SemaphoreType.DMA],
    )
    def k(u_hbm, dest_hbm, xs_hbm, idx_v, rows_v, sem_in, sem_out):
        wid = lax.axis_index("s") * SC_CORES + lax.axis_index("c")

        @pl.loop(0, n_it)
        def _(it):
            t0 = wid * per_worker + it * SC_WINDOW
            pltpu.sync_copy(dest_hbm.at[wid * n_it + it], idx_v)
            loads = [pltpu.async_copy(u_hbm.at[c].at[pl.ds(t0, SC_WINDOW)], rows_v.at[c], sem_in)
                     for c in range(ROW_CHUNKS)]
            for cp in loads:
                cp.wait()
            stores = [pltpu.async_copy(rows_v.at[c], xs_hbm.at[c].at[idx_v.at[kk]], sem_out)
                      for kk in range(TOP_K) for c in range(ROW_CHUNKS)]
            for cp in stores:
                cp.wait()

    return k(u2p, dest)


def _sc_combine(ys, dest):
    n_windows = dest.shape[0]
    T = n_windows * SC_WINDOW
    per_worker = T // SC_WORKERS
    n_it = per_worker // SC_WINDOW

    @functools.partial(
        pl.kernel, mesh=_sc_mesh(),
        out_type=jax.ShapeDtypeStruct((TOP_K, ROW_CHUNKS, T, LANES), jnp.uint32),
        scratch_types=[pltpu.VMEM((TOP_K, SC_WINDOW), jnp.int32),
                       pltpu.VMEM((SC_SLABS, SC_WINDOW, LANES), jnp.uint32),
                       pltpu.SemaphoreType.DMA, pltpu.SemaphoreType.DMA],
    )
    def k(ys_hbm, dest_hbm, yg_hbm, idx_v, rows_v, sem_in, sem_out):
        wid = lax.axis_index("s") * SC_CORES + lax.axis_index("c")

        @pl.loop(0, n_it)
        def _(it):
            t0 = wid * per_worker + it * SC_WINDOW
            pltpu.sync_copy(dest_hbm.at[wid * n_it + it], idx_v)
            for c in range(ROW_CHUNKS):
                for k0 in range(0, TOP_K, SC_SLABS):
                    loads = [pltpu.async_copy(ys_hbm.at[c].at[idx_v.at[k0 + j]], rows_v.at[j], sem_in)
                             for j in range(SC_SLABS)]
                    for cp in loads:
                        cp.wait()
                    stores = [pltpu.async_copy(rows_v.at[j], yg_hbm.at[k0 + j].at[c].at[pl.ds(t0, SC_WINDOW)], sem_out)
                              for j in range(SC_SLABS)]
                    for cp in stores:
                        cp.wait()

    return k(ys, dest)


def _final_kernel(h_ref, u_ref, yg_ref, wt_ref, p_ref, sgu_ref, sd_ref, pg_ref, pp_ref, o_ref):
    wl = wt_ref[...]
    wt = jnp.concatenate([wl, jnp.zeros((LANES - TOP_K, wl.shape[1]), f32)], axis=0).T
    r_lo = r_hi = None
    for k in range(TOP_K):
        lo, hi = _unpack_rows([yg_ref[k, c] for c in range(ROW_CHUNKS)])
        wk = wt[:, k:k + 1]
        r_lo = lo * wk if k == 0 else r_lo + lo * wk
        r_hi = hi * wk if k == 0 else r_hi + hi * wk
    routed = jnp.concatenate([r_lo, r_hi], axis=1)
    u_lo, u_hi = _unpack_rows([u_ref[c] for c in range(ROW_CHUNKS)])
    un = jnp.concatenate([u_lo, u_hi], axis=1).astype(bf16)
    gu = jnp.dot(un, sgu_ref[...], preferred_element_type=f32)
    g, u = gu[:, :D_SHARED], gu[:, D_SHARED:]
    shared = jnp.dot((g * jax.nn.sigmoid(g) * u).astype(bf16), sd_ref[...], preferred_element_type=f32)
    h = h_ref[...] + (routed + shared)
    gate = jax.nn.sigmoid(jnp.dot(h.astype(bf16), pg_ref[...], preferred_element_type=f32))
    emb = jnp.dot(p_ref[...].astype(bf16), pp_ref[...], preferred_element_type=f32)
    o_ref[...] = h + gate * emb


def _final(h1, u2p, yg, wts, p2, prm):
    T = h1.shape[0]
    tm = TOKEN_TILE
    row = lambda i: (i, 0)
    const = lambda i: (0, 0)
    full = lambda a: pl.BlockSpec(a.shape, const)
    return pl.pallas_call(
        _final_kernel,
        grid=(T // tm,),
        in_specs=[pl.BlockSpec((tm, D_MODEL), row),
                  pl.BlockSpec((ROW_CHUNKS, tm, LANES), lambda i: (0, i, 0)),
                  pl.BlockSpec((TOP_K, ROW_CHUNKS, tm, LANES), lambda i: (0, 0, i, 0)),
                  pl.BlockSpec((TOP_K, tm), lambda i: (0, i)),
                  pl.BlockSpec((tm, PLE_DIM), row),
                  full(prm["shared_gu"]), full(prm["shared_down"]), full(prm["ple_gate"]), full(prm["ple_proj"])],
        out_specs=pl.BlockSpec((tm, D_MODEL), row),
        out_shape=jax.ShapeDtypeStruct((T, D_MODEL), f32),
        compiler_params=_cparams(("parallel",)),
        name="final",
    )(h1, u2p, yg, wts, p2, prm["shared_gu"], prm["shared_down"], prm["ple_gate"], prm["ple_proj"])


def _deinterleave(n_heads):
    base = jnp.concatenate([jnp.arange(0, HEAD_DIM, 2), jnp.arange(1, HEAD_DIM, 2)])
    return (jnp.arange(n_heads)[:, None] * HEAD_DIM + base[None, :]).reshape(-1)


def _rope_tables(ang):
    c, s = jnp.cos(ang), jnp.sin(ang)
    return jnp.tile(jnp.concatenate([c, c], axis=-1), (1, 2)), jnp.tile(jnp.concatenate([-s, s], axis=-1), (1, 2))


def _axial_angles(S):
    rows = S // GRID_W
    row = jnp.repeat(jnp.arange(rows, dtype=f32), GRID_W)
    col = jnp.tile(jnp.arange(GRID_W, dtype=f32), rows)
    half = HEAD_DIM // 2
    inv = ROPE_THETA ** (-jnp.arange(0, half, 2, dtype=f32) / half)
    return jnp.concatenate([row[:, None] * inv, col[:, None] * inv], axis=-1)


def _linear_angles(S):
    t = jnp.arange(S, dtype=f32)
    inv = ROPE_THETA ** (-jnp.arange(0, HEAD_DIM, 2, dtype=f32) / HEAD_DIM)
    return t[:, None] * inv


def _prepare(seq_lens, norm_mix, w_in, q_norm_a, k_norm_a, q_norm_b, k_norm_b, w_branch_a, w_branch_b, w_out,
             norm_ffn, router_w, router_bias, expert_gate, expert_up, expert_down, shared_gate, shared_up,
             shared_down, ple_proj, ple_gate):
    cuts = [A_WIDTH, A_WIDTH + A_KV_WIDTH, A_WIDTH + 2 * A_KV_WIDTH,
            A_WIDTH + 2 * A_KV_WIDTH + B_WIDTH, A_WIDTH + 2 * A_KV_WIDTH + 2 * B_WIDTH,
            A_WIDTH + 2 * A_KV_WIDTH + 3 * B_WIDTH, A_WIDTH + 2 * A_KV_WIDTH + 3 * B_WIDTH + D_MODEL]
    wqa, wka, wva, wqb, wkb, wvb, wga, wgb = jnp.split(w_in, cuts, axis=-1)
    wqa = wqa[:, _deinterleave(A_Q_HEADS)]
    wka = wka[:, _deinterleave(A_KV_HEADS)]
    wqb = wqb[:, _deinterleave(B_HEADS)]
    wkb = wkb[:, _deinterleave(B_HEADS)]
    k0, k1 = wka[:, :HEAD_DIM], wka[:, HEAD_DIM:]
    v0, v1 = wva[:, :HEAD_DIM], wva[:, HEAD_DIM:]
    vz = jnp.zeros_like(v0)
    w_all = jnp.concatenate([wqa, k0, k0, k1, k1, v0, vz, v1, vz, wqb, wkb, wvb, wga, wgb], axis=-1).astype(bf16)
    perm = _deinterleave(1)
    scale = HEAD_DIM ** -0.5
    tile = lambda g, n: jnp.tile(g[perm], n)[None, :]
    router_pad = jnp.pad(router_w, ((0, 0), (0, LANES - N_EXPERTS)))
    router_hi = router_pad.astype(bf16)
    prm = {
        "norm_mix": norm_mix[None, :], "w_in": w_all,
        "gqa": tile(q_norm_a, A_Q_HEADS) * (scale * LOG2_E), "gka": tile(k_norm_a, KA_COLS // HEAD_DIM),
        "gqb": tile(q_norm_b, B_HEADS) * (scale * LOG2_E), "gkb": tile(k_norm_b, B_HEADS),
        "bd": jnp.kron(jnp.eye(NORM_COLS // HEAD_DIM, dtype=f32), jnp.ones((HEAD_DIM, HEAD_DIM), f32)).astype(bf16),
        "w_branch_a": w_branch_a.astype(bf16), "w_branch_b": w_branch_b.astype(bf16), "w_out": w_out.astype(bf16),
        "norm_ffn": norm_ffn[None, :],
        "router_hi": router_hi, "router_lo": (router_pad - router_hi.astype(f32)).astype(bf16),
        "router_bias": router_bias[:, None],
        "tri": (jnp.arange(MERGE_SUB)[:, None] < jnp.arange(MERGE_SUB)[None, :]).astype(bf16),
        "expert_gate": expert_gate.astype(bf16), "expert_up": expert_up.astype(bf16),
        "expert_down": expert_down.astype(bf16),
        "shared_gu": jnp.concatenate([shared_gate, shared_up], axis=-1).astype(bf16),
        "shared_down": shared_down.astype(bf16),
        "ple_gate": ple_gate.astype(bf16), "ple_proj": ple_proj.astype(bf16),
        "cos_a": {}, "sin_a": {}, "cos_b": {}, "sin_b": {},
    }
    for S in seq_lens:
        prm["cos_a"][S], prm["sin_a"][S] = _rope_tables(_axial_angles(S))
        prm["cos_b"][S], prm["sin_b"][S] = _rope_tables(_linear_angles(S))
    return prm


def _dispatch_plan(idx, rank, counts, T):
    bm = MOE_ROWS
    n_rows = (T * TOP_K + N_EXPERTS * bm) // bm * bm
    n_blocks = n_rows // bm
    padded = (counts + bm - 1) // bm * bm
    pad_end = jnp.cumsum(padded)
    pad_start = pad_end - padded
    dest = pad_start[idx] + rank
    blk_start = jnp.arange(n_blocks, dtype=jnp.int32) * bm
    blk_e = jnp.minimum(jnp.sum(pad_end[None, :] <= blk_start[:, None], axis=1), N_EXPERTS - 1).astype(jnp.int32)
    n_used = (pad_end[-1:] // bm).astype(jnp.int32)
    own = blk_e[:, None] == jnp.arange(N_EXPERTS, dtype=jnp.int32)[None, :]
    real_end = jnp.sum(jnp.where(own, (pad_start + counts)[None, :], 0), axis=1)
    n_valid = jnp.clip(real_end - blk_start, 0, bm).astype(jnp.int32)
    return dest, n_rows, blk_e, n_used, n_valid


def _layer(x, p, prm):
    B, S, _ = x.shape
    T = B * S
    x2 = x.reshape(T, D_MODEL)
    qa, ka, va, *qkv_b, sga, sgb = _in_proj(x2, S, prm)
    shape3 = lambda a: a.reshape(B, S, a.shape[-1])
    ya = _flash(shape3(qa), shape3(ka), shape3(va)).reshape(T, A_WIDTH)
    n_dil = len(WINDOW_DILATIONS)
    outs, lses = [], []
    for j, (_, dil) in enumerate(WINDOW_DILATIONS):
        o_d, lse_d = _dilated(qkv_b[j], qkv_b[n_dil + j], qkv_b[2 * n_dil + j], B, S, dil)
        outs.append(o_d)
        lses.append(lse_d)
    h1, u2p, idx, wts, rank, cnt = _merge_route(x2, ya, outs, lses, sga, sgb, prm)
    dest, n_rows, blk_e, n_used, n_valid = _dispatch_plan(idx, rank, cnt[:, 0], T)
    xs = _sc_dispatch(u2p, dest, n_rows)
    ys = _experts(xs, blk_e, n_used, n_valid, prm)
    yg = _sc_combine(ys, dest)
    out = _final(h1, u2p, yg, wts, p.reshape(T, PLE_DIM), prm)
    return out.reshape(B, S, D_MODEL)


def kernel(x_prompt, x_sample, p_prompt, p_sample, norm_mix, w_in, q_norm_a, k_norm_a, q_norm_b, k_norm_b,
           w_branch_a, w_branch_b, w_out, norm_ffn, router_w, router_bias, expert_gate, expert_up, expert_down,
           shared_gate, shared_up, shared_down, ple_proj, ple_gate):
    weights = (norm_mix, w_in, q_norm_a, k_norm_a, q_norm_b, k_norm_b, w_branch_a, w_branch_b, w_out, norm_ffn,
               router_w, router_bias, expert_gate, expert_up, expert_down, shared_gate, shared_up, shared_down,
               ple_proj, ple_gate)
    depth = norm_mix.shape[0]
    seq_lens = sorted({x_prompt.shape[1], x_sample.shape[1]})
    hp, hs = x_prompt, x_sample
    for i in range(depth):
        prm = _prepare(seq_lens, *[w[i] for w in weights])
        hp = _layer(hp, p_prompt[i], prm)
        hs = _layer(hs, p_sample[i], prm)
    return hp, hs
```

```python
import functools

import jax
import jax.numpy as jnp
from jax import lax
from jax.experimental import pallas as pl
from jax.experimental.pallas import tpu as pltpu
from jax.experimental.pallas import tpu_sc as plsc

D_MODEL = 1024
HEAD_DIM = 64
A_Q_HEADS = 8
A_KV_HEADS = 2
B_HEADS = 6
GRID_W = 64
ROPE_THETA = 10000.0
WINDOW_DILATIONS = ((128, 1), (512, 4), (2048, 16))
N_EXPERTS = 64
TOP_K = 8
N_GROUPS = 8
TOPK_GROUPS = 4
D_EXPERT = 256
D_SHARED = 256
ROUTE_SCALE = 2.5
PLE_DIM = 256
NEG_BIG = -1e30
EPS = 1e-6
LOG2_E = 1.4426950408889634

A_WIDTH = A_Q_HEADS * HEAD_DIM
A_KV_WIDTH = A_KV_HEADS * HEAD_DIM
B_WIDTH = B_HEADS * HEAD_DIM
GROUP_Q = A_Q_HEADS // A_KV_HEADS
EXPERTS_PER_GROUP = N_EXPERTS // N_GROUPS

LANES = 128
NORM_COLS = 256
HALF_WINDOW = 64
KEY_SPAN = 256

KA_COLS = 2 * LANES
VA_COLS = 2 * LANES
OFF_QA = 0
OFF_KA = OFF_QA + A_WIDTH
OFF_VA = OFF_KA + KA_COLS
OFF_QB = OFF_VA + VA_COLS
OFF_KB = OFF_QB + B_WIDTH
OFF_VB = OFF_KB + B_WIDTH
OFF_GA = OFF_VB + B_WIDTH
OFF_GB = OFF_GA + D_MODEL
PROJ_COLS = OFF_GB + D_MODEL

TOKEN_TILE = 512
MERGE_SUB = 256
FLASH_TQ = 256
FLASH_TK = 1024
FLASH_ITEMS = 8
DIL_QBLOCK = 128
DIL_STEP_TOKENS = 2048
MOE_ROWS = 1024
MOE_SUB_ROWS = 256

VMEM_LIMIT = 52 * 1024 * 1024

HALF_MODEL = D_MODEL // 2
ROW_CHUNKS = HALF_MODEL // LANES
SC_CORES = 2
SC_SUBCORES = 16
SC_WORKERS = SC_CORES * SC_SUBCORES
SC_WINDOW = 128
SC_SLABS = 4

bf16 = jnp.bfloat16
f32 = jnp.float32


def _cparams(sem):
    return pltpu.CompilerParams(dimension_semantics=sem, vmem_limit_bytes=VMEM_LIMIT)


def _pack_rows(v):
    lo = lax.bitcast_convert_type(v[:, :HALF_MODEL].astype(bf16).astype(f32), jnp.uint32) >> 16
    hi = lax.bitcast_convert_type(v[:, HALF_MODEL:].astype(bf16).astype(f32), jnp.uint32) & jnp.uint32(0xFFFF0000)
    w = lo | hi
    return [w[:, c * LANES:(c + 1) * LANES] for c in range(ROW_CHUNKS)]


def _unpack_rows(chunks):
    w = jnp.concatenate(chunks, axis=1)
    lo = lax.bitcast_convert_type(w << 16, f32)
    hi = lax.bitcast_convert_type(w & jnp.uint32(0xFFFF0000), f32)
    return lo, hi


def _in_proj_kernel(x_ref, nm_ref, w_ref, gqa_ref, gka_ref, gqb_ref, gkb_ref,
                    ca_ref, sa_ref, cb_ref, sb_ref, bd_ref,
                    qa_ref, ka_ref, va_ref, qb1_ref, qb4_ref, qb16_ref, kb1_ref, kb4_ref, kb16_ref,
                    vb1_ref, vb4_ref, vb16_ref, ga_ref, gb_ref, st_ref):
    x = x_ref[...]
    r = lax.rsqrt(jnp.mean(x * x, axis=-1, keepdims=True) + EPS)
    u = (x * r * nm_ref[...]).astype(bf16)
    rows = x.shape[0]
    lane = lax.broadcasted_iota(jnp.int32, (rows, LANES), 1)
    first_half = (lane % HEAD_DIM) < (HEAD_DIM // 2)

    def proj(off, width):
        return jnp.dot(u, w_ref[:, off:off + width], preferred_element_type=f32)

    def norm_rope(off, width, g_ref, c_ref, s_ref):
        z = proj(off, width)
        c = c_ref[...]
        s = s_ref[...]
        chunks = []
        for c0 in range(0, width, NORM_COLS):
            cw = min(NORM_COLS, width - c0)
            zc = z[:, c0:c0 + cw]
            ms = jnp.dot((zc * zc).astype(bf16), bd_ref[:cw, :cw], preferred_element_type=f32) * (1.0 / HEAD_DIM)
            zn = zc * lax.rsqrt(ms + EPS) * g_ref[:, c0:c0 + cw]
            for h0 in range(0, cw, LANES):
                zh = zn[:, h0:h0 + LANES]
                sw = jnp.where(first_half, pltpu.roll(zh, LANES - HEAD_DIM // 2, 1), pltpu.roll(zh, HEAD_DIM // 2, 1))
                chunks.append(zh * c + sw * s)
        return chunks

    def store(chunks, out_ref):
        for j, ch in enumerate(chunks):
            out_ref[:, j * LANES:(j + 1) * LANES] = ch.astype(out_ref.dtype)

    def store_dilated(chunks, out_refs):
        for j, ch in enumerate(chunks):
            st_ref[j] = ch
        for (_, dil), out_ref in zip(WINDOW_DILATIONS, out_refs):
            for r in range(dil):
                for j in range(len(chunks)):
                    src = pl.ds(r, rows // dil, stride=dil) if dil > 1 else pl.ds(0, rows)
                    out_ref[r, :, j * LANES:(j + 1) * LANES] = st_ref[j, src, :].astype(out_ref.dtype)

    store(norm_rope(OFF_QA, A_WIDTH, gqa_ref, ca_ref, sa_ref), qa_ref)
    store(norm_rope(OFF_KA, KA_COLS, gka_ref, ca_ref, sa_ref), ka_ref)
    va = proj(OFF_VA, VA_COLS)
    va_ref[...] = jnp.concatenate(
        [jnp.where(lane < HEAD_DIM, va[:, c0:c0 + LANES], 1.0) for c0 in range(0, VA_COLS, LANES)],
        axis=1).astype(va_ref.dtype)
    store_dilated(norm_rope(OFF_QB, B_WIDTH, gqb_ref, cb_ref, sb_ref), (qb1_ref, qb4_ref, qb16_ref))
    store_dilated(norm_rope(OFF_KB, B_WIDTH, gkb_ref, cb_ref, sb_ref), (kb1_ref, kb4_ref, kb16_ref))
    vb = proj(OFF_VB, B_WIDTH)
    store_dilated([vb[:, c0:c0 + LANES] for c0 in range(0, B_WIDTH, LANES)], (vb1_ref, vb4_ref, vb16_ref))
    ga_ref[...] = jax.nn.sigmoid(proj(OFF_GA, D_MODEL)).astype(ga_ref.dtype)
    gb_ref[...] = jax.nn.sigmoid(proj(OFF_GB, D_MODEL)).astype(gb_ref.dtype)


def _in_proj(x2, S, prm):
    T = x2.shape[0]
    tm = TOKEN_TILE
    n_pos = S // tm
    row = lambda i: (i, 0)
    pos = lambda i: (i % n_pos, 0)
    const = lambda i: (0, 0)
    dils = [dil for _, dil in WINDOW_DILATIONS]
    flat = lambda w: (pl.BlockSpec((tm, w), row), jax.ShapeDtypeStruct((T, w), bf16))
    strided = lambda d: (pl.BlockSpec((d, tm // d, B_WIDTH), lambda i: (0, i, 0)),
                         jax.ShapeDtypeStruct((d, T // d, B_WIDTH), bf16))
    outs = ([flat(A_WIDTH), flat(KA_COLS), flat(VA_COLS)] + 3 * [strided(d) for d in dils]
            + [flat(D_MODEL), flat(D_MODEL)])
    return pl.pallas_call(
        _in_proj_kernel,
        grid=(T // tm,),
        in_specs=[
            pl.BlockSpec((tm, D_MODEL), row),
            pl.BlockSpec((1, D_MODEL), const),
            pl.BlockSpec((D_MODEL, PROJ_COLS), const),
            pl.BlockSpec((1, A_WIDTH), const),
            pl.BlockSpec((1, KA_COLS), const),
            pl.BlockSpec((1, B_WIDTH), const),
            pl.BlockSpec((1, B_WIDTH), const),
            pl.BlockSpec((tm, LANES), pos),
            pl.BlockSpec((tm, LANES), pos),
            pl.BlockSpec((tm, LANES), pos),
            pl.BlockSpec((tm, LANES), pos),
            pl.BlockSpec((NORM_COLS, NORM_COLS), const),
        ],
        out_specs=[spec for spec, _ in outs],
        out_shape=[shape for _, shape in outs],
        scratch_shapes=[pltpu.VMEM((B_WIDTH // LANES, tm, LANES), f32)],
        compiler_params=_cparams(("parallel",)),
        name="in_proj",
    )(x2, prm["norm_mix"], prm["w_in"], prm["gqa"], prm["gka"], prm["gqb"], prm["gkb"],
      prm["cos_a"][S], prm["sin_a"][S], prm["cos_b"][S], prm["sin_b"][S], prm["bd"])


def _flash_kernel(q_ref, k_ref, v_ref, o_ref, s_ref, *, tq, tk, seq, nq):
    lane = lax.broadcasted_iota(jnp.int32, (tq, LANES), 1)
    lo = lane < HEAD_DIM
    n_chunks = seq // tk

    def masked_heads(b):
        q = q_ref[0, b * tq:(b + 1) * tq, :]
        q01, q23 = q[:, :LANES], q[:, LANES:]
        zero = jnp.zeros_like(q01)
        return [jnp.where(lo, q01, zero), jnp.where(lo, zero, q01), jnp.where(lo, q23, zero), jnp.where(lo, zero, q23)]

    def score(qh, chunk):
        kk = k_ref[0, chunk * tk:(chunk + 1) * tk, :]
        return lax.dot_general(qh, kk, (((1,), (1,)), ((), ())), preferred_element_type=f32)

    qs = [masked_heads(b) for b in range(nq)]
    items = [(b, c) for b in range(nq) for c in range(n_chunks)]
    for h in range(GROUP_Q):
        s_ref[0, h] = score(qs[0][h], 0)
    ms = accs = None
    for n, (b, chunk) in enumerate(items):
        slot = n % 2
        nxt = items[n + 1] if n + 1 < len(items) else None
        if chunk == 0:
            ms = [jnp.full((tq, 1), NEG_BIG, f32) for _ in range(GROUP_Q)]
            accs = [jnp.zeros((tq, LANES), f32) for _ in range(GROUP_Q)]
        vv = v_ref[0, chunk * tk:(chunk + 1) * tk, :]
        for h in range(GROUP_Q):
            if nxt is not None:
                s_ref[1 - slot, h] = score(qs[nxt[0]][h], nxt[1])
            s = s_ref[slot, h]
            m_new = jnp.maximum(ms[h], jnp.max(s, axis=-1, keepdims=True))
            p = jnp.exp2((s - m_new).astype(bf16))
            alpha = jnp.exp2(ms[h] - m_new)
            accs[h] = accs[h] * alpha + jnp.dot(p, vv, preferred_element_type=f32)
            ms[h] = m_new
        if chunk == n_chunks - 1:
            heads = [acc / jnp.where(lo, pltpu.roll(acc, HEAD_DIM, 1), 1.0) for acc in accs]
            out01 = jnp.where(lo, heads[0], pltpu.roll(heads[1], HEAD_DIM, 1))
            out23 = jnp.where(lo, heads[2], pltpu.roll(heads[3], HEAD_DIM, 1))
            o_ref[0, b * tq:(b + 1) * tq, :] = jnp.concatenate([out01, out23], axis=1).astype(o_ref.dtype)


def _flash(qa, ka, va):
    B, S, _ = qa.shape
    tq, tk = FLASH_TQ, FLASH_TK
    nq = max(1, FLASH_ITEMS // (S // tk))
    return pl.pallas_call(
        functools.partial(_flash_kernel, tq=tq, tk=tk, seq=S, nq=nq),
        grid=(B, A_KV_HEADS, S // (nq * tq)),
        in_specs=[
            pl.BlockSpec((1, nq * tq, GROUP_Q * HEAD_DIM), lambda b, g, i: (b, i, g)),
            pl.BlockSpec((1, S, LANES), lambda b, g, i: (b, 0, g)),
            pl.BlockSpec((1, S, LANES), lambda b, g, i: (b, 0, g)),
        ],
        out_specs=pl.BlockSpec((1, nq * tq, GROUP_Q * HEAD_DIM), lambda b, g, i: (b, i, g)),
        out_shape=jax.ShapeDtypeStruct((B, S, A_WIDTH), bf16),
        scratch_shapes=[pltpu.VMEM((2, GROUP_Q, tq, tk), f32)],
        compiler_params=_cparams(("parallel", "parallel", "arbitrary")),
        name="flash_gqa",
    )(qa, ka, va)


def _dilated_kernel(q_ref, k_ref, v_ref, o_ref, lse_ref, so_ref, sl_ref, *, length, tms, dil):
    i = pl.program_id(1)
    lane = lax.broadcasted_iota(jnp.int32, (DIL_QBLOCK, LANES), 1)
    lo = lane < HEAD_DIM
    n_pairs = B_WIDTH // LANES
    ones = jnp.ones((KEY_SPAN, LANES), bf16)

    def unit(u, carry):
        sb, r = u // dil, u % dil
        r0 = pl.multiple_of(sb * DIL_QBLOCK, DIL_QBLOCK)
        m0 = i * tms + r0
        start = pl.multiple_of(jnp.clip(m0 - HALF_WINDOW, 0, length - KEY_SPAN), HALF_WINDOW)
        qpos = m0 + lax.broadcasted_iota(jnp.int32, (DIL_QBLOCK, KEY_SPAN), 0)
        kpos = start + lax.broadcasted_iota(jnp.int32, (DIL_QBLOCK, KEY_SPAN), 1)
        valid = jnp.abs(kpos - qpos) <= HALF_WINDOW
        rows = pl.ds(r0 * dil + r, DIL_QBLOCK, stride=dil) if dil > 1 else pl.ds(r0, DIL_QBLOCK)
        scores = []
        for c in range(n_pairs):
            cs = slice(c * LANES, (c + 1) * LANES)
            qp = q_ref[r, 0, pl.ds(r0, DIL_QBLOCK), cs]
            kp = k_ref[r, 0, pl.ds(start, KEY_SPAN), cs]
            zero = jnp.zeros_like(qp)
            for half in range(2):
                qm = jnp.where(lo, qp, zero) if half == 0 else jnp.where(lo, zero, qp)
                scores.append(lax.dot_general(qm, kp, (((1,), (1,)), ((), ())), preferred_element_type=f32))
        for c in range(n_pairs):
            vp_ones = jnp.concatenate([v_ref[r, 0, pl.ds(start, KEY_SPAN), c * LANES:(c + 1) * LANES], ones], axis=1)
            ols, ms = [], []
            for half in range(2):
                s = jnp.where(valid, scores[2 * c + half], NEG_BIG)
                m = jnp.max(s, axis=-1, keepdims=True)
                e = jnp.exp2((s - m).astype(bf16))
                ols.append(jnp.dot(e, vp_ones, preferred_element_type=f32))
                ms.append(m)
            num = jnp.where(lo, ols[0][:, :LANES], ols[1][:, :LANES])
            den = jnp.where(lo, ols[0][:, LANES:], ols[1][:, LANES:])
            so_ref[c, rows, :] = num / den
            sl_ref[c, rows, :] = jnp.where(lo, ms[0], ms[1]) + jnp.log2(den)
        return carry

    lax.fori_loop(0, (tms // DIL_QBLOCK) * dil, unit, 0, unroll=8)
    o_ref[0] = jnp.concatenate([so_ref[c] for c in range(n_pairs)], axis=1).astype(o_ref.dtype)
    lse_ref[0] = jnp.concatenate([sl_ref[c] for c in range(n_pairs)], axis=1)


def _dilated(qb, kb, vb, B, S, dil):
    length = S // dil
    tms = DIL_STEP_TOKENS // dil
    view = lambda a: a.reshape(dil, B, length, B_WIDTH)
    o, lse = pl.pallas_call(
        functools.partial(_dilated_kernel, length=length, tms=tms, dil=dil),
        grid=(B, length // tms),
        in_specs=[
            pl.BlockSpec((dil, 1, tms, B_WIDTH), lambda b, i: (0, b, i, 0)),
            pl.BlockSpec((dil, 1, length, B_WIDTH), lambda b, i: (0, b, 0, 0)),
            pl.BlockSpec((dil, 1, length, B_WIDTH), lambda b, i: (0, b, 0, 0)),
        ],
        out_specs=[
            pl.BlockSpec((1, tms * dil, B_WIDTH), lambda b, i: (b, i, 0)),
            pl.BlockSpec((1, tms * dil, B_WIDTH), lambda b, i: (b, i, 0)),
        ],
        out_shape=[
            jax.ShapeDtypeStruct((B, S, B_WIDTH), bf16),
            jax.ShapeDtypeStruct((B, S, B_WIDTH), f32),
        ],
        scratch_shapes=[pltpu.VMEM((B_WIDTH // LANES, tms * dil, LANES), f32),
                        pltpu.VMEM((B_WIDTH // LANES, tms * dil, LANES), f32)],
        compiler_params=_cparams(("parallel", "arbitrary")),
        name=f"dilated_{dil}",
    )(view(qb), view(kb), view(vb))
    return o.reshape(B * S, B_WIDTH), lse.reshape(B * S, B_WIDTH)


def _first_index_of_max(cur, idx, n):
    mx = jnp.max(cur, axis=0, keepdims=True)
    return jnp.min(jnp.where(cur == mx, idx, n), axis=0, keepdims=True)


def _merge_route_kernel(x_ref, ya_ref, o1_ref, o2_ref, o3_ref, l1_ref, l2_ref, l3_ref, ga_ref, gb_ref,
                        wa_ref, wb_ref, wo_ref, nf_ref, rhi_ref, rlo_ref, bias_ref, tri_ref,
                        h_ref, u_ref, idx_ref, wt_ref, rank_ref, cnt_ref, run_ref):
    step = pl.program_id(0)

    @pl.when(step == 0)
    def _():
        run_ref[...] = jnp.zeros_like(run_ref)

    tm = x_ref.shape[0]
    sub = tri_ref.shape[0]

    def dense(j):
        rs = slice(j * sub, (j + 1) * sub)
        l1, l2, l3 = l1_ref[rs, :], l2_ref[rs, :], l3_ref[rs, :]
        lm = jnp.maximum(jnp.maximum(l1, l2), l3)
        e1, e2, e3 = jnp.exp2(l1 - lm), jnp.exp2(l2 - lm), jnp.exp2(l3 - lm)
        yb = ((e1 * o1_ref[rs, :].astype(f32) + e2 * o2_ref[rs, :].astype(f32) + e3 * o3_ref[rs, :].astype(f32))
              / (e1 + e2 + e3))
        pa = jnp.dot(ya_ref[rs, :], wa_ref[...], preferred_element_type=f32)
        pb = jnp.dot(yb.astype(bf16), wb_ref[...], preferred_element_type=f32)
        merged = ga_ref[rs, :].astype(f32) * pa + gb_ref[rs, :].astype(f32) * pb
        h = x_ref[rs, :] + jnp.dot(merged.astype(bf16), wo_ref[...], preferred_element_type=f32)
        h_ref[rs, :] = h
        r = lax.rsqrt(jnp.mean(h * h, axis=-1, keepdims=True) + EPS)
        uf = h * r * nf_ref[...]
        u_hi = uf.astype(bf16)
        for c, chunk in enumerate(_pack_rows(uf)):
            u_ref[c, rs, :] = chunk
        u_lo = (uf - u_hi.astype(f32)).astype(bf16)
        return (jnp.dot(u_hi, rhi_ref[...], preferred_element_type=f32)
                + jnp.dot(u_lo, rhi_ref[...], preferred_element_type=f32)
                + jnp.dot(u_hi, rlo_ref[...], preferred_element_type=f32))

    def route(j, logits):
        scores = jax.nn.sigmoid(logits.T[:N_EXPERTS, :])
        sel = scores + bias_ref[...]
        i8 = lax.broadcasted_iota(jnp.int32, (EXPERTS_PER_GROUP, sub), 0)
        neg_inf = jnp.float32(-jnp.inf)
        gscore = jnp.zeros((N_GROUPS, sub), f32)
        for g in range(N_GROUPS):
            blk = sel[g * EXPERTS_PER_GROUP:(g + 1) * EXPERTS_PER_GROUP, :]
            top1 = jnp.max(blk, axis=0, keepdims=True)
            first = _first_index_of_max(blk, i8, EXPERTS_PER_GROUP)
            top2 = jnp.max(jnp.where(i8 == first, neg_inf, blk), axis=0, keepdims=True)
            gscore = jnp.where(i8 == g, top1 + top2, gscore)
        gkeep = jnp.zeros((N_GROUPS, sub), jnp.bool_)
        cur = gscore
        for _ in range(TOPK_GROUPS):
            pick = i8 == _first_index_of_max(cur, i8, N_GROUPS)
            gkeep = jnp.logical_or(gkeep, pick)
            cur = jnp.where(pick, neg_inf, cur)
        cur = jnp.concatenate(
            [jnp.where(gkeep[g:g + 1, :], sel[g * EXPERTS_PER_GROUP:(g + 1) * EXPERTS_PER_GROUP, :], NEG_BIG)
             for g in range(N_GROUPS)], axis=0)

        ie = lax.broadcasted_iota(jnp.int32, (N_EXPERTS, sub), 0)
        ik = lax.broadcasted_iota(jnp.int32, (TOP_K, sub), 0)
        idx_out = jnp.zeros((TOP_K, sub), jnp.int32)
        w_out = jnp.zeros((TOP_K, sub), f32)
        chosen = jnp.zeros((N_EXPERTS, sub), jnp.bool_)
        for k in range(TOP_K):
            fi = _first_index_of_max(cur, ie, N_EXPERTS)
            pick = ie == fi
            wk = jnp.sum(jnp.where(pick, scores, 0.0), axis=0, keepdims=True)
            idx_out = jnp.where(ik == k, fi, idx_out)
            w_out = jnp.where(ik == k, wk, w_out)
            chosen = jnp.logical_or(chosen, pick)
            cur = jnp.where(pick, neg_inf, cur)
        w_out = w_out / jnp.sum(w_out, axis=0, keepdims=True) * ROUTE_SCALE
        wt_ref[:, j * sub:(j + 1) * sub] = w_out

        sel01 = jnp.where(chosen, 1.0, 0.0)
        before = jnp.dot(sel01.astype(bf16), tri_ref[...], preferred_element_type=f32) + run_ref[:, 0:1]
        rank_out = jnp.zeros((TOP_K, sub), f32)
        for k in range(TOP_K):
            rk = jnp.sum(jnp.where(ie == idx_out[k:k + 1, :], before, 0.0), axis=0, keepdims=True)
            rank_out = jnp.where(ik == k, rk, rank_out)
        rank_i = rank_out.astype(jnp.int32)
        for w in range(sub // SC_WINDOW):
            win = j * (sub // SC_WINDOW) + w
            idx_ref[win] = idx_out[:, w * SC_WINDOW:(w + 1) * SC_WINDOW]
            rank_ref[win] = rank_i[:, w * SC_WINDOW:(w + 1) * SC_WINDOW]
        run_ref[...] = run_ref[...] + jnp.sum(sel01, axis=1, keepdims=True)

    logits = [dense(j) for j in range(tm // sub)]
    for j in range(tm // sub):
        route(j, logits[j])
    cnt_ref[...] = run_ref[...].astype(jnp.int32)


def _merge_route(x2, ya, outs, lses, sga, sgb, prm):
    T = x2.shape[0]
    tm = TOKEN_TILE
    row = lambda i: (i, 0)
    col = lambda i: (0, i)
    const = lambda i: (0, 0)
    tok = lambda w: pl.BlockSpec((tm, w), row)
    full = lambda a: pl.BlockSpec(a.shape, const)
    win = pl.BlockSpec((tm // SC_WINDOW, TOP_K, SC_WINDOW), lambda i: (i, 0, 0))
    return pl.pallas_call(
        _merge_route_kernel,
        grid=(T // tm,),
        in_specs=[tok(D_MODEL), tok(A_WIDTH), tok(B_WIDTH), tok(B_WIDTH), tok(B_WIDTH),
                  tok(B_WIDTH), tok(B_WIDTH), tok(B_WIDTH), tok(D_MODEL), tok(D_MODEL),
                  full(prm["w_branch_a"]), full(prm["w_branch_b"]), full(prm["w_out"]), full(prm["norm_ffn"]),
                  full(prm["router_hi"]), full(prm["router_lo"]), full(prm["router_bias"]), full(prm["tri"])],
        out_specs=[tok(D_MODEL), pl.BlockSpec((ROW_CHUNKS, tm, LANES), lambda i: (0, i, 0)),
                   win, pl.BlockSpec((TOP_K, tm), col), win,
                   pl.BlockSpec((N_EXPERTS, LANES), const)],
        out_shape=[jax.ShapeDtypeStruct((T, D_MODEL), f32), jax.ShapeDtypeStruct((ROW_CHUNKS, T, LANES), jnp.uint32),
                   jax.ShapeDtypeStruct((T // SC_WINDOW, TOP_K, SC_WINDOW), jnp.int32),
                   jax.ShapeDtypeStruct((TOP_K, T), f32),
                   jax.ShapeDtypeStruct((T // SC_WINDOW, TOP_K, SC_WINDOW), jnp.int32),
                   jax.ShapeDtypeStruct((N_EXPERTS, LANES), jnp.int32)],
        scratch_shapes=[pltpu.VMEM((N_EXPERTS, LANES), f32)],
        compiler_params=_cparams(("arbitrary",)),
        name="merge_route",
    )(x2, ya, *outs, *lses, sga, sgb,
      prm["w_branch_a"], prm["w_branch_b"], prm["w_out"], prm["norm_ffn"],
      prm["router_hi"], prm["router_lo"], prm["router_bias"], prm["tri"])


def _experts_kernel(be_ref, nu_ref, nv_ref, x_ref, wg_ref, wu_ref, wd_ref, y_ref):
    del be_ref
    i = pl.program_id(0)

    @pl.when(i < nu_ref[0])
    def _():
        n_sub = x_ref.shape[1] // MOE_SUB_ROWS

        def up(j):
            r0 = j * MOE_SUB_ROWS
            lo, hi = _unpack_rows([x_ref[c, r0:r0 + MOE_SUB_ROWS, :] for c in range(ROW_CHUNKS)])
            live = r0 + lax.broadcasted_iota(jnp.int32, lo.shape, 0) < nv_ref[i]
            x = jnp.concatenate([jnp.where(live, lo, 0.0), jnp.where(live, hi, 0.0)], axis=1).astype(bf16)
            return (jnp.dot(x, wg_ref[0], preferred_element_type=f32),
                    jnp.dot(x, wu_ref[0], preferred_element_type=f32))

        def down(j, gu):
            g, u = gu
            a = g * jax.nn.sigmoid(g) * u
            y = jnp.dot(a.astype(bf16), wd_ref[0], preferred_element_type=f32)
            r0 = j * MOE_SUB_ROWS
            for c, chunk in enumerate(_pack_rows(y)):
                y_ref[c, r0:r0 + MOE_SUB_ROWS, :] = chunk

        gu_prev = up(0)
        for j in range(1, n_sub):
            gu_next = up(j)
            down(j - 1, gu_prev)
            gu_prev = gu_next
        down(n_sub - 1, gu_prev)

    @pl.when(i >= nu_ref[0])
    def _():
        y_ref[...] = jnp.zeros_like(y_ref)


def _experts(xs, blk_e, n_used, n_valid, prm):
    n_rows = xs.shape[1]
    bm = MOE_ROWS
    rows = lambda i, be, nu, nv: (0, jnp.minimum(i, nu[0] - 1), 0)
    return pl.pallas_call(
        _experts_kernel,
        grid_spec=pltpu.PrefetchScalarGridSpec(
            num_scalar_prefetch=3,
            grid=(n_rows // bm,),
            in_specs=[
                pl.BlockSpec((ROW_CHUNKS, bm, LANES), rows),
                pl.BlockSpec((1, D_MODEL, D_EXPERT), lambda i, be, nu, nv: (be[i], 0, 0)),
                pl.BlockSpec((1, D_MODEL, D_EXPERT), lambda i, be, nu, nv: (be[i], 0, 0)),
                pl.BlockSpec((1, D_EXPERT, D_MODEL), lambda i, be, nu, nv: (be[i], 0, 0)),
            ],
            out_specs=pl.BlockSpec((ROW_CHUNKS, bm, LANES), lambda i, be, nu, nv: (0, i, 0)),
        ),
        out_shape=jax.ShapeDtypeStruct((ROW_CHUNKS, n_rows, LANES), jnp.uint32),
        compiler_params=_cparams(("arbitrary",)),
        name="experts",
    )(blk_e, n_used, n_valid, xs, prm["expert_gate"], prm["expert_up"], prm["expert_down"])


def _sc_mesh():
    return plsc.VectorSubcoreMesh(core_axis_name="c", subcore_axis_name="s")


def _sc_dispatch(u2p, dest, n_rows):
    T = u2p.shape[1]
    per_worker = T // SC_WORKERS
    n_it = per_worker // SC_WINDOW

    @functools.partial(
        pl.kernel, mesh=_sc_mesh(),
        out_type=jax.ShapeDtypeStruct((ROW_CHUNKS, n_rows, LANES), jnp.uint32),
        scratch_types=[pltpu.VMEM((TOP_K, SC_WINDOW), jnp.int32),
                       pltpu.VMEM((ROW_CHUNKS, SC_WINDOW, LANES), jnp.uint32),
                       pltpu.SemaphoreType.DMA, pltpu.SemaphoreType.DMA],
    )
    def k(u_hbm, dest_hbm, xs_hbm, idx_v, rows_v, sem_in, sem_out):
        wid = lax.axis_index("s") * SC_CORES + lax.axis_index("c")

        @pl.loop(0, n_it)
        def _(it):
            t0 = wid * per_worker + it * SC_WINDOW
            pltpu.sync_copy(dest_hbm.at[wid * n_it + it], idx_v)
            loads = [pltpu.async_copy(u_hbm.at[c].at[pl.ds(t0, SC_WINDOW)], rows_v.at[c], sem_in)
                     for c in range(ROW_CHUNKS)]
            for cp in loads:
                cp.wait()
            stores = [pltpu.async_copy(rows_v.at[c], xs_hbm.at[c].at[idx_v.at[kk]], sem_out)
                      for kk in range(TOP_K) for c in range(ROW_CHUNKS)]
            for cp in stores:
                cp.wait()

    return k(u2p, dest)


def _sc_combine(ys, dest):
    n_windows = dest.shape[0]
    T = n_windows * SC_WINDOW
    per_worker = T // SC_WORKERS
    n_it = per_worker // SC_WINDOW

    @functools.partial(
        pl.kernel, mesh=_sc_mesh(),
        out_type=jax.ShapeDtypeStruct((TOP_K, ROW_CHUNKS, T, LANES), jnp.uint32),
        scratch_types=[pltpu.VMEM((TOP_K, SC_WINDOW), jnp.int32),
                       pltpu.VMEM((SC_SLABS, SC_WINDOW, LANES), jnp.uint32),
                       pltpu.SemaphoreType.DMA, pltpu.SemaphoreType.DMA],
    )
    def k(ys_hbm, dest_hbm, yg_hbm, idx_v, rows_v, sem_in, sem_out):
        wid = lax.axis_index("s") * SC_CORES + lax.axis_index("c")

        @pl.loop(0, n_it)
        def _(it):
            t0 = wid * per_worker + it * SC_WINDOW
            pltpu.sync_copy(dest_hbm.at[wid * n_it + it], idx_v)
            for c in range(ROW_CHUNKS):
                for k0 in range(0, TOP_K, SC_SLABS):
                    loads = [pltpu.async_copy(ys_hbm.at[c].at[idx_v.at[k0 + j]], rows_v.at[j], sem_in)
                             for j in range(SC_SLABS)]
                    for cp in loads:
                        cp.wait()
                    stores = [pltpu.async_copy(rows_v.at[j], yg_hbm.at[k0 + j].at[c].at[pl.ds(t0, SC_WINDOW)], sem_out)
                              for j in range(SC_SLABS)]
                    for cp in stores:
                        cp.wait()

    return k(ys, dest)


def _final_kernel(h_ref, u_ref, yg_ref, wt_ref, p_ref, sgu_ref, sd_ref, pg_ref, pp_ref, o_ref):
    wl = wt_ref[...]
    wt = jnp.concatenate([wl, jnp.zeros((LANES - TOP_K, wl.shape[1]), f32)], axis=0).T
    r_lo = r_hi = None
    for k in range(TOP_K):
        lo, hi = _unpack_rows([yg_ref[k, c] for c in range(ROW_CHUNKS)])
        wk = wt[:, k:k + 1]
        r_lo = lo * wk if k == 0 else r_lo + lo * wk
        r_hi = hi * wk if k == 0 else r_hi + hi * wk
    routed = jnp.concatenate([r_lo, r_hi], axis=1)
    u_lo, u_hi = _unpack_rows([u_ref[c] for c in range(ROW_CHUNKS)])
    un = jnp.concatenate([u_lo, u_hi], axis=1).astype(bf16)
    gu = jnp.dot(un, sgu_ref[...], preferred_element_type=f32)
    g, u = gu[:, :D_SHARED], gu[:, D_SHARED:]
    shared = jnp.dot((g * jax.nn.sigmoid(g) * u).astype(bf16), sd_ref[...], preferred_element_type=f32)
    h = h_ref[...] + (routed + shared)
    gate = jax.nn.sigmoid(jnp.dot(h.astype(bf16), pg_ref[...], preferred_element_type=f32))
    emb = jnp.dot(p_ref[...].astype(bf16), pp_ref[...], preferred_element_type=f32)
    o_ref[...] = h + gate * emb


def _final(h1, u2p, yg, wts, p2, prm):
    T = h1.shape[0]
    tm = TOKEN_TILE
    row = lambda i: (i, 0)
    const = lambda i: (0, 0)
    full = lambda a: pl.BlockSpec(a.shape, const)
    return pl.pallas_call(
        _final_kernel,
        grid=(T // tm,),
        in_specs=[pl.BlockSpec((tm, D_MODEL), row),
                  pl.BlockSpec((ROW_CHUNKS, tm, LANES), lambda i: (0, i, 0)),
                  pl.BlockSpec((TOP_K, ROW_CHUNKS, tm, LANES), lambda i: (0, 0, i, 0)),
                  pl.BlockSpec((TOP_K, tm), lambda i: (0, i)),
                  pl.BlockSpec((tm, PLE_DIM), row),
                  full(prm["shared_gu"]), full(prm["shared_down"]), full(prm["ple_gate"]), full(prm["ple_proj"])],
        out_specs=pl.BlockSpec((tm, D_MODEL), row),
        out_shape=jax.ShapeDtypeStruct((T, D_MODEL), f32),
        compiler_params=_cparams(("parallel",)),
        name="final",
    )(h1, u2p, yg, wts, p2, prm["shared_gu"], prm["shared_down"], prm["ple_gate"], prm["ple_proj"])


def _deinterleave(n_heads):
    base = jnp.concatenate([jnp.arange(0, HEAD_DIM, 2), jnp.arange(1, HEAD_DIM, 2)])
    return (jnp.arange(n_heads)[:, None] * HEAD_DIM + base[None, :]).reshape(-1)


def _rope_tables(ang):
    c, s = jnp.cos(ang), jnp.sin(ang)
    return jnp.tile(jnp.concatenate([c, c], axis=-1), (1, 2)), jnp.tile(jnp.concatenate([-s, s], axis=-1), (1, 2))


def _axial_angles(S):
    rows = S // GRID_W
    row = jnp.repeat(jnp.arange(rows, dtype=f32), GRID_W)
    col = jnp.tile(jnp.arange(GRID_W, dtype=f32), rows)
    half = HEAD_DIM // 2
    inv = ROPE_THETA ** (-jnp.arange(0, half, 2, dtype=f32) / half)
    return jnp.concatenate([row[:, None] * inv, col[:, None] * inv], axis=-1)


def _linear_angles(S):
    t = jnp.arange(S, dtype=f32)
    inv = ROPE_THETA ** (-jnp.arange(0, HEAD_DIM, 2, dtype=f32) / HEAD_DIM)
    return t[:, None] * inv


def _prepare(seq_lens, norm_mix, w_in, q_norm_a, k_norm_a, q_norm_b, k_norm_b, w_branch_a, w_branch_b, w_out,
             norm_ffn, router_w, router_bias, expert_gate, expert_up, expert_down, shared_gate, shared_up,
             shared_down, ple_proj, ple_gate):
    cuts = [A_WIDTH, A_WIDTH + A_KV_WIDTH, A_WIDTH + 2 * A_KV_WIDTH,
            A_WIDTH + 2 * A_KV_WIDTH + B_WIDTH, A_WIDTH + 2 * A_KV_WIDTH + 2 * B_WIDTH,
            A_WIDTH + 2 * A_KV_WIDTH + 3 * B_WIDTH, A_WIDTH + 2 * A_KV_WIDTH + 3 * B_WIDTH + D_MODEL]
    wqa, wka, wva, wqb, wkb, wvb, wga, wgb = jnp.split(w_in, cuts, axis=-1)
    wqa = wqa[:, _deinterleave(A_Q_HEADS)]
    wka = wka[:, _deinterleave(A_KV_HEADS)]
    wqb = wqb[:, _deinterleave(B_HEADS)]
    wkb = wkb[:, _deinterleave(B_HEADS)]
    k0, k1 = wka[:, :HEAD_DIM], wka[:, HEAD_DIM:]
    v0, v1 = wva[:, :HEAD_DIM], wva[:, HEAD_DIM:]
    vz = jnp.zeros_like(v0)
    w_all = jnp.concatenate([wqa, k0, k0, k1, k1, v0, vz, v1, vz, wqb, wkb, wvb, wga, wgb], axis=-1).astype(bf16)
    perm = _deinterleave(1)
    scale = HEAD_DIM ** -0.5
    tile = lambda g, n: jnp.tile(g[perm], n)[None, :]
    router_pad = jnp.pad(router_w, ((0, 0), (0, LANES - N_EXPERTS)))
    router_hi = router_pad.astype(bf16)
    prm = {
        "norm_mix": norm_mix[None, :], "w_in": w_all,
        "gqa": tile(q_norm_a, A_Q_HEADS) * (scale * LOG2_E), "gka": tile(k_norm_a, KA_COLS // HEAD_DIM),
        "gqb": tile(q_norm_b, B_HEADS) * (scale * LOG2_E), "gkb": tile(k_norm_b, B_HEADS),
        "bd": jnp.kron(jnp.eye(NORM_COLS // HEAD_DIM, dtype=f32), jnp.ones((HEAD_DIM, HEAD_DIM), f32)).astype(bf16),
        "w_branch_a": w_branch_a.astype(bf16), "w_branch_b": w_branch_b.astype(bf16), "w_out": w_out.astype(bf16),
        "norm_ffn": norm_ffn[None, :],
        "router_hi": router_hi, "router_lo": (router_pad - router_hi.astype(f32)).astype(bf16),
        "router_bias": router_bias[:, None],
        "tri": (jnp.arange(MERGE_SUB)[:, None] < jnp.arange(MERGE_SUB)[None, :]).astype(bf16),
        "expert_gate": expert_gate.astype(bf16), "expert_up": expert_up.astype(bf16),
        "expert_down": expert_down.astype(bf16),
        "shared_gu": jnp.concatenate([shared_gate, shared_up], axis=-1).astype(bf16),
        "shared_down": shared_down.astype(bf16),
        "ple_gate": ple_gate.astype(bf16), "ple_proj": ple_proj.astype(bf16),
        "cos_a": {}, "sin_a": {}, "cos_b": {}, "sin_b": {},
    }
    for S in seq_lens:
        prm["cos_a"][S], prm["sin_a"][S] = _rope_tables(_axial_angles(S))
        prm["cos_b"][S], prm["sin_b"][S] = _rope_tables(_linear_angles(S))
    return prm


def _dispatch_plan(idx, rank, counts, T):
    bm = MOE_ROWS
    n_rows = (T * TOP_K + N_EXPERTS * bm) // bm * bm
    n_blocks = n_rows // bm
    padded = (counts + bm - 1) // bm * bm
    pad_end = jnp.cumsum(padded)
    pad_start = pad_end - padded
    dest = pad_start[idx] + rank
    blk_start = jnp.arange(n_blocks, dtype=jnp.int32) * bm
    blk_e = jnp.minimum(jnp.sum(pad_end[None, :] <= blk_start[:, None], axis=1), N_EXPERTS - 1).astype(jnp.int32)
    n_used = (pad_end[-1:] // bm).astype(jnp.int32)
    own = blk_e[:, None] == jnp.arange(N_EXPERTS, dtype=jnp.int32)[None, :]
    real_end = jnp.sum(jnp.where(own, (pad_start + counts)[None, :], 0), axis=1)
    n_valid = jnp.clip(real_end - blk_start, 0, bm).astype(jnp.int32)
    return dest, n_rows, blk_e, n_used, n_valid


def _layer(x, p, prm):
    B, S, _ = x.shape
    T = B * S
    x2 = x.reshape(T, D_MODEL)
    qa, ka, va, *qkv_b, sga, sgb = _in_proj(x2, S, prm)
    shape3 = lambda a: a.reshape(B, S, a.shape[-1])
    ya = _flash(shape3(qa), shape3(ka), shape3(va)).reshape(T, A_WIDTH)
    n_dil = len(WINDOW_DILATIONS)
    outs, lses = [], []
    for j, (_, dil) in enumerate(WINDOW_DILATIONS):
        o_d, lse_d = _dilated(qkv_b[j], qkv_b[n_dil + j], qkv_b[2 * n_dil + j], B, S, dil)
        outs.append(o_d)
        lses.append(lse_d)
    h1, u2p, idx, wts, rank, cnt = _merge_route(x2, ya, outs, lses, sga, sgb, prm)
    dest, n_rows, blk_e, n_used, n_valid = _dispatch_plan(idx, rank, cnt[:, 0], T)
    xs = _sc_dispatch(u2p, dest, n_rows)
    ys = _experts(xs, blk_e, n_used, n_valid, prm)
    yg = _sc_combine(ys, dest)
    out = _final(h1, u2p, yg, wts, p.reshape(T, PLE_DIM), prm)
    return out.reshape(B, S, D_MODEL)


def kernel(x_prompt, x_sample, p_prompt, p_sample, norm_mix, w_in, q_norm_a, k_norm_a, q_norm_b, k_norm_b,
           w_branch_a, w_branch_b, w_out, norm_ffn, router_w, router_bias, expert_gate, expert_up, expert_down,
           shared_gate, shared_up, shared_down, ple_proj, ple_gate):
    weights = (norm_mix, w_in, q_norm_a, k_norm_a, q_norm_b, k_norm_b, w_branch_a, w_branch_b, w_out, norm_ffn,
               router_w, router_bias, expert_gate, expert_up, expert_down, shared_gate, shared_up, shared_down,
               ple_proj, ple_gate)
    depth = norm_mix.shape[0]
    seq_lens = sorted({x_prompt.shape[1], x_sample.shape[1]})
    hp, hs = x_prompt, x_sample
    for i in range(depth):
        prm = _prepare(seq_lens, *[w[i] for w in weights])
        hp = _layer(hp, p_prompt[i], prm)
        hs = _layer(hs, p_sample[i], prm)
    return hp, hs
```

```python
import functools

import jax
import jax.numpy as jnp
from jax import lax
from jax.experimental import pallas as pl
from jax.experimental.pallas import tpu as pltpu
from jax.experimental.pallas import tpu_sc as plsc

D_MODEL = 1024
HEAD_DIM = 64
A_Q_HEADS = 8
A_KV_HEADS = 2
B_HEADS = 6
GRID_W = 64
ROPE_THETA = 10000.0
WINDOW_DILATIONS = ((128, 1), (512, 4), (2048, 16))
N_EXPERTS = 64
TOP_K = 8
N_GROUPS = 8
TOPK_GROUPS = 4
D_EXPERT = 256
D_SHARED = 256
ROUTE_SCALE = 2.5
PLE_DIM = 256
NEG_BIG = -1e30
EPS = 1e-6
LOG2_E = 1.4426950408889634

A_WIDTH = A_Q_HEADS * HEAD_DIM
A_KV_WIDTH = A_KV_HEADS * HEAD_DIM
B_WIDTH = B_HEADS * HEAD_DIM
GROUP_Q = A_Q_HEADS // A_KV_HEADS
EXPERTS_PER_GROUP = N_EXPERTS // N_GROUPS

LANES = 128
NORM_COLS = 256
HALF_WINDOW = 64
KEY_SPAN = 256

KA_COLS = 2 * LANES
VA_COLS = 2 * LANES
OFF_QA = 0
OFF_KA = OFF_QA + A_WIDTH
OFF_VA = OFF_KA + KA_COLS
OFF_QB = OFF_VA + VA_COLS
OFF_KB = OFF_QB + B_WIDTH
OFF_VB = OFF_KB + B_WIDTH
OFF_GA = OFF_VB + B_WIDTH
OFF_GB = OFF_GA + D_MODEL
PROJ_COLS = OFF_GB + D_MODEL

TOKEN_TILE = 512
MERGE_SUB = 256
FLASH_TQ = 256
FLASH_TK = 1024
FLASH_ITEMS = 8
DIL_QBLOCK = 128
DIL_STEP_TOKENS = 2048
MOE_ROWS = 1024
MOE_SUB_ROWS = 256

V7X_VMEM_BYTES = 64 * 1024 * 1024
VMEM_LIMIT = V7X_VMEM_BYTES * 13 // 16

HALF_MODEL = D_MODEL // 2
ROW_CHUNKS = HALF_MODEL // LANES
SC_CORES = 2
SC_SUBCORES = 16
SC_WORKERS = SC_CORES * SC_SUBCORES
SC_WINDOW = 128
SC_PAIR = 2

bf16 = jnp.bfloat16
f32 = jnp.float32


def _cparams(sem):
    return pltpu.CompilerParams(dimension_semantics=sem, vmem_limit_bytes=VMEM_LIMIT)


def _pack_rows(v):
    lo = lax.bitcast_convert_type(v[:, :HALF_MODEL].astype(bf16).astype(f32), jnp.uint32) >> 16
    hi = lax.bitcast_convert_type(v[:, HALF_MODEL:].astype(bf16).astype(f32), jnp.uint32) & jnp.uint32(0xFFFF0000)
    w = lo | hi
    return [w[:, c * LANES:(c + 1) * LANES] for c in range(ROW_CHUNKS)]


def _unpack_rows(chunks):
    w = jnp.concatenate(chunks, axis=1)
    lo = lax.bitcast_convert_type(w << 16, f32)
    hi = lax.bitcast_convert_type(w & jnp.uint32(0xFFFF0000), f32)
    return lo, hi


def _in_proj_kernel(x_ref, nm_ref, w_ref, gqa_ref, gka_ref, gqb_ref, gkb_ref,
                    ca_ref, sa_ref, cb_ref, sb_ref, bd_ref,
                    qa_ref, ka_ref, va_ref, qb1_ref, qb4_ref, qb16_ref, kb1_ref, kb4_ref, kb16_ref,
                    vb1_ref, vb4_ref, vb16_ref, ga_ref, gb_ref, st_ref):
    x = x_ref[...]
    r = lax.rsqrt(jnp.mean(x * x, axis=-1, keepdims=True) + EPS)
    u = (x * r * nm_ref[...]).astype(bf16)
    rows = x.shape[0]
    lane = lax.broadcasted_iota(jnp.int32, (rows, LANES), 1)
    first_half = (lane % HEAD_DIM) < (HEAD_DIM // 2)

    def proj(off, width):
        return jnp.dot(u, w_ref[:, off:off + width], preferred_element_type=f32)

    def norm_rope(off, width, g_ref, c_ref, s_ref):
        z = proj(off, width)
        c = c_ref[...]
        s = s_ref[...]
        chunks = []
        for c0 in range(0, width, NORM_COLS):
            cw = min(NORM_COLS, width - c0)
            zc = z[:, c0:c0 + cw]
            ms = jnp.dot((zc * zc).astype(bf16), bd_ref[:cw, :cw], preferred_element_type=f32) * (1.0 / HEAD_DIM)
            zn = zc * lax.rsqrt(ms + EPS) * g_ref[:, c0:c0 + cw]
            for h0 in range(0, cw, LANES):
                zh = zn[:, h0:h0 + LANES]
                sw = jnp.where(first_half, pltpu.roll(zh, LANES - HEAD_DIM // 2, 1), pltpu.roll(zh, HEAD_DIM // 2, 1))
                chunks.append(zh * c + sw * s)
        return chunks

    def store(chunks, out_ref):
        for j, ch in enumerate(chunks):
            out_ref[:, j * LANES:(j + 1) * LANES] = ch.astype(out_ref.dtype)

    def store_dilated(chunks, out_refs):
        for j, ch in enumerate(chunks):
            st_ref[j] = ch
        for (_, dil), out_ref in zip(WINDOW_DILATIONS, out_refs):
            for r in range(dil):
                for j in range(len(chunks)):
                    src = pl.ds(r, rows // dil, stride=dil) if dil > 1 else pl.ds(0, rows)
                    out_ref[r, :, j * LANES:(j + 1) * LANES] = st_ref[j, src, :].astype(out_ref.dtype)

    store(norm_rope(OFF_QA, A_WIDTH, gqa_ref, ca_ref, sa_ref), qa_ref)
    store(norm_rope(OFF_KA, KA_COLS, gka_ref, ca_ref, sa_ref), ka_ref)
    va = proj(OFF_VA, VA_COLS)
    va_ref[...] = jnp.concatenate(
        [jnp.where(lane < HEAD_DIM, va[:, c0:c0 + LANES], 1.0) for c0 in range(0, VA_COLS, LANES)],
        axis=1).astype(va_ref.dtype)
    store_dilated(norm_rope(OFF_QB, B_WIDTH, gqb_ref, cb_ref, sb_ref), (qb1_ref, qb4_ref, qb16_ref))
    store_dilated(norm_rope(OFF_KB, B_WIDTH, gkb_ref, cb_ref, sb_ref), (kb1_ref, kb4_ref, kb16_ref))
    vb = proj(OFF_VB, B_WIDTH)
    store_dilated([vb[:, c0:c0 + LANES] for c0 in range(0, B_WIDTH, LANES)], (vb1_ref, vb4_ref, vb16_ref))
    ga_ref[...] = jax.nn.sigmoid(proj(OFF_GA, D_MODEL)).astype(ga_ref.dtype)
    gb_ref[...] = jax.nn.sigmoid(proj(OFF_GB, D_MODEL)).astype(gb_ref.dtype)


def _in_proj(x2, S, prm):
    T = x2.shape[0]
    tm = TOKEN_TILE
    n_pos = S // tm
    row = lambda i: (i, 0)
    pos = lambda i: (i % n_pos, 0)
    const = lambda i: (0, 0)
    dils = [dil for _, dil in WINDOW_DILATIONS]
    flat = lambda w: (pl.BlockSpec((tm, w), row), jax.ShapeDtypeStruct((T, w), bf16))
    strided = lambda d: (pl.BlockSpec((d, tm // d, B_WIDTH), lambda i: (0, i, 0)),
                         jax.ShapeDtypeStruct((d, T // d, B_WIDTH), bf16))
    outs = ([flat(A_WIDTH), flat(KA_COLS), flat(VA_COLS)] + 3 * [strided(d) for d in dils]
            + [flat(D_MODEL), flat(D_MODEL)])
    return pl.pallas_call(
        _in_proj_kernel,
        grid=(T // tm,),
        in_specs=[
            pl.BlockSpec((tm, D_MODEL), row),
            pl.BlockSpec((1, D_MODEL), const),
            pl.BlockSpec((D_MODEL, PROJ_COLS), const),
            pl.BlockSpec((1, A_WIDTH), const),
            pl.BlockSpec((1, KA_COLS), const),
            pl.BlockSpec((1, B_WIDTH), const),
            pl.BlockSpec((1, B_WIDTH), const),
            pl.BlockSpec((tm, LANES), pos),
            pl.BlockSpec((tm, LANES), pos),
            pl.BlockSpec((tm, LANES), pos),
            pl.BlockSpec((tm, LANES), pos),
            pl.BlockSpec((NORM_COLS, NORM_COLS), const),
        ],
        out_specs=[spec for spec, _ in outs],
        out_shape=[shape for _, shape in outs],
        scratch_shapes=[pltpu.VMEM((B_WIDTH // LANES, tm, LANES), f32)],
        compiler_params=_cparams(("parallel",)),
        name="in_proj",
    )(x2, prm["norm_mix"], prm["w_in"], prm["gqa"], prm["gka"], prm["gqb"], prm["gkb"],
      prm["cos_a"][S], prm["sin_a"][S], prm["cos_b"][S], prm["sin_b"][S], prm["bd"])


def _flash_kernel(q_ref, k_ref, v_ref, o_ref, s_ref, *, tq, tk, seq, nq):
    lane = lax.broadcasted_iota(jnp.int32, (tq, LANES), 1)
    lo = lane < HEAD_DIM
    n_chunks = seq // tk

    def masked_heads(b):
        q = q_ref[0, b * tq:(b + 1) * tq, :]
        q01, q23 = q[:, :LANES], q[:, LANES:]
        zero = jnp.zeros_like(q01)
        return [jnp.where(lo, q01, zero), jnp.where(lo, zero, q01), jnp.where(lo, q23, zero), jnp.where(lo, zero, q23)]

    def score(qh, chunk):
        kk = k_ref[0, chunk * tk:(chunk + 1) * tk, :]
        return lax.dot_general(qh, kk, (((1,), (1,)), ((), ())), preferred_element_type=f32)

    qs = [masked_heads(b) for b in range(nq)]
    items = [(b, c) for b in range(nq) for c in range(n_chunks)]
    for h in range(GROUP_Q):
        s_ref[0, h] = score(qs[0][h], 0)
    ms = accs = None
    for n, (b, chunk) in enumerate(items):
        slot = n % 2
        nxt = items[n + 1] if n + 1 < len(items) else None
        if chunk == 0:
            ms = [jnp.full((tq, 1), NEG_BIG, f32) for _ in range(GROUP_Q)]
            accs = [jnp.zeros((tq, LANES), f32) for _ in range(GROUP_Q)]
        vv = v_ref[0, chunk * tk:(chunk + 1) * tk, :]
        for h in range(GROUP_Q):
            if nxt is not None:
                s_ref[1 - slot, h] = score(qs[nxt[0]][h], nxt[1])
            s = s_ref[slot, h]
            m_new = jnp.maximum(ms[h], jnp.max(s, axis=-1, keepdims=True))
            p = jnp.exp2((s - m_new).astype(bf16))
            alpha = jnp.exp2(ms[h] - m_new)
            accs[h] = accs[h] * alpha + jnp.dot(p, vv, preferred_element_type=f32)
            ms[h] = m_new
        if chunk == n_chunks - 1:
            heads = [acc / jnp.where(lo, pltpu.roll(acc, HEAD_DIM, 1), 1.0) for acc in accs]
            out01 = jnp.where(lo, heads[0], pltpu.roll(heads[1], HEAD_DIM, 1))
            out23 = jnp.where(lo, heads[2], pltpu.roll(heads[3], HEAD_DIM, 1))
            o_ref[0, b * tq:(b + 1) * tq, :] = jnp.concatenate([out01, out23], axis=1).astype(o_ref.dtype)


def _flash(qa, ka, va):
    B, S, _ = qa.shape
    tq, tk = FLASH_TQ, FLASH_TK
    nq = max(1, FLASH_ITEMS // (S // tk))
    return pl.pallas_call(
        functools.partial(_flash_kernel, tq=tq, tk=tk, seq=S, nq=nq),
        grid=(B, A_KV_HEADS, S // (nq * tq)),
        in_specs=[
            pl.BlockSpec((1, nq * tq, GROUP_Q * HEAD_DIM), lambda b, g, i: (b, i, g)),
            pl.BlockSpec((1, S, LANES), lambda b, g, i: (b, 0, g)),
            pl.BlockSpec((1, S, LANES), lambda b, g, i: (b, 0, g)),
        ],
        out_specs=pl.BlockSpec((1, nq * tq, GROUP_Q * HEAD_DIM), lambda b, g, i: (b, i, g)),
        out_shape=jax.ShapeDtypeStruct((B, S, A_WIDTH), bf16),
        scratch_shapes=[pltpu.VMEM((2, GROUP_Q, tq, tk), f32)],
        compiler_params=_cparams(("parallel", "parallel", "arbitrary")),
        name="flash_gqa",
    )(qa, ka, va)


def _dilated_kernel(q_ref, k_ref, v_ref, o_ref, lse_ref, so_ref, sl_ref, *, length, tms, dil):
    i = pl.program_id(1)
    lane = lax.broadcasted_iota(jnp.int32, (DIL_QBLOCK, LANES), 1)
    lo = lane < HEAD_DIM
    n_pairs = B_WIDTH // LANES
    ones = jnp.ones((KEY_SPAN, LANES), bf16)

    def unit(u, carry):
        sb, r = u // dil, u % dil
        r0 = pl.multiple_of(sb * DIL_QBLOCK, DIL_QBLOCK)
        m0 = i * tms + r0
        start = pl.multiple_of(jnp.clip(m0 - HALF_WINDOW, 0, length - KEY_SPAN), HALF_WINDOW)
        qpos = m0 + lax.broadcasted_iota(jnp.int32, (DIL_QBLOCK, KEY_SPAN), 0)
        kpos = start + lax.broadcasted_iota(jnp.int32, (DIL_QBLOCK, KEY_SPAN), 1)
        valid = jnp.abs(kpos - qpos) <= HALF_WINDOW
        rows = pl.ds(r0 * dil + r, DIL_QBLOCK, stride=dil) if dil > 1 else pl.ds(r0, DIL_QBLOCK)
        scores = []
        for c in range(n_pairs):
            cs = slice(c * LANES, (c + 1) * LANES)
            qp = q_ref[r, 0, pl.ds(r0, DIL_QBLOCK), cs]
            kp = k_ref[r, 0, pl.ds(start, KEY_SPAN), cs]
            zero = jnp.zeros_like(qp)
            for half in range(2):
                qm = jnp.where(lo, qp, zero) if half == 0 else jnp.where(lo, zero, qp)
                scores.append(lax.dot_general(qm, kp, (((1,), (1,)), ((), ())), preferred_element_type=f32))
        for c in range(n_pairs):
            vp_ones = jnp.concatenate([v_ref[r, 0, pl.ds(start, KEY_SPAN), c * LANES:(c + 1) * LANES], ones], axis=1)
            ols, ms = [], []
            for half in range(2):
                s = jnp.where(valid, scores[2 * c + half], NEG_BIG)
                m = jnp.max(s, axis=-1, keepdims=True)
                e = jnp.exp2((s - m).astype(bf16))
                ols.append(jnp.dot(e, vp_ones, preferred_element_type=f32))
                ms.append(m)
            num = jnp.where(lo, ols[0][:, :LANES], ols[1][:, :LANES])
            den = jnp.where(lo, ols[0][:, LANES:], ols[1][:, LANES:])
            so_ref[c, rows, :] = num / den
            sl_ref[c, rows, :] = jnp.where(lo, ms[0], ms[1]) + jnp.log2(den)
        return carry

    lax.fori_loop(0, (tms // DIL_QBLOCK) * dil, unit, 0, unroll=8)
    o_ref[0] = jnp.concatenate([so_ref[c] for c in range(n_pairs)], axis=1).astype(o_ref.dtype)
    lse_ref[0] = jnp.concatenate([sl_ref[c] for c in range(n_pairs)], axis=1)


def _dilated(qb, kb, vb, B, S, dil):
    length = S // dil
    tms = DIL_STEP_TOKENS // dil
    view = lambda a: a.reshape(dil, B, length, B_WIDTH)
    o, lse = pl.pallas_call(
        functools.partial(_dilated_kernel, length=length, tms=tms, dil=dil),
        grid=(B, length // tms),
        in_specs=[
            pl.BlockSpec((dil, 1, tms, B_WIDTH), lambda b, i: (0, b, i, 0)),
            pl.BlockSpec((dil, 1, length, B_WIDTH), lambda b, i: (0, b, 0, 0)),
            pl.BlockSpec((dil, 1, length, B_WIDTH), lambda b, i: (0, b, 0, 0)),
        ],
        out_specs=[
            pl.BlockSpec((1, tms * dil, B_WIDTH), lambda b, i: (b, i, 0)),
            pl.BlockSpec((1, tms * dil, B_WIDTH), lambda b, i: (b, i, 0)),
        ],
        out_shape=[
            jax.ShapeDtypeStruct((B, S, B_WIDTH), bf16),
            jax.ShapeDtypeStruct((B, S, B_WIDTH), f32),
        ],
        scratch_shapes=[pltpu.VMEM((B_WIDTH // LANES, tms * dil, LANES), f32),
                        pltpu.VMEM((B_WIDTH // LANES, tms * dil, LANES), f32)],
        compiler_params=_cparams(("parallel", "arbitrary")),
        name=f"dilated_{dil}",
    )(view(qb), view(kb), view(vb))
    return o.reshape(B * S, B_WIDTH), lse.reshape(B * S, B_WIDTH)


def _first_index_of_max(cur, idx, n):
    mx = jnp.max(cur, axis=0, keepdims=True)
    return jnp.min(jnp.where(cur == mx, idx, n), axis=0, keepdims=True)


def _merge_route_kernel(x_ref, ya_ref, o1_ref, o2_ref, o3_ref, l1_ref, l2_ref, l3_ref, ga_ref, gb_ref,
                        wa_ref, wb_ref, wo_ref, nf_ref, rhi_ref, rlo_ref, bias_ref, tri_ref,
                        h_ref, u_ref, idx_ref, wt_ref, rank_ref, cnt_ref, run_ref):
    step = pl.program_id(0)

    @pl.when(step == 0)
    def _():
        run_ref[...] = jnp.zeros_like(run_ref)

    tm = x_ref.shape[0]
    sub = tri_ref.shape[0]

    def dense(j):
        rs = slice(j * sub, (j + 1) * sub)
        l1, l2, l3 = l1_ref[rs, :], l2_ref[rs, :], l3_ref[rs, :]
        lm = jnp.maximum(jnp.maximum(l1, l2), l3)
        e1, e2, e3 = jnp.exp2(l1 - lm), jnp.exp2(l2 - lm), jnp.exp2(l3 - lm)
        yb = ((e1 * o1_ref[rs, :].astype(f32) + e2 * o2_ref[rs, :].astype(f32) + e3 * o3_ref[rs, :].astype(f32))
              / (e1 + e2 + e3))
        pa = jnp.dot(ya_ref[rs, :], wa_ref[...], preferred_element_type=f32)
        pb = jnp.dot(yb.astype(bf16), wb_ref[...], preferred_element_type=f32)
        merged = ga_ref[rs, :].astype(f32) * pa + gb_ref[rs, :].astype(f32) * pb
        h = x_ref[rs, :] + jnp.dot(merged.astype(bf16), wo_ref[...], preferred_element_type=f32)
        h_ref[rs, :] = h
        r = lax.rsqrt(jnp.mean(h * h, axis=-1, keepdims=True) + EPS)
        uf = h * r * nf_ref[...]
        u_hi = uf.astype(bf16)
        for c, chunk in enumerate(_pack_rows(uf)):
            u_ref[c, rs, :] = chunk
        u_lo = (uf - u_hi.astype(f32)).astype(bf16)
        return (jnp.dot(u_hi, rhi_ref[...], preferred_element_type=f32)
                + jnp.dot(u_lo, rhi_ref[...], preferred_element_type=f32)
                + jnp.dot(u_hi, rlo_ref[...], preferred_element_type=f32))

    def route(j, logits):
        scores = jax.nn.sigmoid(logits.T[:N_EXPERTS, :])
        sel = scores + bias_ref[...]
        i8 = lax.broadcasted_iota(jnp.int32, (EXPERTS_PER_GROUP, sub), 0)
        neg_inf = jnp.float32(-jnp.inf)
        gscore = jnp.zeros((N_GROUPS, sub), f32)
        for g in range(N_GROUPS):
            blk = sel[g * EXPERTS_PER_GROUP:(g + 1) * EXPERTS_PER_GROUP, :]
            top1 = jnp.max(blk, axis=0, keepdims=True)
            first = _first_index_of_max(blk, i8, EXPERTS_PER_GROUP)
            top2 = jnp.max(jnp.where(i8 == first, neg_inf, blk), axis=0, keepdims=True)
            gscore = jnp.where(i8 == g, top1 + top2, gscore)
        gkeep = jnp.zeros((N_GROUPS, sub), jnp.bool_)
        cur = gscore
        for _ in range(TOPK_GROUPS):
            pick = i8 == _first_index_of_max(cur, i8, N_GROUPS)
            gkeep = jnp.logical_or(gkeep, pick)
            cur = jnp.where(pick, neg_inf, cur)
        cur = jnp.concatenate(
            [jnp.where(gkeep[g:g + 1, :], sel[g * EXPERTS_PER_GROUP:(g + 1) * EXPERTS_PER_GROUP, :], NEG_BIG)
             for g in range(N_GROUPS)], axis=0)

        ie = lax.broadcasted_iota(jnp.int32, (N_EXPERTS, sub), 0)
        ik = lax.broadcasted_iota(jnp.int32, (TOP_K, sub), 0)
        idx_out = jnp.zeros((TOP_K, sub), jnp.int32)
        w_out = jnp.zeros((TOP_K, sub), f32)
        chosen = jnp.zeros((N_EXPERTS, sub), jnp.bool_)
        for k in range(TOP_K):
            fi = _first_index_of_max(cur, ie, N_EXPERTS)
            pick = ie == fi
            wk = jnp.sum(jnp.where(pick, scores, 0.0), axis=0, keepdims=True)
            idx_out = jnp.where(ik == k, fi, idx_out)
            w_out = jnp.where(ik == k, wk, w_out)
            chosen = jnp.logical_or(chosen, pick)
            cur = jnp.where(pick, neg_inf, cur)
        w_out = w_out / jnp.sum(w_out, axis=0, keepdims=True) * ROUTE_SCALE
        wt_ref[:, j * sub:(j + 1) * sub] = w_out

        sel01 = jnp.where(chosen, 1.0, 0.0)
        before = jnp.dot(sel01.astype(bf16), tri_ref[...], preferred_element_type=f32) + run_ref[:, 0:1]
        rank_out = jnp.zeros((TOP_K, sub), f32)
        for k in range(TOP_K):
            rk = jnp.sum(jnp.where(ie == idx_out[k:k + 1, :], before, 0.0), axis=0, keepdims=True)
            rank_out = jnp.where(ik == k, rk, rank_out)
        rank_i = rank_out.astype(jnp.int32)
        for w in range(sub // SC_WINDOW):
            win = j * (sub // SC_WINDOW) + w
            idx_ref[win] = idx_out[:, w * SC_WINDOW:(w + 1) * SC_WINDOW]
            rank_ref[win] = rank_i[:, w * SC_WINDOW:(w + 1) * SC_WINDOW]
        run_ref[...] = run_ref[...] + jnp.sum(sel01, axis=1, keepdims=True)

    logits = [dense(j) for j in range(tm // sub)]
    for j in range(tm // sub):
        route(j, logits[j])
    cnt_ref[...] = run_ref[...].astype(jnp.int32)


def _merge_route(x2, ya, outs, lses, sga, sgb, prm):
    T = x2.shape[0]
    tm = TOKEN_TILE
    row = lambda i: (i, 0)
    col = lambda i: (0, i)
    const = lambda i: (0, 0)
    tok = lambda w: pl.BlockSpec((tm, w), row)
    full = lambda a: pl.BlockSpec(a.shape, const)
    win = pl.BlockSpec((tm // SC_WINDOW, TOP_K, SC_WINDOW), lambda i: (i, 0, 0))
    return pl.pallas_call(
        _merge_route_kernel,
        grid=(T // tm,),
        in_specs=[tok(D_MODEL), tok(A_WIDTH), tok(B_WIDTH), tok(B_WIDTH), tok(B_WIDTH),
                  tok(B_WIDTH), tok(B_WIDTH), tok(B_WIDTH), tok(D_MODEL), tok(D_MODEL),
                  full(prm["w_branch_a"]), full(prm["w_branch_b"]), full(prm["w_out"]), full(prm["norm_ffn"]),
                  full(prm["router_hi"]), full(prm["router_lo"]), full(prm["router_bias"]), full(prm["tri"])],
        out_specs=[tok(D_MODEL), pl.BlockSpec((ROW_CHUNKS, tm, LANES), lambda i: (0, i, 0)),
                   win, pl.BlockSpec((TOP_K, tm), col), win,
                   pl.BlockSpec((N_EXPERTS, LANES), const)],
        out_shape=[jax.ShapeDtypeStruct((T, D_MODEL), f32), jax.ShapeDtypeStruct((ROW_CHUNKS, T, LANES), jnp.uint32),
                   jax.ShapeDtypeStruct((T // SC_WINDOW, TOP_K, SC_WINDOW), jnp.int32),
                   jax.ShapeDtypeStruct((TOP_K, T), f32),
                   jax.ShapeDtypeStruct((T // SC_WINDOW, TOP_K, SC_WINDOW), jnp.int32),
                   jax.ShapeDtypeStruct((N_EXPERTS, LANES), jnp.int32)],
        scratch_shapes=[pltpu.VMEM((N_EXPERTS, LANES), f32)],
        compiler_params=_cparams(("arbitrary",)),
        name="merge_route",
    )(x2, ya, *outs, *lses, sga, sgb,
      prm["w_branch_a"], prm["w_branch_b"], prm["w_out"], prm["norm_ffn"],
      prm["router_hi"], prm["router_lo"], prm["router_bias"], prm["tri"])


def _experts_kernel(be_ref, nu_ref, nv_ref, x_ref, wg_ref, wu_ref, wd_ref, y_ref):
    del be_ref
    i = pl.program_id(0)

    @pl.when(i < nu_ref[0])
    def _():
        n_sub = x_ref.shape[1] // MOE_SUB_ROWS

        def up(j):
            r0 = j * MOE_SUB_ROWS
            lo, hi = _unpack_rows([x_ref[c, r0:r0 + MOE_SUB_ROWS, :] for c in range(ROW_CHUNKS)])
            live = r0 + lax.broadcasted_iota(jnp.int32, lo.shape, 0) < nv_ref[i]
            x = jnp.concatenate([jnp.where(live, lo, 0.0), jnp.where(live, hi, 0.0)], axis=1).astype(bf16)
            return (jnp.dot(x, wg_ref[0], preferred_element_type=f32),
                    jnp.dot(x, wu_ref[0], preferred_element_type=f32))

        def down(j, gu):
            g, u = gu
            a = g * jax.nn.sigmoid(g) * u
            y = jnp.dot(a.astype(bf16), wd_ref[0], preferred_element_type=f32)
            r0 = j * MOE_SUB_ROWS
            for c, chunk in enumerate(_pack_rows(y)):
                y_ref[c, r0:r0 + MOE_SUB_ROWS, :] = chunk

        gu_prev = up(0)
        for j in range(1, n_sub):
            gu_next = up(j)
            down(j - 1, gu_prev)
            gu_prev = gu_next
        down(n_sub - 1, gu_prev)

    @pl.when(i >= nu_ref[0])
    def _():
        y_ref[...] = jnp.zeros_like(y_ref)


def _experts(xs, blk_e, n_used, n_valid, prm):
    n_rows = xs.shape[1]
    bm = MOE_ROWS
    rows = lambda i, be, nu, nv: (0, jnp.minimum(i, nu[0] - 1), 0)
    return pl.pallas_call(
        _experts_kernel,
        grid_spec=pltpu.PrefetchScalarGridSpec(
            num_scalar_prefetch=3,
            grid=(n_rows // bm,),
            in_specs=[
                pl.BlockSpec((ROW_CHUNKS, bm, LANES), rows),
                pl.BlockSpec((1, D_MODEL, D_EXPERT), lambda i, be, nu, nv: (be[i], 0, 0)),
                pl.BlockSpec((1, D_MODEL, D_EXPERT), lambda i, be, nu, nv: (be[i], 0, 0)),
                pl.BlockSpec((1, D_EXPERT, D_MODEL), lambda i, be, nu, nv: (be[i], 0, 0)),
            ],
            out_specs=pl.BlockSpec((ROW_CHUNKS, bm, LANES), lambda i, be, nu, nv: (0, i, 0)),
        ),
        out_shape=jax.ShapeDtypeStruct((ROW_CHUNKS, n_rows, LANES), jnp.uint32),
        compiler_params=_cparams(("arbitrary",)),
        name="experts",
    )(blk_e, n_used, n_valid, xs, prm["expert_gate"], prm["expert_up"], prm["expert_down"])


def _sc_mesh():
    return plsc.VectorSubcoreMesh(core_axis_name="c", subcore_axis_name="s")


def _sc_dispatch(u2p, dest, n_rows):
    T = u2p.shape[1]
    per_worker = T // SC_WORKERS
    n_it = per_worker // SC_WINDOW

    @functools.partial(
        pl.kernel, mesh=_sc_mesh(),
        out_type=jax.ShapeDtypeStruct((ROW_CHUNKS, n_rows, LANES), jnp.uint32),
        scratch_types=[pltpu.VMEM((TOP_K, SC_WINDOW), jnp.int32),
                       pltpu.VMEM((ROW_CHUNKS, SC_WINDOW, LANES), jnp.uint32),
                       pltpu.SemaphoreType.DMA, pltpu.SemaphoreType.DMA],
    )
    def k(u_hbm, dest_hbm, xs_hbm, idx_v, rows_v, sem_in, sem_out):
        wid = lax.axis_index("s") * SC_CORES + lax.axis_index("c")

        @pl.loop(0, n_it)
        def _(it):
            t0 = wid * per_worker + it * SC_WINDOW
            pltpu.sync_copy(dest_hbm.at[wid * n_it + it], idx_v)
            loads = [pltpu.async_copy(u_hbm.at[c].at[pl.ds(t0, SC_WINDOW)], rows_v.at[c], sem_in)
                     for c in range(ROW_CHUNKS)]
            for cp in loads:
                cp.wait()
            stores = [pltpu.async_copy(rows_v.at[c], xs_hbm.at[c].at[idx_v.at[kk]], sem_out)
                      for kk in range(TOP_K) for c in range(ROW_CHUNKS)]
            for cp in stores:
                cp.wait()

    return k(u2p, dest)


def _sc_combine(ys, dest):
    n_windows = dest.shape[0]
    T = n_windows * SC_WINDOW
    per_worker = T // SC_WORKERS
    n_it = per_worker // SC_WINDOW

    items = [(c, k0) for c in range(ROW_CHUNKS) for k0 in range(0, TOP_K, SC_PAIR)]

    @functools.partial(
        pl.kernel, mesh=_sc_mesh(),
        out_type=jax.ShapeDtypeStruct((TOP_K, ROW_CHUNKS, T, LANES), jnp.uint32),
        scratch_types=[pltpu.VMEM((TOP_K, SC_WINDOW), jnp.int32),
                       pltpu.VMEM((2, SC_PAIR, SC_WINDOW, LANES), jnp.uint32),
                       pltpu.SemaphoreType.DMA, pltpu.SemaphoreType.DMA,
                       pltpu.SemaphoreType.DMA, pltpu.SemaphoreType.DMA],
    )
    def k(ys_hbm, dest_hbm, yg_hbm, idx_v, rows_v, sem_in0, sem_in1, sem_out0, sem_out1):
        sem_in, sem_out = (sem_in0, sem_in1), (sem_out0, sem_out1)
        wid = lax.axis_index("s") * SC_CORES + lax.axis_index("c")

        @pl.loop(0, n_it)
        def _(it):
            t0 = wid * per_worker + it * SC_WINDOW
            pltpu.sync_copy(dest_hbm.at[wid * n_it + it], idx_v)

            def gathers(n):
                c, k0 = items[n]
                return [pltpu.async_copy(ys_hbm.at[c].at[idx_v.at[k0 + j]], rows_v.at[n % 2, j], sem_in[n % 2])
                        for j in range(SC_PAIR)]

            def stores(n):
                c, k0 = items[n]
                return [pltpu.async_copy(rows_v.at[n % 2, j], yg_hbm.at[k0 + j].at[c].at[pl.ds(t0, SC_WINDOW)],
                                         sem_out[n % 2]) for j in range(SC_PAIR)]

            pending_gathers = gathers(0)
            pending_stores = {}
            for n in range(len(items)):
                for cp in pending_gathers:
                    cp.wait()
                if n + 1 < len(items):
                    for cp in pending_stores.pop((n + 1) % 2, []):
                        cp.wait()
                    pending_gathers = gathers(n + 1)
                pending_stores[n % 2] = stores(n)
            for copies in pending_stores.values():
                for cp in copies:
                    cp.wait()

    return k(ys, dest)


def _final_kernel(h_ref, u_ref, yg_ref, wt_ref, p_ref, sgu_ref, sd_ref, pg_ref, pp_ref, o_ref):
    wl = wt_ref[...]
    wt = jnp.concatenate([wl, jnp.zeros((LANES - TOP_K, wl.shape[1]), f32)], axis=0).T
    r_lo = r_hi = None
    for k in range(TOP_K):
        lo, hi = _unpack_rows([yg_ref[k, c] for c in range(ROW_CHUNKS)])
        wk = wt[:, k:k + 1]
        r_lo = lo * wk if k == 0 else r_lo + lo * wk
        r_hi = hi * wk if k == 0 else r_hi + hi * wk
    routed = jnp.concatenate([r_lo, r_hi], axis=1)
    u_lo, u_hi = _unpack_rows([u_ref[c] for c in range(ROW_CHUNKS)])
    un = jnp.concatenate([u_lo, u_hi], axis=1).astype(bf16)
    gu = jnp.dot(un, sgu_ref[...], preferred_element_type=f32)
    g, u = gu[:, :D_SHARED], gu[:, D_SHARED:]
    shared = jnp.dot((g * jax.nn.sigmoid(g) * u).astype(bf16), sd_ref[...], preferred_element_type=f32)
    h = h_ref[...] + (routed + shared)
    gate = jax.nn.sigmoid(jnp.dot(h.astype(bf16), pg_ref[...], preferred_element_type=f32))
    emb = jnp.dot(p_ref[...].astype(bf16), pp_ref[...], preferred_element_type=f32)
    o_ref[...] = h + gate * emb


def _final(h1, u2p, yg, wts, p2, prm):
    T = h1.shape[0]
    tm = TOKEN_TILE
    row = lambda i: (i, 0)
    const = lambda i: (0, 0)
    full = lambda a: pl.BlockSpec(a.shape, const)
    return pl.pallas_call(
        _final_kernel,
        grid=(T // tm,),
        in_specs=[pl.BlockSpec((tm, D_MODEL), row),
                  pl.BlockSpec((ROW_CHUNKS, tm, LANES), lambda i: (0, i, 0)),
                  pl.BlockSpec((TOP_K, ROW_CHUNKS, tm, LANES), lambda i: (0, 0, i, 0)),
                  pl.BlockSpec((TOP_K, tm), lambda i: (0, i)),
                  pl.BlockSpec((tm, PLE_DIM), row),
                  full(prm["shared_gu"]), full(prm["shared_down"]), full(prm["ple_gate"]), full(prm["ple_proj"])],
        out_specs=pl.BlockSpec((tm, D_MODEL), row),
        out_shape=jax.ShapeDtypeStruct((T, D_MODEL), f32),
        compiler_params=_cparams(("parallel",)),
        name="final",
    )(h1, u2p, yg, wts, p2, prm["shared_gu"], prm["shared_down"], prm["ple_gate"], prm["ple_proj"])


def _deinterleave(n_heads):
    base = jnp.concatenate([jnp.arange(0, HEAD_DIM, 2), jnp.arange(1, HEAD_DIM, 2)])
    return (jnp.arange(n_heads)[:, None] * HEAD_DIM + base[None, :]).reshape(-1)


def _rope_tables(ang):
    c, s = jnp.cos(ang), jnp.sin(ang)
    return jnp.tile(jnp.concatenate([c, c], axis=-1), (1, 2)), jnp.tile(jnp.concatenate([-s, s], axis=-1), (1, 2))


def _axial_angles(S):
    rows = S // GRID_W
    row = jnp.repeat(jnp.arange(rows, dtype=f32), GRID_W)
    col = jnp.tile(jnp.arange(GRID_W, dtype=f32), rows)
    half = HEAD_DIM // 2
    inv = ROPE_THETA ** (-jnp.arange(0, half, 2, dtype=f32) / half)
    return jnp.concatenate([row[:, None] * inv, col[:, None] * inv], axis=-1)


def _linear_angles(S):
    t = jnp.arange(S, dtype=f32)
    inv = ROPE_THETA ** (-jnp.arange(0, HEAD_DIM, 2, dtype=f32) / HEAD_DIM)
    return t[:, None] * inv


def _prepare(seq_lens, norm_mix, w_in, q_norm_a, k_norm_a, q_norm_b, k_norm_b, w_branch_a, w_branch_b, w_out,
             norm_ffn, router_w, router_bias, expert_gate, expert_up, expert_down, shared_gate, shared_up,
             shared_down, ple_proj, ple_gate):
    cuts = [A_WIDTH, A_WIDTH + A_KV_WIDTH, A_WIDTH + 2 * A_KV_WIDTH,
            A_WIDTH + 2 * A_KV_WIDTH + B_WIDTH, A_WIDTH + 2 * A_KV_WIDTH + 2 * B_WIDTH,
            A_WIDTH + 2 * A_KV_WIDTH + 3 * B_WIDTH, A_WIDTH + 2 * A_KV_WIDTH + 3 * B_WIDTH + D_MODEL]
    wqa, wka, wva, wqb, wkb, wvb, wga, wgb = jnp.split(w_in, cuts, axis=-1)
    wqa = wqa[:, _deinterleave(A_Q_HEADS)]
    wka = wka[:, _deinterleave(A_KV_HEADS)]
    wqb = wqb[:, _deinterleave(B_HEADS)]
    wkb = wkb[:, _deinterleave(B_HEADS)]
    k0, k1 = wka[:, :HEAD_DIM], wka[:, HEAD_DIM:]
    v0, v1 = wva[:, :HEAD_DIM], wva[:, HEAD_DIM:]
    vz = jnp.zeros_like(v0)
    w_all = jnp.concatenate([wqa, k0, k0, k1, k1, v0, vz, v1, vz, wqb, wkb, wvb, wga, wgb], axis=-1).astype(bf16)
    perm = _deinterleave(1)
    scale = HEAD_DIM ** -0.5
    tile = lambda g, n: jnp.tile(g[perm], n)[None, :]
    router_pad = jnp.pad(router_w, ((0, 0), (0, LANES - N_EXPERTS)))
    router_hi = router_pad.astype(bf16)
    prm = {
        "norm_mix": norm_mix[None, :], "w_in": w_all,
        "gqa": tile(q_norm_a, A_Q_HEADS) * (scale * LOG2_E), "gka": tile(k_norm_a, KA_COLS // HEAD_DIM),
        "gqb": tile(q_norm_b, B_HEADS) * (scale * LOG2_E), "gkb": tile(k_norm_b, B_HEADS),
        "bd": jnp.kron(jnp.eye(NORM_COLS // HEAD_DIM, dtype=f32), jnp.ones((HEAD_DIM, HEAD_DIM), f32)).astype(bf16),
        "w_branch_a": w_branch_a.astype(bf16), "w_branch_b": w_branch_b.astype(bf16), "w_out": w_out.astype(bf16),
        "norm_ffn": norm_ffn[None, :],
        "router_hi": router_hi, "router_lo": (router_pad - router_hi.astype(f32)).astype(bf16),
        "router_bias": router_bias[:, None],
        "tri": (jnp.arange(MERGE_SUB)[:, None] < jnp.arange(MERGE_SUB)[None, :]).astype(bf16),
        "expert_gate": expert_gate.astype(bf16), "expert_up": expert_up.astype(bf16),
        "expert_down": expert_down.astype(bf16),
        "shared_gu": jnp.concatenate([shared_gate, shared_up], axis=-1).astype(bf16),
        "shared_down": shared_down.astype(bf16),
        "ple_gate": ple_gate.astype(bf16), "ple_proj": ple_proj.astype(bf16),
        "cos_a": {}, "sin_a": {}, "cos_b": {}, "sin_b": {},
    }
    for S in seq_lens:
        prm["cos_a"][S], prm["sin_a"][S] = _rope_tables(_axial_angles(S))
        prm["cos_b"][S], prm["sin_b"][S] = _rope_tables(_linear_angles(S))
    return prm


def _dispatch_plan(idx, rank, counts, T):
    bm = MOE_ROWS
    n_rows = (T * TOP_K + N_EXPERTS * bm) // bm * bm
    n_blocks = n_rows // bm
    padded = (counts + bm - 1) // bm * bm
    pad_end = jnp.cumsum(padded)
    pad_start = pad_end - padded
    dest = pad_start[idx] + rank
    blk_start = jnp.arange(n_blocks, dtype=jnp.int32) * bm
    blk_e = jnp.minimum(jnp.sum(pad_end[None, :] <= blk_start[:, None], axis=1), N_EXPERTS - 1).astype(jnp.int32)
    n_used = (pad_end[-1:] // bm).astype(jnp.int32)
    own = blk_e[:, None] == jnp.arange(N_EXPERTS, dtype=jnp.int32)[None, :]
    real_end = jnp.sum(jnp.where(own, (pad_start + counts)[None, :], 0), axis=1)
    n_valid = jnp.clip(real_end - blk_start, 0, bm).astype(jnp.int32)
    return dest, n_rows, blk_e, n_used, n_valid


def _layer(x, p, prm):
    B, S, _ = x.shape
    T = B * S
    x2 = x.reshape(T, D_MODEL)
    qa, ka, va, *qkv_b, sga, sgb = _in_proj(x2, S, prm)
    shape3 = lambda a: a.reshape(B, S, a.shape[-1])
    ya = _flash(shape3(qa), shape3(ka), shape3(va)).reshape(T, A_WIDTH)
    n_dil = len(WINDOW_DILATIONS)
    outs, lses = [], []
    for j, (_, dil) in enumerate(WINDOW_DILATIONS):
        o_d, lse_d = _dilated(qkv_b[j], qkv_b[n_dil + j], qkv_b[2 * n_dil + j], B, S, dil)
        outs.append(o_d)
        lses.append(lse_d)
    h1, u2p, idx, wts, rank, cnt = _merge_route(x2, ya, outs, lses, sga, sgb, prm)
    dest, n_rows, blk_e, n_used, n_valid = _dispatch_plan(idx, rank, cnt[:, 0], T)
    xs = _sc_dispatch(u2p, dest, n_rows)
    ys = _experts(xs, blk_e, n_used, n_valid, prm)
    yg = _sc_combine(ys, dest)
    out = _final(h1, u2p, yg, wts, p.reshape(T, PLE_DIM), prm)
    return out.reshape(B, S, D_MODEL)


def kernel(x_prompt, x_sample, p_prompt, p_sample, norm_mix, w_in, q_norm_a, k_norm_a, q_norm_b, k_norm_b,
           w_branch_a, w_branch_b, w_out, norm_ffn, router_w, router_bias, expert_gate, expert_up, expert_down,
           shared_gate, shared_up, shared_down, ple_proj, ple_gate):
    weights = (norm_mix, w_in, q_norm_a, k_norm_a, q_norm_b, k_norm_b, w_branch_a, w_branch_b, w_out, norm_ffn,
               router_w, router_bias, expert_gate, expert_up, expert_down, shared_gate, shared_up, shared_down,
               ple_proj, ple_gate)
    depth = norm_mix.shape[0]
    seq_lens = sorted({x_prompt.shape[1], x_sample.shape[1]})
    hp, hs = x_prompt, x_sample
    for i in range(depth):
        prm = _prepare(seq_lens, *[w[i] for w in weights])
        hp = _layer(hp, p_prompt[i], prm)
        hs = _layer(hs, p_sample[i], prm)
    return hp, hs
```

```python
import functools

import jax
import jax.numpy as jnp
from jax import lax
from jax.experimental import pallas as pl
from jax.experimental.pallas import tpu as pltpu
from jax.experimental.pallas import tpu_sc as plsc

D_MODEL = 1024
HEAD_DIM = 64
A_Q_HEADS = 8
A_KV_HEADS = 2
B_HEADS = 6
GRID_W = 64
ROPE_THETA = 10000.0
WINDOW_DILATIONS = ((128, 1), (512, 4), (2048, 16))
N_EXPERTS = 64
TOP_K = 8
N_GROUPS = 8
TOPK_GROUPS = 4
D_EXPERT = 256
D_SHARED = 256
ROUTE_SCALE = 2.5
PLE_DIM = 256
NEG_BIG = -1e30
EPS = 1e-6
LOG2_E = 1.4426950408889634

A_WIDTH = A_Q_HEADS * HEAD_DIM
A_KV_WIDTH = A_KV_HEADS * HEAD_DIM
B_WIDTH = B_HEADS * HEAD_DIM
GROUP_Q = A_Q_HEADS // A_KV_HEADS
EXPERTS_PER_GROUP = N_EXPERTS // N_GROUPS

LANES = 128
NORM_COLS = 256
HALF_WINDOW = 64
KEY_SPAN = 256

KA_COLS = 2 * LANES
VA_COLS = 2 * LANES
OFF_QA = 0
OFF_KA = OFF_QA + A_WIDTH
OFF_VA = OFF_KA + KA_COLS
OFF_QB = OFF_VA + VA_COLS
OFF_KB = OFF_QB + B_WIDTH
OFF_VB = OFF_KB + B_WIDTH
OFF_GA = OFF_VB + B_WIDTH
OFF_GB = OFF_GA + D_MODEL
PROJ_COLS = OFF_GB + D_MODEL

TOKEN_TILE = 512
MERGE_SUB = 256
FLASH_TQ = 256
FLASH_TK = 1024
FLASH_ITEMS = 8
DIL_QBLOCK = 128
DIL_STEP_TOKENS = 2048
MOE_ROWS = 1024
MOE_SUB_ROWS = 256

V7X_VMEM_BYTES = 64 * 1024 * 1024
VMEM_LIMIT = V7X_VMEM_BYTES * 13 // 16

HALF_MODEL = D_MODEL // 2
ROW_CHUNKS = HALF_MODEL // LANES
SC_CORES = 2
SC_SUBCORES = 16
SC_WORKERS = SC_CORES * SC_SUBCORES
SC_WINDOW = 128
SC_PAIR = 2

bf16 = jnp.bfloat16
f32 = jnp.float32


def _cparams(sem):
    return pltpu.CompilerParams(dimension_semantics=sem, vmem_limit_bytes=VMEM_LIMIT)


def _pack_rows(v):
    lo = lax.bitcast_convert_type(v[:, :HALF_MODEL].astype(bf16).astype(f32), jnp.uint32) >> 16
    hi = lax.bitcast_convert_type(v[:, HALF_MODEL:].astype(bf16).astype(f32), jnp.uint32) & jnp.uint32(0xFFFF0000)
    w = lo | hi
    return [w[:, c * LANES:(c + 1) * LANES] for c in range(ROW_CHUNKS)]


def _unpack_rows(chunks):
    w = jnp.concatenate(chunks, axis=1)
    lo = lax.bitcast_convert_type(w << 16, f32)
    hi = lax.bitcast_convert_type(w & jnp.uint32(0xFFFF0000), f32)
    return lo, hi


def _in_proj_kernel(x_ref, nm_ref, w_ref, gqa_ref, gka_ref, gqb_ref, gkb_ref,
                    ca_ref, sa_ref, cb_ref, sb_ref, bd_ref,
                    qa_ref, ka_ref, va_ref, qb1_ref, qb4_ref, qb16_ref, kb1_ref, kb4_ref, kb16_ref,
                    vb1_ref, vb4_ref, vb16_ref, ga_ref, gb_ref, st_ref):
    x = x_ref[...]
    r = lax.rsqrt(jnp.mean(x * x, axis=-1, keepdims=True) + EPS)
    u = (x * r * nm_ref[...]).astype(bf16)
    rows = x.shape[0]
    lane = lax.broadcasted_iota(jnp.int32, (rows, LANES), 1)
    first_half = (lane % HEAD_DIM) < (HEAD_DIM // 2)

    def proj(off, width):
        return jnp.dot(u, w_ref[:, off:off + width], preferred_element_type=f32)

    def norm_rope(off, width, g_ref, c_ref, s_ref):
        z = proj(off, width)
        c = c_ref[...]
        s = s_ref[...]
        chunks = []
        for c0 in range(0, width, NORM_COLS):
            cw = min(NORM_COLS, width - c0)
            zc = z[:, c0:c0 + cw]
            ms = jnp.dot((zc * zc).astype(bf16), bd_ref[:cw, :cw], preferred_element_type=f32) * (1.0 / HEAD_DIM)
            zn = zc * lax.rsqrt(ms + EPS) * g_ref[:, c0:c0 + cw]
            for h0 in range(0, cw, LANES):
                zh = zn[:, h0:h0 + LANES]
                sw = jnp.where(first_half, pltpu.roll(zh, LANES - HEAD_DIM // 2, 1), pltpu.roll(zh, HEAD_DIM // 2, 1))
                chunks.append(zh * c + sw * s)
        return chunks

    def store(chunks, out_ref):
        for j, ch in enumerate(chunks):
            out_ref[:, j * LANES:(j + 1) * LANES] = ch.astype(out_ref.dtype)

    def store_dilated(chunks, out_refs):
        for j, ch in enumerate(chunks):
            st_ref[j] = ch
        for (_, dil), out_ref in zip(WINDOW_DILATIONS, out_refs):
            for r in range(dil):
                for j in range(len(chunks)):
                    src = pl.ds(r, rows // dil, stride=dil) if dil > 1 else pl.ds(0, rows)
                    out_ref[r, :, j * LANES:(j + 1) * LANES] = st_ref[j, src, :].astype(out_ref.dtype)

    store(norm_rope(OFF_QA, A_WIDTH, gqa_ref, ca_ref, sa_ref), qa_ref)
    store(norm_rope(OFF_KA, KA_COLS, gka_ref, ca_ref, sa_ref), ka_ref)
    va = proj(OFF_VA, VA_COLS)
    va_ref[...] = jnp.concatenate(
        [jnp.where(lane < HEAD_DIM, va[:, c0:c0 + LANES], 1.0) for c0 in range(0, VA_COLS, LANES)],
        axis=1).astype(va_ref.dtype)
    store_dilated(norm_rope(OFF_QB, B_WIDTH, gqb_ref, cb_ref, sb_ref), (qb1_ref, qb4_ref, qb16_ref))
    store_dilated(norm_rope(OFF_KB, B_WIDTH, gkb_ref, cb_ref, sb_ref), (kb1_ref, kb4_ref, kb16_ref))
    vb = proj(OFF_VB, B_WIDTH)
    store_dilated([vb[:, c0:c0 + LANES] for c0 in range(0, B_WIDTH, LANES)], (vb1_ref, vb4_ref, vb16_ref))
    ga_ref[...] = jax.nn.sigmoid(proj(OFF_GA, D_MODEL)).astype(ga_ref.dtype)
    gb_ref[...] = jax.nn.sigmoid(proj(OFF_GB, D_MODEL)).astype(gb_ref.dtype)


def _in_proj(x2, S, prm):
    T = x2.shape[0]
    tm = TOKEN_TILE
    n_pos = S // tm
    row = lambda i: (i, 0)
    pos = lambda i: (i % n_pos, 0)
    const = lambda i: (0, 0)
    dils = [dil for _, dil in WINDOW_DILATIONS]
    flat = lambda w: (pl.BlockSpec((tm, w), row), jax.ShapeDtypeStruct((T, w), bf16))
    strided = lambda d: (pl.BlockSpec((d, tm // d, B_WIDTH), lambda i: (0, i, 0)),
                         jax.ShapeDtypeStruct((d, T // d, B_WIDTH), bf16))
    outs = ([flat(A_WIDTH), flat(KA_COLS), flat(VA_COLS)] + 3 * [strided(d) for d in dils]
            + [flat(D_MODEL), flat(D_MODEL)])
    return pl.pallas_call(
        _in_proj_kernel,
        grid=(T // tm,),
        in_specs=[
            pl.BlockSpec((tm, D_MODEL), row),
            pl.BlockSpec((1, D_MODEL), const),
            pl.BlockSpec((D_MODEL, PROJ_COLS), const),
            pl.BlockSpec((1, A_WIDTH), const),
            pl.BlockSpec((1, KA_COLS), const),
            pl.BlockSpec((1, B_WIDTH), const),
            pl.BlockSpec((1, B_WIDTH), const),
            pl.BlockSpec((tm, LANES), pos),
            pl.BlockSpec((tm, LANES), pos),
            pl.BlockSpec((tm, LANES), pos),
            pl.BlockSpec((tm, LANES), pos),
            pl.BlockSpec((NORM_COLS, NORM_COLS), const),
        ],
        out_specs=[spec for spec, _ in outs],
        out_shape=[shape for _, shape in outs],
        scratch_shapes=[pltpu.VMEM((B_WIDTH // LANES, tm, LANES), f32)],
        compiler_params=_cparams(("parallel",)),
        name="in_proj",
    )(x2, prm["norm_mix"], prm["w_in"], prm["gqa"], prm["gka"], prm["gqb"], prm["gkb"],
      prm["cos_a"][S], prm["sin_a"][S], prm["cos_b"][S], prm["sin_b"][S], prm["bd"])


def _flash_kernel(q_ref, k_ref, v_ref, o_ref, s_ref, *, tq, tk, seq, nq):
    lane = lax.broadcasted_iota(jnp.int32, (tq, LANES), 1)
    lo = lane < HEAD_DIM
    n_chunks = seq // tk

    def masked_heads(b):
        q = q_ref[0, b * tq:(b + 1) * tq, :]
        q01, q23 = q[:, :LANES], q[:, LANES:]
        zero = jnp.zeros_like(q01)
        return [jnp.where(lo, q01, zero), jnp.where(lo, zero, q01), jnp.where(lo, q23, zero), jnp.where(lo, zero, q23)]

    def score(qh, chunk):
        kk = k_ref[0, chunk * tk:(chunk + 1) * tk, :]
        return lax.dot_general(qh, kk, (((1,), (1,)), ((), ())), preferred_element_type=f32)

    qs = [masked_heads(b) for b in range(nq)]
    items = [(b, c) for b in range(nq) for c in range(n_chunks)]
    for h in range(GROUP_Q):
        s_ref[0, h] = score(qs[0][h], 0)
    ms = accs = None
    for n, (b, chunk) in enumerate(items):
        slot = n % 2
        nxt = items[n + 1] if n + 1 < len(items) else None
        if chunk == 0:
            ms = [jnp.full((tq, 1), NEG_BIG, f32) for _ in range(GROUP_Q)]
            accs = [jnp.zeros((tq, LANES), f32) for _ in range(GROUP_Q)]
        vv = v_ref[0, chunk * tk:(chunk + 1) * tk, :]
        for h in range(GROUP_Q):
            if nxt is not None:
                s_ref[1 - slot, h] = score(qs[nxt[0]][h], nxt[1])
            s = s_ref[slot, h]
            m_new = jnp.maximum(ms[h], jnp.max(s, axis=-1, keepdims=True))
            p = jnp.exp2((s - m_new).astype(bf16))
            alpha = jnp.exp2(ms[h] - m_new)
            accs[h] = accs[h] * alpha + jnp.dot(p, vv, preferred_element_type=f32)
            ms[h] = m_new
        if chunk == n_chunks - 1:
            heads = [acc / jnp.where(lo, pltpu.roll(acc, HEAD_DIM, 1), 1.0) for acc in accs]
            out01 = jnp.where(lo, heads[0], pltpu.roll(heads[1], HEAD_DIM, 1))
            out23 = jnp.where(lo, heads[2], pltpu.roll(heads[3], HEAD_DIM, 1))
            o_ref[0, b * tq:(b + 1) * tq, :] = jnp.concatenate([out01, out23], axis=1).astype(o_ref.dtype)


def _flash(qa, ka, va):
    B, S, _ = qa.shape
    tq, tk = FLASH_TQ, FLASH_TK
    nq = max(1, FLASH_ITEMS // (S // tk))
    return pl.pallas_call(
        functools.partial(_flash_kernel, tq=tq, tk=tk, seq=S, nq=nq),
        grid=(B, A_KV_HEADS, S // (nq * tq)),
        in_specs=[
            pl.BlockSpec((1, nq * tq, GROUP_Q * HEAD_DIM), lambda b, g, i: (b, i, g)),
            pl.BlockSpec((1, S, LANES), lambda b, g, i: (b, 0, g)),
            pl.BlockSpec((1, S, LANES), lambda b, g, i: (b, 0, g)),
        ],
        out_specs=pl.BlockSpec((1, nq * tq, GROUP_Q * HEAD_DIM), lambda b, g, i: (b, i, g)),
        out_shape=jax.ShapeDtypeStruct((B, S, A_WIDTH), bf16),
        scratch_shapes=[pltpu.VMEM((2, GROUP_Q, tq, tk), f32)],
        compiler_params=_cparams(("parallel", "parallel", "arbitrary")),
        name="flash_gqa",
    )(qa, ka, va)


def _dilated_kernel(q_ref, k_ref, v_ref, o_ref, lse_ref, so_ref, sl_ref, *, length, tms, dil):
    i = pl.program_id(1)
    lane = lax.broadcasted_iota(jnp.int32, (DIL_QBLOCK, LANES), 1)
    lo = lane < HEAD_DIM
    n_pairs = B_WIDTH // LANES
    ones = jnp.ones((KEY_SPAN, LANES), bf16)

    def unit(u, carry):
        sb, r = u // dil, u % dil
        r0 = pl.multiple_of(sb * DIL_QBLOCK, DIL_QBLOCK)
        m0 = i * tms + r0
        start = pl.multiple_of(jnp.clip(m0 - HALF_WINDOW, 0, length - KEY_SPAN), HALF_WINDOW)
        qpos = m0 + lax.broadcasted_iota(jnp.int32, (DIL_QBLOCK, KEY_SPAN), 0)
        kpos = start + lax.broadcasted_iota(jnp.int32, (DIL_QBLOCK, KEY_SPAN), 1)
        valid = jnp.abs(kpos - qpos) <= HALF_WINDOW
        rows = pl.ds(r0 * dil + r, DIL_QBLOCK, stride=dil) if dil > 1 else pl.ds(r0, DIL_QBLOCK)
        scores = []
        for c in range(n_pairs):
            cs = slice(c * LANES, (c + 1) * LANES)
            qp = q_ref[r, 0, pl.ds(r0, DIL_QBLOCK), cs]
            kp = k_ref[r, 0, pl.ds(start, KEY_SPAN), cs]
            zero = jnp.zeros_like(qp)
            for half in range(2):
                qm = jnp.where(lo, qp, zero) if half == 0 else jnp.where(lo, zero, qp)
                scores.append(lax.dot_general(qm, kp, (((1,), (1,)), ((), ())), preferred_element_type=f32))
        for c in range(n_pairs):
            vp_ones = jnp.concatenate([v_ref[r, 0, pl.ds(start, KEY_SPAN), c * LANES:(c + 1) * LANES], ones], axis=1)
            ols, ms = [], []
            for half in range(2):
                s = jnp.where(valid, scores[2 * c + half], NEG_BIG)
                m = jnp.max(s, axis=-1, keepdims=True)
                e = jnp.exp2((s - m).astype(bf16))
                ols.append(jnp.dot(e, vp_ones, preferred_element_type=f32))
                ms.append(m)
            num = jnp.where(lo, ols[0][:, :LANES], ols[1][:, :LANES])
            den = jnp.where(lo, ols[0][:, LANES:], ols[1][:, LANES:])
            so_ref[c, rows, :] = num / den
            sl_ref[c, rows, :] = jnp.where(lo, ms[0], ms[1]) + jnp.log2(den)
        return carry

    lax.fori_loop(0, (tms // DIL_QBLOCK) * dil, unit, 0, unroll=8)
    o_ref[0] = jnp.concatenate([so_ref[c] for c in range(n_pairs)], axis=1).astype(o_ref.dtype)
    lse_ref[0] = jnp.concatenate([sl_ref[c] for c in range(n_pairs)], axis=1)


def _dilated(qb, kb, vb, B, S, dil):
    length = S // dil
    tms = DIL_STEP_TOKENS // dil
    view = lambda a: a.reshape(dil, B, length, B_WIDTH)
    o, lse = pl.pallas_call(
        functools.partial(_dilated_kernel, length=length, tms=tms, dil=dil),
        grid=(B, length // tms),
        in_specs=[
            pl.BlockSpec((dil, 1, tms, B_WIDTH), lambda b, i: (0, b, i, 0)),
            pl.BlockSpec((dil, 1, length, B_WIDTH), lambda b, i: (0, b, 0, 0)),
            pl.BlockSpec((dil, 1, length, B_WIDTH), lambda b, i: (0, b, 0, 0)),
        ],
        out_specs=[
            pl.BlockSpec((1, tms * dil, B_WIDTH), lambda b, i: (b, i, 0)),
            pl.BlockSpec((1, tms * dil, B_WIDTH), lambda b, i: (b, i, 0)),
        ],
        out_shape=[
            jax.ShapeDtypeStruct((B, S, B_WIDTH), bf16),
            jax.ShapeDtypeStruct((B, S, B_WIDTH), f32),
        ],
        scratch_shapes=[pltpu.VMEM((B_WIDTH // LANES, tms * dil, LANES), f32),
                        pltpu.VMEM((B_WIDTH // LANES, tms * dil, LANES), f32)],
        compiler_params=_cparams(("parallel", "arbitrary")),
        name=f"dilated_{dil}",
    )(view(qb), view(kb), view(vb))
    return o.reshape(B * S, B_WIDTH), lse.reshape(B * S, B_WIDTH)


def _first_index_of_max(cur, idx, n):
    mx = jnp.max(cur, axis=0, keepdims=True)
    return jnp.min(jnp.where(cur == mx, idx, n), axis=0, keepdims=True)


def _merge_route_kernel(x_ref, ya_ref, o1_ref, o2_ref, o3_ref, l1_ref, l2_ref, l3_ref, ga_ref, gb_ref,
                        wa_ref, wb_ref, wo_ref, nf_ref, rhi_ref, rlo_ref, bias_ref, tri_ref,
                        h_ref, u_ref, idx_ref, wt_ref, rank_ref, cnt_ref, run_ref, lg_ref):
    step = pl.program_id(0)

    @pl.when(step == 0)
    def _():
        run_ref[...] = jnp.zeros_like(run_ref)
        lg_ref[...] = jnp.zeros_like(lg_ref)

    routing_live = step > 0
    cur, prev = step % 2, (step + 1) % 2

    tm = x_ref.shape[0]
    sub = tri_ref.shape[0]

    def dense(j):
        rs = slice(j * sub, (j + 1) * sub)
        l1, l2, l3 = l1_ref[rs, :], l2_ref[rs, :], l3_ref[rs, :]
        lm = jnp.maximum(jnp.maximum(l1, l2), l3)
        e1, e2, e3 = jnp.exp2(l1 - lm), jnp.exp2(l2 - lm), jnp.exp2(l3 - lm)
        yb = ((e1 * o1_ref[rs, :].astype(f32) + e2 * o2_ref[rs, :].astype(f32) + e3 * o3_ref[rs, :].astype(f32))
              / (e1 + e2 + e3))
        pa = jnp.dot(ya_ref[rs, :], wa_ref[...], preferred_element_type=f32)
        pb = jnp.dot(yb.astype(bf16), wb_ref[...], preferred_element_type=f32)
        merged = ga_ref[rs, :].astype(f32) * pa + gb_ref[rs, :].astype(f32) * pb
        h = x_ref[rs, :] + jnp.dot(merged.astype(bf16), wo_ref[...], preferred_element_type=f32)
        h_ref[rs, :] = h
        r = lax.rsqrt(jnp.mean(h * h, axis=-1, keepdims=True) + EPS)
        uf = h * r * nf_ref[...]
        u_hi = uf.astype(bf16)
        for c, chunk in enumerate(_pack_rows(uf)):
            u_ref[c, rs, :] = chunk
        u_lo = (uf - u_hi.astype(f32)).astype(bf16)
        return (jnp.dot(u_hi, rhi_ref[...], preferred_element_type=f32)
                + jnp.dot(u_lo, rhi_ref[...], preferred_element_type=f32)
                + jnp.dot(u_hi, rlo_ref[...], preferred_element_type=f32))

    def route(j, logits):
        scores = jax.nn.sigmoid(logits.T[:N_EXPERTS, :])
        sel = scores + bias_ref[...]
        i8 = lax.broadcasted_iota(jnp.int32, (EXPERTS_PER_GROUP, sub), 0)
        neg_inf = jnp.float32(-jnp.inf)
        gscore = jnp.zeros((N_GROUPS, sub), f32)
        for g in range(N_GROUPS):
            blk = sel[g * EXPERTS_PER_GROUP:(g + 1) * EXPERTS_PER_GROUP, :]
            top1 = jnp.max(blk, axis=0, keepdims=True)
            first = _first_index_of_max(blk, i8, EXPERTS_PER_GROUP)
            top2 = jnp.max(jnp.where(i8 == first, neg_inf, blk), axis=0, keepdims=True)
            gscore = jnp.where(i8 == g, top1 + top2, gscore)
        gkeep = jnp.zeros((N_GROUPS, sub), jnp.bool_)
        cur = gscore
        for _ in range(TOPK_GROUPS):
            pick = i8 == _first_index_of_max(cur, i8, N_GROUPS)
            gkeep = jnp.logical_or(gkeep, pick)
            cur = jnp.where(pick, neg_inf, cur)
        cur = jnp.concatenate(
            [jnp.where(gkeep[g:g + 1, :], sel[g * EXPERTS_PER_GROUP:(g + 1) * EXPERTS_PER_GROUP, :], NEG_BIG)
             for g in range(N_GROUPS)], axis=0)

        ie = lax.broadcasted_iota(jnp.int32, (N_EXPERTS, sub), 0)
        ik = lax.broadcasted_iota(jnp.int32, (TOP_K, sub), 0)
        idx_out = jnp.zeros((TOP_K, sub), jnp.int32)
        w_out = jnp.zeros((TOP_K, sub), f32)
        chosen = jnp.zeros((N_EXPERTS, sub), jnp.bool_)
        for k in range(TOP_K):
            fi = _first_index_of_max(cur, ie, N_EXPERTS)
            pick = ie == fi
            wk = jnp.sum(jnp.where(pick, scores, 0.0), axis=0, keepdims=True)
            idx_out = jnp.where(ik == k, fi, idx_out)
            w_out = jnp.where(ik == k, wk, w_out)
            chosen = jnp.logical_or(chosen, pick)
            cur = jnp.where(pick, neg_inf, cur)
        w_out = w_out / jnp.sum(w_out, axis=0, keepdims=True) * ROUTE_SCALE
        wt_ref[:, j * sub:(j + 1) * sub] = w_out

        sel01 = jnp.where(chosen, 1.0, 0.0)
        before = jnp.dot(sel01.astype(bf16), tri_ref[...], preferred_element_type=f32) + run_ref[:, 0:1]
        rank_out = jnp.zeros((TOP_K, sub), f32)
        for k in range(TOP_K):
            rk = jnp.sum(jnp.where(ie == idx_out[k:k + 1, :], before, 0.0), axis=0, keepdims=True)
            rank_out = jnp.where(ik == k, rk, rank_out)
        rank_i = rank_out.astype(jnp.int32)
        for w in range(sub // SC_WINDOW):
            win = j * (sub // SC_WINDOW) + w
            idx_ref[win] = idx_out[:, w * SC_WINDOW:(w + 1) * SC_WINDOW]
            rank_ref[win] = rank_i[:, w * SC_WINDOW:(w + 1) * SC_WINDOW]
        run_ref[...] = run_ref[...] + jnp.where(routing_live, jnp.sum(sel01, axis=1, keepdims=True), 0.0)

    n_sub = tm // sub
    prev_logits = [lg_ref[prev, j * sub:(j + 1) * sub, :] for j in range(n_sub)]
    for j in range(n_sub):
        lg_ref[cur, j * sub:(j + 1) * sub, :] = dense(j)
    for j in range(n_sub):
        route(j, prev_logits[j])
    cnt_ref[...] = run_ref[...].astype(jnp.int32)


def _merge_route(x2, ya, outs, lses, sga, sgb, prm):
    T = x2.shape[0]
    tm = TOKEN_TILE
    n_tiles = T // tm
    dense_i = lambda i: jnp.minimum(i, n_tiles - 1)
    route_i = lambda i: jnp.maximum(i - 1, 0)
    const = lambda i: (0, 0)
    tok = lambda w: pl.BlockSpec((tm, w), lambda i: (dense_i(i), 0))
    full = lambda a: pl.BlockSpec(a.shape, const)
    win = pl.BlockSpec((tm // SC_WINDOW, TOP_K, SC_WINDOW), lambda i: (route_i(i), 0, 0))
    return pl.pallas_call(
        _merge_route_kernel,
        grid=(n_tiles + 1,),
        in_specs=[tok(D_MODEL), tok(A_WIDTH), tok(B_WIDTH), tok(B_WIDTH), tok(B_WIDTH),
                  tok(B_WIDTH), tok(B_WIDTH), tok(B_WIDTH), tok(D_MODEL), tok(D_MODEL),
                  full(prm["w_branch_a"]), full(prm["w_branch_b"]), full(prm["w_out"]), full(prm["norm_ffn"]),
                  full(prm["router_hi"]), full(prm["router_lo"]), full(prm["router_bias"]), full(prm["tri"])],
        out_specs=[tok(D_MODEL), pl.BlockSpec((ROW_CHUNKS, tm, LANES), lambda i: (0, dense_i(i), 0)),
                   win, pl.BlockSpec((TOP_K, tm), lambda i: (0, route_i(i))), win,
                   pl.BlockSpec((N_EXPERTS, LANES), const)],
        out_shape=[jax.ShapeDtypeStruct((T, D_MODEL), f32), jax.ShapeDtypeStruct((ROW_CHUNKS, T, LANES), jnp.uint32),
                   jax.ShapeDtypeStruct((T // SC_WINDOW, TOP_K, SC_WINDOW), jnp.int32),
                   jax.ShapeDtypeStruct((TOP_K, T), f32),
                   jax.ShapeDtypeStruct((T // SC_WINDOW, TOP_K, SC_WINDOW), jnp.int32),
                   jax.ShapeDtypeStruct((N_EXPERTS, LANES), jnp.int32)],
        scratch_shapes=[pltpu.VMEM((N_EXPERTS, LANES), f32), pltpu.VMEM((2, tm, LANES), f32)],
        compiler_params=_cparams(("arbitrary",)),
        name="merge_route",
    )(x2, ya, *outs, *lses, sga, sgb,
      prm["w_branch_a"], prm["w_branch_b"], prm["w_out"], prm["norm_ffn"],
      prm["router_hi"], prm["router_lo"], prm["router_bias"], prm["tri"])


def _experts_kernel(be_ref, nu_ref, nv_ref, x_ref, wg_ref, wu_ref, wd_ref, y_ref):
    del be_ref
    i = pl.program_id(0)

    @pl.when(i < nu_ref[0])
    def _():
        n_sub = x_ref.shape[1] // MOE_SUB_ROWS

        def up(j):
            r0 = j * MOE_SUB_ROWS
            lo, hi = _unpack_rows([x_ref[c, r0:r0 + MOE_SUB_ROWS, :] for c in range(ROW_CHUNKS)])
            live = r0 + lax.broadcasted_iota(jnp.int32, lo.shape, 0) < nv_ref[i]
            x = jnp.concatenate([jnp.where(live, lo, 0.0), jnp.where(live, hi, 0.0)], axis=1).astype(bf16)
            return (jnp.dot(x, wg_ref[0], preferred_element_type=f32),
                    jnp.dot(x, wu_ref[0], preferred_element_type=f32))

        def down(j, gu):
            g, u = gu
            a = g * jax.nn.sigmoid(g) * u
            y = jnp.dot(a.astype(bf16), wd_ref[0], preferred_element_type=f32)
            r0 = j * MOE_SUB_ROWS
            for c, chunk in enumerate(_pack_rows(y)):
                y_ref[c, r0:r0 + MOE_SUB_ROWS, :] = chunk

        gu_prev = up(0)
        for j in range(1, n_sub):
            gu_next = up(j)
            down(j - 1, gu_prev)
            gu_prev = gu_next
        down(n_sub - 1, gu_prev)

    @pl.when(i >= nu_ref[0])
    def _():
        y_ref[...] = jnp.zeros_like(y_ref)


def _experts(xs, blk_e, n_used, n_valid, prm):
    n_rows = xs.shape[1]
    bm = MOE_ROWS
    rows = lambda i, be, nu, nv: (0, jnp.minimum(i, nu[0] - 1), 0)
    return pl.pallas_call(
        _experts_kernel,
        grid_spec=pltpu.PrefetchScalarGridSpec(
            num_scalar_prefetch=3,
            grid=(n_rows // bm,),
            in_specs=[
                pl.BlockSpec((ROW_CHUNKS, bm, LANES), rows),
                pl.BlockSpec((1, D_MODEL, D_EXPERT), lambda i, be, nu, nv: (be[i], 0, 0)),
                pl.BlockSpec((1, D_MODEL, D_EXPERT), lambda i, be, nu, nv: (be[i], 0, 0)),
                pl.BlockSpec((1, D_EXPERT, D_MODEL), lambda i, be, nu, nv: (be[i], 0, 0)),
            ],
            out_specs=pl.BlockSpec((ROW_CHUNKS, bm, LANES), lambda i, be, nu, nv: (0, i, 0)),
        ),
        out_shape=jax.ShapeDtypeStruct((ROW_CHUNKS, n_rows, LANES), jnp.uint32),
        compiler_params=_cparams(("arbitrary",)),
        name="experts",
    )(blk_e, n_used, n_valid, xs, prm["expert_gate"], prm["expert_up"], prm["expert_down"])


def _sc_mesh():
    return plsc.VectorSubcoreMesh(core_axis_name="c", subcore_axis_name="s")


def _sc_dispatch(u2p, dest, n_rows):
    T = u2p.shape[1]
    per_worker = T // SC_WORKERS
    n_it = per_worker // SC_WINDOW

    @functools.partial(
        pl.kernel, mesh=_sc_mesh(),
        out_type=jax.ShapeDtypeStruct((ROW_CHUNKS, n_rows, LANES), jnp.uint32),
        scratch_types=[pltpu.VMEM((TOP_K, SC_WINDOW), jnp.int32),
                       pltpu.VMEM((ROW_CHUNKS, SC_WINDOW, LANES), jnp.uint32),
                       pltpu.SemaphoreType.DMA, pltpu.SemaphoreType.DMA],
    )
    def k(u_hbm, dest_hbm, xs_hbm, idx_v, rows_v, sem_in, sem_out):
        wid = lax.axis_index("s") * SC_CORES + lax.axis_index("c")

        @pl.loop(0, n_it)
        def _(it):
            t0 = wid * per_worker + it * SC_WINDOW
            pltpu.sync_copy(dest_hbm.at[wid * n_it + it], idx_v)
            loads = [pltpu.async_copy(u_hbm.at[c].at[pl.ds(t0, SC_WINDOW)], rows_v.at[c], sem_in)
                     for c in range(ROW_CHUNKS)]
            for cp in loads:
                cp.wait()
            stores = [pltpu.async_copy(rows_v.at[c], xs_hbm.at[c].at[idx_v.at[kk]], sem_out)
                      for kk in range(TOP_K) for c in range(ROW_CHUNKS)]
            for cp in stores:
                cp.wait()

    return k(u2p, dest)


def _sc_combine(ys, dest):
    n_windows = dest.shape[0]
    T = n_windows * SC_WINDOW
    per_worker = T // SC_WORKERS
    n_it = per_worker // SC_WINDOW

    items = [(c, k0) for c in range(ROW_CHUNKS) for k0 in range(0, TOP_K, SC_PAIR)]

    @functools.partial(
        pl.kernel, mesh=_sc_mesh(),
        out_type=jax.ShapeDtypeStruct((TOP_K, ROW_CHUNKS, T, LANES), jnp.uint32),
        scratch_types=[pltpu.VMEM((TOP_K, SC_WINDOW), jnp.int32),
                       pltpu.VMEM((2, SC_PAIR, SC_WINDOW, LANES), jnp.uint32),
                       pltpu.SemaphoreType.DMA, pltpu.SemaphoreType.DMA,
                       pltpu.SemaphoreType.DMA, pltpu.SemaphoreType.DMA],
    )
    def k(ys_hbm, dest_hbm, yg_hbm, idx_v, rows_v, sem_in0, sem_in1, sem_out0, sem_out1):
        sem_in, sem_out = (sem_in0, sem_in1), (sem_out0, sem_out1)
        wid = lax.axis_index("s") * SC_CORES + lax.axis_index("c")

        @pl.loop(0, n_it)
        def _(it):
            t0 = wid * per_worker + it * SC_WINDOW
            pltpu.sync_copy(dest_hbm.at[wid * n_it + it], idx_v)

            def gathers(n):
                c, k0 = items[n]
                return [pltpu.async_copy(ys_hbm.at[c].at[idx_v.at[k0 + j]], rows_v.at[n % 2, j], sem_in[n % 2])
                        for j in range(SC_PAIR)]

            def stores(n):
                c, k0 = items[n]
                return [pltpu.async_copy(rows_v.at[n % 2, j], yg_hbm.at[k0 + j].at[c].at[pl.ds(t0, SC_WINDOW)],
                                         sem_out[n % 2]) for j in range(SC_PAIR)]

            pending_gathers = gathers(0)
            pending_stores = {}
            for n in range(len(items)):
                for cp in pending_gathers:
                    cp.wait()
                if n + 1 < len(items):
                    for cp in pending_stores.pop((n + 1) % 2, []):
                        cp.wait()
                    pending_gathers = gathers(n + 1)
                pending_stores[n % 2] = stores(n)
            for copies in pending_stores.values():
                for cp in copies:
                    cp.wait()

    return k(ys, dest)


def _final_kernel(h_ref, u_ref, yg_ref, wt_ref, p_ref, sgu_ref, sd_ref, pg_ref, pp_ref, o_ref):
    wl = wt_ref[...]
    wt = jnp.concatenate([wl, jnp.zeros((LANES - TOP_K, wl.shape[1]), f32)], axis=0).T
    r_lo = r_hi = None
    for k in range(TOP_K):
        lo, hi = _unpack_rows([yg_ref[k, c] for c in range(ROW_CHUNKS)])
        wk = wt[:, k:k + 1]
        r_lo = lo * wk if k == 0 else r_lo + lo * wk
        r_hi = hi * wk if k == 0 else r_hi + hi * wk
    routed = jnp.concatenate([r_lo, r_hi], axis=1)
    u_lo, u_hi = _unpack_rows([u_ref[c] for c in range(ROW_CHUNKS)])
    un = jnp.concatenate([u_lo, u_hi], axis=1).astype(bf16)
    gu = jnp.dot(un, sgu_ref[...], preferred_element_type=f32)
    g, u = gu[:, :D_SHARED], gu[:, D_SHARED:]
    shared = jnp.dot((g * jax.nn.sigmoid(g) * u).astype(bf16), sd_ref[...], preferred_element_type=f32)
    h = h_ref[...] + (routed + shared)
    gate = jax.nn.sigmoid(jnp.dot(h.astype(bf16), pg_ref[...], preferred_element_type=f32))
    emb = jnp.dot(p_ref[...].astype(bf16), pp_ref[...], preferred_element_type=f32)
    o_ref[...] = h + gate * emb


def _final(h1, u2p, yg, wts, p2, prm):
    T = h1.shape[0]
    tm = TOKEN_TILE
    row = lambda i: (i, 0)
    const = lambda i: (0, 0)
    full = lambda a: pl.BlockSpec(a.shape, const)
    return pl.pallas_call(
        _final_kernel,
        grid=(T // tm,),
        in_specs=[pl.BlockSpec((tm, D_MODEL), row),
                  pl.BlockSpec((ROW_CHUNKS, tm, LANES), lambda i: (0, i, 0)),
                  pl.BlockSpec((TOP_K, ROW_CHUNKS, tm, LANES), lambda i: (0, 0, i, 0)),
                  pl.BlockSpec((TOP_K, tm), lambda i: (0, i)),
                  pl.BlockSpec((tm, PLE_DIM), row),
                  full(prm["shared_gu"]), full(prm["shared_down"]), full(prm["ple_gate"]), full(prm["ple_proj"])],
        out_specs=pl.BlockSpec((tm, D_MODEL), row),
        out_shape=jax.ShapeDtypeStruct((T, D_MODEL), f32),
        compiler_params=_cparams(("parallel",)),
        name="final",
    )(h1, u2p, yg, wts, p2, prm["shared_gu"], prm["shared_down"], prm["ple_gate"], prm["ple_proj"])


def _deinterleave(n_heads):
    base = jnp.concatenate([jnp.arange(0, HEAD_DIM, 2), jnp.arange(1, HEAD_DIM, 2)])
    return (jnp.arange(n_heads)[:, None] * HEAD_DIM + base[None, :]).reshape(-1)


def _rope_tables(ang):
    c, s = jnp.cos(ang), jnp.sin(ang)
    return jnp.tile(jnp.concatenate([c, c], axis=-1), (1, 2)), jnp.tile(jnp.concatenate([-s, s], axis=-1), (1, 2))


def _axial_angles(S):
    rows = S // GRID_W
    row = jnp.repeat(jnp.arange(rows, dtype=f32), GRID_W)
    col = jnp.tile(jnp.arange(GRID_W, dtype=f32), rows)
    half = HEAD_DIM // 2
    inv = ROPE_THETA ** (-jnp.arange(0, half, 2, dtype=f32) / half)
    return jnp.concatenate([row[:, None] * inv, col[:, None] * inv], axis=-1)


def _linear_angles(S):
    t = jnp.arange(S, dtype=f32)
    inv = ROPE_THETA ** (-jnp.arange(0, HEAD_DIM, 2, dtype=f32) / HEAD_DIM)
    return t[:, None] * inv


def _prepare(seq_lens, norm_mix, w_in, q_norm_a, k_norm_a, q_norm_b, k_norm_b, w_branch_a, w_branch_b, w_out,
             norm_ffn, router_w, router_bias, expert_gate, expert_up, expert_down, shared_gate, shared_up,
             shared_down, ple_proj, ple_gate):
    cuts = [A_WIDTH, A_WIDTH + A_KV_WIDTH, A_WIDTH + 2 * A_KV_WIDTH,
            A_WIDTH + 2 * A_KV_WIDTH + B_WIDTH, A_WIDTH + 2 * A_KV_WIDTH + 2 * B_WIDTH,
            A_WIDTH + 2 * A_KV_WIDTH + 3 * B_WIDTH, A_WIDTH + 2 * A_KV_WIDTH + 3 * B_WIDTH + D_MODEL]
    wqa, wka, wva, wqb, wkb, wvb, wga, wgb = jnp.split(w_in, cuts, axis=-1)
    wqa = wqa[:, _deinterleave(A_Q_HEADS)]
    wka = wka[:, _deinterleave(A_KV_HEADS)]
    wqb = wqb[:, _deinterleave(B_HEADS)]
    wkb = wkb[:, _deinterleave(B_HEADS)]
    k0, k1 = wka[:, :HEAD_DIM], wka[:, HEAD_DIM:]
    v0, v1 = wva[:, :HEAD_DIM], wva[:, HEAD_DIM:]
    vz = jnp.zeros_like(v0)
    w_all = jnp.concatenate([wqa, k0, k0, k1, k1, v0, vz, v1, vz, wqb, wkb, wvb, wga, wgb], axis=-1).astype(bf16)
    perm = _deinterleave(1)
    scale = HEAD_DIM ** -0.5
    tile = lambda g, n: jnp.tile(g[perm], n)[None, :]
    router_pad = jnp.pad(router_w, ((0, 0), (0, LANES - N_EXPERTS)))
    router_hi = router_pad.astype(bf16)
    prm = {
        "norm_mix": norm_mix[None, :], "w_in": w_all,
        "gqa": tile(q_norm_a, A_Q_HEADS) * (scale * LOG2_E), "gka": tile(k_norm_a, KA_COLS // HEAD_DIM),
        "gqb": tile(q_norm_b, B_HEADS) * (scale * LOG2_E), "gkb": tile(k_norm_b, B_HEADS),
        "bd": jnp.kron(jnp.eye(NORM_COLS // HEAD_DIM, dtype=f32), jnp.ones((HEAD_DIM, HEAD_DIM), f32)).astype(bf16),
        "w_branch_a": w_branch_a.astype(bf16), "w_branch_b": w_branch_b.astype(bf16), "w_out": w_out.astype(bf16),
        "norm_ffn": norm_ffn[None, :],
        "router_hi": router_hi, "router_lo": (router_pad - router_hi.astype(f32)).astype(bf16),
        "router_bias": router_bias[:, None],
        "tri": (jnp.arange(MERGE_SUB)[:, None] < jnp.arange(MERGE_SUB)[None, :]).astype(bf16),
        "expert_gate": expert_gate.astype(bf16), "expert_up": expert_up.astype(bf16),
        "expert_down": expert_down.astype(bf16),
        "shared_gu": jnp.concatenate([shared_gate, shared_up], axis=-1).astype(bf16),
        "shared_down": shared_down.astype(bf16),
        "ple_gate": ple_gate.astype(bf16), "ple_proj": ple_proj.astype(bf16),
        "cos_a": {}, "sin_a": {}, "cos_b": {}, "sin_b": {},
    }
    for S in seq_lens:
        prm["cos_a"][S], prm["sin_a"][S] = _rope_tables(_axial_angles(S))
        prm["cos_b"][S], prm["sin_b"][S] = _rope_tables(_linear_angles(S))
    return prm


def _dispatch_plan(idx, rank, counts, T):
    bm = MOE_ROWS
    n_rows = (T * TOP_K + N_EXPERTS * bm) // bm * bm
    n_blocks = n_rows // bm
    padded = (counts + bm - 1) // bm * bm
    pad_end = jnp.cumsum(padded)
    pad_start = pad_end - padded
    dest = pad_start[idx] + rank
    blk_start = jnp.arange(n_blocks, dtype=jnp.int32) * bm
    blk_e = jnp.minimum(jnp.sum(pad_end[None, :] <= blk_start[:, None], axis=1), N_EXPERTS - 1).astype(jnp.int32)
    n_used = (pad_end[-1:] // bm).astype(jnp.int32)
    own = blk_e[:, None] == jnp.arange(N_EXPERTS, dtype=jnp.int32)[None, :]
    real_end = jnp.sum(jnp.where(own, (pad_start + counts)[None, :], 0), axis=1)
    n_valid = jnp.clip(real_end - blk_start, 0, bm).astype(jnp.int32)
    return dest, n_rows, blk_e, n_used, n_valid


def _layer(x, p, prm):
    B, S, _ = x.shape
    T = B * S
    x2 = x.reshape(T, D_MODEL)
    qa, ka, va, *qkv_b, sga, sgb = _in_proj(x2, S, prm)
    shape3 = lambda a: a.reshape(B, S, a.shape[-1])
    ya = _flash(shape3(qa), shape3(ka), shape3(va)).reshape(T, A_WIDTH)
    n_dil = len(WINDOW_DILATIONS)
    outs, lses = [], []
    for j, (_, dil) in enumerate(WINDOW_DILATIONS):
        o_d, lse_d = _dilated(qkv_b[j], qkv_b[n_dil + j], qkv_b[2 * n_dil + j], B, S, dil)
        outs.append(o_d)
        lses.append(lse_d)
    h1, u2p, idx, wts, rank, cnt = _merge_route(x2, ya, outs, lses, sga, sgb, prm)
    dest, n_rows, blk_e, n_used, n_valid = _dispatch_plan(idx, rank, cnt[:, 0], T)
    xs = _sc_dispatch(u2p, dest, n_rows)
    ys = _experts(xs, blk_e, n_used, n_valid, prm)
    yg = _sc_combine(ys, dest)
    out = _final(h1, u2p, yg, wts, p.reshape(T, PLE_DIM), prm)
    return out.reshape(B, S, D_MODEL)


def kernel(x_prompt, x_sample, p_prompt, p_sample, norm_mix, w_in, q_norm_a, k_norm_a, q_norm_b, k_norm_b,
           w_branch_a, w_branch_b, w_out, norm_ffn, router_w, router_bias, expert_gate, expert_up, expert_down,
           shared_gate, shared_up, shared_down, ple_proj, ple_gate):
    weights = (norm_mix, w_in, q_norm_a, k_norm_a, q_norm_b, k_norm_b, w_branch_a, w_branch_b, w_out, norm_ffn,
               router_w, router_bias, expert_gate, expert_up, expert_down, shared_gate, shared_up, shared_down,
               ple_proj, ple_gate)
    depth = norm_mix.shape[0]
    seq_lens = sorted({x_prompt.shape[1], x_sample.shape[1]})
    hp, hs = x_prompt, x_sample
    for i in range(depth):
        prm = _prepare(seq_lens, *[w[i] for w in weights])
        hp = _layer(hp, p_prompt[i], prm)
        hs = _layer(hs, p_sample[i], prm)
    return hp, hs
```

```python
import functools

import jax
import jax.numpy as jnp
from jax import lax
from jax.experimental import pallas as pl
from jax.experimental.pallas import tpu as pltpu
from jax.experimental.pallas import tpu_sc as plsc

D_MODEL = 1024
HEAD_DIM = 64
A_Q_HEADS = 8
A_KV_HEADS = 2
B_HEADS = 6
GRID_W = 64
ROPE_THETA = 10000.0
WINDOW_DILATIONS = ((128, 1), (512, 4), (2048, 16))
N_EXPERTS = 64
TOP_K = 8
N_GROUPS = 8
TOPK_GROUPS = 4
D_EXPERT = 256
D_SHARED = 256
ROUTE_SCALE = 2.5
PLE_DIM = 256
NEG_BIG = -1e30
EPS = 1e-6
LOG2_E = 1.4426950408889634

A_WIDTH = A_Q_HEADS * HEAD_DIM
A_KV_WIDTH = A_KV_HEADS * HEAD_DIM
B_WIDTH = B_HEADS * HEAD_DIM
GROUP_Q = A_Q_HEADS // A_KV_HEADS
EXPERTS_PER_GROUP = N_EXPERTS // N_GROUPS

LANES = 128
NORM_COLS = 256
HALF_WINDOW = 64
KEY_SPAN = 256

KA_COLS = 2 * LANES
VA_COLS = 2 * LANES
OFF_QA = 0
OFF_KA = OFF_QA + A_WIDTH
OFF_VA = OFF_KA + KA_COLS
OFF_QB = OFF_VA + VA_COLS
OFF_KB = OFF_QB + B_WIDTH
OFF_VB = OFF_KB + B_WIDTH
OFF_GA = OFF_VB + B_WIDTH
OFF_GB = OFF_GA + D_MODEL
PROJ_COLS = OFF_GB + D_MODEL

TOKEN_TILE = 512
MERGE_SUB = 256
FLASH_TQ = 256
FLASH_TK = 1024
FLASH_ITEMS = 16
DIL_QBLOCK = 128
DIL_STEP_TOKENS = 2048
MOE_ROWS = 1024
MOE_SUB_ROWS = 256

V7X_VMEM_BYTES = 64 * 1024 * 1024
VMEM_LIMIT = V7X_VMEM_BYTES * 13 // 16

HALF_MODEL = D_MODEL // 2
ROW_CHUNKS = HALF_MODEL // LANES
SC_CORES = 2
SC_SUBCORES = 16
SC_WORKERS = SC_CORES * SC_SUBCORES
SC_WINDOW = 128
SC_PAIR = 2

bf16 = jnp.bfloat16
f32 = jnp.float32


def _cparams(sem):
    return pltpu.CompilerParams(dimension_semantics=sem, vmem_limit_bytes=VMEM_LIMIT)


def _pack_rows(v):
    lo = lax.bitcast_convert_type(v[:, :HALF_MODEL].astype(bf16).astype(f32), jnp.uint32) >> 16
    hi = lax.bitcast_convert_type(v[:, HALF_MODEL:].astype(bf16).astype(f32), jnp.uint32) & jnp.uint32(0xFFFF0000)
    w = lo | hi
    return [w[:, c * LANES:(c + 1) * LANES] for c in range(ROW_CHUNKS)]


def _unpack_rows(chunks):
    w = jnp.concatenate(chunks, axis=1)
    lo = lax.bitcast_convert_type(w << 16, f32)
    hi = lax.bitcast_convert_type(w & jnp.uint32(0xFFFF0000), f32)
    return lo, hi


def _in_proj_kernel(x_ref, nm_ref, w_ref, gqa_ref, gka_ref, gqb_ref, gkb_ref,
                    ca_ref, sa_ref, cb_ref, sb_ref, bd_ref,
                    qa_ref, ka_ref, va_ref, qb1_ref, qb4_ref, qb16_ref, kb1_ref, kb4_ref, kb16_ref,
                    vb1_ref, vb4_ref, vb16_ref, ga_ref, gb_ref, st_ref):
    x = x_ref[...]
    r = lax.rsqrt(jnp.mean(x * x, axis=-1, keepdims=True) + EPS)
    u = (x * r * nm_ref[...]).astype(bf16)
    rows = x.shape[0]
    lane = lax.broadcasted_iota(jnp.int32, (rows, LANES), 1)
    first_half = (lane % HEAD_DIM) < (HEAD_DIM // 2)

    def proj(off, width):
        return jnp.dot(u, w_ref[:, off:off + width], preferred_element_type=f32)

    def norm_rope(off, width, g_ref, c_ref, s_ref):
        z = proj(off, width)
        c = c_ref[...]
        s = s_ref[...]
        chunks = []
        for c0 in range(0, width, NORM_COLS):
            cw = min(NORM_COLS, width - c0)
            zc = z[:, c0:c0 + cw]
            ms = jnp.dot((zc * zc).astype(bf16), bd_ref[:cw, :cw], preferred_element_type=f32) * (1.0 / HEAD_DIM)
            zn = zc * lax.rsqrt(ms + EPS) * g_ref[:, c0:c0 + cw]
            for h0 in range(0, cw, LANES):
                zh = zn[:, h0:h0 + LANES]
                sw = jnp.where(first_half, pltpu.roll(zh, LANES - HEAD_DIM // 2, 1), pltpu.roll(zh, HEAD_DIM // 2, 1))
                chunks.append(zh * c + sw * s)
        return chunks

    def store(chunks, out_ref):
        for j, ch in enumerate(chunks):
            out_ref[:, j * LANES:(j + 1) * LANES] = ch.astype(out_ref.dtype)

    def store_dilated(chunks, out_refs):
        for j, ch in enumerate(chunks):
            st_ref[j] = ch
        for (_, dil), out_ref in zip(WINDOW_DILATIONS, out_refs):
            for r in range(dil):
                for j in range(len(chunks)):
                    src = pl.ds(r, rows // dil, stride=dil) if dil > 1 else pl.ds(0, rows)
                    out_ref[r, :, j * LANES:(j + 1) * LANES] = st_ref[j, src, :].astype(out_ref.dtype)

    store(norm_rope(OFF_QA, A_WIDTH, gqa_ref, ca_ref, sa_ref), qa_ref)
    store(norm_rope(OFF_KA, KA_COLS, gka_ref, ca_ref, sa_ref), ka_ref)
    va = proj(OFF_VA, VA_COLS)
    va_ref[...] = jnp.concatenate(
        [jnp.where(lane < HEAD_DIM, va[:, c0:c0 + LANES], 1.0) for c0 in range(0, VA_COLS, LANES)],
        axis=1).astype(va_ref.dtype)
    store_dilated(norm_rope(OFF_QB, B_WIDTH, gqb_ref, cb_ref, sb_ref), (qb1_ref, qb4_ref, qb16_ref))
    store_dilated(norm_rope(OFF_KB, B_WIDTH, gkb_ref, cb_ref, sb_ref), (kb1_ref, kb4_ref, kb16_ref))
    vb = proj(OFF_VB, B_WIDTH)
    store_dilated([vb[:, c0:c0 + LANES] for c0 in range(0, B_WIDTH, LANES)], (vb1_ref, vb4_ref, vb16_ref))
    ga_ref[...] = jax.nn.sigmoid(proj(OFF_GA, D_MODEL)).astype(ga_ref.dtype)
    gb_ref[...] = jax.nn.sigmoid(proj(OFF_GB, D_MODEL)).astype(gb_ref.dtype)


def _in_proj(x2, S, prm):
    T = x2.shape[0]
    tm = TOKEN_TILE
    n_pos = S // tm
    row = lambda i: (i, 0)
    pos = lambda i: (i % n_pos, 0)
    const = lambda i: (0, 0)
    dils = [dil for _, dil in WINDOW_DILATIONS]
    flat = lambda w: (pl.BlockSpec((tm, w), row), jax.ShapeDtypeStruct((T, w), bf16))
    strided = lambda d: (pl.BlockSpec((d, tm // d, B_WIDTH), lambda i: (0, i, 0)),
                         jax.ShapeDtypeStruct((d, T // d, B_WIDTH), bf16))
    outs = ([flat(A_WIDTH), flat(KA_COLS), flat(VA_COLS)] + 3 * [strided(d) for d in dils]
            + [flat(D_MODEL), flat(D_MODEL)])
    return pl.pallas_call(
        _in_proj_kernel,
        grid=(T // tm,),
        in_specs=[
            pl.BlockSpec((tm, D_MODEL), row),
            pl.BlockSpec((1, D_MODEL), const),
            pl.BlockSpec((D_MODEL, PROJ_COLS), const),
            pl.BlockSpec((1, A_WIDTH), const),
            pl.BlockSpec((1, KA_COLS), const),
            pl.BlockSpec((1, B_WIDTH), const),
            pl.BlockSpec((1, B_WIDTH), const),
            pl.BlockSpec((tm, LANES), pos),
            pl.BlockSpec((tm, LANES), pos),
            pl.BlockSpec((tm, LANES), pos),
            pl.BlockSpec((tm, LANES), pos),
            pl.BlockSpec((NORM_COLS, NORM_COLS), const),
        ],
        out_specs=[spec for spec, _ in outs],
        out_shape=[shape for _, shape in outs],
        scratch_shapes=[pltpu.VMEM((B_WIDTH // LANES, tm, LANES), f32)],
        compiler_params=_cparams(("parallel",)),
        name="in_proj",
    )(x2, prm["norm_mix"], prm["w_in"], prm["gqa"], prm["gka"], prm["gqb"], prm["gkb"],
      prm["cos_a"][S], prm["sin_a"][S], prm["cos_b"][S], prm["sin_b"][S], prm["bd"])


def _flash_kernel(q_ref, k_ref, v_ref, o_ref, s_ref, *, tq, tk, seq, nq):
    lane = lax.broadcasted_iota(jnp.int32, (tq, LANES), 1)
    lo = lane < HEAD_DIM
    n_chunks = seq // tk

    def masked_heads(b):
        q = q_ref[0, b * tq:(b + 1) * tq, :]
        q01, q23 = q[:, :LANES], q[:, LANES:]
        zero = jnp.zeros_like(q01)
        return [jnp.where(lo, q01, zero), jnp.where(lo, zero, q01), jnp.where(lo, q23, zero), jnp.where(lo, zero, q23)]

    def score(qh, chunk):
        kk = k_ref[0, chunk * tk:(chunk + 1) * tk, :]
        return lax.dot_general(qh, kk, (((1,), (1,)), ((), ())), preferred_element_type=f32)

    qs = [masked_heads(b) for b in range(nq)]
    items = [(b, c) for b in range(nq) for c in range(n_chunks)]
    for h in range(GROUP_Q):
        s_ref[0, h] = score(qs[0][h], 0)
    ms = accs = None
    for n, (b, chunk) in enumerate(items):
        slot = n % 2
        nxt = items[n + 1] if n + 1 < len(items) else None
        if chunk == 0:
            ms = [jnp.full((tq, 1), NEG_BIG, f32) for _ in range(GROUP_Q)]
            accs = [jnp.zeros((tq, LANES), f32) for _ in range(GROUP_Q)]
        vv = v_ref[0, chunk * tk:(chunk + 1) * tk, :]
        for h in range(GROUP_Q):
            if nxt is not None:
                s_ref[1 - slot, h] = score(qs[nxt[0]][h], nxt[1])
            s = s_ref[slot, h]
            m_new = jnp.maximum(ms[h], jnp.max(s, axis=-1, keepdims=True))
            p = jnp.exp2((s - m_new).astype(bf16))
            alpha = jnp.exp2(ms[h] - m_new)
            accs[h] = accs[h] * alpha + jnp.dot(p, vv, preferred_element_type=f32)
            ms[h] = m_new
        if chunk == n_chunks - 1:
            heads = [acc / jnp.where(lo, pltpu.roll(acc, HEAD_DIM, 1), 1.0) for acc in accs]
            out01 = jnp.where(lo, heads[0], pltpu.roll(heads[1], HEAD_DIM, 1))
            out23 = jnp.where(lo, heads[2], pltpu.roll(heads[3], HEAD_DIM, 1))
            o_ref[0, b * tq:(b + 1) * tq, :] = jnp.concatenate([out01, out23], axis=1).astype(o_ref.dtype)


def _flash(qa, ka, va):
    B, S, _ = qa.shape
    tq, tk = FLASH_TQ, FLASH_TK
    nq = max(1, FLASH_ITEMS // (S // tk))
    return pl.pallas_call(
        functools.partial(_flash_kernel, tq=tq, tk=tk, seq=S, nq=nq),
        grid=(B, A_KV_HEADS, S // (nq * tq)),
        in_specs=[
            pl.BlockSpec((1, nq * tq, GROUP_Q * HEAD_DIM), lambda b, g, i: (b, i, g)),
            pl.BlockSpec((1, S, LANES), lambda b, g, i: (b, 0, g)),
            pl.BlockSpec((1, S, LANES), lambda b, g, i: (b, 0, g)),
        ],
        out_specs=pl.BlockSpec((1, nq * tq, GROUP_Q * HEAD_DIM), lambda b, g, i: (b, i, g)),
        out_shape=jax.ShapeDtypeStruct((B, S, A_WIDTH), bf16),
        scratch_shapes=[pltpu.VMEM((2, GROUP_Q, tq, tk), f32)],
        compiler_params=_cparams(("parallel", "parallel", "arbitrary")),
        name="flash_gqa",
    )(qa, ka, va)


def _dilated_kernel(q_ref, k_ref, v_ref, o_ref, lse_ref, so_ref, sl_ref, *, length, tms, dil):
    i = pl.program_id(1)
    lane = lax.broadcasted_iota(jnp.int32, (DIL_QBLOCK, LANES), 1)
    lo = lane < HEAD_DIM
    n_pairs = B_WIDTH // LANES
    ones = jnp.ones((KEY_SPAN, LANES), bf16)

    def unit(u, carry):
        sb, r = u // dil, u % dil
        r0 = pl.multiple_of(sb * DIL_QBLOCK, DIL_QBLOCK)
        m0 = i * tms + r0
        start = pl.multiple_of(jnp.clip(m0 - HALF_WINDOW, 0, length - KEY_SPAN), HALF_WINDOW)
        qpos = m0 + lax.broadcasted_iota(jnp.int32, (DIL_QBLOCK, KEY_SPAN), 0)
        kpos = start + lax.broadcasted_iota(jnp.int32, (DIL_QBLOCK, KEY_SPAN), 1)
        valid = jnp.abs(kpos - qpos) <= HALF_WINDOW
        rows = pl.ds(r0 * dil + r, DIL_QBLOCK, stride=dil) if dil > 1 else pl.ds(r0, DIL_QBLOCK)
        scores = []
        for c in range(n_pairs):
            cs = slice(c * LANES, (c + 1) * LANES)
            qp = q_ref[r, 0, pl.ds(r0, DIL_QBLOCK), cs]
            kp = k_ref[r, 0, pl.ds(start, KEY_SPAN), cs]
            zero = jnp.zeros_like(qp)
            for half in range(2):
                qm = jnp.where(lo, qp, zero) if half == 0 else jnp.where(lo, zero, qp)
                scores.append(lax.dot_general(qm, kp, (((1,), (1,)), ((), ())), preferred_element_type=f32))
        for c in range(n_pairs):
            vp_ones = jnp.concatenate([v_ref[r, 0, pl.ds(start, KEY_SPAN), c * LANES:(c + 1) * LANES], ones], axis=1)
            ols, ms = [], []
            for half in range(2):
                s = jnp.where(valid, scores[2 * c + half], NEG_BIG)
                m = jnp.max(s, axis=-1, keepdims=True)
                e = jnp.exp2((s - m).astype(bf16))
                ols.append(jnp.dot(e, vp_ones, preferred_element_type=f32))
                ms.append(m)
            num = jnp.where(lo, ols[0][:, :LANES], ols[1][:, :LANES])
            den = jnp.where(lo, ols[0][:, LANES:], ols[1][:, LANES:])
            so_ref[c, rows, :] = num / den
            sl_ref[c, rows, :] = jnp.where(lo, ms[0], ms[1]) + jnp.log2(den)
        return carry

    lax.fori_loop(0, (tms // DIL_QBLOCK) * dil, unit, 0, unroll=8)
    o_ref[0] = jnp.concatenate([so_ref[c] for c in range(n_pairs)], axis=1).astype(o_ref.dtype)
    lse_ref[0] = jnp.concatenate([sl_ref[c] for c in range(n_pairs)], axis=1)


def _dilated(qb, kb, vb, B, S, dil):
    length = S // dil
    tms = DIL_STEP_TOKENS // dil
    view = lambda a: a.reshape(dil, B, length, B_WIDTH)
    o, lse = pl.pallas_call(
        functools.partial(_dilated_kernel, length=length, tms=tms, dil=dil),
        grid=(B, length // tms),
        in_specs=[
            pl.BlockSpec((dil, 1, tms, B_WIDTH), lambda b, i: (0, b, i, 0)),
            pl.BlockSpec((dil, 1, length, B_WIDTH), lambda b, i: (0, b, 0, 0)),
            pl.BlockSpec((dil, 1, length, B_WIDTH), lambda b, i: (0, b, 0, 0)),
        ],
        out_specs=[
            pl.BlockSpec((1, tms * dil, B_WIDTH), lambda b, i: (b, i, 0)),
            pl.BlockSpec((1, tms * dil, B_WIDTH), lambda b, i: (b, i, 0)),
        ],
        out_shape=[
            jax.ShapeDtypeStruct((B, S, B_WIDTH), bf16),
            jax.ShapeDtypeStruct((B, S, B_WIDTH), f32),
        ],
        scratch_shapes=[pltpu.VMEM((B_WIDTH // LANES, tms * dil, LANES), f32),
                        pltpu.VMEM((B_WIDTH // LANES, tms * dil, LANES), f32)],
        compiler_params=_cparams(("parallel", "arbitrary")),
        name=f"dilated_{dil}",
    )(view(qb), view(kb), view(vb))
    return o.reshape(B * S, B_WIDTH), lse.reshape(B * S, B_WIDTH)


def _first_index_of_max(cur, idx, n):
    mx = jnp.max(cur, axis=0, keepdims=True)
    return jnp.min(jnp.where(cur == mx, idx, n), axis=0, keepdims=True)


def _merge_route_kernel(x_ref, ya_ref, o1_ref, o2_ref, o3_ref, l1_ref, l2_ref, l3_ref, ga_ref, gb_ref,
                        wa_ref, wb_ref, wo_ref, nf_ref, rhi_ref, rlo_ref, bias_ref, tri_ref,
                        h_ref, u_ref, idx_ref, wt_ref, rank_ref, cnt_ref, run_ref, lg_ref):
    step = pl.program_id(0)

    @pl.when(step == 0)
    def _():
        run_ref[...] = jnp.zeros_like(run_ref)
        lg_ref[...] = jnp.zeros_like(lg_ref)

    routing_live = step > 0
    cur, prev = step % 2, (step + 1) % 2

    tm = x_ref.shape[0]
    sub = tri_ref.shape[0]

    def dense(j):
        rs = slice(j * sub, (j + 1) * sub)
        l1, l2, l3 = l1_ref[rs, :], l2_ref[rs, :], l3_ref[rs, :]
        lm = jnp.maximum(jnp.maximum(l1, l2), l3)
        e1, e2, e3 = jnp.exp2(l1 - lm), jnp.exp2(l2 - lm), jnp.exp2(l3 - lm)
        yb = ((e1 * o1_ref[rs, :].astype(f32) + e2 * o2_ref[rs, :].astype(f32) + e3 * o3_ref[rs, :].astype(f32))
              / (e1 + e2 + e3))
        pa = jnp.dot(ya_ref[rs, :], wa_ref[...], preferred_element_type=f32)
        pb = jnp.dot(yb.astype(bf16), wb_ref[...], preferred_element_type=f32)
        merged = ga_ref[rs, :].astype(f32) * pa + gb_ref[rs, :].astype(f32) * pb
        h = x_ref[rs, :] + jnp.dot(merged.astype(bf16), wo_ref[...], preferred_element_type=f32)
        h_ref[rs, :] = h
        r = lax.rsqrt(jnp.mean(h * h, axis=-1, keepdims=True) + EPS)
        uf = h * r * nf_ref[...]
        u_hi = uf.astype(bf16)
        for c, chunk in enumerate(_pack_rows(uf)):
            u_ref[c, rs, :] = chunk
        u_lo = (uf - u_hi.astype(f32)).astype(bf16)
        return (jnp.dot(u_hi, rhi_ref[...], preferred_element_type=f32)
                + jnp.dot(u_lo, rhi_ref[...], preferred_element_type=f32)
                + jnp.dot(u_hi, rlo_ref[...], preferred_element_type=f32))

    def route(j, logits):
        scores = jax.nn.sigmoid(logits.T[:N_EXPERTS, :])
        sel = scores + bias_ref[...]
        i8 = lax.broadcasted_iota(jnp.int32, (EXPERTS_PER_GROUP, sub), 0)
        neg_inf = jnp.float32(-jnp.inf)
        gscore = jnp.zeros((N_GROUPS, sub), f32)
        for g in range(N_GROUPS):
            blk = sel[g * EXPERTS_PER_GROUP:(g + 1) * EXPERTS_PER_GROUP, :]
            top1 = jnp.max(blk, axis=0, keepdims=True)
            first = _first_index_of_max(blk, i8, EXPERTS_PER_GROUP)
            top2 = jnp.max(jnp.where(i8 == first, neg_inf, blk), axis=0, keepdims=True)
            gscore = jnp.where(i8 == g, top1 + top2, gscore)
        gkeep = jnp.zeros((N_GROUPS, sub), jnp.bool_)
        cur = gscore
        for _ in range(TOPK_GROUPS):
            pick = i8 == _first_index_of_max(cur, i8, N_GROUPS)
            gkeep = jnp.logical_or(gkeep, pick)
            cur = jnp.where(pick, neg_inf, cur)
        cur = jnp.concatenate(
            [jnp.where(gkeep[g:g + 1, :], sel[g * EXPERTS_PER_GROUP:(g + 1) * EXPERTS_PER_GROUP, :], NEG_BIG)
             for g in range(N_GROUPS)], axis=0)

        ie = lax.broadcasted_iota(jnp.int32, (N_EXPERTS, sub), 0)
        ik = lax.broadcasted_iota(jnp.int32, (TOP_K, sub), 0)
        idx_out = jnp.zeros((TOP_K, sub), jnp.int32)
        w_out = jnp.zeros((TOP_K, sub), f32)
        chosen = jnp.zeros((N_EXPERTS, sub), jnp.bool_)
        for k in range(TOP_K):
            fi = _first_index_of_max(cur, ie, N_EXPERTS)
            pick = ie == fi
            wk = jnp.sum(jnp.where(pick, scores, 0.0), axis=0, keepdims=True)
            idx_out = jnp.where(ik == k, fi, idx_out)
            w_out = jnp.where(ik == k, wk, w_out)
            chosen = jnp.logical_or(chosen, pick)
            cur = jnp.where(pick, neg_inf, cur)
        w_out = w_out / jnp.sum(w_out, axis=0, keepdims=True) * ROUTE_SCALE
        wt_ref[:, j * sub:(j + 1) * sub] = w_out

        sel01 = jnp.where(chosen, 1.0, 0.0)
        before = jnp.dot(sel01.astype(bf16), tri_ref[...], preferred_element_type=f32) + run_ref[:, 0:1]
        rank_out = jnp.zeros((TOP_K, sub), f32)
        for k in range(TOP_K):
            rk = jnp.sum(jnp.where(ie == idx_out[k:k + 1, :], before, 0.0), axis=0, keepdims=True)
            rank_out = jnp.where(ik == k, rk, rank_out)
        rank_i = rank_out.astype(jnp.int32)
        for w in range(sub // SC_WINDOW):
            win = j * (sub // SC_WINDOW) + w
            idx_ref[win] = idx_out[:, w * SC_WINDOW:(w + 1) * SC_WINDOW]
            rank_ref[win] = rank_i[:, w * SC_WINDOW:(w + 1) * SC_WINDOW]
        run_ref[...] = run_ref[...] + jnp.where(routing_live, jnp.sum(sel01, axis=1, keepdims=True), 0.0)

    n_sub = tm // sub
    prev_logits = [lg_ref[prev, j * sub:(j + 1) * sub, :] for j in range(n_sub)]
    for j in range(n_sub):
        lg_ref[cur, j * sub:(j + 1) * sub, :] = dense(j)
    for j in range(n_sub):
        route(j, prev_logits[j])
    cnt_ref[...] = run_ref[...].astype(jnp.int32)


def _merge_route(x2, ya, outs, lses, sga, sgb, prm):
    T = x2.shape[0]
    tm = TOKEN_TILE
    n_tiles = T // tm
    dense_i = lambda i: jnp.minimum(i, n_tiles - 1)
    route_i = lambda i: jnp.maximum(i - 1, 0)
    const = lambda i: (0, 0)
    tok = lambda w: pl.BlockSpec((tm, w), lambda i: (dense_i(i), 0))
    full = lambda a: pl.BlockSpec(a.shape, const)
    win = pl.BlockSpec((tm // SC_WINDOW, TOP_K, SC_WINDOW), lambda i: (route_i(i), 0, 0))
    return pl.pallas_call(
        _merge_route_kernel,
        grid=(n_tiles + 1,),
        in_specs=[tok(D_MODEL), tok(A_WIDTH), tok(B_WIDTH), tok(B_WIDTH), tok(B_WIDTH),
                  tok(B_WIDTH), tok(B_WIDTH), tok(B_WIDTH), tok(D_MODEL), tok(D_MODEL),
                  full(prm["w_branch_a"]), full(prm["w_branch_b"]), full(prm["w_out"]), full(prm["norm_ffn"]),
                  full(prm["router_hi"]), full(prm["router_lo"]), full(prm["router_bias"]), full(prm["tri"])],
        out_specs=[tok(D_MODEL), pl.BlockSpec((ROW_CHUNKS, tm, LANES), lambda i: (0, dense_i(i), 0)),
                   win, pl.BlockSpec((TOP_K, tm), lambda i: (0, route_i(i))), win,
                   pl.BlockSpec((N_EXPERTS, LANES), const)],
        out_shape=[jax.ShapeDtypeStruct((T, D_MODEL), f32), jax.ShapeDtypeStruct((ROW_CHUNKS, T, LANES), jnp.uint32),
                   jax.ShapeDtypeStruct((T // SC_WINDOW, TOP_K, SC_WINDOW), jnp.int32),
                   jax.ShapeDtypeStruct((TOP_K, T), f32),
                   jax.ShapeDtypeStruct((T // SC_WINDOW, TOP_K, SC_WINDOW), jnp.int32),
                   jax.ShapeDtypeStruct((N_EXPERTS, LANES), jnp.int32)],
        scratch_shapes=[pltpu.VMEM((N_EXPERTS, LANES), f32), pltpu.VMEM((2, tm, LANES), f32)],
        compiler_params=_cparams(("arbitrary",)),
        name="merge_route",
    )(x2, ya, *outs, *lses, sga, sgb,
      prm["w_branch_a"], prm["w_branch_b"], prm["w_out"], prm["norm_ffn"],
      prm["router_hi"], prm["router_lo"], prm["router_bias"], prm["tri"])


def _experts_kernel(be_ref, nu_ref, nv_ref, x_ref, wg_ref, wu_ref, wd_ref, y_ref):
    del be_ref
    i = pl.program_id(0)

    @pl.when(i < nu_ref[0])
    def _():
        n_sub = x_ref.shape[1] // MOE_SUB_ROWS

        def up(j):
            r0 = j * MOE_SUB_ROWS
            lo, hi = _unpack_rows([x_ref[c, r0:r0 + MOE_SUB_ROWS, :] for c in range(ROW_CHUNKS)])
            live = r0 + lax.broadcasted_iota(jnp.int32, lo.shape, 0) < nv_ref[i]
            x = jnp.concatenate([jnp.where(live, lo, 0.0), jnp.where(live, hi, 0.0)], axis=1).astype(bf16)
            return (jnp.dot(x, wg_ref[0], preferred_element_type=f32),
                    jnp.dot(x, wu_ref[0], preferred_element_type=f32))

        def down(j, gu):
            g, u = gu
            a = g * jax.nn.sigmoid(g) * u
            y = jnp.dot(a.astype(bf16), wd_ref[0], preferred_element_type=f32)
            r0 = j * MOE_SUB_ROWS
            for c, chunk in enumerate(_pack_rows(y)):
                y_ref[c, r0:r0 + MOE_SUB_ROWS, :] = chunk

        gu_prev = up(0)
        for j in range(1, n_sub):
            gu_next = up(j)
            down(j - 1, gu_prev)
            gu_prev = gu_next
        down(n_sub - 1, gu_prev)

    @pl.when(i >= nu_ref[0])
    def _():
        y_ref[...] = jnp.zeros_like(y_ref)


def _experts(xs, blk_e, n_used, n_valid, prm):
    n_rows = xs.shape[1]
    bm = MOE_ROWS
    rows = lambda i, be, nu, nv: (0, jnp.minimum(i, nu[0] - 1), 0)
    return pl.pallas_call(
        _experts_kernel,
        grid_spec=pltpu.PrefetchScalarGridSpec(
            num_scalar_prefetch=3,
            grid=(n_rows // bm,),
            in_specs=[
                pl.BlockSpec((ROW_CHUNKS, bm, LANES), rows),
                pl.BlockSpec((1, D_MODEL, D_EXPERT), lambda i, be, nu, nv: (be[i], 0, 0)),
                pl.BlockSpec((1, D_MODEL, D_EXPERT), lambda i, be, nu, nv: (be[i], 0, 0)),
                pl.BlockSpec((1, D_EXPERT, D_MODEL), lambda i, be, nu, nv: (be[i], 0, 0)),
            ],
            out_specs=pl.BlockSpec((ROW_CHUNKS, bm, LANES), lambda i, be, nu, nv: (0, i, 0)),
        ),
        out_shape=jax.ShapeDtypeStruct((ROW_CHUNKS, n_rows, LANES), jnp.uint32),
        compiler_params=_cparams(("arbitrary",)),
        name="experts",
    )(blk_e, n_used, n_valid, xs, prm["expert_gate"], prm["expert_up"], prm["expert_down"])


def _sc_mesh():
    return plsc.VectorSubcoreMesh(core_axis_name="c", subcore_axis_name="s")


def _sc_dispatch(u2p, dest, n_rows):
    T = u2p.shape[1]
    per_worker = T // SC_WORKERS
    n_it = per_worker // SC_WINDOW

    @functools.partial(
        pl.kernel, mesh=_sc_mesh(),
        out_type=jax.ShapeDtypeStruct((ROW_CHUNKS, n_rows, LANES), jnp.uint32),
        scratch_types=[pltpu.VMEM((TOP_K, SC_WINDOW), jnp.int32),
                       pltpu.VMEM((ROW_CHUNKS, SC_WINDOW, LANES), jnp.uint32),
                       pltpu.SemaphoreType.DMA, pltpu.SemaphoreType.DMA],
    )
    def k(u_hbm, dest_hbm, xs_hbm, idx_v, rows_v, sem_in, sem_out):
        wid = lax.axis_index("s") * SC_CORES + lax.axis_index("c")

        @pl.loop(0, n_it)
        def _(it):
            t0 = wid * per_worker + it * SC_WINDOW
            pltpu.sync_copy(dest_hbm.at[wid * n_it + it], idx_v)
            loads = [pltpu.async_copy(u_hbm.at[c].at[pl.ds(t0, SC_WINDOW)], rows_v.at[c], sem_in)
                     for c in range(ROW_CHUNKS)]
            for cp in loads:
                cp.wait()
            stores = [pltpu.async_copy(rows_v.at[c], xs_hbm.at[c].at[idx_v.at[kk]], sem_out)
                      for kk in range(TOP_K) for c in range(ROW_CHUNKS)]
            for cp in stores:
                cp.wait()

    return k(u2p, dest)


def _sc_combine(ys, dest):
    n_windows = dest.shape[0]
    T = n_windows * SC_WINDOW
    per_worker = T // SC_WORKERS
    n_it = per_worker // SC_WINDOW

    items = [(c, k0) for c in range(ROW_CHUNKS) for k0 in range(0, TOP_K, SC_PAIR)]

    @functools.partial(
        pl.kernel, mesh=_sc_mesh(),
        out_type=jax.ShapeDtypeStruct((TOP_K, ROW_CHUNKS, T, LANES), jnp.uint32),
        scratch_types=[pltpu.VMEM((TOP_K, SC_WINDOW), jnp.int32),
                       pltpu.VMEM((2, SC_PAIR, SC_WINDOW, LANES), jnp.uint32),
                       pltpu.SemaphoreType.DMA, pltpu.SemaphoreType.DMA,
                       pltpu.SemaphoreType.DMA, pltpu.SemaphoreType.DMA],
    )
    def k(ys_hbm, dest_hbm, yg_hbm, idx_v, rows_v, sem_in0, sem_in1, sem_out0, sem_out1):
        sem_in, sem_out = (sem_in0, sem_in1), (sem_out0, sem_out1)
        wid = lax.axis_index("s") * SC_CORES + lax.axis_index("c")

        @pl.loop(0, n_it)
        def _(it):
            t0 = wid * per_worker + it * SC_WINDOW
            pltpu.sync_copy(dest_hbm.at[wid * n_it + it], idx_v)

            def gathers(n):
                c, k0 = items[n]
                return [pltpu.async_copy(ys_hbm.at[c].at[idx_v.at[k0 + j]], rows_v.at[n % 2, j], sem_in[n % 2])
                        for j in range(SC_PAIR)]

            def stores(n):
                c, k0 = items[n]
                return [pltpu.async_copy(rows_v.at[n % 2, j], yg_hbm.at[k0 + j].at[c].at[pl.ds(t0, SC_WINDOW)],
                                         sem_out[n % 2]) for j in range(SC_PAIR)]

            pending_gathers = gathers(0)
            pending_stores = {}
            for n in range(len(items)):
                for cp in pending_gathers:
                    cp.wait()
                if n + 1 < len(items):
                    for cp in pending_stores.pop((n + 1) % 2, []):
                        cp.wait()
                    pending_gathers = gathers(n + 1)
                pending_stores[n % 2] = stores(n)
            for copies in pending_stores.values():
                for cp in copies:
                    cp.wait()

    return k(ys, dest)


def _final_kernel(h_ref, u_ref, yg_ref, wt_ref, p_ref, sgu_ref, sd_ref, pg_ref, pp_ref, o_ref):
    wl = wt_ref[...]
    wt = jnp.concatenate([wl, jnp.zeros((LANES - TOP_K, wl.shape[1]), f32)], axis=0).T
    r_lo = r_hi = None
    for k in range(TOP_K):
        lo, hi = _unpack_rows([yg_ref[k, c] for c in range(ROW_CHUNKS)])
        wk = wt[:, k:k + 1]
        r_lo = lo * wk if k == 0 else r_lo + lo * wk
        r_hi = hi * wk if k == 0 else r_hi + hi * wk
    routed = jnp.concatenate([r_lo, r_hi], axis=1)
    u_lo, u_hi = _unpack_rows([u_ref[c] for c in range(ROW_CHUNKS)])
    un = jnp.concatenate([u_lo, u_hi], axis=1).astype(bf16)
    gu = jnp.dot(un, sgu_ref[...], preferred_element_type=f32)
    g, u = gu[:, :D_SHARED], gu[:, D_SHARED:]
    shared = jnp.dot((g * jax.nn.sigmoid(g) * u).astype(bf16), sd_ref[...], preferred_element_type=f32)
    h = h_ref[...] + (routed + shared)
    gate = jax.nn.sigmoid(jnp.dot(h.astype(bf16), pg_ref[...], preferred_element_type=f32))
    emb = jnp.dot(p_ref[...].astype(bf16), pp_ref[...], preferred_element_type=f32)
    o_ref[...] = h + gate * emb


def _final(h1, u2p, yg, wts, p2, prm):
    T = h1.shape[0]
    tm = TOKEN_TILE
    row = lambda i: (i, 0)
    const = lambda i: (0, 0)
    full = lambda a: pl.BlockSpec(a.shape, const)
    return pl.pallas_call(
        _final_kernel,
        grid=(T // tm,),
        in_specs=[pl.BlockSpec((tm, D_MODEL), row),
                  pl.BlockSpec((ROW_CHUNKS, tm, LANES), lambda i: (0, i, 0)),
                  pl.BlockSpec((TOP_K, ROW_CHUNKS, tm, LANES), lambda i: (0, 0, i, 0)),
                  pl.BlockSpec((TOP_K, tm), lambda i: (0, i)),
                  pl.BlockSpec((tm, PLE_DIM), row),
                  full(prm["shared_gu"]), full(prm["shared_down"]), full(prm["ple_gate"]), full(prm["ple_proj"])],
        out_specs=pl.BlockSpec((tm, D_MODEL), row),
        out_shape=jax.ShapeDtypeStruct((T, D_MODEL), f32),
        compiler_params=_cparams(("parallel",)),
        name="final",
    )(h1, u2p, yg, wts, p2, prm["shared_gu"], prm["shared_down"], prm["ple_gate"], prm["ple_proj"])


def _deinterleave(n_heads):
    base = jnp.concatenate([jnp.arange(0, HEAD_DIM, 2), jnp.arange(1, HEAD_DIM, 2)])
    return (jnp.arange(n_heads)[:, None] * HEAD_DIM + base[None, :]).reshape(-1)


def _rope_tables(ang):
    c, s = jnp.cos(ang), jnp.sin(ang)
    return jnp.tile(jnp.concatenate([c, c], axis=-1), (1, 2)), jnp.tile(jnp.concatenate([-s, s], axis=-1), (1, 2))


def _axial_angles(S):
    rows = S // GRID_W
    row = jnp.repeat(jnp.arange(rows, dtype=f32), GRID_W)
    col = jnp.tile(jnp.arange(GRID_W, dtype=f32), rows)
    half = HEAD_DIM // 2
    inv = ROPE_THETA ** (-jnp.arange(0, half, 2, dtype=f32) / half)
    return jnp.concatenate([row[:, None] * inv, col[:, None] * inv], axis=-1)


def _linear_angles(S):
    t = jnp.arange(S, dtype=f32)
    inv = ROPE_THETA ** (-jnp.arange(0, HEAD_DIM, 2, dtype=f32) / HEAD_DIM)
    return t[:, None] * inv


def _prepare(seq_lens, norm_mix, w_in, q_norm_a, k_norm_a, q_norm_b, k_norm_b, w_branch_a, w_branch_b, w_out,
             norm_ffn, router_w, router_bias, expert_gate, expert_up, expert_down, shared_gate, shared_up,
             shared_down, ple_proj, ple_gate):
    cuts = [A_WIDTH, A_WIDTH + A_KV_WIDTH, A_WIDTH + 2 * A_KV_WIDTH,
            A_WIDTH + 2 * A_KV_WIDTH + B_WIDTH, A_WIDTH + 2 * A_KV_WIDTH + 2 * B_WIDTH,
            A_WIDTH + 2 * A_KV_WIDTH + 3 * B_WIDTH, A_WIDTH + 2 * A_KV_WIDTH + 3 * B_WIDTH + D_MODEL]
    wqa, wka, wva, wqb, wkb, wvb, wga, wgb = jnp.split(w_in, cuts, axis=-1)
    wqa = wqa[:, _deinterleave(A_Q_HEADS)]
    wka = wka[:, _deinterleave(A_KV_HEADS)]
    wqb = wqb[:, _deinterleave(B_HEADS)]
    wkb = wkb[:, _deinterleave(B_HEADS)]
    k0, k1 = wka[:, :HEAD_DIM], wka[:, HEAD_DIM:]
    v0, v1 = wva[:, :HEAD_DIM], wva[:, HEAD_DIM:]
    vz = jnp.zeros_like(v0)
    w_all = jnp.concatenate([wqa, k0, k0, k1, k1, v0, vz, v1, vz, wqb, wkb, wvb, wga, wgb], axis=-1).astype(bf16)
    perm = _deinterleave(1)
    scale = HEAD_DIM ** -0.5
    tile = lambda g, n: jnp.tile(g[perm], n)[None, :]
    router_pad = jnp.pad(router_w, ((0, 0), (0, LANES - N_EXPERTS)))
    router_hi = router_pad.astype(bf16)
    prm = {
        "norm_mix": norm_mix[None, :], "w_in": w_all,
        "gqa": tile(q_norm_a, A_Q_HEADS) * (scale * LOG2_E), "gka": tile(k_norm_a, KA_COLS // HEAD_DIM),
        "gqb": tile(q_norm_b, B_HEADS) * (scale * LOG2_E), "gkb": tile(k_norm_b, B_HEADS),
        "bd": jnp.kron(jnp.eye(NORM_COLS // HEAD_DIM, dtype=f32), jnp.ones((HEAD_DIM, HEAD_DIM), f32)).astype(bf16),
        "w_branch_a": w_branch_a.astype(bf16), "w_branch_b": w_branch_b.astype(bf16), "w_out": w_out.astype(bf16),
        "norm_ffn": norm_ffn[None, :],
        "router_hi": router_hi, "router_lo": (router_pad - router_hi.astype(f32)).astype(bf16),
        "router_bias": router_bias[:, None],
        "tri": (jnp.arange(MERGE_SUB)[:, None] < jnp.arange(MERGE_SUB)[None, :]).astype(bf16),
        "expert_gate": expert_gate.astype(bf16), "expert_up": expert_up.astype(bf16),
        "expert_down": expert_down.astype(bf16),
        "shared_gu": jnp.concatenate([shared_gate, shared_up], axis=-1).astype(bf16),
        "shared_down": shared_down.astype(bf16),
        "ple_gate": ple_gate.astype(bf16), "ple_proj": ple_proj.astype(bf16),
        "cos_a": {}, "sin_a": {}, "cos_b": {}, "sin_b": {},
    }
    for S in seq_lens:
        prm["cos_a"][S], prm["sin_a"][S] = _rope_tables(_axial_angles(S))
        prm["cos_b"][S], prm["sin_b"][S] = _rope_tables(_linear_angles(S))
    return prm


def _dispatch_plan(idx, rank, counts, T):
    bm = MOE_ROWS
    n_rows = (T * TOP_K + N_EXPERTS * bm) // bm * bm
    n_blocks = n_rows // bm
    padded = (counts + bm - 1) // bm * bm
    pad_end = jnp.cumsum(padded)
    pad_start = pad_end - padded
    dest = pad_start[idx] + rank
    blk_start = jnp.arange(n_blocks, dtype=jnp.int32) * bm
    blk_e = jnp.minimum(jnp.sum(pad_end[None, :] <= blk_start[:, None], axis=1), N_EXPERTS - 1).astype(jnp.int32)
    n_used = (pad_end[-1:] // bm).astype(jnp.int32)
    own = blk_e[:, None] == jnp.arange(N_EXPERTS, dtype=jnp.int32)[None, :]
    real_end = jnp.sum(jnp.where(own, (pad_start + counts)[None, :], 0), axis=1)
    n_valid = jnp.clip(real_end - blk_start, 0, bm).astype(jnp.int32)
    return dest, n_rows, blk_e, n_used, n_valid


def _layer(x, p, prm):
    B, S, _ = x.shape
    T = B * S
    x2 = x.reshape(T, D_MODEL)
    qa, ka, va, *qkv_b, sga, sgb = _in_proj(x2, S, prm)
    shape3 = lambda a: a.reshape(B, S, a.shape[-1])
    ya = _flash(shape3(qa), shape3(ka), shape3(va)).reshape(T, A_WIDTH)
    n_dil = len(WINDOW_DILATIONS)
    outs, lses = [], []
    for j, (_, dil) in enumerate(WINDOW_DILATIONS):
        o_d, lse_d = _dilated(qkv_b[j], qkv_b[n_dil + j], qkv_b[2 * n_dil + j], B, S, dil)
        outs.append(o_d)
        lses.append(lse_d)
    h1, u2p, idx, wts, rank, cnt = _merge_route(x2, ya, outs, lses, sga, sgb, prm)
    dest, n_rows, blk_e, n_used, n_valid = _dispatch_plan(idx, rank, cnt[:, 0], T)
    xs = _sc_dispatch(u2p, dest, n_rows)
    ys = _experts(xs, blk_e, n_used, n_valid, prm)
    yg = _sc_combine(ys, dest)
    out = _final(h1, u2p, yg, wts, p.reshape(T, PLE_DIM), prm)
    return out.reshape(B, S, D_MODEL)


def kernel(x_prompt, x_sample, p_prompt, p_sample, norm_mix, w_in, q_norm_a, k_norm_a, q_norm_b, k_norm_b,
           w_branch_a, w_branch_b, w_out, norm_ffn, router_w, router_bias, expert_gate, expert_up, expert_down,
           shared_gate, shared_up, shared_down, ple_proj, ple_gate):
    weights = (norm_mix, w_in, q_norm_a, k_norm_a, q_norm_b, k_norm_b, w_branch_a, w_branch_b, w_out, norm_ffn,
               router_w, router_bias, expert_gate, expert_up, expert_down, shared_gate, shared_up, shared_down,
               ple_proj, ple_gate)
    depth = norm_mix.shape[0]
    seq_lens = sorted({x_prompt.shape[1], x_sample.shape[1]})
    hp, hs = x_prompt, x_sample
    for i in range(depth):
        prm = _prepare(seq_lens, *[w[i] for w in weights])
        hp = _layer(hp, p_prompt[i], prm)
        hs = _layer(hs, p_sample[i], prm)
    return hp, hs
```

```python
import functools

import jax
import jax.numpy as jnp
from jax import lax
from jax.experimental import pallas as pl
from jax.experimental.pallas import tpu as pltpu
from jax.experimental.pallas import tpu_sc as plsc

D_MODEL = 1024
HEAD_DIM = 64
A_Q_HEADS = 8
A_KV_HEADS = 2
B_HEADS = 6
GRID_W = 64
ROPE_THETA = 10000.0
WINDOW_DILATIONS = ((128, 1), (512, 4), (2048, 16))
N_EXPERTS = 64
TOP_K = 8
N_GROUPS = 8
TOPK_GROUPS = 4
D_EXPERT = 256
D_SHARED = 256
ROUTE_SCALE = 2.5
PLE_DIM = 256
NEG_BIG = -1e30
EPS = 1e-6
LOG2_E = 1.4426950408889634

A_WIDTH = A_Q_HEADS * HEAD_DIM
A_KV_WIDTH = A_KV_HEADS * HEAD_DIM
B_WIDTH = B_HEADS * HEAD_DIM
GROUP_Q = A_Q_HEADS // A_KV_HEADS
EXPERTS_PER_GROUP = N_EXPERTS // N_GROUPS

LANES = 128
NORM_COLS = 256
HALF_WINDOW = 64
KEY_SPAN = 256

KA_COLS = 2 * LANES
VA_COLS = 2 * LANES
OFF_QA = 0
OFF_KA = OFF_QA + A_WIDTH
OFF_VA = OFF_KA + KA_COLS
OFF_QB = OFF_VA + VA_COLS
OFF_KB = OFF_QB + B_WIDTH
OFF_VB = OFF_KB + B_WIDTH
OFF_GA = OFF_VB + B_WIDTH
OFF_GB = OFF_GA + D_MODEL
PROJ_COLS = OFF_GB + D_MODEL

TOKEN_TILE = 512
MERGE_SUB = 256
FLASH_TQ = 256
FLASH_TK = 1024
FLASH_ITEMS = 16
DIL_QBLOCK = 128
DIL_STEP_TOKENS = 2048
MOE_ROWS = 1024
MOE_SUB_ROWS = 256

V7X_VMEM_BYTES = 64 * 1024 * 1024
VMEM_LIMIT = V7X_VMEM_BYTES * 13 // 16

HALF_MODEL = D_MODEL // 2
ROW_CHUNKS = HALF_MODEL // LANES
SC_CORES = 2
SC_SUBCORES = 16
SC_WORKERS = SC_CORES * SC_SUBCORES
SC_WINDOW = 128
SC_PAIR = 2

bf16 = jnp.bfloat16
f32 = jnp.float32


def _cparams(sem):
    return pltpu.CompilerParams(dimension_semantics=sem, vmem_limit_bytes=VMEM_LIMIT)


def _pack_rows(v):
    lo = lax.bitcast_convert_type(v[:, :HALF_MODEL].astype(bf16).astype(f32), jnp.uint32) >> 16
    hi = lax.bitcast_convert_type(v[:, HALF_MODEL:].astype(bf16).astype(f32), jnp.uint32) & jnp.uint32(0xFFFF0000)
    w = lo | hi
    return [w[:, c * LANES:(c + 1) * LANES] for c in range(ROW_CHUNKS)]


def _unpack_rows(chunks):
    w = jnp.concatenate(chunks, axis=1)
    lo = lax.bitcast_convert_type(w << 16, f32)
    hi = lax.bitcast_convert_type(w & jnp.uint32(0xFFFF0000), f32)
    return lo, hi


def _in_proj_kernel(x_ref, nm_ref, w_ref, gqa_ref, gka_ref, gqb_ref, gkb_ref,
                    ca_ref, sa_ref, cb_ref, sb_ref, bd_ref,
                    qa_ref, ka_ref, va_ref, qb1_ref, qb4_ref, qb16_ref, kb1_ref, kb4_ref, kb16_ref,
                    vb1_ref, vb4_ref, vb16_ref, ga_ref, gb_ref, st_ref):
    x = x_ref[...]
    r = lax.rsqrt(jnp.mean(x * x, axis=-1, keepdims=True) + EPS)
    u = (x * r * nm_ref[...]).astype(bf16)
    rows = x.shape[0]
    lane = lax.broadcasted_iota(jnp.int32, (rows, LANES), 1)
    first_half = (lane % HEAD_DIM) < (HEAD_DIM // 2)

    def proj(off, width):
        return jnp.dot(u, w_ref[:, off:off + width], preferred_element_type=f32)

    def norm_rope(off, width, g_ref, c_ref, s_ref):
        z = proj(off, width)
        c = c_ref[...]
        s = s_ref[...]
        chunks = []
        for c0 in range(0, width, NORM_COLS):
            cw = min(NORM_COLS, width - c0)
            zc = z[:, c0:c0 + cw]
            ms = jnp.dot((zc * zc).astype(bf16), bd_ref[:cw, :cw], preferred_element_type=f32) * (1.0 / HEAD_DIM)
            zn = zc * lax.rsqrt(ms + EPS) * g_ref[:, c0:c0 + cw]
            for h0 in range(0, cw, LANES):
                zh = zn[:, h0:h0 + LANES]
                sw = jnp.where(first_half, pltpu.roll(zh, LANES - HEAD_DIM // 2, 1), pltpu.roll(zh, HEAD_DIM // 2, 1))
                chunks.append(zh * c + sw * s)
        return chunks

    def store(chunks, out_ref):
        for j, ch in enumerate(chunks):
            out_ref[:, j * LANES:(j + 1) * LANES] = ch.astype(out_ref.dtype)

    def store_dilated(chunks, out_refs):
        for j, ch in enumerate(chunks):
            st_ref[j] = ch
        for (_, dil), out_ref in zip(WINDOW_DILATIONS, out_refs):
            for r in range(dil):
                for j in range(len(chunks)):
                    src = pl.ds(r, rows // dil, stride=dil) if dil > 1 else pl.ds(0, rows)
                    out_ref[r, :, j * LANES:(j + 1) * LANES] = st_ref[j, src, :].astype(out_ref.dtype)

    store(norm_rope(OFF_QA, A_WIDTH, gqa_ref, ca_ref, sa_ref), qa_ref)
    store(norm_rope(OFF_KA, KA_COLS, gka_ref, ca_ref, sa_ref), ka_ref)
    va = proj(OFF_VA, VA_COLS)
    va_ref[...] = jnp.concatenate(
        [jnp.where(lane < HEAD_DIM, va[:, c0:c0 + LANES], 1.0) for c0 in range(0, VA_COLS, LANES)],
        axis=1).astype(va_ref.dtype)
    store_dilated(norm_rope(OFF_QB, B_WIDTH, gqb_ref, cb_ref, sb_ref), (qb1_ref, qb4_ref, qb16_ref))
    store_dilated(norm_rope(OFF_KB, B_WIDTH, gkb_ref, cb_ref, sb_ref), (kb1_ref, kb4_ref, kb16_ref))
    vb = proj(OFF_VB, B_WIDTH)
    store_dilated([vb[:, c0:c0 + LANES] for c0 in range(0, B_WIDTH, LANES)], (vb1_ref, vb4_ref, vb16_ref))
    ga_ref[...] = jax.nn.sigmoid(proj(OFF_GA, D_MODEL)).astype(ga_ref.dtype)
    gb_ref[...] = jax.nn.sigmoid(proj(OFF_GB, D_MODEL)).astype(gb_ref.dtype)


def _in_proj(x2, S, prm):
    T = x2.shape[0]
    tm = TOKEN_TILE
    n_pos = S // tm
    row = lambda i: (i, 0)
    pos = lambda i: (i % n_pos, 0)
    const = lambda i: (0, 0)
    dils = [dil for _, dil in WINDOW_DILATIONS]
    flat = lambda w: (pl.BlockSpec((tm, w), row), jax.ShapeDtypeStruct((T, w), bf16))
    strided = lambda d: (pl.BlockSpec((d, tm // d, B_WIDTH), lambda i: (0, i, 0)),
                         jax.ShapeDtypeStruct((d, T // d, B_WIDTH), bf16))
    outs = ([flat(A_WIDTH), flat(KA_COLS), flat(VA_COLS)] + 3 * [strided(d) for d in dils]
            + [flat(D_MODEL), flat(D_MODEL)])
    return pl.pallas_call(
        _in_proj_kernel,
        grid=(T // tm,),
        in_specs=[
            pl.BlockSpec((tm, D_MODEL), row),
            pl.BlockSpec((1, D_MODEL), const),
            pl.BlockSpec((D_MODEL, PROJ_COLS), const),
            pl.BlockSpec((1, A_WIDTH), const),
            pl.BlockSpec((1, KA_COLS), const),
            pl.BlockSpec((1, B_WIDTH), const),
            pl.BlockSpec((1, B_WIDTH), const),
            pl.BlockSpec((tm, LANES), pos),
            pl.BlockSpec((tm, LANES), pos),
            pl.BlockSpec((tm, LANES), pos),
            pl.BlockSpec((tm, LANES), pos),
            pl.BlockSpec((NORM_COLS, NORM_COLS), const),
        ],
        out_specs=[spec for spec, _ in outs],
        out_shape=[shape for _, shape in outs],
        scratch_shapes=[pltpu.VMEM((B_WIDTH // LANES, tm, LANES), f32)],
        compiler_params=_cparams(("parallel",)),
        name="in_proj",
    )(x2, prm["norm_mix"], prm["w_in"], prm["gqa"], prm["gka"], prm["gqb"], prm["gkb"],
      prm["cos_a"][S], prm["sin_a"][S], prm["cos_b"][S], prm["sin_b"][S], prm["bd"])


def _flash_kernel(q_ref, k_ref, v_ref, o_ref, s_ref, *, tq, tk, seq, nq):
    lane = lax.broadcasted_iota(jnp.int32, (tq, LANES), 1)
    lo = lane < HEAD_DIM
    n_chunks = seq // tk

    def masked_heads(b):
        q = q_ref[0, b * tq:(b + 1) * tq, :]
        q01, q23 = q[:, :LANES], q[:, LANES:]
        zero = jnp.zeros_like(q01)
        return [jnp.where(lo, q01, zero), jnp.where(lo, zero, q01), jnp.where(lo, q23, zero), jnp.where(lo, zero, q23)]

    def score(qh, chunk):
        kk = k_ref[0, chunk * tk:(chunk + 1) * tk, :]
        return lax.dot_general(qh, kk, (((1,), (1,)), ((), ())), preferred_element_type=f32)

    qs = [masked_heads(b) for b in range(nq)]
    items = [(b, c) for b in range(nq) for c in range(n_chunks)]
    for h in range(GROUP_Q):
        s_ref[0, h] = score(qs[0][h], 0)
    ms = accs = None
    for n, (b, chunk) in enumerate(items):
        slot = n % 2
        nxt = items[n + 1] if n + 1 < len(items) else None
        if chunk == 0:
            ms = [jnp.full((tq, 1), NEG_BIG, f32) for _ in range(GROUP_Q)]
            accs = [jnp.zeros((tq, LANES), f32) for _ in range(GROUP_Q)]
        vv = v_ref[0, chunk * tk:(chunk + 1) * tk, :]
        for h in range(GROUP_Q):
            if nxt is not None:
                s_ref[1 - slot, h] = score(qs[nxt[0]][h], nxt[1])
            s = s_ref[slot, h]
            m_new = jnp.maximum(ms[h], jnp.max(s, axis=-1, keepdims=True))
            p = jnp.exp2((s - m_new).astype(bf16))
            alpha = jnp.exp2(ms[h] - m_new)
            accs[h] = accs[h] * alpha + jnp.dot(p, vv, preferred_element_type=f32)
            ms[h] = m_new
        if chunk == n_chunks - 1:
            heads = [acc / jnp.where(lo, pltpu.roll(acc, HEAD_DIM, 1), 1.0) for acc in accs]
            out01 = jnp.where(lo, heads[0], pltpu.roll(heads[1], HEAD_DIM, 1))
            out23 = jnp.where(lo, heads[2], pltpu.roll(heads[3], HEAD_DIM, 1))
            o_ref[0, b * tq:(b + 1) * tq, :] = jnp.concatenate([out01, out23], axis=1).astype(o_ref.dtype)


def _flash(qa, ka, va):
    B, S, _ = qa.shape
    tq, tk = FLASH_TQ, FLASH_TK
    nq = max(1, FLASH_ITEMS // (S // tk))
    return pl.pallas_call(
        functools.partial(_flash_kernel, tq=tq, tk=tk, seq=S, nq=nq),
        grid=(B, A_KV_HEADS, S // (nq * tq)),
        in_specs=[
            pl.BlockSpec((1, nq * tq, GROUP_Q * HEAD_DIM), lambda b, g, i: (b, i, g)),
            pl.BlockSpec((1, S, LANES), lambda b, g, i: (b, 0, g)),
            pl.BlockSpec((1, S, LANES), lambda b, g, i: (b, 0, g)),
        ],
        out_specs=pl.BlockSpec((1, nq * tq, GROUP_Q * HEAD_DIM), lambda b, g, i: (b, i, g)),
        out_shape=jax.ShapeDtypeStruct((B, S, A_WIDTH), bf16),
        scratch_shapes=[pltpu.VMEM((2, GROUP_Q, tq, tk), f32)],
        compiler_params=_cparams(("parallel", "parallel", "arbitrary")),
        name="flash_gqa",
    )(qa, ka, va)


def _dilated_kernel(q_ref, k_ref, v_ref, o_ref, lse_ref, so_ref, sl_ref, *, length, tms, dil):
    i = pl.program_id(1)
    lane = lax.broadcasted_iota(jnp.int32, (DIL_QBLOCK, LANES), 1)
    lo = lane < HEAD_DIM
    n_pairs = B_WIDTH // LANES
    ones = jnp.ones((KEY_SPAN, LANES), bf16)

    def unit(u, carry):
        sb, r = u // dil, u % dil
        r0 = pl.multiple_of(sb * DIL_QBLOCK, DIL_QBLOCK)
        m0 = i * tms + r0
        start = pl.multiple_of(jnp.clip(m0 - HALF_WINDOW, 0, length - KEY_SPAN), HALF_WINDOW)
        qpos = m0 + lax.broadcasted_iota(jnp.int32, (DIL_QBLOCK, KEY_SPAN), 0)
        kpos = start + lax.broadcasted_iota(jnp.int32, (DIL_QBLOCK, KEY_SPAN), 1)
        valid = jnp.abs(kpos - qpos) <= HALF_WINDOW
        rows = pl.ds(r0 * dil + r, DIL_QBLOCK, stride=dil) if dil > 1 else pl.ds(r0, DIL_QBLOCK)
        scores = []
        for c in range(n_pairs):
            cs = slice(c * LANES, (c + 1) * LANES)
            qp = q_ref[r, 0, pl.ds(r0, DIL_QBLOCK), cs]
            kp = k_ref[r, 0, pl.ds(start, KEY_SPAN), cs]
            zero = jnp.zeros_like(qp)
            for half in range(2):
                qm = jnp.where(lo, qp, zero) if half == 0 else jnp.where(lo, zero, qp)
                scores.append(lax.dot_general(qm, kp, (((1,), (1,)), ((), ())), preferred_element_type=f32))
        for c in range(n_pairs):
            vp_ones = jnp.concatenate([v_ref[r, 0, pl.ds(start, KEY_SPAN), c * LANES:(c + 1) * LANES], ones], axis=1)
            ols, ms = [], []
            for half in range(2):
                s = jnp.where(valid, scores[2 * c + half], NEG_BIG)
                m = jnp.max(s, axis=-1, keepdims=True)
                e = jnp.exp2((s - m).astype(bf16))
                ols.append(jnp.dot(e, vp_ones, preferred_element_type=f32))
                ms.append(m)
            num = jnp.where(lo, ols[0][:, :LANES], ols[1][:, :LANES])
            den = jnp.where(lo, ols[0][:, LANES:], ols[1][:, LANES:])
            so_ref[c, rows, :] = num / den
            sl_ref[c, rows, :] = jnp.where(lo, ms[0], ms[1]) + jnp.log2(den)
        return carry

    lax.fori_loop(0, (tms // DIL_QBLOCK) * dil, unit, 0, unroll=8)
    o_ref[0] = jnp.concatenate([so_ref[c] for c in range(n_pairs)], axis=1).astype(o_ref.dtype)
    lse_ref[0] = jnp.concatenate([sl_ref[c] for c in range(n_pairs)], axis=1)


def _dilated(qb, kb, vb, B, S, dil):
    length = S // dil
    tms = DIL_STEP_TOKENS // dil
    view = lambda a: a.reshape(dil, B, length, B_WIDTH)
    o, lse = pl.pallas_call(
        functools.partial(_dilated_kernel, length=length, tms=tms, dil=dil),
        grid=(B, length // tms),
        in_specs=[
            pl.BlockSpec((dil, 1, tms, B_WIDTH), lambda b, i: (0, b, i, 0)),
            pl.BlockSpec((dil, 1, length, B_WIDTH), lambda b, i: (0, b, 0, 0)),
            pl.BlockSpec((dil, 1, length, B_WIDTH), lambda b, i: (0, b, 0, 0)),
        ],
        out_specs=[
            pl.BlockSpec((1, tms * dil, B_WIDTH), lambda b, i: (b, i, 0)),
            pl.BlockSpec((1, tms * dil, B_WIDTH), lambda b, i: (b, i, 0)),
        ],
        out_shape=[
            jax.ShapeDtypeStruct((B, S, B_WIDTH), bf16),
            jax.ShapeDtypeStruct((B, S, B_WIDTH), f32),
        ],
        scratch_shapes=[pltpu.VMEM((B_WIDTH // LANES, tms * dil, LANES), f32),
                        pltpu.VMEM((B_WIDTH // LANES, tms * dil, LANES), f32)],
        compiler_params=_cparams(("parallel", "arbitrary")),
        name=f"dilated_{dil}",
    )(view(qb), view(kb), view(vb))
    return o.reshape(B * S, B_WIDTH), lse.reshape(B * S, B_WIDTH)


def _first_index_of_max(cur, idx, n):
    mx = jnp.max(cur, axis=0, keepdims=True)
    return jnp.min(jnp.where(cur == mx, idx, n), axis=0, keepdims=True)


def _merge_route_kernel(x_ref, ya_ref, o1_ref, o2_ref, o3_ref, l1_ref, l2_ref, l3_ref, ga_ref, gb_ref,
                        wa_ref, wb_ref, wo_ref, nf_ref, rhi_ref, rlo_ref, bias_ref, tri_ref,
                        h_ref, u_ref, idx_ref, wt_ref, rank_ref, cnt_ref, run_ref, lg_ref):
    step = pl.program_id(0)

    @pl.when(step == 0)
    def _():
        run_ref[...] = jnp.zeros_like(run_ref)
        lg_ref[...] = jnp.zeros_like(lg_ref)

    routing_live = step > 0
    cur, prev = step % 2, (step + 1) % 2

    tm = x_ref.shape[0]
    sub = tri_ref.shape[0]

    def dense(j):
        rs = slice(j * sub, (j + 1) * sub)
        l1, l2, l3 = l1_ref[rs, :], l2_ref[rs, :], l3_ref[rs, :]
        lm = jnp.maximum(jnp.maximum(l1, l2), l3)
        e1, e2, e3 = jnp.exp2(l1 - lm), jnp.exp2(l2 - lm), jnp.exp2(l3 - lm)
        yb = ((e1 * o1_ref[rs, :].astype(f32) + e2 * o2_ref[rs, :].astype(f32) + e3 * o3_ref[rs, :].astype(f32))
              / (e1 + e2 + e3))
        pa = jnp.dot(ya_ref[rs, :], wa_ref[...], preferred_element_type=f32)
        pb = jnp.dot(yb.astype(bf16), wb_ref[...], preferred_element_type=f32)
        merged = ga_ref[rs, :].astype(f32) * pa + gb_ref[rs, :].astype(f32) * pb
        h = x_ref[rs, :] + jnp.dot(merged.astype(bf16), wo_ref[...], preferred_element_type=f32)
        h_ref[rs, :] = h
        r = lax.rsqrt(jnp.mean(h * h, axis=-1, keepdims=True) + EPS)
        uf = h * r * nf_ref[...]
        u_hi = uf.astype(bf16)
        for c, chunk in enumerate(_pack_rows(uf)):
            u_ref[c, rs, :] = chunk
        u_lo = (uf - u_hi.astype(f32)).astype(bf16)
        return (jnp.dot(u_hi, rhi_ref[...], preferred_element_type=f32)
                + jnp.dot(u_lo, rhi_ref[...], preferred_element_type=f32)
                + jnp.dot(u_hi, rlo_ref[...], preferred_element_type=f32))

    def route(j, logits):
        scores = jax.nn.sigmoid(logits.T[:N_EXPERTS, :])
        sel = scores + bias_ref[...]
        i8 = lax.broadcasted_iota(jnp.int32, (EXPERTS_PER_GROUP, sub), 0)
        neg_inf = jnp.float32(-jnp.inf)
        gscore = jnp.zeros((N_GROUPS, sub), f32)
        for g in range(N_GROUPS):
            blk = sel[g * EXPERTS_PER_GROUP:(g + 1) * EXPERTS_PER_GROUP, :]
            top1 = jnp.max(blk, axis=0, keepdims=True)
            first = _first_index_of_max(blk, i8, EXPERTS_PER_GROUP)
            top2 = jnp.max(jnp.where(i8 == first, neg_inf, blk), axis=0, keepdims=True)
            gscore = jnp.where(i8 == g, top1 + top2, gscore)
        gkeep = jnp.zeros((N_GROUPS, sub), jnp.bool_)
        cur = gscore
        for _ in range(TOPK_GROUPS):
            pick = i8 == _first_index_of_max(cur, i8, N_GROUPS)
            gkeep = jnp.logical_or(gkeep, pick)
            cur = jnp.where(pick, neg_inf, cur)
        cur = jnp.concatenate(
            [jnp.where(gkeep[g:g + 1, :], sel[g * EXPERTS_PER_GROUP:(g + 1) * EXPERTS_PER_GROUP, :], NEG_BIG)
             for g in range(N_GROUPS)], axis=0)

        ie = lax.broadcasted_iota(jnp.int32, (N_EXPERTS, sub), 0)
        ik = lax.broadcasted_iota(jnp.int32, (TOP_K, sub), 0)
        idx_out = jnp.zeros((TOP_K, sub), jnp.int32)
        w_out = jnp.zeros((TOP_K, sub), f32)
        chosen = jnp.zeros((N_EXPERTS, sub), jnp.bool_)
        for k in range(TOP_K):
            fi = _first_index_of_max(cur, ie, N_EXPERTS)
            pick = ie == fi
            wk = jnp.sum(jnp.where(pick, scores, 0.0), axis=0, keepdims=True)
            idx_out = jnp.where(ik == k, fi, idx_out)
            w_out = jnp.where(ik == k, wk, w_out)
            chosen = jnp.logical_or(chosen, pick)
            cur = jnp.where(pick, neg_inf, cur)
        w_out = w_out / jnp.sum(w_out, axis=0, keepdims=True) * ROUTE_SCALE
        wt_ref[:, j * sub:(j + 1) * sub] = w_out

        sel01 = jnp.where(chosen, 1.0, 0.0)
        before = jnp.dot(sel01.astype(bf16), tri_ref[...], preferred_element_type=f32) + run_ref[:, 0:1]
        rank_out = jnp.zeros((TOP_K, sub), f32)
        for k in range(TOP_K):
            rk = jnp.sum(jnp.where(ie == idx_out[k:k + 1, :], before, 0.0), axis=0, keepdims=True)
            rank_out = jnp.where(ik == k, rk, rank_out)
        rank_i = rank_out.astype(jnp.int32)
        for w in range(sub // SC_WINDOW):
            win = j * (sub // SC_WINDOW) + w
            idx_ref[win] = idx_out[:, w * SC_WINDOW:(w + 1) * SC_WINDOW]
            rank_ref[win] = rank_i[:, w * SC_WINDOW:(w + 1) * SC_WINDOW]
        run_ref[...] = run_ref[...] + jnp.where(routing_live, jnp.sum(sel01, axis=1, keepdims=True), 0.0)

    n_sub = tm // sub
    prev_logits = [lg_ref[prev, j * sub:(j + 1) * sub, :] for j in range(n_sub)]
    for j in range(n_sub):
        lg_ref[cur, j * sub:(j + 1) * sub, :] = dense(j)
    for j in range(n_sub):
        route(j, prev_logits[j])
    cnt_ref[...] = run_ref[...].astype(jnp.int32)


def _merge_route(x2, ya, outs, lses, sga, sgb, prm):
    T = x2.shape[0]
    tm = TOKEN_TILE
    n_tiles = T // tm
    dense_i = lambda i: jnp.minimum(i, n_tiles - 1)
    route_i = lambda i: jnp.maximum(i - 1, 0)
    const = lambda i: (0, 0)
    tok = lambda w: pl.BlockSpec((tm, w), lambda i: (dense_i(i), 0))
    full = lambda a: pl.BlockSpec(a.shape, const)
    win = pl.BlockSpec((tm // SC_WINDOW, TOP_K, SC_WINDOW), lambda i: (route_i(i), 0, 0))
    return pl.pallas_call(
        _merge_route_kernel,
        grid=(n_tiles + 1,),
        in_specs=[tok(D_MODEL), tok(A_WIDTH), tok(B_WIDTH), tok(B_WIDTH), tok(B_WIDTH),
                  tok(B_WIDTH), tok(B_WIDTH), tok(B_WIDTH), tok(D_MODEL), tok(D_MODEL),
                  full(prm["w_branch_a"]), full(prm["w_branch_b"]), full(prm["w_out"]), full(prm["norm_ffn"]),
                  full(prm["router_hi"]), full(prm["router_lo"]), full(prm["router_bias"]), full(prm["tri"])],
        out_specs=[tok(D_MODEL), pl.BlockSpec((ROW_CHUNKS, tm, LANES), lambda i: (0, dense_i(i), 0)),
                   win, pl.BlockSpec((TOP_K, tm), lambda i: (0, route_i(i))), win,
                   pl.BlockSpec((N_EXPERTS, LANES), const)],
        out_shape=[jax.ShapeDtypeStruct((T, D_MODEL), f32), jax.ShapeDtypeStruct((ROW_CHUNKS, T, LANES), jnp.uint32),
                   jax.ShapeDtypeStruct((T // SC_WINDOW, TOP_K, SC_WINDOW), jnp.int32),
                   jax.ShapeDtypeStruct((TOP_K, T), f32),
                   jax.ShapeDtypeStruct((T // SC_WINDOW, TOP_K, SC_WINDOW), jnp.int32),
                   jax.ShapeDtypeStruct((N_EXPERTS, LANES), jnp.int32)],
        scratch_shapes=[pltpu.VMEM((N_EXPERTS, LANES), f32), pltpu.VMEM((2, tm, LANES), f32)],
        compiler_params=_cparams(("arbitrary",)),
        name="merge_route",
    )(x2, ya, *outs, *lses, sga, sgb,
      prm["w_branch_a"], prm["w_branch_b"], prm["w_out"], prm["norm_ffn"],
      prm["router_hi"], prm["router_lo"], prm["router_bias"], prm["tri"])


def _experts_kernel(be_ref, nu_ref, nv_ref, x_ref, wg_ref, wu_ref, wd_ref, y_ref):
    del be_ref
    i = pl.program_id(0)

    @pl.when(i < nu_ref[0])
    def _():
        n_sub = x_ref.shape[1] // MOE_SUB_ROWS

        def up(j):
            r0 = j * MOE_SUB_ROWS
            lo, hi = _unpack_rows([x_ref[c, r0:r0 + MOE_SUB_ROWS, :] for c in range(ROW_CHUNKS)])
            live = r0 + lax.broadcasted_iota(jnp.int32, lo.shape, 0) < nv_ref[i]
            x = jnp.concatenate([jnp.where(live, lo, 0.0), jnp.where(live, hi, 0.0)], axis=1).astype(bf16)
            return (jnp.dot(x, wg_ref[0], preferred_element_type=f32),
                    jnp.dot(x, wu_ref[0], preferred_element_type=f32))

        def down(j, gu):
            g, u = gu
            a = g * jax.nn.sigmoid(g) * u
            y = jnp.dot(a.astype(bf16), wd_ref[0], preferred_element_type=f32)
            r0 = j * MOE_SUB_ROWS
            for c, chunk in enumerate(_pack_rows(y)):
                y_ref[c, r0:r0 + MOE_SUB_ROWS, :] = chunk

        gu_prev = up(0)
        for j in range(1, n_sub):
            gu_next = up(j)
            down(j - 1, gu_prev)
            gu_prev = gu_next
        down(n_sub - 1, gu_prev)

    @pl.when(i >= nu_ref[0])
    def _():
        y_ref[...] = jnp.zeros_like(y_ref)


def _experts(xs, blk_e, n_used, n_valid, prm):
    n_rows = xs.shape[1]
    bm = MOE_ROWS
    rows = lambda i, be, nu, nv: (0, jnp.minimum(i, nu[0] - 1), 0)
    return pl.pallas_call(
        _experts_kernel,
        grid_spec=pltpu.PrefetchScalarGridSpec(
            num_scalar_prefetch=3,
            grid=(n_rows // bm,),
            in_specs=[
                pl.BlockSpec((ROW_CHUNKS, bm, LANES), rows),
                pl.BlockSpec((1, D_MODEL, D_EXPERT), lambda i, be, nu, nv: (be[i], 0, 0)),
                pl.BlockSpec((1, D_MODEL, D_EXPERT), lambda i, be, nu, nv: (be[i], 0, 0)),
                pl.BlockSpec((1, D_EXPERT, D_MODEL), lambda i, be, nu, nv: (be[i], 0, 0)),
            ],
            out_specs=pl.BlockSpec((ROW_CHUNKS, bm, LANES), lambda i, be, nu, nv: (0, i, 0)),
        ),
        out_shape=jax.ShapeDtypeStruct((ROW_CHUNKS, n_rows, LANES), jnp.uint32),
        compiler_params=_cparams(("arbitrary",)),
        name="experts",
    )(blk_e, n_used, n_valid, xs, prm["expert_gate"], prm["expert_up"], prm["expert_down"])


def _sc_mesh():
    return plsc.VectorSubcoreMesh(core_axis_name="c", subcore_axis_name="s")


def _sc_dispatch(u2p, dest, n_rows):
    T = u2p.shape[1]
    per_worker = T // SC_WORKERS
    n_it = per_worker // SC_WINDOW

    @functools.partial(
        pl.kernel, mesh=_sc_mesh(),
        out_type=jax.ShapeDtypeStruct((ROW_CHUNKS, n_rows, LANES), jnp.uint32),
        scratch_types=[pltpu.VMEM((TOP_K, SC_WINDOW), jnp.int32),
                       pltpu.VMEM((ROW_CHUNKS, SC_WINDOW, LANES), jnp.uint32),
                       pltpu.SemaphoreType.DMA, pltpu.SemaphoreType.DMA],
    )
    def k(u_hbm, dest_hbm, xs_hbm, idx_v, rows_v, sem_in, sem_out):
        wid = lax.axis_index("s") * SC_CORES + lax.axis_index("c")

        @pl.loop(0, n_it)
        def _(it):
            t0 = wid * per_worker + it * SC_WINDOW
            pltpu.sync_copy(dest_hbm.at[wid * n_it + it], idx_v)
            loads = [pltpu.async_copy(u_hbm.at[c].at[pl.ds(t0, SC_WINDOW)], rows_v.at[c], sem_in)
                     for c in range(ROW_CHUNKS)]
            for cp in loads:
                cp.wait()
            stores = [pltpu.async_copy(rows_v.at[c], xs_hbm.at[c].at[idx_v.at[kk]], sem_out)
                      for kk in range(TOP_K) for c in range(ROW_CHUNKS)]
            for cp in stores:
                cp.wait()

    return k(u2p, dest)


def _sc_combine(ys, dest):
    n_windows = dest.shape[0]
    T = n_windows * SC_WINDOW
    per_worker = T // SC_WORKERS
    n_it = per_worker // SC_WINDOW

    items = [(c, k0) for c in range(ROW_CHUNKS) for k0 in range(0, TOP_K, SC_PAIR)]

    @functools.partial(
        pl.kernel, mesh=_sc_mesh(),
        out_type=jax.ShapeDtypeStruct((TOP_K, ROW_CHUNKS, T, LANES), jnp.uint32),
        scratch_types=[pltpu.VMEM((TOP_K, SC_WINDOW), jnp.int32),
                       pltpu.VMEM((2, SC_PAIR, SC_WINDOW, LANES), jnp.uint32),
                       pltpu.SemaphoreType.DMA, pltpu.SemaphoreType.DMA,
                       pltpu.SemaphoreType.DMA, pltpu.SemaphoreType.DMA],
    )
    def k(ys_hbm, dest_hbm, yg_hbm, idx_v, rows_v, sem_in0, sem_in1, sem_out0, sem_out1):
        sem_in, sem_out = (sem_in0, sem_in1), (sem_out0, sem_out1)
        wid = lax.axis_index("s") * SC_CORES + lax.axis_index("c")

        @pl.loop(0, n_it)
        def _(it):
            t0 = wid * per_worker + it * SC_WINDOW
            pltpu.sync_copy(dest_hbm.at[wid * n_it + it], idx_v)

            def gathers(n):
                c, k0 = items[n]
                return [pltpu.async_copy(ys_hbm.at[c].at[idx_v.at[k0 + j]], rows_v.at[n % 2, j], sem_in[n % 2])
                        for j in range(SC_PAIR)]

            def stores(n):
                c, k0 = items[n]
                return [pltpu.async_copy(rows_v.at[n % 2, j], yg_hbm.at[k0 + j].at[c].at[pl.ds(t0, SC_WINDOW)],
                                         sem_out[n % 2]) for j in range(SC_PAIR)]

            pending_gathers = gathers(0)
            pending_stores = {}
            for n in range(len(items)):
                for cp in pending_gathers:
                    cp.wait()
                if n + 1 < len(items):
                    for cp in pending_stores.pop((n + 1) % 2, []):
                        cp.wait()
                    pending_gathers = gathers(n + 1)
                pending_stores[n % 2] = stores(n)
            for copies in pending_stores.values():
                for cp in copies:
                    cp.wait()

    return k(ys, dest)


def _final_kernel(h_ref, u_ref, yg_ref, wt_ref, p_ref, sgu_ref, sd_ref, pg_ref, pp_ref, o_ref):
    wl = wt_ref[...]
    wt = jnp.concatenate([wl, jnp.zeros((LANES - TOP_K, wl.shape[1]), f32)], axis=0).T
    r_lo = r_hi = None
    for k in range(TOP_K):
        lo, hi = _unpack_rows([yg_ref[k, c] for c in range(ROW_CHUNKS)])
        wk = wt[:, k:k + 1]
        r_lo = lo * wk if k == 0 else r_lo + lo * wk
        r_hi = hi * wk if k == 0 else r_hi + hi * wk
    routed = jnp.concatenate([r_lo, r_hi], axis=1)
    u_lo, u_hi = _unpack_rows([u_ref[c] for c in range(ROW_CHUNKS)])
    un = jnp.concatenate([u_lo, u_hi], axis=1).astype(bf16)
    gu = jnp.dot(un, sgu_ref[...], preferred_element_type=f32)
    g, u = gu[:, :D_SHARED], gu[:, D_SHARED:]
    shared = jnp.dot((g * jax.nn.sigmoid(g) * u).astype(bf16), sd_ref[...], preferred_element_type=f32)
    h = h_ref[...] + (routed + shared)
    gate = jax.nn.sigmoid(jnp.dot(h.astype(bf16), pg_ref[...], preferred_element_type=f32))
    emb = jnp.dot(p_ref[...].astype(bf16), pp_ref[...], preferred_element_type=f32)
    o_ref[...] = h + gate * emb


def _final(h1, u2p, yg, wts, p2, prm):
    T = h1.shape[0]
    tm = TOKEN_TILE
    row = lambda i: (i, 0)
    const = lambda i: (0, 0)
    full = lambda a: pl.BlockSpec(a.shape, const)
    return pl.pallas_call(
        _final_kernel,
        grid=(T // tm,),
        in_specs=[pl.BlockSpec((tm, D_MODEL), row),
                  pl.BlockSpec((ROW_CHUNKS, tm, LANES), lambda i: (0, i, 0)),
                  pl.BlockSpec((TOP_K, ROW_CHUNKS, tm, LANES), lambda i: (0, 0, i, 0)),
                  pl.BlockSpec((TOP_K, tm), lambda i: (0, i)),
                  pl.BlockSpec((tm, PLE_DIM), row),
                  full(prm["shared_gu"]), full(prm["shared_down"]), full(prm["ple_gate"]), full(prm["ple_proj"])],
        out_specs=pl.BlockSpec((tm, D_MODEL), row),
        out_shape=jax.ShapeDtypeStruct((T, D_MODEL), f32),
        compiler_params=_cparams(("parallel",)),
        name="final",
    )(h1, u2p, yg, wts, p2, prm["shared_gu"], prm["shared_down"], prm["ple_gate"], prm["ple_proj"])


def _deinterleave(n_heads):
    base = jnp.concatenate([jnp.arange(0, HEAD_DIM, 2), jnp.arange(1, HEAD_DIM, 2)])
    return (jnp.arange(n_heads)[:, None] * HEAD_DIM + base[None, :]).reshape(-1)


def _rope_tables(ang):
    c, s = jnp.cos(ang), jnp.sin(ang)
    return jnp.tile(jnp.concatenate([c, c], axis=-1), (1, 2)), jnp.tile(jnp.concatenate([-s, s], axis=-1), (1, 2))


def _axial_angles(S):
    rows = S // GRID_W
    row = jnp.repeat(jnp.arange(rows, dtype=f32), GRID_W)
    col = jnp.tile(jnp.arange(GRID_W, dtype=f32), rows)
    half = HEAD_DIM // 2
    inv = ROPE_THETA ** (-jnp.arange(0, half, 2, dtype=f32) / half)
    return jnp.concatenate([row[:, None] * inv, col[:, None] * inv], axis=-1)


def _linear_angles(S):
    t = jnp.arange(S, dtype=f32)
    inv = ROPE_THETA ** (-jnp.arange(0, HEAD_DIM, 2, dtype=f32) / HEAD_DIM)
    return t[:, None] * inv


def _prepare(seq_lens, norm_mix, w_in, q_norm_a, k_norm_a, q_norm_b, k_norm_b, w_branch_a, w_branch_b, w_out,
             norm_ffn, router_w, router_bias, expert_gate, expert_up, expert_down, shared_gate, shared_up,
             shared_down, ple_proj, ple_gate):
    cuts = [A_WIDTH, A_WIDTH + A_KV_WIDTH, A_WIDTH + 2 * A_KV_WIDTH,
            A_WIDTH + 2 * A_KV_WIDTH + B_WIDTH, A_WIDTH + 2 * A_KV_WIDTH + 2 * B_WIDTH,
            A_WIDTH + 2 * A_KV_WIDTH + 3 * B_WIDTH, A_WIDTH + 2 * A_KV_WIDTH + 3 * B_WIDTH + D_MODEL]
    wqa, wka, wva, wqb, wkb, wvb, wga, wgb = jnp.split(w_in, cuts, axis=-1)
    wqa = wqa[:, _deinterleave(A_Q_HEADS)]
    wka = wka[:, _deinterleave(A_KV_HEADS)]
    wqb = wqb[:, _deinterleave(B_HEADS)]
    wkb = wkb[:, _deinterleave(B_HEADS)]
    k0, k1 = wka[:, :HEAD_DIM], wka[:, HEAD_DIM:]
    v0, v1 = wva[:, :HEAD_DIM], wva[:, HEAD_DIM:]
    vz = jnp.zeros_like(v0)
    w_all = jnp.concatenate([wqa, k0, k0, k1, k1, v0, vz, v1, vz, wqb, wkb, wvb, wga, wgb], axis=-1).astype(bf16)
    perm = _deinterleave(1)
    scale = HEAD_DIM ** -0.5
    tile = lambda g, n: jnp.tile(g[perm], n)[None, :]
    router_pad = jnp.pad(router_w, ((0, 0), (0, LANES - N_EXPERTS)))
    router_hi = router_pad.astype(bf16)
    prm = {
        "norm_mix": norm_mix[None, :], "w_in": w_all,
        "gqa": tile(q_norm_a, A_Q_HEADS) * (scale * LOG2_E), "gka": tile(k_norm_a, KA_COLS // HEAD_DIM),
        "gqb": tile(q_norm_b, B_HEADS) * (scale * LOG2_E), "gkb": tile(k_norm_b, B_HEADS),
        "bd": jnp.kron(jnp.eye(NORM_COLS // HEAD_DIM, dtype=f32), jnp.ones((HEAD_DIM, HEAD_DIM), f32)).astype(bf16),
        "w_branch_a": w_branch_a.astype(bf16), "w_branch_b": w_branch_b.astype(bf16), "w_out": w_out.astype(bf16),
        "norm_ffn": norm_ffn[None, :],
        "router_hi": router_hi, "router_lo": (router_pad - router_hi.astype(f32)).astype(bf16),
        "router_bias": router_bias[:, None],
        "tri": (jnp.arange(MERGE_SUB)[:, None] < jnp.arange(MERGE_SUB)[None, :]).astype(bf16),
        "expert_gate": expert_gate.astype(bf16), "expert_up": expert_up.astype(bf16),
        "expert_down": expert_down.astype(bf16),
        "shared_gu": jnp.concatenate([shared_gate, shared_up], axis=-1).astype(bf16),
        "shared_down": shared_down.astype(bf16),
        "ple_gate": ple_gate.astype(bf16), "ple_proj": ple_proj.astype(bf16),
        "cos_a": {}, "sin_a": {}, "cos_b": {}, "sin_b": {},
    }
    for S in seq_lens:
        prm["cos_a"][S], prm["sin_a"][S] = _rope_tables(_axial_angles(S))
        prm["cos_b"][S], prm["sin_b"][S] = _rope_tables(_linear_angles(S))
    return prm


def _dispatch_plan(idx, rank, counts, T):
    bm = MOE_ROWS
    n_rows = (T * TOP_K + N_EXPERTS * bm) // bm * bm
    n_blocks = n_rows // bm
    padded = (counts + bm - 1) // bm * bm
    pad_end = jnp.cumsum(padded)
    pad_start = pad_end - padded
    dest = pad_start[idx] + rank
    blk_start = jnp.arange(n_blocks, dtype=jnp.int32) * bm
    blk_e = jnp.minimum(jnp.sum(pad_end[None, :] <= blk_start[:, None], axis=1), N_EXPERTS - 1).astype(jnp.int32)
    n_used = (pad_end[-1:] // bm).astype(jnp.int32)
    own = blk_e[:, None] == jnp.arange(N_EXPERTS, dtype=jnp.int32)[None, :]
    real_end = jnp.sum(jnp.where(own, (pad_start + counts)[None, :], 0), axis=1)
    n_valid = jnp.clip(real_end - blk_start, 0, bm).astype(jnp.int32)
    return dest, n_rows, blk_e, n_used, n_valid


def _layer(x, p, prm):
    B, S, _ = x.shape
    T = B * S
    x2 = x.reshape(T, D_MODEL)
    qa, ka, va, *qkv_b, sga, sgb = _in_proj(x2, S, prm)
    shape3 = lambda a: a.reshape(B, S, a.shape[-1])
    ya = _flash(shape3(qa), shape3(ka), shape3(va)).reshape(T, A_WIDTH)
    n_dil = len(WINDOW_DILATIONS)
    outs, lses = [], []
    for j, (_, dil) in enumerate(WINDOW_DILATIONS):
        o_d, lse_d = _dilated(qkv_b[j], qkv_b[n_dil + j], qkv_b[2 * n_dil + j], B, S, dil)
        outs.append(o_d)
        lses.append(lse_d)
    h1, u2p, idx, wts, rank, cnt = _merge_route(x2, ya, outs, lses, sga, sgb, prm)
    dest, n_rows, blk_e, n_used, n_valid = _dispatch_plan(idx, rank, cnt[:, 0], T)
    xs = _sc_dispatch(u2p, dest, n_rows)
    ys = _experts(xs, blk_e, n_used, n_valid, prm)
    yg = _sc_combine(ys, dest)
    out = _final(h1, u2p, yg, wts, p.reshape(T, PLE_DIM), prm)
    return out.reshape(B, S, D_MODEL)


def kernel(x_prompt, x_sample, p_prompt, p_sample, norm_mix, w_in, q_norm_a, k_norm_a, q_norm_b, k_norm_b,
           w_branch_a, w_branch_b, w_out, norm_ffn, router_w, router_bias, expert_gate, expert_up, expert_down,
           shared_gate, shared_up, shared_down, ple_proj, ple_gate):
    weights = (norm_mix, w_in, q_norm_a, k_norm_a, q_norm_b, k_norm_b, w_branch_a, w_branch_b, w_out, norm_ffn,
               router_w, router_bias, expert_gate, expert_up, expert_down, shared_gate, shared_up, shared_down,
               ple_proj, ple_gate)
    depth = norm_mix.shape[0]
    seq_lens = sorted({x_prompt.shape[1], x_sample.shape[1]})
    hp, hs = x_prompt, x_sample
    for i in range(depth):
        prm = _prepare(seq_lens, *[w[i] for w in weights])
        hs = _layer(hs, p_sample[i], prm)
        hp = _layer(hp, p_prompt[i], prm)
    return hp, hs
```
